```python
import jax, jax.numpy as jnp
from jax import lax
import numpy as np

D_MODEL = 2048
BATCH = 2
SEQ = 8192
DEPTH = 4

HEAD_DIM = 64
HALF_DIM = HEAD_DIM // 2
N_MIXERS = 4
N_HEADS_PER_MIXER = 8
MIX_WIDTH = N_MIXERS * N_HEADS_PER_MIXER * HEAD_DIM
IN_COLS = 4 * MIX_WIDTH + N_HEADS_PER_MIXER
N_QK_NORMED = 3
ROPE_THETA = 10000.0
RMS_EPS = 1e-6
Q_BLOCK = 128
DILATED_SEGMENTS = ((128, 1), (512, 4), (2048, 16))
MOBA_BLOCK = 256
MOBA_TOPK = 3
MOBA_Q_CHUNK = 64
NEG = -1e30
SCALE = HEAD_DIM ** -0.5

kernel_name = "hybrid_parallel_heads_dilated_stickbreak_moba_fox"


def rms_norm(x, g):
    xf = x.astype(jnp.float32)
    y = xf * lax.rsqrt(jnp.mean(xf * xf, axis=-1, keepdims=True) + RMS_EPS)
    return y * g.astype(jnp.float32)


def head_rms_norm(t, g):
    return t * lax.rsqrt(jnp.mean(t * t, axis=-1, keepdims=True) + RMS_EPS) * g.astype(jnp.float32)


def rope_tables(seq):
    inv = 1.0 / (ROPE_THETA ** (jnp.arange(0, HEAD_DIM, 2, dtype=jnp.float32) / HEAD_DIM))
    ang = jnp.arange(seq, dtype=jnp.float32)[:, None] * inv[None, :]
    return jnp.cos(ang), jnp.sin(ang)


def apply_rope(t, cos, sin):
    t1, t2 = t[..., :HALF_DIM], t[..., HALF_DIM:]
    return jnp.concatenate([t1 * cos - t2 * sin, t2 * cos + t1 * sin], axis=-1)


def banded_attention(q, k, v, span):
    L = q.shape[-2]
    lead = q.shape[:-2]
    nb = -(-L // Q_BLOCK)
    Lp = nb * Q_BLOCK
    pad = [(0, 0)] * len(lead) + [(0, Lp - L), (0, 0)]
    q, k, v = (jnp.pad(t, pad) for t in (q, k, v))
    qb = q.reshape(*lead, nb, Q_BLOCK, HEAD_DIM)
    kb = k.reshape(*lead, nb, Q_BLOCK, HEAD_DIM)
    vb = v.reshape(*lead, nb, Q_BLOCK, HEAD_DIM)
    zero = jnp.zeros_like(kb[..., :1, :, :])
    kwin = jnp.concatenate([jnp.concatenate([zero, kb[..., :-1, :, :]], axis=-3), kb], axis=-2)
    vwin = jnp.concatenate([jnp.concatenate([zero, vb[..., :-1, :, :]], axis=-3), vb], axis=-2)
    s = jnp.einsum('...nqd,...nkd->...nqk', qb, kwin) * SCALE
    blk = jnp.arange(nb)[:, None]
    qpos = blk * Q_BLOCK + jnp.arange(Q_BLOCK)[None, :]
    kpos = (blk - 1) * Q_BLOCK + jnp.arange(2 * Q_BLOCK)[None, :]
    dist = qpos[:, :, None] - kpos[:, None, :]
    mask = (dist >= 0) & (dist <= span) & (kpos[:, None, :] >= 0)
    s = jnp.where(mask, s, NEG)
    lse = jax.nn.logsumexp(s, axis=-1)
    p = jnp.exp(s - lse[..., None])
    o = jnp.einsum('...nqk,...nkd->...nqd', p, vwin)
    return o.reshape(*lead, Lp, HEAD_DIM)[..., :L, :], lse.reshape(*lead, Lp)[..., :L]


def dilated_attention(q, k, v):
    bsz, nh, seq, hd = q.shape
    outs, lses = [], []
    for window, dil in DILATED_SEGMENTS:
        L = seq // dil
        sub = lambda t: t.reshape(bsz, nh, L, dil, hd).swapaxes(2, 3)
        o, lse = banded_attention(sub(q), sub(k), sub(v), window // dil)
        outs.append(o.swapaxes(2, 3).reshape(bsz, nh, seq, hd))
        lses.append(lse.swapaxes(2, 3).reshape(bsz, nh, seq))
    w = jax.nn.softmax(jnp.stack(lses, axis=0), axis=0)
    return sum(w[i][..., None] * outs[i] for i in range(len(outs)))


def stick_breaking_attention(q, k, v):
    bsz, nh, seq, hd = q.shape
    kpos = jnp.arange(seq)

    def block(i):
        qi = lax.dynamic_slice_in_dim(q, i * Q_BLOCK, Q_BLOCK, axis=2)
        z = jnp.einsum('bhqd,bhkd->bhqk', qi, k) * SCALE
        qpos = i * Q_BLOCK + jnp.arange(Q_BLOCK)
        causal = kpos[None, :] < qpos[:, None]
        log_1mb = jnp.where(causal, jax.nn.log_sigmoid(-z), 0.0)
        after = lax.cumsum(log_1mb, axis=3, reverse=True) - log_1mb
        log_a = jnp.where(causal, jax.nn.log_sigmoid(z) + after, NEG)
        return jnp.einsum('bhqk,bhkd->bhqd', jnp.exp(log_a), v)

    out = lax.map(block, jnp.arange(seq // Q_BLOCK))
    return out.transpose(1, 2, 0, 3, 4).reshape(bsz, nh, seq, hd)


def moba_attention(q, k, v):
    bsz, nh, seq, hd = q.shape
    nblk = -(-seq // MOBA_BLOCK)
    sp = nblk * MOBA_BLOCK
    pad = [(0, 0), (0, 0), (0, sp - seq), (0, 0)]
    kb = jnp.pad(k, pad).reshape(bsz, nh, nblk, MOBA_BLOCK, hd)
    vb = jnp.pad(v, pad).reshape(bsz, nh, nblk, MOBA_BLOCK, hd)
    kmean = jnp.mean(kb, axis=3)
    topk = min(MOBA_TOPK, nblk)
    b_idx = jnp.arange(bsz)[:, None, None, None]
    h_idx = jnp.arange(nh)[None, :, None, None]
    blk_ids = jnp.arange(nblk)

    def chunk(i):
        start = i * MOBA_Q_CHUNK
        qi = lax.dynamic_slice_in_dim(q, start, MOBA_Q_CHUNK, axis=2)
        qpos = start + jnp.arange(MOBA_Q_CHUNK)
        qblk = qpos // MOBA_BLOCK
        gate = jnp.einsum('bhqd,bhnd->bhqn', qi, kmean)
        gate = jnp.where(blk_ids[None, :] < qblk[:, None], gate, NEG)
        _, sel = lax.top_k(gate, topk)
        valid = sel < qblk[None, None, :, None]
        ks = kb[b_idx, h_idx, sel]
        vs = vb[b_idx, h_idx, sel]
        s_sel = jnp.einsum('bhqd,bhqnkd->bhqnk', qi, ks) * SCALE
        s_sel = jnp.where(valid[..., None], s_sel, NEG).reshape(bsz, nh, MOBA_Q_CHUNK, topk * MOBA_BLOCK)
        own = start // MOBA_BLOCK
        k_own = lax.dynamic_index_in_dim(kb, own, axis=2, keepdims=False)
        v_own = lax.dynamic_index_in_dim(vb, own, axis=2, keepdims=False)
        s_own = jnp.einsum('bhqd,bhkd->bhqk', qi, k_own) * SCALE
        own_pos = own * MOBA_BLOCK + jnp.arange(MOBA_BLOCK)
        s_own = jnp.where(own_pos[None, :] <= qpos[:, None], s_own, NEG)
        p = jax.nn.softmax(jnp.concatenate([s_sel, s_own], axis=-1), axis=-1)
        p_sel = p[..., :topk * MOBA_BLOCK].reshape(bsz, nh, MOBA_Q_CHUNK, topk, MOBA_BLOCK)
        p_own = p[..., topk * MOBA_BLOCK:]
        return (jnp.einsum('bhqnk,bhqnkd->bhqd', p_sel, vs)
                + jnp.einsum('bhqk,bhkd->bhqd', p_own, v_own))

    out = lax.map(chunk, jnp.arange(seq // MOBA_Q_CHUNK))
    return out.transpose(1, 2, 0, 3, 4).reshape(bsz, nh, seq, hd)


def forgetting_attention(q, k, v, log_f):
    bsz, nh, seq, hd = q.shape
    cum = lax.cumsum(log_f, axis=2)
    kpos = jnp.arange(seq)

    def block(i):
        qi = lax.dynamic_slice_in_dim(q, i * Q_BLOCK, Q_BLOCK, axis=2)
        fi = lax.dynamic_slice_in_dim(cum, i * Q_BLOCK, Q_BLOCK, axis=2)
        s = jnp.einsum('bhqd,bhkd->bhqk', qi, k) * SCALE + fi[..., :, None] - cum[..., None, :]
        qpos = i * Q_BLOCK + jnp.arange(Q_BLOCK)
        s = jnp.where(kpos[None, :] <= qpos[:, None], s, NEG)
        return jnp.einsum('bhqk,bhkd->bhqd', jax.nn.softmax(s, axis=-1), v)

    out = lax.map(block, jnp.arange(seq // Q_BLOCK))
    return out.transpose(1, 2, 0, 3, 4).reshape(bsz, nh, seq, hd)


def setup_inputs(seed: int = 0) -> dict:
    key = jax.random.key(seed)
    ks = jax.random.split(key, 7)
    x = jax.random.normal(ks[0], (BATCH, SEQ, D_MODEL), jnp.float32)
    norm_gain = 1.0 + 0.02 * jax.random.normal(ks[1], (DEPTH, D_MODEL), jnp.float32)
    w_in = jax.random.normal(ks[2], (DEPTH, D_MODEL, IN_COLS), jnp.float32) * D_MODEL ** -0.5
    q_norm_gain = 1.0 + 0.02 * jax.random.normal(ks[3], (DEPTH, N_QK_NORMED, HEAD_DIM), jnp.float32)
    k_norm_gain = 1.0 + 0.02 * jax.random.normal(ks[4], (DEPTH, N_QK_NORMED, HEAD_DIM), jnp.float32)
    forget_bias = jax.random.uniform(ks[5], (DEPTH, N_HEADS_PER_MIXER), jnp.float32, 1.0, 5.0)
    w_out = jax.random.normal(ks[6], (DEPTH, MIX_WIDTH, D_MODEL), jnp.float32) * MIX_WIDTH ** -0.5
    return {"x": x, "norm_gain": norm_gain, "w_in": w_in, "q_norm_gain": q_norm_gain,
            "k_norm_gain": k_norm_gain, "forget_bias": forget_bias, "w_out": w_out}


def reference(x, norm_gain, w_in, q_norm_gain, k_norm_gain, forget_bias, w_out):
    bsz, seq, _ = x.shape
    cos, sin = rope_tables(seq)
    for layer in range(DEPTH):
        h = rms_norm(x, norm_gain[layer])
        proj = jnp.einsum('bsd,dc->bsc', h, w_in[layer].astype(jnp.float32))
        qkv = proj[..., :3 * MIX_WIDTH].reshape(bsz, seq, 3, N_MIXERS, N_HEADS_PER_MIXER, HEAD_DIM)
        qkv = qkv.transpose(2, 3, 0, 4, 1, 5)
        q, k, v = qkv[0], qkv[1], qkv[2]
        gate = proj[..., 3 * MIX_WIDTH:4 * MIX_WIDTH]
        log_f = jax.nn.log_sigmoid(proj[..., 4 * MIX_WIDTH:]
                                   + forget_bias[layer].astype(jnp.float32)).transpose(0, 2, 1)
        qn, kn = q_norm_gain[layer], k_norm_gain[layer]
        o_a = dilated_attention(apply_rope(head_rms_norm(q[0], qn[0]), cos, sin),
                                apply_rope(head_rms_norm(k[0], kn[0]), cos, sin), v[0])
        o_b = stick_breaking_attention(q[1], k[1], v[1])
        o_c = moba_attention(apply_rope(head_rms_norm(q[2], qn[1]), cos, sin),
                             apply_rope(head_rms_norm(k[2], kn[1]), cos, sin), v[2])
        o_d = forgetting_attention(head_rms_norm(q[3], qn[2]), head_rms_norm(k[3], kn[2]), v[3], log_f)
        y = jnp.stack([o_a, o_b, o_c, o_d], axis=1)
        y = y.transpose(0, 3, 1, 2, 4).reshape(bsz, seq, MIX_WIDTH)
        y = y * jax.nn.silu(gate)
        x = x + jnp.einsum('bsc,cd->bsd', y, w_out[layer].astype(jnp.float32)).astype(x.dtype)
    return x
```

```python
import functools

import jax
import jax.numpy as jnp
from jax import lax
from jax.experimental import pallas as pl
from jax.experimental.pallas import tpu as pltpu

F32 = jnp.float32
BF16 = jnp.bfloat16

HEAD_DIM = 64
HALF_DIM = HEAD_DIM // 2
LANES = 128
N_MIXERS = 4
HEADS_PER_MIXER = 8
PAIRS_PER_MIXER = HEADS_PER_MIXER // 2
MIXER_WIDTH = HEADS_PER_MIXER * HEAD_DIM
MIX_WIDTH = N_MIXERS * MIXER_WIDTH
N_SECTIONS = 4
PROJ_COLS = N_SECTIONS * MIX_WIDTH
COL_BLOCKS_PER_SECTION = MIX_WIDTH // LANES
ROPE_THETA = 10000.0
RMS_EPS = 1e-6
SCALE = HEAD_DIM ** -0.5
NEG = -1e30
DILATED_SEGMENTS = ((128, 1), (512, 4), (2048, 16))
BAND = 128
MOBA_BLOCK = 256
MOBA_TOPK = 3
SB_LOG_FLOOR = -110.0
VMEM_LIMIT = 56 * 1024 * 1024

_NT = (((1,), (1,)), ((), ()))


def _dot(a, b):
    return jnp.dot(a, b, preferred_element_type=F32)


def _dot_nt(a, b):
    return lax.dot_general(a, b, _NT, preferred_element_type=F32)


def _split2(x):
    hi = x.astype(BF16)
    lo = (x - hi.astype(F32)).astype(BF16)
    return hi, lo


def _split3(x):
    b1 = x.astype(BF16)
    r1 = x - b1.astype(F32)
    b2 = r1.astype(BF16)
    r2 = r1 - b2.astype(F32)
    return b1, b2, r2.astype(BF16)


def _head_lane_mask(shape, h):
    lane = lax.broadcasted_iota(jnp.int32, shape, len(shape) - 1)
    return (lane >= h * HEAD_DIM) & (lane < (h + 1) * HEAD_DIM)


def _inproj_body(x_ref, g_ref, w_ref, wf_ref, fb_ref, gain_ref, cos_ref, sin_ref, mavg_ref, tri_ref,
                 proj_ref, cum_ref, h_scr, carry_scr, *, blocks_per_batch, tm):
    i = pl.program_id(0)
    j = pl.program_id(1)

    @pl.when(j == 0)
    def _prologue():
        x = x_ref[...]
        ms = jnp.mean(x * x, axis=-1, keepdims=True)
        h = x * lax.rsqrt(ms + RMS_EPS) * g_ref[...]
        h_hi, h_lo = _split2(h)
        h_scr[...] = h_hi
        wf = wf_ref[...]
        t = _dot(h_hi, wf)
        u = _dot(h_lo, wf[:, :LANES])
        logit = t[:, :LANES] + t[:, LANES:] + u + fb_ref[...]
        lf = jnp.minimum(logit, 0.0) - jnp.log1p(jnp.exp(-jnp.abs(logit)))

        @pl.when(i % blocks_per_batch == 0)
        def _():
            carry_scr[...] = jnp.zeros_like(carry_scr)

        carry = carry_scr[...]
        tri = tri_ref[...]
        sub = tri.shape[0]
        for r in range(tm // sub):
            b1, b2, b3 = _split3(lf[r * sub:(r + 1) * sub])
            c = _dot(tri, jnp.concatenate([b1, b2, b3], axis=1))
            c = c[:, :LANES] + c[:, LANES:2 * LANES] + c[:, 2 * LANES:] + carry
            cum_ref[r * sub:(r + 1) * sub, :] = c
            carry = c[sub - 1:sub, :]
        carry_scr[...] = carry

    acc = _dot(h_scr[...], w_ref[...])
    sec = j // N_MIXERS
    mix = j % N_MIXERS
    normed = jnp.logical_and(sec < 2, mix != 1)
    roped = jnp.logical_or(mix == 0, mix == 2)
    scale = jnp.where(sec == 0, SCALE, 1.0).astype(F32)

    @pl.when(jnp.logical_not(normed))
    def _plain():
        proj_ref[...] = (acc * scale).astype(BF16)

    @pl.when(normed)
    def _normed():
        hi, lo = _split2(acc * acc)
        mavg = mavg_ref[...]
        ms = _dot(hi, mavg) + _dot(lo, mavg)
        t = acc * lax.rsqrt(ms + RMS_EPS) * gain_ref[0]

        @pl.when(jnp.logical_not(roped))
        def _():
            proj_ref[...] = (t * scale).astype(BF16)

        @pl.when(roped)
        def _():
            cosv = cos_ref[...]
            sinv = sin_ref[...]
            lane = lax.broadcasted_iota(jnp.int32, (tm, LANES), 1)
            first_half = (lane % HEAD_DIM) < HALF_DIM
            for c in range(MIXER_WIDTH // LANES):
                tc = t[:, c * LANES:(c + 1) * LANES]
                partner = jnp.where(first_half,
                                    pltpu.roll(tc, LANES - HALF_DIM, 1),
                                    pltpu.roll(tc, HALF_DIM, 1))
                proj_ref[:, c * LANES:(c + 1) * LANES] = (
                    (tc * cosv + partner * sinv) * scale).astype(BF16)


def _inproj(x2, g, w_main, wf_cat, fb_pad, gain_tab, cos_t, sin_t, mavg, tri, *, seq, tm):
    rows, d = x2.shape
    blocks_per_batch = seq // tm
    n_col = PROJ_COLS // MIXER_WIDTH
    body = functools.partial(_inproj_body, blocks_per_batch=blocks_per_batch, tm=tm)
    return pl.pallas_call(
        body,
        grid=(rows // tm, n_col),
        in_specs=[
            pl.BlockSpec((tm, d), lambda i, j: (i, 0)),
            pl.BlockSpec((1, d), lambda i, j: (0, 0)),
            pl.BlockSpec((d, MIXER_WIDTH), lambda i, j: (0, j)),
            pl.BlockSpec((d, 2 * LANES), lambda i, j: (0, 0)),
            pl.BlockSpec((1, LANES), lambda i, j: (0, 0)),
            pl.BlockSpec((1, 1, MIXER_WIDTH), lambda i, j: (j, 0, 0)),
            pl.BlockSpec((tm, LANES), lambda i, j: (i % blocks_per_batch, 0)),
            pl.BlockSpec((tm, LANES), lambda i, j: (i % blocks_per_batch, 0)),
            pl.BlockSpec((MIXER_WIDTH, MIXER_WIDTH), lambda i, j: (0, 0)),
            pl.BlockSpec(tri.shape, lambda i, j: (0, 0)),
        ],
        out_specs=[
            pl.BlockSpec((tm, MIXER_WIDTH), lambda i, j: (i, j)),
            pl.BlockSpec((tm, LANES), lambda i, j: (i, 0)),
        ],
        out_shape=[
            jax.ShapeDtypeStruct((rows, PROJ_COLS), BF16),
            jax.ShapeDtypeStruct((rows, LANES), F32),
        ],
        scratch_shapes=[pltpu.VMEM((tm, d), BF16), pltpu.VMEM((1, LANES), F32)],
        compiler_params=pltpu.CompilerParams(
            dimension_semantics=("arbitrary", "arbitrary"), vmem_limit_bytes=VMEM_LIMIT),
        name="inproj",
    )(x2, g, w_main, wf_cat, fb_pad, gain_tab, cos_t, sin_t, mavg, tri)


def _band_body(q_ref, k_ref, v_ref, kp_ref, vp_ref, o_ref, lse_ref, *, tq):
    first = pl.program_id(3) == 0
    q = q_ref[...]
    k = k_ref[...]
    v = v_ref[...]
    row = lax.broadcasted_iota(jnp.int32, (BAND, 2 * BAND), 0)
    col = lax.broadcasted_iota(jnp.int32, (BAND, 2 * BAND), 1)
    in_band = jnp.logical_or(jnp.logical_and(col < BAND, col >= row),
                             jnp.logical_and(col >= BAND, col - BAND <= row))
    lane = lax.broadcasted_iota(jnp.int32, (BAND, LANES), 1)
    head0 = lane < HEAD_DIM
    for c in range(tq // BAND):
        qc = q[c * BAND:(c + 1) * BAND]
        if c == 0:
            kw = jnp.concatenate([kp_ref[...], k[:BAND]], axis=0)
            vw = jnp.concatenate([vp_ref[...], v[:BAND]], axis=0)
            mask = jnp.logical_and(in_band, jnp.logical_or(col >= BAND, jnp.logical_not(first)))
        else:
            kw = k[(c - 1) * BAND:(c + 1) * BAND]
            vw = v[(c - 1) * BAND:(c + 1) * BAND]
            mask = in_band
        outs, lses = [], []
        for h in range(2):
            qh = jnp.where(_head_lane_mask(qc.shape, h), qc, jnp.zeros_like(qc))
            s = jnp.where(mask, _dot_nt(qh, kw), NEG)
            m = jnp.max(s, axis=1, keepdims=True)
            p = jnp.exp(s - m)
            l = jnp.sum(p, axis=1, keepdims=True)
            outs.append(_dot(p.astype(BF16), vw) / l)
            lses.append(m + jnp.log(l))
        o_ref[c * BAND:(c + 1) * BAND, :] = jnp.where(head0, outs[0], outs[1]).astype(BF16)
        lse_ref[c * BAND:(c + 1) * BAND, :] = jnp.where(head0, lses[0], lses[1])


def _band_segment(proj, *, bsz, seq, dil, tq):
    rows = proj.shape[0]
    sub_len = seq // dil
    tq = min(tq, sub_len)
    nq = sub_len // tq
    view = proj.reshape(rows // dil, dil * PROJ_COLS)
    blocks_per_row = PROJ_COLS // LANES
    band_per_tq = tq // BAND

    def col(section, r, p):
        return r * blocks_per_row + section * COL_BLOCKS_PER_SECTION + p

    def cur(section):
        return pl.BlockSpec((tq, LANES), lambda b, r, p, i: (b * nq + i, col(section, r, p)))

    def prev(section):
        return pl.BlockSpec(
            (BAND, LANES),
            lambda b, r, p, i: (jnp.maximum((b * nq + i) * band_per_tq - 1, 0), col(section, r, p)))

    out_spec = pl.BlockSpec((tq, LANES), lambda b, r, p, i: (b * nq + i, r * PAIRS_PER_MIXER + p))
    o, lse = pl.pallas_call(
        functools.partial(_band_body, tq=tq),
        grid=(bsz, dil, PAIRS_PER_MIXER, nq),
        in_specs=[cur(0), cur(1), cur(2), prev(1), prev(2)],
        out_specs=[out_spec, out_spec],
        out_shape=[
            jax.ShapeDtypeStruct((rows // dil, dil * MIXER_WIDTH), BF16),
            jax.ShapeDtypeStruct((rows // dil, dil * MIXER_WIDTH), F32),
        ],
        compiler_params=pltpu.CompilerParams(
            dimension_semantics=("arbitrary",) * 4, vmem_limit_bytes=VMEM_LIMIT),
        name=f"band_d{dil}",
    )(view, view, view, view, view)
    return o.reshape(rows, MIXER_WIDTH), lse.reshape(rows, MIXER_WIDTH)


def _softmax_step(carry, s, v_blk):
    m, l, acc = carry
    m_new = jnp.maximum(m, jnp.max(s, axis=1, keepdims=True))
    alpha = jnp.exp(m - m_new)
    p = jnp.exp(s - m_new)
    l = alpha * l + jnp.sum(p, axis=1, keepdims=True)
    acc = alpha * acc + _dot(p.astype(BF16), v_blk)
    return m_new, l, acc


def _softmax_init(tq):
    return (jnp.full((tq, 1), NEG, F32), jnp.zeros((tq, 1), F32), jnp.zeros((tq, LANES), F32))


def _seq_mixer_specs(*, mixer, nq, tq, seq):
    def colblk(section, p):
        return section * COL_BLOCKS_PER_SECTION + mixer * PAIRS_PER_MIXER + p

    q_spec = pl.BlockSpec((tq, LANES), lambda b, p, i: (b * nq + i, colblk(0, p)))
    k_spec = pl.BlockSpec((seq, LANES), lambda b, p, i: (b, colblk(1, p)))
    v_spec = pl.BlockSpec((seq, LANES), lambda b, p, i: (b, colblk(2, p)))
    o_spec = pl.BlockSpec((tq, LANES), lambda b, p, i: (b * nq + i, p))
    return q_spec, k_spec, v_spec, o_spec


_SEQ_PARAMS = pltpu.CompilerParams(
    dimension_semantics=("arbitrary",) * 3, vmem_limit_bytes=VMEM_LIMIT)


def _fox_body(q_ref, k_ref, v_ref, fcol_ref, frow_ref, o_ref, *, tq):
    p = pl.program_id(1)
    iq = pl.program_id(2)
    q = q_ref[...]
    fc_all = fcol_ref[...]
    lane = lax.broadcasted_iota(jnp.int32, (tq, LANES), 1)
    row = lax.broadcasted_iota(jnp.int32, (tq, tq), 0)
    col = lax.broadcasted_iota(jnp.int32, (tq, tq), 1)
    causal = col <= row
    outs = []
    for h in range(2):
        qh = jnp.where(_head_lane_mask(q.shape, h), q, jnp.zeros_like(q))
        fcol = jnp.sum(jnp.where(lane == 2 * p + h, fc_all, 0.0), axis=1, keepdims=True)

        def scores(j):
            start = pl.multiple_of(j * tq, tq)
            k_blk = k_ref[pl.ds(start, tq), :]
            v_blk = v_ref[pl.ds(start, tq), :]
            frow = frow_ref[0, 0, h, pl.ds(j, 1), :]
            return _dot_nt(qh, k_blk) + (fcol - frow), v_blk

        s, v_blk = scores(iq)
        carry = _softmax_step(_softmax_init(tq), jnp.where(causal, s, NEG), v_blk)

        def past(j, carry):
            s, v_blk = scores(j)
            return _softmax_step(carry, s, v_blk)

        _, l, acc = lax.fori_loop(0, iq, past, carry)
        outs.append(acc / l)
    o_ref[...] = jnp.where(lane < HEAD_DIM, outs[0], outs[1]).astype(BF16)


def _fox(proj, cum, cum_rows, *, bsz, seq, tq):
    rows = proj.shape[0]
    nq = seq // tq
    q_spec, k_spec, v_spec, o_spec = _seq_mixer_specs(mixer=3, nq=nq, tq=tq, seq=seq)
    return pl.pallas_call(
        functools.partial(_fox_body, tq=tq),
        grid=(bsz, PAIRS_PER_MIXER, nq),
        in_specs=[
            q_spec, k_spec, v_spec,
            pl.BlockSpec((tq, LANES), lambda b, p, i: (b * nq + i, 0)),
            pl.BlockSpec((1, 1, 2, nq, tq), lambda b, p, i: (b, p, 0, 0, 0)),
        ],
        out_specs=o_spec,
        out_shape=jax.ShapeDtypeStruct((rows, MIXER_WIDTH), BF16),
        compiler_params=_SEQ_PARAMS,
        name="fox",
    )(proj, proj, proj, cum, cum_rows)


def _moba_body(q_ref, k_ref, v_ref, o_ref, kmean_scr, *, tq, seq):
    iq = pl.program_id(2)
    nblk = seq // tq

    @pl.when(iq == 0)
    def _block_means():
        r = lax.broadcasted_iota(jnp.int32, (LANES, seq), 0)
        c = lax.broadcasted_iota(jnp.int32, (LANES, seq), 1)
        member = jnp.where(c // tq == r, 1.0 / tq, 0.0).astype(BF16)
        kmean = _dot(member, k_ref[...])
        hi, lo = _split2(kmean)
        kmean_scr[0] = hi
        kmean_scr[1] = lo

    q = q_ref[...]
    lane = lax.broadcasted_iota(jnp.int32, (tq, LANES), 1)
    row = lax.broadcasted_iota(jnp.int32, (tq, tq), 0)
    col = lax.broadcasted_iota(jnp.int32, (tq, tq), 1)
    causal = col <= row
    outs = []
    for h in range(2):
        qh = jnp.where(_head_lane_mask(q.shape, h), q, jnp.zeros_like(q))
        gate = _dot_nt(qh, kmean_scr[0]) + _dot_nt(qh, kmean_scr[1])
        gate = jnp.where(lane < iq, gate, NEG)
        chosen = jnp.zeros((tq, LANES), F32)
        work = gate
        for _ in range(min(MOBA_TOPK, nblk)):
            best = jnp.max(work, axis=1, keepdims=True)
            idx = jnp.min(jnp.where(work == best, lane, LANES), axis=1, keepdims=True)
            hit = lane == idx
            chosen = jnp.where(jnp.logical_and(hit, lane < iq), 1.0, chosen)
            work = jnp.where(hit, -jnp.inf, work)

        def block(j):
            start = pl.multiple_of(j * tq, tq)
            return _dot_nt(qh, k_ref[pl.ds(start, tq), :]), v_ref[pl.ds(start, tq), :]

        s, v_blk = block(iq)
        carry = _softmax_step(_softmax_init(tq), jnp.where(causal, s, NEG), v_blk)

        def past(j, carry):
            s, v_blk = block(j)
            picked = jnp.sum(jnp.where(lane == j, chosen, 0.0), axis=1, keepdims=True) > 0.0
            return _softmax_step(carry, jnp.where(picked, s, NEG), v_blk)

        _, l, acc = lax.fori_loop(0, iq, past, carry)
        outs.append(acc / l)
    o_ref[...] = jnp.where(lane < HEAD_DIM, outs[0], outs[1]).astype(BF16)


def _moba(proj, *, bsz, seq):
    rows = proj.shape[0]
    tq = MOBA_BLOCK
    nq = seq // tq
    assert nq <= LANES
    q_spec, k_spec, v_spec, o_spec = _seq_mixer_specs(mixer=2, nq=nq, tq=tq, seq=seq)
    return pl.pallas_call(
        functools.partial(_moba_body, tq=tq, seq=seq),
        grid=(bsz, PAIRS_PER_MIXER, nq),
        in_specs=[q_spec, k_spec, v_spec],
        out_specs=o_spec,
        out_shape=jax.ShapeDtypeStruct((rows, MIXER_WIDTH), BF16),
        scratch_shapes=[pltpu.VMEM((2, LANES, LANES), BF16)],
        compiler_params=_SEQ_PARAMS,
        name="moba",
    )(proj, proj, proj)


def _stick_body(q_ref, k_ref, v_ref, o_ref, *, tq):
    iq = pl.program_id(2)
    q = q_ref[...]
    lane = lax.broadcasted_iota(jnp.int32, (tq, LANES), 1)
    row = lax.broadcasted_iota(jnp.int32, (tq, tq), 0)
    col = lax.broadcasted_iota(jnp.int32, (tq, tq), 1)
    strictly_past = col < row
    later = jnp.where(row > col, 1.0, 0.0).astype(BF16)
    outs = []
    for h in range(2):
        qh = jnp.where(_head_lane_mask(q.shape, h), q, jnp.zeros_like(q))

        def block(j, carry_sum, acc, diagonal):
            start = pl.multiple_of(j * tq, tq)
            z = _dot_nt(qh, k_ref[pl.ds(start, tq), :])
            softplus = jnp.maximum(z, 0.0) + jnp.log1p(jnp.exp(-jnp.abs(z)))
            log_keep = -softplus
            if diagonal:
                log_keep = jnp.where(strictly_past, log_keep, 0.0)
            hi, lo = _split2(log_keep)
            after = _dot(hi, later) + _dot(lo, later) + carry_sum
            log_a = z - softplus + after
            if diagonal:
                log_a = jnp.where(strictly_past, log_a, NEG)
            acc = acc + _dot(jnp.exp(log_a).astype(BF16), v_ref[pl.ds(start, tq), :])
            return carry_sum + jnp.sum(log_keep, axis=1, keepdims=True), acc

        carry_sum, acc = block(iq, jnp.zeros((tq, 1), F32), jnp.zeros((tq, LANES), F32), True)

        def cond(state):
            j, carry_sum, _ = state
            return jnp.logical_and(j >= 0, jnp.max(carry_sum) > SB_LOG_FLOOR)

        def body(state):
            j, carry_sum, acc = state
            carry_sum, acc = block(j, carry_sum, acc, False)
            return j - 1, carry_sum, acc

        _, _, acc = lax.while_loop(cond, body, (iq - 1, carry_sum, acc))
        outs.append(acc)
    o_ref[...] = jnp.where(lane < HEAD_DIM, outs[0], outs[1]).astype(BF16)


def _stick(proj, *, bsz, seq, tq):
    rows = proj.shape[0]
    nq = seq // tq
    q_spec, k_spec, v_spec, o_spec = _seq_mixer_specs(mixer=1, nq=nq, tq=tq, seq=seq)
    return pl.pallas_call(
        functools.partial(_stick_body, tq=tq),
        grid=(bsz, PAIRS_PER_MIXER, nq),
        in_specs=[q_spec, k_spec, v_spec],
        out_specs=o_spec,
        out_shape=jax.ShapeDtypeStruct((rows, MIXER_WIDTH), BF16),
        compiler_params=_SEQ_PARAMS,
        name="stick",
    )(proj, proj, proj)


def _outproj_body(o1_ref, o4_ref, o16_ref, l1_ref, l4_ref, l16_ref, yb_ref, yc_ref, yd_ref,
                  gate_ref, x_ref, w_ref, out_ref, y_scr):
    j = pl.program_id(1)

    @pl.when(j == 0)
    def _gate():
        l1, l4, l16 = l1_ref[...], l4_ref[...], l16_ref[...]
        m = jnp.maximum(jnp.maximum(l1, l4), l16)
        e1, e4, e16 = jnp.exp(l1 - m), jnp.exp(l4 - m), jnp.exp(l16 - m)
        ya = (e1 * o1_ref[...].astype(F32) + e4 * o4_ref[...].astype(F32)
              + e16 * o16_ref[...].astype(F32)) / (e1 + e4 + e16)
        parts = (ya, yb_ref[...].astype(F32), yc_ref[...].astype(F32), yd_ref[...].astype(F32))
        for mxr, y in enumerate(parts):
            g = gate_ref[:, mxr * MIXER_WIDTH:(mxr + 1) * MIXER_WIDTH].astype(F32)
            silu = g / (1.0 + jnp.exp(-g))
            y_scr[:, mxr * MIXER_WIDTH:(mxr + 1) * MIXER_WIDTH] = (y * silu).astype(BF16)

    out_ref[...] = x_ref[...] + _dot(y_scr[...], w_ref[...])


def _outproj(seg_o, seg_lse, yb, yc, yd, proj, x2, w_out, *, tm, tn):
    rows, d = x2.shape
    row_blk = lambda i, j: (i, 0)
    mix_spec = pl.BlockSpec((tm, MIXER_WIDTH), row_blk)
    return pl.pallas_call(
        _outproj_body,
        grid=(rows // tm, d // tn),
        in_specs=[mix_spec] * 9 + [
            pl.BlockSpec((tm, MIX_WIDTH), lambda i, j: (i, N_SECTIONS - 1)),
            pl.BlockSpec((tm, tn), lambda i, j: (i, j)),
            pl.BlockSpec((MIX_WIDTH, tn), lambda i, j: (0, j)),
        ],
        out_specs=pl.BlockSpec((tm, tn), lambda i, j: (i, j)),
        out_shape=jax.ShapeDtypeStruct((rows, d), F32),
        scratch_shapes=[pltpu.VMEM((tm, MIX_WIDTH), BF16)],
        compiler_params=pltpu.CompilerParams(
            dimension_semantics=("arbitrary", "arbitrary"), vmem_limit_bytes=VMEM_LIMIT),
        name="outproj",
    )(*seg_o, *seg_lse, yb, yc, yd, proj, x2, w_out)


def _rope_tables(seq):
    inv = 1.0 / (ROPE_THETA ** (jnp.arange(0, HEAD_DIM, 2, dtype=F32) / HEAD_DIM))
    ang = jnp.arange(seq, dtype=F32)[:, None] * inv[None, :]
    cos, sin = jnp.cos(ang), jnp.sin(ang)
    reps = LANES // HEAD_DIM
    cos_t = jnp.tile(jnp.concatenate([cos, cos], axis=1), (1, reps))
    sin_t = jnp.tile(jnp.concatenate([-sin, sin], axis=1), (1, reps))
    return cos_t, sin_t


def _gain_table(qn, kn):
    ones = jnp.ones((HEAD_DIM,), F32)
    per_mixer_q = (qn[0], ones, qn[1], qn[2])
    per_mixer_k = (kn[0], ones, kn[1], kn[2])
    blocks = list(per_mixer_q) + list(per_mixer_k) + [ones] * (2 * N_MIXERS)
    tab = jnp.stack([jnp.tile(g.astype(F32), HEADS_PER_MIXER) for g in blocks])
    return tab[:, None, :]


def kernel(x, norm_gain, w_in, q_norm_gain, k_norm_gain, forget_bias, w_out):
    bsz, seq, d = x.shape
    depth = w_in.shape[0]
    rows = bsz * seq
    tq = MOBA_BLOCK
    nq = seq // tq
    tm_in = min(512, seq)
    cos_t, sin_t = _rope_tables(seq)
    head_of_lane = jnp.arange(MIXER_WIDTH) // HEAD_DIM
    mavg = jnp.where(head_of_lane[:, None] == head_of_lane[None, :], 1.0 / HEAD_DIM, 0.0).astype(BF16)
    tri = jnp.tril(jnp.ones((256, 256), F32)).astype(BF16)

    x2 = x.reshape(rows, d)
    for layer in range(depth):
        w_l = w_in[layer]
        w_main = w_l[:, :PROJ_COLS].astype(BF16)
        wf = jnp.pad(w_l[:, PROJ_COLS:], ((0, 0), (0, LANES - HEADS_PER_MIXER)))
        wf_hi = wf.astype(BF16)
        wf_lo = (wf - wf_hi.astype(F32)).astype(BF16)
        wf_cat = jnp.concatenate([wf_hi, wf_lo], axis=1)
        fb_pad = jnp.pad(forget_bias[layer].astype(F32), (0, LANES - HEADS_PER_MIXER))[None, :]
        gain_tab = _gain_table(q_norm_gain[layer], k_norm_gain[layer])

        proj, cum = _inproj(x2, norm_gain[layer][None, :].astype(F32), w_main, wf_cat, fb_pad,
                            gain_tab, cos_t, sin_t, mavg, tri, seq=seq, tm=tm_in)
        cum_rows = cum.reshape(bsz, seq, LANES)[:, :, :HEADS_PER_MIXER].transpose(0, 2, 1)
        cum_rows = cum_rows.reshape(bsz, PAIRS_PER_MIXER, 2, nq, tq)

        seg = [_band_segment(proj, bsz=bsz, seq=seq, dil=dil, tq=512) for _, dil in DILATED_SEGMENTS]
        yb = _stick(proj, bsz=bsz, seq=seq, tq=tq)
        yc = _moba(proj, bsz=bsz, seq=seq)
        yd = _fox(proj, cum, cum_rows, bsz=bsz, seq=seq, tq=tq)
        x2 = _outproj([s[0] for s in seg], [s[1] for s in seg], yb, yc, yd, proj, x2,
                      w_out[layer].astype(BF16), tm=512, tn=min(1024, d))
    return x2.reshape(bsz, seq, d)
```

```python
import functools

import jax
import jax.numpy as jnp
from jax import lax
from jax.experimental import pallas as pl
from jax.experimental.pallas import tpu as pltpu

F32 = jnp.float32
BF16 = jnp.bfloat16

HEAD_DIM = 64
HALF_DIM = HEAD_DIM // 2
LANES = 128
N_MIXERS = 4
HEADS_PER_MIXER = 8
PAIRS_PER_MIXER = HEADS_PER_MIXER // 2
MIXER_WIDTH = HEADS_PER_MIXER * HEAD_DIM
LANE_BLOCKS_PER_MIXER = MIXER_WIDTH // LANES
MIX_WIDTH = N_MIXERS * MIXER_WIDTH
N_SECTIONS = 4
PROJ_COLS = N_SECTIONS * MIX_WIDTH
COL_BLOCKS_PER_SECTION = MIX_WIDTH // LANES
ROPE_THETA = 10000.0
RMS_EPS = 1e-6
SCALE = HEAD_DIM ** -0.5
LOG2E = 1.4426950408889634
NEG = -1e30
DILATIONS = (1, 4, 16)
STRIDED_DILATIONS = DILATIONS[1:]
BAND = 128
MOBA_BLOCK = 256
MOBA_TOPK = 3
SEQ_TILE = 2 * MOBA_BLOCK
STICK_TILE = 256
SB_LOG_FLOOR = -110.0
VMEM_LIMIT = 56 * 1024 * 1024

_NT = (((1,), (1,)), ((), ()))


def _dot(a, b):
    return jnp.dot(a, b, preferred_element_type=F32)


def _dot_nt(a, b):
    return lax.dot_general(a, b, _NT, preferred_element_type=F32)


def _split2(x):
    hi = x.astype(BF16)
    lo = (x - hi.astype(F32)).astype(BF16)
    return hi, lo


def _split3(x):
    b1 = x.astype(BF16)
    r1 = x - b1.astype(F32)
    b2 = r1.astype(BF16)
    r2 = r1 - b2.astype(F32)
    return b1, b2, r2.astype(BF16)


def _head_lane_mask(shape, h):
    lane = lax.broadcasted_iota(jnp.int32, shape, len(shape) - 1)
    return (lane >= h * HEAD_DIM) & (lane < (h + 1) * HEAD_DIM)


def _own_head(x, h):
    return jnp.where(_head_lane_mask(x.shape, h), x, jnp.zeros_like(x))


def _lane_column(x, n):
    lane = lax.broadcasted_iota(jnp.int32, x.shape, 1)
    return jnp.sum(jnp.where(lane == n, x, 0.0), axis=1, keepdims=True)


def _inproj_body(x_ref, g_ref, w_ref, wf_ref, fb_ref, gain_ref, cos_ref, sin_ref, mavg_ref, tri_ref,
                 proj_ref, cum_ref, *rest, blocks_per_batch, tm):
    dil_refs = rest[:3 * len(STRIDED_DILATIONS)]
    h_scr, carry_scr, dil_scr = rest[3 * len(STRIDED_DILATIONS):]
    i = pl.program_id(0)
    j = pl.program_id(1)

    @pl.when(j == 0)
    def _prologue():
        x = x_ref[...]
        ms = jnp.mean(x * x, axis=-1, keepdims=True)
        h = x * lax.rsqrt(ms + RMS_EPS) * g_ref[...]
        h_hi, h_lo = _split2(h)
        h_scr[...] = h_hi
        wf = wf_ref[...]
        t = _dot(h_hi, wf)
        u = _dot(h_lo, wf[:, :LANES])
        logit = t[:, :LANES] + t[:, LANES:] + u + fb_ref[...]
        lf = jnp.minimum(logit, 0.0) - jnp.log1p(jnp.exp(-jnp.abs(logit)))

        @pl.when(i % blocks_per_batch == 0)
        def _():
            carry_scr[...] = jnp.zeros_like(carry_scr)

        carry = carry_scr[...]
        tri = tri_ref[...]
        sub = tri.shape[0]
        for r in range(tm // sub):
            b1, b2, b3 = _split3(lf[r * sub:(r + 1) * sub])
            c = _dot(tri, jnp.concatenate([b1, b2, b3], axis=1))
            c = c[:, :LANES] + c[:, LANES:2 * LANES] + c[:, 2 * LANES:] + carry
            cum_ref[r * sub:(r + 1) * sub, :] = c
            carry = c[sub - 1:sub, :]
        carry_scr[...] = carry

    acc = _dot(h_scr[...], w_ref[...])
    sec = j // N_MIXERS
    mix = j % N_MIXERS
    normed = jnp.logical_and(sec < 2, mix != 1)
    roped = jnp.logical_or(mix == 0, mix == 2)
    q_scale = jnp.where(mix == 1, SCALE, SCALE * LOG2E)
    scale = jnp.where(sec == 0, q_scale, 1.0).astype(F32)

    @pl.when(jnp.logical_not(normed))
    def _plain():
        proj_ref[...] = (acc * scale).astype(BF16)

    @pl.when(normed)
    def _normed():
        hi, lo = _split2(acc * acc)
        mavg = mavg_ref[...]
        ms = _dot(hi, mavg) + _dot(lo, mavg)
        t = acc * lax.rsqrt(ms + RMS_EPS) * gain_ref[0]

        @pl.when(jnp.logical_not(roped))
        def _():
            proj_ref[...] = (t * scale).astype(BF16)

        @pl.when(roped)
        def _():
            cosv = cos_ref[...]
            sinv = sin_ref[...]
            lane = lax.broadcasted_iota(jnp.int32, (tm, LANES), 1)
            first_half = (lane % HEAD_DIM) < HALF_DIM
            for c in range(MIXER_WIDTH // LANES):
                tc = t[:, c * LANES:(c + 1) * LANES]
                partner = jnp.where(first_half,
                                    pltpu.roll(tc, LANES - HALF_DIM, 1),
                                    pltpu.roll(tc, HALF_DIM, 1))
                proj_ref[:, c * LANES:(c + 1) * LANES] = (
                    (tc * cosv + partner * sinv) * scale).astype(BF16)

    @pl.when(jnp.logical_and(mix == 0, sec < 3))
    def _dilated():
        for c in range(LANE_BLOCKS_PER_MIXER):
            dil_scr[c] = proj_ref[:, c * LANES:(c + 1) * LANES].astype(F32)
        for section in range(3):
            @pl.when(sec == section)
            def _():
                for di, dil in enumerate(STRIDED_DILATIONS):
                    ref = dil_refs[3 * di + section]
                    n = tm // dil
                    for r in range(dil):
                        for c in range(LANE_BLOCKS_PER_MIXER):
                            lo = r * MIXER_WIDTH + c * LANES
                            ref[:, lo:lo + LANES] = dil_scr[c, pl.ds(r, n, stride=dil), :].astype(BF16)


def _inproj(x2, g, w_main, wf_cat, fb_pad, gain_tab, cos_t, sin_t, mavg, tri, *, seq, tm):
    rows, d = x2.shape
    blocks_per_batch = seq // tm
    n_col = PROJ_COLS // MIXER_WIDTH
    body = functools.partial(_inproj_body, blocks_per_batch=blocks_per_batch, tm=tm)
    dil_specs, dil_shapes = [], []
    for dil in STRIDED_DILATIONS:
        for _ in range(3):
            dil_specs.append(pl.BlockSpec((tm // dil, dil * MIXER_WIDTH), lambda i, j: (i, 0)))
            dil_shapes.append(jax.ShapeDtypeStruct((rows // dil, dil * MIXER_WIDTH), BF16))
    return pl.pallas_call(
        body,
        grid=(rows // tm, n_col),
        in_specs=[
            pl.BlockSpec((tm, d), lambda i, j: (i, 0)),
            pl.BlockSpec((1, d), lambda i, j: (0, 0)),
            pl.BlockSpec((d, MIXER_WIDTH), lambda i, j: (0, j)),
            pl.BlockSpec((d, 2 * LANES), lambda i, j: (0, 0)),
            pl.BlockSpec((1, LANES), lambda i, j: (0, 0)),
            pl.BlockSpec((1, 1, MIXER_WIDTH), lambda i, j: (j, 0, 0)),
            pl.BlockSpec((tm, LANES), lambda i, j: (i % blocks_per_batch, 0)),
            pl.BlockSpec((tm, LANES), lambda i, j: (i % blocks_per_batch, 0)),
            pl.BlockSpec((MIXER_WIDTH, MIXER_WIDTH), lambda i, j: (0, 0)),
            pl.BlockSpec(tri.shape, lambda i, j: (0, 0)),
        ],
        out_specs=[
            pl.BlockSpec((tm, MIXER_WIDTH), lambda i, j: (i, j)),
            pl.BlockSpec((tm, LANES), lambda i, j: (i, 0)),
        ] + dil_specs,
        out_shape=[
            jax.ShapeDtypeStruct((rows, PROJ_COLS), BF16),
            jax.ShapeDtypeStruct((rows, LANES), F32),
        ] + dil_shapes,
        scratch_shapes=[pltpu.VMEM((tm, d), BF16), pltpu.VMEM((1, LANES), F32),
                        pltpu.VMEM((LANE_BLOCKS_PER_MIXER, tm, LANES), F32)],
        compiler_params=pltpu.CompilerParams(
            dimension_semantics=("arbitrary", "arbitrary"), vmem_limit_bytes=VMEM_LIMIT),
        name="inproj",
    )(x2, g, w_main, wf_cat, fb_pad, gain_tab, cos_t, sin_t, mavg, tri)


def _band_body(q_ref, k_ref, v_ref, kp_ref, vp_ref, o_ref, lse_ref, *, tq):
    first = pl.program_id(3) == 0
    q = q_ref[...]
    k = k_ref[...]
    v = v_ref[...]
    row = lax.broadcasted_iota(jnp.int32, (BAND, 2 * BAND), 0)
    col = lax.broadcasted_iota(jnp.int32, (BAND, 2 * BAND), 1)
    in_band = jnp.logical_or(jnp.logical_and(col < BAND, col >= row),
                             jnp.logical_and(col >= BAND, col - BAND <= row))
    lane = lax.broadcasted_iota(jnp.int32, (BAND, LANES), 1)
    head0 = lane < HEAD_DIM
    for c in range(tq // BAND):
        qc = q[c * BAND:(c + 1) * BAND]
        if c == 0:
            kw = jnp.concatenate([kp_ref[...], k[:BAND]], axis=0)
            vw = jnp.concatenate([vp_ref[...], v[:BAND]], axis=0)
            mask = jnp.logical_and(in_band, jnp.logical_or(col >= BAND, jnp.logical_not(first)))
        else:
            kw = k[(c - 1) * BAND:(c + 1) * BAND]
            vw = v[(c - 1) * BAND:(c + 1) * BAND]
            mask = in_band
        outs, lses = [], []
        for h in range(2):
            s = jnp.where(mask, _dot_nt(_own_head(qc, h), kw), NEG)
            m = jnp.max(s, axis=1, keepdims=True)
            p = jnp.exp2(s - m)
            l = jnp.sum(p, axis=1, keepdims=True)
            outs.append(_dot(p.astype(BF16), vw) / l)
            lses.append(m + jnp.log2(l))
        o_ref[c * BAND:(c + 1) * BAND, :] = jnp.where(head0, outs[0], outs[1]).astype(BF16)
        lse_ref[c * BAND:(c + 1) * BAND, :] = jnp.where(head0, lses[0], lses[1])


def _band_segment(q_arr, k_arr, v_arr, col_of, *, bsz, seq, dil, tq):
    sub_rows = q_arr.shape[0]
    sub_len = seq // dil
    tq = min(tq, sub_len)
    nq = sub_len // tq
    band_per_tq = tq // BAND

    def cur(section):
        return pl.BlockSpec((tq, LANES), lambda b, r, p, i: (b * nq + i, col_of(section, r, p)))

    def prev(section):
        return pl.BlockSpec(
            (BAND, LANES),
            lambda b, r, p, i: (jnp.maximum((b * nq + i) * band_per_tq - 1, 0), col_of(section, r, p)))

    out_spec = pl.BlockSpec((tq, LANES), lambda b, r, p, i: (b * nq + i, r * PAIRS_PER_MIXER + p))
    return pl.pallas_call(
        functools.partial(_band_body, tq=tq),
        grid=(bsz, dil, PAIRS_PER_MIXER, nq),
        in_specs=[cur(0), cur(1), cur(2), prev(1), prev(2)],
        out_specs=[out_spec, out_spec],
        out_shape=[
            jax.ShapeDtypeStruct((sub_rows, dil * MIXER_WIDTH), BF16),
            jax.ShapeDtypeStruct((sub_rows, dil * MIXER_WIDTH), F32),
        ],
        compiler_params=pltpu.CompilerParams(
            dimension_semantics=("arbitrary",) * 4, vmem_limit_bytes=VMEM_LIMIT),
        name=f"band_d{dil}",
    )(q_arr, k_arr, v_arr, k_arr, v_arr)


def _softmax_step(carry, s, v_blk):
    m, l, acc = carry
    m_new = jnp.maximum(m, jnp.max(s, axis=1, keepdims=True))
    alpha = jnp.exp2(m - m_new)
    p = jnp.exp2(s - m_new)
    l = alpha * l + jnp.sum(p, axis=1, keepdims=True)
    acc = alpha * acc + _dot(p.astype(BF16), v_blk)
    return m_new, l, acc


def _softmax_init(tq):
    return (jnp.full((tq, 1), NEG, F32), jnp.zeros((tq, 1), F32), jnp.zeros((tq, LANES), F32))


def _merge_heads(outs):
    lane = lax.broadcasted_iota(jnp.int32, outs[0].shape, 1)
    return jnp.where(lane < HEAD_DIM, outs[0], outs[1])


def _seq_mixer_specs(*, mixer, nq, tq, seq):
    def colblk(section, p):
        return section * COL_BLOCKS_PER_SECTION + mixer * PAIRS_PER_MIXER + p

    q_spec = pl.BlockSpec((tq, LANES), lambda b, p, i: (b * nq + i, colblk(0, p)))
    k_spec = pl.BlockSpec((seq, LANES), lambda b, p, i: (b, colblk(1, p)))
    v_spec = pl.BlockSpec((seq, LANES), lambda b, p, i: (b, colblk(2, p)))
    o_spec = pl.BlockSpec((tq, LANES), lambda b, p, i: (b * nq + i, p))
    return q_spec, k_spec, v_spec, o_spec


_SEQ_PARAMS = pltpu.CompilerParams(
    dimension_semantics=("arbitrary",) * 3, vmem_limit_bytes=VMEM_LIMIT)


def _fox_augment(x, cum, head, h, key_side):
    lane = lax.broadcasted_iota(jnp.int32, x.shape, 1)
    g = jnp.broadcast_to(_lane_column(cum, head) * LOG2E, x.shape)
    g1, g2, g3 = (piece.astype(F32) for piece in _split3(g))
    one = jnp.ones_like(g1)
    pieces = (one, one, one, -g1, -g2, -g3) if key_side else (g1, g2, g3, one, one, one)
    base = HEAD_DIM * (1 - h)
    aug = jnp.zeros_like(g1)
    for n, piece in enumerate(pieces):
        aug = jnp.where(lane == base + n, piece, aug)
    return jnp.where(_head_lane_mask(x.shape, h), x.astype(F32), aug).astype(BF16)


def _fox_body(q_ref, k_ref, v_ref, cumq_ref, cumk_ref, o_ref, kaug_scr, *, t, seq):
    p = pl.program_id(1)
    iq = pl.program_id(2)

    @pl.when(iq == 0)
    def _augment_keys():
        def chunk(c, _):
            rows = pl.ds(pl.multiple_of(c * t, t), t)
            for h in range(2):
                kaug_scr[h, rows, :] = _fox_augment(k_ref[rows, :], cumk_ref[rows, :], 2 * p + h, h, True)
            return 0
        lax.fori_loop(0, seq // t, chunk, 0)

    q = q_ref[...]
    cumq = cumq_ref[...]
    q_aug = [_fox_augment(q, cumq, 2 * p + h, h, False) for h in range(2)]
    row = lax.broadcasted_iota(jnp.int32, (t, t), 0)
    col = lax.broadcasted_iota(jnp.int32, (t, t), 1)
    causal = col <= row

    def tile(j, state, diagonal):
        rows = pl.ds(pl.multiple_of(j * t, t), t)
        v_blk = v_ref[rows, :]
        new = []
        for h in range(2):
            s = _dot_nt(q_aug[h], kaug_scr[h, rows, :])
            if diagonal:
                s = jnp.where(causal, s, NEG)
            new.append(_softmax_step(state[h], s, v_blk))
        return tuple(new)

    state = tile(iq, (_softmax_init(t), _softmax_init(t)), True)
    state = lax.fori_loop(0, iq, lambda j, st: tile(j, st, False), state)
    o_ref[...] = _merge_heads([acc / l for _, l, acc in state]).astype(BF16)


def _fox(proj, cum, *, bsz, seq, t):
    rows = proj.shape[0]
    nq = seq // t
    q_spec, k_spec, v_spec, o_spec = _seq_mixer_specs(mixer=3, nq=nq, tq=t, seq=seq)
    return pl.pallas_call(
        functools.partial(_fox_body, t=t, seq=seq),
        grid=(bsz, PAIRS_PER_MIXER, nq),
        in_specs=[
            q_spec, k_spec, v_spec,
            pl.BlockSpec((t, LANES), lambda b, p, i: (b * nq + i, 0)),
            pl.BlockSpec((seq, LANES), lambda b, p, i: (b, 0)),
        ],
        out_specs=o_spec,
        out_shape=jax.ShapeDtypeStruct((rows, MIXER_WIDTH), BF16),
        scratch_shapes=[pltpu.VMEM((2, seq, LANES), BF16)],
        compiler_params=_SEQ_PARAMS,
        name="fox",
    )(proj, proj, proj, cum, cum)


def _moba_body(q_ref, k_ref, v_ref, o_ref, kmean_scr, *, t, seq):
    iq = pl.program_id(2)
    nblk = seq // MOBA_BLOCK
    first_blk = 2 * iq

    @pl.when(iq == 0)
    def _block_means():
        r = lax.broadcasted_iota(jnp.int32, (LANES, seq), 0)
        c = lax.broadcasted_iota(jnp.int32, (LANES, seq), 1)
        member = jnp.where(c // MOBA_BLOCK == r, 1.0 / MOBA_BLOCK, 0.0).astype(BF16)
        hi, lo = _split2(_dot(member, k_ref[...]))
        kmean_scr[0] = hi
        kmean_scr[1] = lo

    q = q_ref[...]
    lane = lax.broadcasted_iota(jnp.int32, (t, LANES), 1)
    qblk = first_blk + lax.broadcasted_iota(jnp.int32, (t, LANES), 0) // MOBA_BLOCK
    in_second = lax.broadcasted_iota(jnp.int32, (t, 1), 0) >= MOBA_BLOCK
    row = lax.broadcasted_iota(jnp.int32, (t, MOBA_BLOCK), 0)
    col = lax.broadcasted_iota(jnp.int32, (t, MOBA_BLOCK), 1)
    causal0 = col <= row
    causal1 = col + MOBA_BLOCK <= row
    q_own, chosen = [], []
    for h in range(2):
        qh = _own_head(q, h)
        gate = _dot_nt(qh, kmean_scr[0]) + _dot_nt(qh, kmean_scr[1])
        past = lane < qblk
        work = jnp.where(past, gate, NEG)
        picks = jnp.zeros((t, LANES), F32)
        for _ in range(min(MOBA_TOPK, nblk)):
            best = jnp.max(work, axis=1, keepdims=True)
            idx = jnp.min(jnp.where(work == best, lane, LANES), axis=1, keepdims=True)
            hit = lane == idx
            picks = jnp.where(hit, jnp.where(past, 1.0, 0.0), picks)
            work = jnp.where(hit, -jnp.inf, work)
        q_own.append(qh)
        chosen.append(picks)

    def halves(j, h):
        rows = pl.ds(pl.multiple_of(j * t, t), t)
        s = _dot_nt(q_own[h], k_ref[rows, :])
        pick0 = _lane_column(chosen[h], 2 * j) > 0.0
        pick1 = _lane_column(chosen[h], 2 * j + 1) > 0.0
        return s[:, :MOBA_BLOCK], s[:, MOBA_BLOCK:], pick0, pick1

    def tile(j, state, diagonal):
        v_blk = v_ref[pl.ds(pl.multiple_of(j * t, t), t), :]
        new = []
        for h in range(2):
            s0, s1, pick0, pick1 = halves(j, h)
            if diagonal:
                allow0 = jnp.logical_or(jnp.logical_not(in_second), pick0)
                s0 = jnp.where(causal0, jnp.where(allow0, s0, NEG), NEG)
                s1 = jnp.where(causal1, s1, NEG)
            else:
                s0 = jnp.where(pick0, s0, NEG)
                s1 = jnp.where(pick1, s1, NEG)
            new.append(_softmax_step(state[h], jnp.concatenate([s0, s1], axis=1), v_blk))
        return tuple(new)

    state = tile(iq, (_softmax_init(t), _softmax_init(t)), True)
    state = lax.fori_loop(0, iq, lambda j, st: tile(j, st, False), state)
    o_ref[...] = _merge_heads([acc / l for _, l, acc in state]).astype(BF16)


def _moba(proj, *, bsz, seq, t):
    assert t == 2 * MOBA_BLOCK and seq // MOBA_BLOCK <= LANES
    rows = proj.shape[0]
    nq = seq // t
    q_spec, k_spec, v_spec, o_spec = _seq_mixer_specs(mixer=2, nq=nq, tq=t, seq=seq)
    return pl.pallas_call(
        functools.partial(_moba_body, t=t, seq=seq),
        grid=(bsz, PAIRS_PER_MIXER, nq),
        in_specs=[q_spec, k_spec, v_spec],
        out_specs=o_spec,
        out_shape=jax.ShapeDtypeStruct((rows, MIXER_WIDTH), BF16),
        scratch_shapes=[pltpu.VMEM((2, LANES, LANES), BF16)],
        compiler_params=_SEQ_PARAMS,
        name="moba",
    )(proj, proj, proj)


def _stick_body(q_ref, k_ref, v_ref, o_ref, *, tq):
    iq = pl.program_id(2)
    q = q_ref[...]
    q_own = [_own_head(q, h) for h in range(2)]
    row = lax.broadcasted_iota(jnp.int32, (tq, tq), 0)
    col = lax.broadcasted_iota(jnp.int32, (tq, tq), 1)
    strictly_past = col < row
    later = jnp.where(row > col, 1.0, 0.0).astype(BF16)

    def block(j, state, diagonal):
        rows = pl.ds(pl.multiple_of(j * tq, tq), tq)
        k_blk = k_ref[rows, :]
        v_blk = v_ref[rows, :]
        new = []
        for h in range(2):
            carry_sum, acc = state[h]
            z = _dot_nt(q_own[h], k_blk)
            softplus = jnp.maximum(z, 0.0) + jnp.log1p(jnp.exp(-jnp.abs(z)))
            log_keep = -softplus
            if diagonal:
                log_keep = jnp.where(strictly_past, log_keep, 0.0)
            hi, lo = _split2(log_keep)
            after = _dot(hi, later) + _dot(lo, later) + carry_sum
            log_a = z - softplus + after
            if diagonal:
                log_a = jnp.where(strictly_past, log_a, NEG)
            acc = acc + _dot(jnp.exp(log_a).astype(BF16), v_blk)
            new.append((carry_sum + jnp.sum(log_keep, axis=1, keepdims=True), acc))
        return tuple(new)

    zero = (jnp.zeros((tq, 1), F32), jnp.zeros((tq, LANES), F32))
    state = block(iq, (zero, zero), True)

    def cond(loop):
        j, state = loop
        alive = jnp.max(jnp.maximum(state[0][0], state[1][0])) > SB_LOG_FLOOR
        return jnp.logical_and(j >= 0, alive)

    def body(loop):
        j, state = loop
        return j - 1, block(j, state, False)

    _, state = lax.while_loop(cond, body, (iq - 1, state))
    o_ref[...] = _merge_heads([acc for _, acc in state]).astype(BF16)


def _stick(proj, *, bsz, seq, tq):
    rows = proj.shape[0]
    nq = seq // tq
    q_spec, k_spec, v_spec, o_spec = _seq_mixer_specs(mixer=1, nq=nq, tq=tq, seq=seq)
    return pl.pallas_call(
        functools.partial(_stick_body, tq=tq),
        grid=(bsz, PAIRS_PER_MIXER, nq),
        in_specs=[q_spec, k_spec, v_spec],
        out_specs=o_spec,
        out_shape=jax.ShapeDtypeStruct((rows, MIXER_WIDTH), BF16),
        compiler_params=_SEQ_PARAMS,
        name="stick",
    )(proj, proj, proj)


def _outproj_body(o1_ref, o4_ref, o16_ref, l1_ref, l4_ref, l16_ref, yb_ref, yc_ref, yd_ref,
                  gate_ref, x_ref, w_ref, out_ref, y_scr, o_scr, l_scr, *, tm):
    j = pl.program_id(1)

    @pl.when(j == 0)
    def _gate():
        for slot, (dil, o_ref, l_ref) in enumerate(zip(STRIDED_DILATIONS, (o4_ref, o16_ref), (l4_ref, l16_ref))):
            n = tm // dil
            for r in range(dil):
                for c in range(LANE_BLOCKS_PER_MIXER):
                    cols = slice(r * MIXER_WIDTH + c * LANES, r * MIXER_WIDTH + (c + 1) * LANES)
                    o_scr[slot, c, pl.ds(r, n, stride=dil), :] = o_ref[:, cols].astype(F32)
                    l_scr[slot, c, pl.ds(r, n, stride=dil), :] = l_ref[:, cols]

        def natural(scr, slot):
            return jnp.concatenate([scr[slot, c] for c in range(LANE_BLOCKS_PER_MIXER)], axis=1)

        l1, l4, l16 = l1_ref[...], natural(l_scr, 0), natural(l_scr, 1)
        m = jnp.maximum(jnp.maximum(l1, l4), l16)
        e1, e4, e16 = jnp.exp2(l1 - m), jnp.exp2(l4 - m), jnp.exp2(l16 - m)
        ya = (e1 * o1_ref[...].astype(F32) + e4 * natural(o_scr, 0)
              + e16 * natural(o_scr, 1)) / (e1 + e4 + e16)
        parts = (ya, yb_ref[...].astype(F32), yc_ref[...].astype(F32), yd_ref[...].astype(F32))
        for mxr, y in enumerate(parts):
            g = gate_ref[:, mxr * MIXER_WIDTH:(mxr + 1) * MIXER_WIDTH].astype(F32)
            silu = g / (1.0 + jnp.exp(-g))
            y_scr[:, mxr * MIXER_WIDTH:(mxr + 1) * MIXER_WIDTH] = (y * silu).astype(BF16)

    out_ref[...] = x_ref[...] + _dot(y_scr[...], w_ref[...])


def _outproj(seg_o, seg_lse, yb, yc, yd, proj, x2, w_out, *, tm, tn):
    rows, d = x2.shape
    row_blk = lambda i, j: (i, 0)
    mix_spec = pl.BlockSpec((tm, MIXER_WIDTH), row_blk)
    seg_specs = [pl.BlockSpec((tm // dil, dil * MIXER_WIDTH), row_blk) for dil in DILATIONS]
    return pl.pallas_call(
        functools.partial(_outproj_body, tm=tm),
        grid=(rows // tm, d // tn),
        in_specs=seg_specs + seg_specs + [mix_spec] * 3 + [
            pl.BlockSpec((tm, MIX_WIDTH), lambda i, j: (i, N_SECTIONS - 1)),
            pl.BlockSpec((tm, tn), lambda i, j: (i, j)),
            pl.BlockSpec((MIX_WIDTH, tn), lambda i, j: (0, j)),
        ],
        out_specs=pl.BlockSpec((tm, tn), lambda i, j: (i, j)),
        out_shape=jax.ShapeDtypeStruct((rows, d), F32),
        scratch_shapes=[pltpu.VMEM((tm, MIX_WIDTH), BF16),
                        pltpu.VMEM((len(STRIDED_DILATIONS), LANE_BLOCKS_PER_MIXER, tm, LANES), F32),
                        pltpu.VMEM((len(STRIDED_DILATIONS), LANE_BLOCKS_PER_MIXER, tm, LANES), F32)],
        compiler_params=pltpu.CompilerParams(
            dimension_semantics=("arbitrary", "arbitrary"), vmem_limit_bytes=VMEM_LIMIT),
        name="outproj",
    )(*seg_o, *seg_lse, yb, yc, yd, proj, x2, w_out)


def _rope_tables(seq):
    inv = 1.0 / (ROPE_THETA ** (jnp.arange(0, HEAD_DIM, 2, dtype=F32) / HEAD_DIM))
    ang = jnp.arange(seq, dtype=F32)[:, None] * inv[None, :]
    cos, sin = jnp.cos(ang), jnp.sin(ang)
    reps = LANES // HEAD_DIM
    cos_t = jnp.tile(jnp.concatenate([cos, cos], axis=1), (1, reps))
    sin_t = jnp.tile(jnp.concatenate([-sin, sin], axis=1), (1, reps))
    return cos_t, sin_t


def _gain_table(qn, kn):
    ones = jnp.ones((HEAD_DIM,), F32)
    per_mixer_q = (qn[0], ones, qn[1], qn[2])
    per_mixer_k = (kn[0], ones, kn[1], kn[2])
    blocks = list(per_mixer_q) + list(per_mixer_k) + [ones] * (2 * N_MIXERS)
    tab = jnp.stack([jnp.tile(g.astype(F32), HEADS_PER_MIXER) for g in blocks])
    return tab[:, None, :]


def kernel(x, norm_gain, w_in, q_norm_gain, k_norm_gain, forget_bias, w_out):
    bsz, seq, d = x.shape
    depth = w_in.shape[0]
    rows = bsz * seq
    tm = min(512, seq)
    cos_t, sin_t = _rope_tables(seq)
    head_of_lane = jnp.arange(MIXER_WIDTH) // HEAD_DIM
    mavg = jnp.where(head_of_lane[:, None] == head_of_lane[None, :], 1.0 / HEAD_DIM, 0.0).astype(BF16)
    tri = jnp.tril(jnp.ones((256, 256), F32)).astype(BF16)

    def natural_col(section, r, p):
        return section * COL_BLOCKS_PER_SECTION + p

    def strided_col(section, r, p):
        return r * PAIRS_PER_MIXER + p

    x2 = x.reshape(rows, d)
    for layer in range(depth):
        w_l = w_in[layer]
        w_main = w_l[:, :PROJ_COLS].astype(BF16)
        wf = jnp.pad(w_l[:, PROJ_COLS:], ((0, 0), (0, LANES - HEADS_PER_MIXER)))
        wf_hi = wf.astype(BF16)
        wf_lo = (wf - wf_hi.astype(F32)).astype(BF16)
        wf_cat = jnp.concatenate([wf_hi, wf_lo], axis=1)
        fb_pad = jnp.pad(forget_bias[layer].astype(F32), (0, LANES - HEADS_PER_MIXER))[None, :]
        gain_tab = _gain_table(q_norm_gain[layer], k_norm_gain[layer])

        proj, cum, *strided = _inproj(x2, norm_gain[layer][None, :].astype(F32), w_main, wf_cat, fb_pad,
                                      gain_tab, cos_t, sin_t, mavg, tri, seq=seq, tm=tm)
        seg = [_band_segment(proj, proj, proj, natural_col, bsz=bsz, seq=seq, dil=1, tq=512)]
        for di, dil in enumerate(STRIDED_DILATIONS):
            qd, kd, vd = strided[3 * di:3 * di + 3]
            seg.append(_band_segment(qd, kd, vd, strided_col, bsz=bsz, seq=seq, dil=dil, tq=512))
        yb = _stick(proj, bsz=bsz, seq=seq, tq=STICK_TILE)
        yc = _moba(proj, bsz=bsz, seq=seq, t=SEQ_TILE)
        yd = _fox(proj, cum, bsz=bsz, seq=seq, t=SEQ_TILE)
        x2 = _outproj([s[0] for s in seg], [s[1] for s in seg], yb, yc, yd, proj, x2,
                      w_out[layer].astype(BF16), tm=tm, tn=min(1024, d))
    return x2.reshape(bsz, seq, d)
```

```python
import functools

import jax
import jax.numpy as jnp
from jax import lax
from jax.experimental import pallas as pl
from jax.experimental.pallas import tpu as pltpu

F32 = jnp.float32
BF16 = jnp.bfloat16

HEAD_DIM = 64
HALF_DIM = HEAD_DIM // 2
LANES = 128
N_MIXERS = 4
HEADS_PER_MIXER = 8
PAIRS_PER_MIXER = HEADS_PER_MIXER // 2
MIXER_WIDTH = HEADS_PER_MIXER * HEAD_DIM
LANE_BLOCKS_PER_MIXER = MIXER_WIDTH // LANES
MIX_WIDTH = N_MIXERS * MIXER_WIDTH
N_SECTIONS = 4
PROJ_COLS = N_SECTIONS * MIX_WIDTH
COL_BLOCKS_PER_SECTION = MIX_WIDTH // LANES
ROPE_THETA = 10000.0
RMS_EPS = 1e-6
SCALE = HEAD_DIM ** -0.5
LOG2E = 1.4426950408889634
NEG = -1e30
DILATIONS = (1, 4, 16)
STRIDED_DILATIONS = DILATIONS[1:]
BAND = 128
MOBA_BLOCK = 256
MOBA_TOPK = 3
SEQ_TILE = 2 * MOBA_BLOCK
STICK_TILE = 256
UNDERFLOW_LOG2 = -160.0
VMEM_LIMIT = 56 * 1024 * 1024

_NT = (((1,), (1,)), ((), ()))


def _dot(a, b):
    return jnp.dot(a, b, preferred_element_type=F32)


def _dot_nt(a, b):
    return lax.dot_general(a, b, _NT, preferred_element_type=F32)


def _split2(x):
    hi = x.astype(BF16)
    lo = (x - hi.astype(F32)).astype(BF16)
    return hi, lo


def _split3(x):
    b1 = x.astype(BF16)
    r1 = x - b1.astype(F32)
    b2 = r1.astype(BF16)
    r2 = r1 - b2.astype(F32)
    return b1, b2, r2.astype(BF16)


def _head_lane_mask(shape, h):
    lane = lax.broadcasted_iota(jnp.int32, shape, len(shape) - 1)
    return (lane >= h * HEAD_DIM) & (lane < (h + 1) * HEAD_DIM)


def _own_head(x, h):
    return jnp.where(_head_lane_mask(x.shape, h), x, jnp.zeros_like(x))


def _lane_column(x, n):
    lane = lax.broadcasted_iota(jnp.int32, x.shape, 1)
    return jnp.sum(jnp.where(lane == n, x, 0.0), axis=1, keepdims=True)


def _inproj_body(x_ref, g_ref, w_ref, wf_ref, fb_ref, gain_ref, cos_ref, sin_ref, mavg_ref, tri_ref,
                 proj_ref, cum_ref, *rest, blocks_per_batch, tm):
    dil_refs = rest[:3 * len(STRIDED_DILATIONS)]
    h_scr, carry_scr, dil_scr = rest[3 * len(STRIDED_DILATIONS):]
    i = pl.program_id(0)
    j = pl.program_id(1)

    @pl.when(j == 0)
    def _prologue():
        x = x_ref[...]
        ms = jnp.mean(x * x, axis=-1, keepdims=True)
        h = x * lax.rsqrt(ms + RMS_EPS) * g_ref[...]
        h_hi, h_lo = _split2(h)
        h_scr[...] = h_hi
        wf = wf_ref[...]
        t = _dot(h_hi, wf)
        u = _dot(h_lo, wf[:, :LANES])
        logit = t[:, :LANES] + t[:, LANES:] + u + fb_ref[...]
        lf = jnp.minimum(logit, 0.0) - jnp.log1p(jnp.exp(-jnp.abs(logit)))

        @pl.when(i % blocks_per_batch == 0)
        def _():
            carry_scr[...] = jnp.zeros_like(carry_scr)

        carry = carry_scr[...]
        tri = tri_ref[...]
        sub = tri.shape[0]
        for r in range(tm // sub):
            b1, b2, b3 = _split3(lf[r * sub:(r + 1) * sub])
            c = _dot(tri, jnp.concatenate([b1, b2, b3], axis=1))
            c = c[:, :LANES] + c[:, LANES:2 * LANES] + c[:, 2 * LANES:] + carry
            cum_ref[r * sub:(r + 1) * sub, :] = c
            carry = c[sub - 1:sub, :]
        carry_scr[...] = carry

    acc = _dot(h_scr[...], w_ref[...])
    sec = j // N_MIXERS
    mix = j % N_MIXERS
    normed = jnp.logical_and(sec < 2, mix != 1)
    roped = jnp.logical_or(mix == 0, mix == 2)
    scale = jnp.where(sec == 0, SCALE * LOG2E, 1.0).astype(F32)

    @pl.when(jnp.logical_not(normed))
    def _plain():
        proj_ref[...] = (acc * scale).astype(BF16)

    @pl.when(normed)
    def _normed():
        hi, lo = _split2(acc * acc)
        mavg = mavg_ref[...]
        ms = _dot(hi, mavg) + _dot(lo, mavg)
        t = acc * lax.rsqrt(ms + RMS_EPS) * (gain_ref[0] * scale)

        @pl.when(jnp.logical_not(roped))
        def _():
            proj_ref[...] = t.astype(BF16)

        @pl.when(roped)
        def _():
            cosv = cos_ref[...]
            sinv = sin_ref[...]
            lane = lax.broadcasted_iota(jnp.int32, (tm, LANES), 1)
            first_half = (lane % HEAD_DIM) < HALF_DIM
            for c in range(MIXER_WIDTH // LANES):
                tc = t[:, c * LANES:(c + 1) * LANES]
                partner = jnp.where(first_half,
                                    pltpu.roll(tc, LANES - HALF_DIM, 1),
                                    pltpu.roll(tc, HALF_DIM, 1))
                proj_ref[:, c * LANES:(c + 1) * LANES] = (tc * cosv + partner * sinv).astype(BF16)

    @pl.when(jnp.logical_and(mix == 0, sec < 3))
    def _dilated():
        for c in range(LANE_BLOCKS_PER_MIXER):
            dil_scr[c] = proj_ref[:, c * LANES:(c + 1) * LANES].astype(F32)
        for section in range(3):
            @pl.when(sec == section)
            def _():
                for di, dil in enumerate(STRIDED_DILATIONS):
                    ref = dil_refs[3 * di + section]
                    n = tm // dil
                    for r in range(dil):
                        for c in range(LANE_BLOCKS_PER_MIXER):
                            lo = r * MIXER_WIDTH + c * LANES
                            ref[:, lo:lo + LANES] = dil_scr[c, pl.ds(r, n, stride=dil), :].astype(BF16)


def _inproj(x2, g, w_main, wf_cat, fb_pad, gain_tab, cos_t, sin_t, mavg, tri, *, seq, tm):
    rows, d = x2.shape
    blocks_per_batch = seq // tm
    n_col = PROJ_COLS // MIXER_WIDTH
    body = functools.partial(_inproj_body, blocks_per_batch=blocks_per_batch, tm=tm)
    dil_specs, dil_shapes = [], []
    for dil in STRIDED_DILATIONS:
        for _ in range(3):
            dil_specs.append(pl.BlockSpec((tm // dil, dil * MIXER_WIDTH), lambda i, j: (i, 0)))
            dil_shapes.append(jax.ShapeDtypeStruct((rows // dil, dil * MIXER_WIDTH), BF16))
    return pl.pallas_call(
        body,
        grid=(rows // tm, n_col),
        in_specs=[
            pl.BlockSpec((tm, d), lambda i, j: (i, 0)),
            pl.BlockSpec((1, d), lambda i, j: (0, 0)),
            pl.BlockSpec((d, MIXER_WIDTH), lambda i, j: (0, j)),
            pl.BlockSpec((d, 2 * LANES), lambda i, j: (0, 0)),
            pl.BlockSpec((1, LANES), lambda i, j: (0, 0)),
            pl.BlockSpec((1, 1, MIXER_WIDTH), lambda i, j: (j, 0, 0)),
            pl.BlockSpec((tm, LANES), lambda i, j: (i % blocks_per_batch, 0)),
            pl.BlockSpec((tm, LANES), lambda i, j: (i % blocks_per_batch, 0)),
            pl.BlockSpec((MIXER_WIDTH, MIXER_WIDTH), lambda i, j: (0, 0)),
            pl.BlockSpec(tri.shape, lambda i, j: (0, 0)),
        ],
        out_specs=[
            pl.BlockSpec((tm, MIXER_WIDTH), lambda i, j: (i, j)),
            pl.BlockSpec((tm, LANES), lambda i, j: (i, 0)),
        ] + dil_specs,
        out_shape=[
            jax.ShapeDtypeStruct((rows, PROJ_COLS), BF16),
            jax.ShapeDtypeStruct((rows, LANES), F32),
        ] + dil_shapes,
        scratch_shapes=[pltpu.VMEM((tm, d), BF16), pltpu.VMEM((1, LANES), F32),
                        pltpu.VMEM((LANE_BLOCKS_PER_MIXER, tm, LANES), F32)],
        compiler_params=pltpu.CompilerParams(
            dimension_semantics=("arbitrary", "arbitrary"), vmem_limit_bytes=VMEM_LIMIT),
        name="inproj",
    )(x2, g, w_main, wf_cat, fb_pad, gain_tab, cos_t, sin_t, mavg, tri)


def _band_body(q_ref, k_ref, v_ref, kp_ref, vp_ref, o_ref, lse_ref, *, tq):
    first = pl.program_id(3) == 0
    q = q_ref[...]
    k = k_ref[...]
    v = v_ref[...]
    row = lax.broadcasted_iota(jnp.int32, (BAND, 2 * BAND), 0)
    col = lax.broadcasted_iota(jnp.int32, (BAND, 2 * BAND), 1)
    in_band = jnp.logical_or(jnp.logical_and(col < BAND, col >= row),
                             jnp.logical_and(col >= BAND, col - BAND <= row))
    lane = lax.broadcasted_iota(jnp.int32, (BAND, LANES), 1)
    head0 = lane < HEAD_DIM
    for c in range(tq // BAND):
        qc = q[c * BAND:(c + 1) * BAND]
        if c == 0:
            kw = jnp.concatenate([kp_ref[...], k[:BAND]], axis=0)
            vw = jnp.concatenate([vp_ref[...], v[:BAND]], axis=0)
            mask = jnp.logical_and(in_band, jnp.logical_or(col >= BAND, jnp.logical_not(first)))
        else:
            kw = k[(c - 1) * BAND:(c + 1) * BAND]
            vw = v[(c - 1) * BAND:(c + 1) * BAND]
            mask = in_band
        outs, lses = [], []
        for h in range(2):
            s = jnp.where(mask, _dot_nt(_own_head(qc, h), kw), NEG)
            m = jnp.max(s, axis=1, keepdims=True)
            p = jnp.exp2(s - m)
            l = jnp.sum(p, axis=1, keepdims=True)
            outs.append(_dot(p.astype(BF16), vw) / l)
            lses.append(m + jnp.log2(l))
        o_ref[c * BAND:(c + 1) * BAND, :] = jnp.where(head0, outs[0], outs[1]).astype(BF16)
        lse_ref[c * BAND:(c + 1) * BAND, :] = jnp.where(head0, lses[0], lses[1])


def _band_segment(q_arr, k_arr, v_arr, col_of, *, bsz, seq, dil, tq):
    sub_rows = q_arr.shape[0]
    sub_len = seq // dil
    tq = min(tq, sub_len)
    nq = sub_len // tq
    band_per_tq = tq // BAND

    def cur(section):
        return pl.BlockSpec((tq, LANES), lambda b, r, p, i: (b * nq + i, col_of(section, r, p)))

    def prev(section):
        return pl.BlockSpec(
            (BAND, LANES),
            lambda b, r, p, i: (jnp.maximum((b * nq + i) * band_per_tq - 1, 0), col_of(section, r, p)))

    out_spec = pl.BlockSpec((tq, LANES), lambda b, r, p, i: (b * nq + i, r * PAIRS_PER_MIXER + p))
    return pl.pallas_call(
        functools.partial(_band_body, tq=tq),
        grid=(bsz, dil, PAIRS_PER_MIXER, nq),
        in_specs=[cur(0), cur(1), cur(2), prev(1), prev(2)],
        out_specs=[out_spec, out_spec],
        out_shape=[
            jax.ShapeDtypeStruct((sub_rows, dil * MIXER_WIDTH), BF16),
            jax.ShapeDtypeStruct((sub_rows, dil * MIXER_WIDTH), F32),
        ],
        compiler_params=pltpu.CompilerParams(
            dimension_semantics=("arbitrary",) * 4, vmem_limit_bytes=VMEM_LIMIT),
        name=f"band_d{dil}",
    )(q_arr, k_arr, v_arr, k_arr, v_arr)


def _softmax_step(carry, s, v_blk):
    m, l, acc = carry
    m_new = jnp.maximum(m, jnp.max(s, axis=1, keepdims=True))
    alpha = jnp.exp2(m - m_new)
    p = jnp.exp2(s - m_new)
    l = alpha * l + jnp.sum(p, axis=1, keepdims=True)
    acc = alpha * acc + _dot(p.astype(BF16), v_blk)
    return m_new, l, acc


def _softmax_init(tq):
    return (jnp.full((tq, 1), NEG, F32), jnp.zeros((tq, 1), F32), jnp.zeros((tq, LANES), F32))


def _merge_heads(outs):
    lane = lax.broadcasted_iota(jnp.int32, outs[0].shape, 1)
    return jnp.where(lane < HEAD_DIM, outs[0], outs[1])


def _seq_mixer_specs(*, mixer, nq, tq, seq):
    def colblk(section, p):
        return section * COL_BLOCKS_PER_SECTION + mixer * PAIRS_PER_MIXER + p

    q_spec = pl.BlockSpec((tq, LANES), lambda b, p, i: (b * nq + i, colblk(0, p)))
    k_spec = pl.BlockSpec((seq, LANES), lambda b, p, i: (b, colblk(1, p)))
    v_spec = pl.BlockSpec((seq, LANES), lambda b, p, i: (b, colblk(2, p)))
    o_spec = pl.BlockSpec((tq, LANES), lambda b, p, i: (b * nq + i, p))
    return q_spec, k_spec, v_spec, o_spec


_SEQ_PARAMS = pltpu.CompilerParams(
    dimension_semantics=("arbitrary",) * 3, vmem_limit_bytes=VMEM_LIMIT)


def _fox_augment(x, cum, head, h, key_side):
    lane = lax.broadcasted_iota(jnp.int32, x.shape, 1)
    g = jnp.broadcast_to(_lane_column(cum, head) * LOG2E, x.shape)
    g1, g2, g3 = (piece.astype(F32) for piece in _split3(g))
    one = jnp.ones_like(g1)
    pieces = (one, one, one, -g1, -g2, -g3) if key_side else (g1, g2, g3, one, one, one)
    base = HEAD_DIM * (1 - h)
    aug = jnp.zeros_like(g1)
    for n, piece in enumerate(pieces):
        aug = jnp.where(lane == base + n, piece, aug)
    return jnp.where(_head_lane_mask(x.shape, h), x.astype(F32), aug).astype(BF16)


def _fox_body(fend_ref, bound_ref, q_ref, k_ref, v_ref, cumq_ref, cumk_ref, o_ref, kaug_scr, *, t, seq):
    b = pl.program_id(0)
    p = pl.program_id(1)
    iq = pl.program_id(2)

    @pl.when(iq == 0)
    def _augment_keys():
        def chunk(c, _):
            rows = pl.ds(pl.multiple_of(c * t, t), t)
            for h in range(2):
                kaug_scr[h, rows, :] = _fox_augment(k_ref[rows, :], cumk_ref[rows, :], 2 * p + h, h, True)
            return 0
        lax.fori_loop(0, seq // t, chunk, 0)

    q = q_ref[...]
    cumq = cumq_ref[...]
    q_aug = [_fox_augment(q, cumq, 2 * p + h, h, False) for h in range(2)]
    row = lax.broadcasted_iota(jnp.int32, (t, t), 0)
    col = lax.broadcasted_iota(jnp.int32, (t, t), 1)
    causal = col <= row

    def tile(j, state, diagonal):
        rows = pl.ds(pl.multiple_of(j * t, t), t)
        v_blk = v_ref[rows, :]
        new = []
        for h in range(2):
            s = _dot_nt(q_aug[h], kaug_scr[h, rows, :])
            if diagonal:
                s = jnp.where(causal, s, NEG)
            new.append(_softmax_step(state[h], s, v_blk))
        return tuple(new)

    state = tile(iq, (_softmax_init(t), _softmax_init(t)), True)

    last = jnp.maximum(iq - 1, 0)

    def reaches(j):
        alive = [LOG2E * (fend_ref[b, 2 * p + h, last] - fend_ref[b, 2 * p + h, j])
                 + 2.0 * bound_ref[0] > UNDERFLOW_LOG2 for h in range(2)]
        return jnp.logical_or(alive[0], alive[1])

    def cond(loop):
        j, _ = loop
        return jnp.logical_and(j >= 0, reaches(jnp.maximum(j, 0)))

    def body(loop):
        j, state = loop
        return j - 1, tile(j, state, False)

    _, state = lax.while_loop(cond, body, (iq - 1, state))
    o_ref[...] = _merge_heads([acc / l for _, l, acc in state]).astype(BF16)


def _fox(proj, cum, tile_end_sums, score_bound, *, bsz, seq, t):
    rows = proj.shape[0]
    nq = seq // t
    q_spec, k_spec, v_spec, o_spec = _seq_mixer_specs(mixer=3, nq=nq, tq=t, seq=seq)
    smem = pl.BlockSpec(memory_space=pltpu.SMEM)
    return pl.pallas_call(
        functools.partial(_fox_body, t=t, seq=seq),
        grid=(bsz, PAIRS_PER_MIXER, nq),
        in_specs=[
            smem, smem, q_spec, k_spec, v_spec,
            pl.BlockSpec((t, LANES), lambda b, p, i: (b * nq + i, 0)),
            pl.BlockSpec((seq, LANES), lambda b, p, i: (b, 0)),
        ],
        out_specs=o_spec,
        out_shape=jax.ShapeDtypeStruct((rows, MIXER_WIDTH), BF16),
        scratch_shapes=[pltpu.VMEM((2, seq, LANES), BF16)],
        compiler_params=_SEQ_PARAMS,
        name="fox",
    )(tile_end_sums, score_bound, proj, proj, proj, cum, cum)


def _moba_body(q_ref, k_ref, v_ref, o_ref, kmean_scr, kaug_scr, *, t, seq):
    iq = pl.program_id(2)
    nblk = seq // MOBA_BLOCK
    blocks_per_tile = t // MOBA_BLOCK

    @pl.when(iq == 0)
    def _prepare_keys():
        r = lax.broadcasted_iota(jnp.int32, (LANES, seq), 0)
        c = lax.broadcasted_iota(jnp.int32, (LANES, seq), 1)
        member = jnp.where(c // MOBA_BLOCK == r % HEAD_DIM, 1.0 / MOBA_BLOCK, 0.0).astype(BF16)
        hi, lo = _split2(_dot(member, k_ref[...]))
        kmean_scr[0] = hi
        kmean_scr[1] = lo

        def chunk(c, _):
            rows = pl.ds(pl.multiple_of(c * t, t), t)
            k = k_ref[rows, :].astype(F32)
            lane = lax.broadcasted_iota(jnp.int32, (t, LANES), 1)
            blk = (c * t + lax.broadcasted_iota(jnp.int32, (t, LANES), 0)) // MOBA_BLOCK
            for h in range(2):
                onehot = jnp.where(lane - HEAD_DIM * (1 - h) == blk, 1.0, 0.0)
                kaug_scr[h, rows, :] = jnp.where(_head_lane_mask(k.shape, h), k, onehot).astype(BF16)
            return 0
        lax.fori_loop(0, seq // t, chunk, 0)

    q = q_ref[...]
    slot = lax.broadcasted_iota(jnp.int32, (LANES, t), 0)
    qblk = blocks_per_tile * iq + lax.broadcasted_iota(jnp.int32, (LANES, t), 1) // MOBA_BLOCK
    q_aug = []
    for h in range(2):
        qh = _own_head(q, h)
        gate = _dot_nt(kmean_scr[0], qh) + _dot_nt(kmean_scr[1], qh)
        blk = slot - HEAD_DIM * (1 - h)
        past = jnp.logical_and(blk >= 0, blk < qblk)
        work = jnp.where(past, gate, NEG)
        bias = jnp.where(blk == qblk, 0.0, NEG)
        for _ in range(min(MOBA_TOPK, nblk)):
            best = jnp.max(work, axis=0, keepdims=True)
            idx = jnp.min(jnp.where(work == best, slot, LANES), axis=0, keepdims=True)
            hit = slot == idx
            bias = jnp.where(hit, jnp.where(past, 0.0, bias), bias)
            work = jnp.where(hit, -jnp.inf, work)
        q_aug.append(jnp.where(_head_lane_mask(q.shape, h), q.astype(F32), bias.T).astype(BF16))

    row = lax.broadcasted_iota(jnp.int32, (t, t), 0)
    col = lax.broadcasted_iota(jnp.int32, (t, t), 1)
    causal = col <= row

    def tile(j, state, diagonal):
        rows = pl.ds(pl.multiple_of(j * t, t), t)
        v_blk = v_ref[rows, :]
        new = []
        for h in range(2):
            s = _dot_nt(q_aug[h], kaug_scr[h, rows, :])
            if diagonal:
                s = jnp.where(causal, s, NEG)
            new.append(_softmax_step(state[h], s, v_blk))
        return tuple(new)

    state = tile(iq, (_softmax_init(t), _softmax_init(t)), True)
    state = lax.fori_loop(0, iq, lambda j, st: tile(j, st, False), state)
    o_ref[...] = _merge_heads([acc / l for _, l, acc in state]).astype(BF16)


def _moba(proj, *, bsz, seq, t):
    assert t % MOBA_BLOCK == 0 and seq // MOBA_BLOCK <= HEAD_DIM
    rows = proj.shape[0]
    nq = seq // t
    q_spec, k_spec, v_spec, o_spec = _seq_mixer_specs(mixer=2, nq=nq, tq=t, seq=seq)
    return pl.pallas_call(
        functools.partial(_moba_body, t=t, seq=seq),
        grid=(bsz, PAIRS_PER_MIXER, nq),
        in_specs=[q_spec, k_spec, v_spec],
        out_specs=o_spec,
        out_shape=jax.ShapeDtypeStruct((rows, MIXER_WIDTH), BF16),
        scratch_shapes=[pltpu.VMEM((2, LANES, LANES), BF16), pltpu.VMEM((2, seq, LANES), BF16)],
        compiler_params=_SEQ_PARAMS,
        name="moba",
    )(proj, proj, proj)


def _stick_body(q_ref, k_ref, v_ref, o_ref, *, tq):
    iq = pl.program_id(2)
    q = q_ref[...]
    q_own = [_own_head(q, h) for h in range(2)]
    row = lax.broadcasted_iota(jnp.int32, (tq, tq), 0)
    col = lax.broadcasted_iota(jnp.int32, (tq, tq), 1)
    strictly_past = col < row
    later = jnp.where(row > col, 1.0, 0.0).astype(BF16)

    def block(j, state, diagonal):
        rows = pl.ds(pl.multiple_of(j * tq, tq), tq)
        k_blk = k_ref[rows, :]
        v_blk = v_ref[rows, :]
        new = []
        for h in range(2):
            carry_sum, acc = state[h]
            z = _dot_nt(q_own[h], k_blk)
            softplus = jnp.maximum(z, 0.0) + jnp.log2(1.0 + jnp.exp2(-jnp.abs(z)))
            log_keep = -softplus
            if diagonal:
                log_keep = jnp.where(strictly_past, log_keep, 0.0)
            hi, lo = _split2(log_keep)
            after = _dot(hi, later) + _dot(lo, later) + carry_sum
            log_a = z - softplus + after
            if diagonal:
                log_a = jnp.where(strictly_past, log_a, NEG)
            acc = acc + _dot(jnp.exp2(log_a).astype(BF16), v_blk)
            new.append((carry_sum + jnp.sum(log_keep, axis=1, keepdims=True), acc))
        return tuple(new)

    zero = (jnp.zeros((tq, 1), F32), jnp.zeros((tq, LANES), F32))
    state = block(iq, (zero, zero), True)

    def cond(loop):
        j, state = loop
        alive = jnp.max(jnp.maximum(state[0][0], state[1][0])) > UNDERFLOW_LOG2
        return jnp.logical_and(j >= 0, alive)

    def body(loop):
        j, state = loop
        return j - 1, block(j, state, False)

    _, state = lax.while_loop(cond, body, (iq - 1, state))
    o_ref[...] = _merge_heads([acc for _, acc in state]).astype(BF16)


def _stick(proj, *, bsz, seq, tq):
    rows = proj.shape[0]
    nq = seq // tq
    q_spec, k_spec, v_spec, o_spec = _seq_mixer_specs(mixer=1, nq=nq, tq=tq, seq=seq)
    return pl.pallas_call(
        functools.partial(_stick_body, tq=tq),
        grid=(bsz, PAIRS_PER_MIXER, nq),
        in_specs=[q_spec, k_spec, v_spec],
        out_specs=o_spec,
        out_shape=jax.ShapeDtypeStruct((rows, MIXER_WIDTH), BF16),
        compiler_params=_SEQ_PARAMS,
        name="stick",
    )(proj, proj, proj)


def _outproj_body(o1_ref, o4_ref, o16_ref, l1_ref, l4_ref, l16_ref, yb_ref, yc_ref, yd_ref,
                  gate_ref, x_ref, w_ref, out_ref, y_scr, o_scr, l_scr, *, tm):
    j = pl.program_id(1)

    @pl.when(j == 0)
    def _gate():
        for slot, (dil, o_ref, l_ref) in enumerate(zip(STRIDED_DILATIONS, (o4_ref, o16_ref), (l4_ref, l16_ref))):
            n = tm // dil
            for r in range(dil):
                for c in range(LANE_BLOCKS_PER_MIXER):
                    cols = slice(r * MIXER_WIDTH + c * LANES, r * MIXER_WIDTH + (c + 1) * LANES)
                    o_scr[slot, c, pl.ds(r, n, stride=dil), :] = o_ref[:, cols].astype(F32)
                    l_scr[slot, c, pl.ds(r, n, stride=dil), :] = l_ref[:, cols]

        def natural(scr, slot):
            return jnp.concatenate([scr[slot, c] for c in range(LANE_BLOCKS_PER_MIXER)], axis=1)

        l1, l4, l16 = l1_ref[...], natural(l_scr, 0), natural(l_scr, 1)
        m = jnp.maximum(jnp.maximum(l1, l4), l16)
        e1, e4, e16 = jnp.exp2(l1 - m), jnp.exp2(l4 - m), jnp.exp2(l16 - m)
        ya = (e1 * o1_ref[...].astype(F32) + e4 * natural(o_scr, 0)
              + e16 * natural(o_scr, 1)) / (e1 + e4 + e16)
        parts = (ya, yb_ref[...].astype(F32), yc_ref[...].astype(F32), yd_ref[...].astype(F32))
        for mxr, y in enumerate(parts):
            g = gate_ref[:, mxr * MIXER_WIDTH:(mxr + 1) * MIXER_WIDTH].astype(F32)
            silu = g / (1.0 + jnp.exp(-g))
            y_scr[:, mxr * MIXER_WIDTH:(mxr + 1) * MIXER_WIDTH] = (y * silu).astype(BF16)

    out_ref[...] = x_ref[...] + _dot(y_scr[...], w_ref[...])


def _outproj(seg_o, seg_lse, yb, yc, yd, proj, x2, w_out, *, tm, tn):
    rows, d = x2.shape
    row_blk = lambda i, j: (i, 0)
    mix_spec = pl.BlockSpec((tm, MIXER_WIDTH), row_blk)
    seg_specs = [pl.BlockSpec((tm // dil, dil * MIXER_WIDTH), row_blk) for dil in DILATIONS]
    return pl.pallas_call(
        functools.partial(_outproj_body, tm=tm),
        grid=(rows // tm, d // tn),
        in_specs=seg_specs + seg_specs + [mix_spec] * 3 + [
            pl.BlockSpec((tm, MIX_WIDTH), lambda i, j: (i, N_SECTIONS - 1)),
            pl.BlockSpec((tm, tn), lambda i, j: (i, j)),
            pl.BlockSpec((MIX_WIDTH, tn), lambda i, j: (0, j)),
        ],
        out_specs=pl.BlockSpec((tm, tn), lambda i, j: (i, j)),
        out_shape=jax.ShapeDtypeStruct((rows, d), F32),
        scratch_shapes=[pltpu.VMEM((tm, MIX_WIDTH), BF16),
                        pltpu.VMEM((len(STRIDED_DILATIONS), LANE_BLOCKS_PER_MIXER, tm, LANES), F32),
                        pltpu.VMEM((len(STRIDED_DILATIONS), LANE_BLOCKS_PER_MIXER, tm, LANES), F32)],
        compiler_params=pltpu.CompilerParams(
            dimension_semantics=("arbitrary", "arbitrary"), vmem_limit_bytes=VMEM_LIMIT),
        name="outproj",
    )(*seg_o, *seg_lse, yb, yc, yd, proj, x2, w_out)


def _rope_tables(seq):
    inv = 1.0 / (ROPE_THETA ** (jnp.arange(0, HEAD_DIM, 2, dtype=F32) / HEAD_DIM))
    ang = jnp.arange(seq, dtype=F32)[:, None] * inv[None, :]
    cos, sin = jnp.cos(ang), jnp.sin(ang)
    reps = LANES // HEAD_DIM
    cos_t = jnp.tile(jnp.concatenate([cos, cos], axis=1), (1, reps))
    sin_t = jnp.tile(jnp.concatenate([-sin, sin], axis=1), (1, reps))
    return cos_t, sin_t


def _gain_table(qn, kn):
    ones = jnp.ones((HEAD_DIM,), F32)
    per_mixer_q = (qn[0], ones, qn[1], qn[2])
    per_mixer_k = (kn[0], ones, kn[1], kn[2])
    blocks = list(per_mixer_q) + list(per_mixer_k) + [ones] * (2 * N_MIXERS)
    tab = jnp.stack([jnp.tile(g.astype(F32), HEADS_PER_MIXER) for g in blocks])
    return tab[:, None, :]


def kernel(x, norm_gain, w_in, q_norm_gain, k_norm_gain, forget_bias, w_out):
    bsz, seq, d = x.shape
    depth = w_in.shape[0]
    rows = bsz * seq
    tm = min(512, seq)
    cos_t, sin_t = _rope_tables(seq)
    head_of_lane = jnp.arange(MIXER_WIDTH) // HEAD_DIM
    mavg = jnp.where(head_of_lane[:, None] == head_of_lane[None, :], 1.0 / HEAD_DIM, 0.0).astype(BF16)
    tri = jnp.tril(jnp.ones((256, 256), F32)).astype(BF16)

    def natural_col(section, r, p):
        return section * COL_BLOCKS_PER_SECTION + p

    def strided_col(section, r, p):
        return r * PAIRS_PER_MIXER + p

    x2 = x.reshape(rows, d)
    for layer in range(depth):
        w_l = w_in[layer]
        w_main = w_l[:, :PROJ_COLS].astype(BF16)
        wf = jnp.pad(w_l[:, PROJ_COLS:], ((0, 0), (0, LANES - HEADS_PER_MIXER)))
        wf_hi = wf.astype(BF16)
        wf_lo = (wf - wf_hi.astype(F32)).astype(BF16)
        wf_cat = jnp.concatenate([wf_hi, wf_lo], axis=1)
        fb_pad = jnp.pad(forget_bias[layer].astype(F32), (0, LANES - HEADS_PER_MIXER))[None, :]
        gain_tab = _gain_table(q_norm_gain[layer], k_norm_gain[layer])

        proj, cum, *strided = _inproj(x2, norm_gain[layer][None, :].astype(F32), w_main, wf_cat, fb_pad,
                                      gain_tab, cos_t, sin_t, mavg, tri, seq=seq, tm=tm)
        seg = [_band_segment(proj, proj, proj, natural_col, bsz=bsz, seq=seq, dil=1, tq=512)]
        for di, dil in enumerate(STRIDED_DILATIONS):
            qd, kd, vd = strided[3 * di:3 * di + 3]
            seg.append(_band_segment(qd, kd, vd, strided_col, bsz=bsz, seq=seq, dil=dil, tq=512))
        yb = _stick(proj, bsz=bsz, seq=seq, tq=STICK_TILE)
        yc = _moba(proj, bsz=bsz, seq=seq, t=SEQ_TILE)
        nt = seq // SEQ_TILE
        tile_end_sums = cum.reshape(bsz, nt, SEQ_TILE, LANES)[:, :, SEQ_TILE - 1, :HEADS_PER_MIXER]
        tile_end_sums = tile_end_sums.transpose(0, 2, 1)
        score_bound = (1.01 * HEAD_DIM * SCALE * LOG2E * jnp.max(jnp.abs(q_norm_gain[layer, 2]))
                       * jnp.max(jnp.abs(k_norm_gain[layer, 2]))).astype(F32).reshape(1)
        yd = _fox(proj, cum, tile_end_sums, score_bound, bsz=bsz, seq=seq, t=SEQ_TILE)
        x2 = _outproj([s[0] for s in seg], [s[1] for s in seg], yb, yc, yd, proj, x2,
                      w_out[layer].astype(BF16), tm=tm, tn=min(1024, d))
    return x2.reshape(bsz, seq, d)
```

```python
import functools

import jax
import jax.numpy as jnp
from jax import lax
from jax.experimental import pallas as pl
from jax.experimental.pallas import tpu as pltpu

F32 = jnp.float32
BF16 = jnp.bfloat16

HEAD_DIM = 64
HALF_DIM = HEAD_DIM // 2
LANES = 128
N_MIXERS = 4
HEADS_PER_MIXER = 8
PAIRS_PER_MIXER = HEADS_PER_MIXER // 2
MIXER_WIDTH = HEADS_PER_MIXER * HEAD_DIM
LANE_BLOCKS_PER_MIXER = MIXER_WIDTH // LANES
MIX_WIDTH = N_MIXERS * MIXER_WIDTH
N_SECTIONS = 4
PROJ_COLS = N_SECTIONS * MIX_WIDTH
COL_BLOCKS_PER_SECTION = MIX_WIDTH // LANES
ROPE_THETA = 10000.0
RMS_EPS = 1e-6
SCALE = HEAD_DIM ** -0.5
LOG2E = 1.4426950408889634
NEG = -1e30
DILATIONS = (1, 4, 16)
STRIDED_DILATIONS = DILATIONS[1:]
BAND = 128
MOBA_BLOCK = 256
MOBA_TOPK = 3
QUERY_TILE = 1024
KEY_TILE = 1024
STICK_TILE = 256
UNDERFLOW_LOG2 = -160.0
VMEM_LIMIT = 56 * 1024 * 1024

_NT = (((1,), (1,)), ((), ()))


def _dot(a, b):
    return jnp.dot(a, b, preferred_element_type=F32)


def _dot_nt(a, b):
    return lax.dot_general(a, b, _NT, preferred_element_type=F32)


def _split2(x):
    hi = x.astype(BF16)
    lo = (x - hi.astype(F32)).astype(BF16)
    return hi, lo


def _split3(x):
    b1 = x.astype(BF16)
    r1 = x - b1.astype(F32)
    b2 = r1.astype(BF16)
    r2 = r1 - b2.astype(F32)
    return b1, b2, r2.astype(BF16)


def _head_lane_mask(shape, h):
    lane = lax.broadcasted_iota(jnp.int32, shape, len(shape) - 1)
    return (lane >= h * HEAD_DIM) & (lane < (h + 1) * HEAD_DIM)


def _own_head(x, h):
    return jnp.where(_head_lane_mask(x.shape, h), x, jnp.zeros_like(x))


def _lane_column(x, n):
    lane = lax.broadcasted_iota(jnp.int32, x.shape, 1)
    return jnp.sum(jnp.where(lane == n, x, 0.0), axis=1, keepdims=True)


def _inproj_body(x_ref, g_ref, w_ref, wf_ref, fb_ref, gain_ref, cos_ref, sin_ref, mavg_ref, tri_ref,
                 proj_ref, cum_ref, *rest, blocks_per_batch, tm):
    dil_refs = rest[:3 * len(STRIDED_DILATIONS)]
    h_scr, carry_scr, dil_scr = rest[3 * len(STRIDED_DILATIONS):]
    i = pl.program_id(0)
    j = pl.program_id(1)

    @pl.when(j == 0)
    def _prologue():
        x = x_ref[...]
        ms = jnp.mean(x * x, axis=-1, keepdims=True)
        h = x * lax.rsqrt(ms + RMS_EPS) * g_ref[...]
        h_hi, h_lo = _split2(h)
        h_scr[...] = h_hi
        wf = wf_ref[...]
        t = _dot(h_hi, wf)
        u = _dot(h_lo, wf[:, :LANES])
        logit = t[:, :LANES] + t[:, LANES:] + u + fb_ref[...]
        lf = jnp.minimum(logit, 0.0) - jnp.log1p(jnp.exp(-jnp.abs(logit)))

        @pl.when(i % blocks_per_batch == 0)
        def _():
            carry_scr[...] = jnp.zeros_like(carry_scr)

        carry = carry_scr[...]
        tri = tri_ref[...]
        sub = tri.shape[0]
        for r in range(tm // sub):
            b1, b2, b3 = _split3(lf[r * sub:(r + 1) * sub])
            c = _dot(tri, jnp.concatenate([b1, b2, b3], axis=1))
            c = c[:, :LANES] + c[:, LANES:2 * LANES] + c[:, 2 * LANES:] + carry
            cum_ref[r * sub:(r + 1) * sub, :] = c
            carry = c[sub - 1:sub, :]
        carry_scr[...] = carry

    acc = _dot(h_scr[...], w_ref[...])
    sec = j // N_MIXERS
    mix = j % N_MIXERS
    normed = jnp.logical_and(sec < 2, mix != 1)
    roped = jnp.logical_or(mix == 0, mix == 2)
    scale = jnp.where(sec == 0, SCALE * LOG2E, 1.0).astype(F32)

    @pl.when(jnp.logical_not(normed))
    def _plain():
        proj_ref[...] = (acc * scale).astype(BF16)

    @pl.when(normed)
    def _normed():
        hi, lo = _split2(acc * acc)
        mavg = mavg_ref[...]
        ms = _dot(hi, mavg) + _dot(lo, mavg)
        t = acc * lax.rsqrt(ms + RMS_EPS) * (gain_ref[0] * scale)

        @pl.when(jnp.logical_not(roped))
        def _():
            proj_ref[...] = t.astype(BF16)

        @pl.when(roped)
        def _():
            cosv = cos_ref[...]
            sinv = sin_ref[...]
            lane = lax.broadcasted_iota(jnp.int32, (tm, LANES), 1)
            first_half = (lane % HEAD_DIM) < HALF_DIM
            for c in range(MIXER_WIDTH // LANES):
                tc = t[:, c * LANES:(c + 1) * LANES]
                partner = jnp.where(first_half,
                                    pltpu.roll(tc, LANES - HALF_DIM, 1),
                                    pltpu.roll(tc, HALF_DIM, 1))
                proj_ref[:, c * LANES:(c + 1) * LANES] = (tc * cosv + partner * sinv).astype(BF16)

    @pl.when(jnp.logical_and(mix == 0, sec < 3))
    def _dilated():
        for c in range(LANE_BLOCKS_PER_MIXER):
            dil_scr[c] = proj_ref[:, c * LANES:(c + 1) * LANES].astype(F32)
        for section in range(3):
            @pl.when(sec == section)
            def _():
                for di, dil in enumerate(STRIDED_DILATIONS):
                    ref = dil_refs[3 * di + section]
                    n = tm // dil
                    for r in range(dil):
                        for c in range(LANE_BLOCKS_PER_MIXER):
                            lo = r * MIXER_WIDTH + c * LANES
                            ref[:, lo:lo + LANES] = dil_scr[c, pl.ds(r, n, stride=dil), :].astype(BF16)


def _inproj(x2, g, w_main, wf_cat, fb_pad, gain_tab, cos_t, sin_t, mavg, tri, *, seq, tm):
    rows, d = x2.shape
    blocks_per_batch = seq // tm
    n_col = PROJ_COLS // MIXER_WIDTH
    body = functools.partial(_inproj_body, blocks_per_batch=blocks_per_batch, tm=tm)
    dil_specs, dil_shapes = [], []
    for dil in STRIDED_DILATIONS:
        for _ in range(3):
            dil_specs.append(pl.BlockSpec((tm // dil, dil * MIXER_WIDTH), lambda i, j: (i, 0)))
            dil_shapes.append(jax.ShapeDtypeStruct((rows // dil, dil * MIXER_WIDTH), BF16))
    return pl.pallas_call(
        body,
        grid=(rows // tm, n_col),
        in_specs=[
            pl.BlockSpec((tm, d), lambda i, j: (i, 0)),
            pl.BlockSpec((1, d), lambda i, j: (0, 0)),
            pl.BlockSpec((d, MIXER_WIDTH), lambda i, j: (0, j)),
            pl.BlockSpec((d, 2 * LANES), lambda i, j: (0, 0)),
            pl.BlockSpec((1, LANES), lambda i, j: (0, 0)),
            pl.BlockSpec((1, 1, MIXER_WIDTH), lambda i, j: (j, 0, 0)),
            pl.BlockSpec((tm, LANES), lambda i, j: (i % blocks_per_batch, 0)),
            pl.BlockSpec((tm, LANES), lambda i, j: (i % blocks_per_batch, 0)),
            pl.BlockSpec((MIXER_WIDTH, MIXER_WIDTH), lambda i, j: (0, 0)),
            pl.BlockSpec(tri.shape, lambda i, j: (0, 0)),
        ],
        out_specs=[
            pl.BlockSpec((tm, MIXER_WIDTH), lambda i, j: (i, j)),
            pl.BlockSpec((tm, LANES), lambda i, j: (i, 0)),
        ] + dil_specs,
        out_shape=[
            jax.ShapeDtypeStruct((rows, PROJ_COLS), BF16),
            jax.ShapeDtypeStruct((rows, LANES), F32),
        ] + dil_shapes,
        scratch_shapes=[pltpu.VMEM((tm, d), BF16), pltpu.VMEM((1, LANES), F32),
                        pltpu.VMEM((LANE_BLOCKS_PER_MIXER, tm, LANES), F32)],
        compiler_params=pltpu.CompilerParams(
            dimension_semantics=("arbitrary", "arbitrary"), vmem_limit_bytes=VMEM_LIMIT),
        name="inproj",
    )(x2, g, w_main, wf_cat, fb_pad, gain_tab, cos_t, sin_t, mavg, tri)


def _band_body(q_ref, k_ref, v_ref, kp_ref, vp_ref, o_ref, lse_ref, *, tq):
    first = pl.program_id(3) == 0
    q = q_ref[...]
    k = k_ref[...]
    v = v_ref[...]
    row = lax.broadcasted_iota(jnp.int32, (BAND, 2 * BAND), 0)
    col = lax.broadcasted_iota(jnp.int32, (BAND, 2 * BAND), 1)
    in_band = jnp.logical_or(jnp.logical_and(col < BAND, col >= row),
                             jnp.logical_and(col >= BAND, col - BAND <= row))
    lane = lax.broadcasted_iota(jnp.int32, (BAND, LANES), 1)
    head0 = lane < HEAD_DIM
    for c in range(tq // BAND):
        qc = q[c * BAND:(c + 1) * BAND]
        if c == 0:
            kw = jnp.concatenate([kp_ref[...], k[:BAND]], axis=0)
            vw = jnp.concatenate([vp_ref[...], v[:BAND]], axis=0)
            mask = jnp.logical_and(in_band, jnp.logical_or(col >= BAND, jnp.logical_not(first)))
        else:
            kw = k[(c - 1) * BAND:(c + 1) * BAND]
            vw = v[(c - 1) * BAND:(c + 1) * BAND]
            mask = in_band
        outs, lses = [], []
        for h in range(2):
            s = jnp.where(mask, _dot_nt(_own_head(qc, h), kw), NEG)
            m = jnp.max(s, axis=1, keepdims=True)
            p = jnp.exp2(s - m)
            l = jnp.sum(p, axis=1, keepdims=True)
            outs.append(_dot(p.astype(BF16), vw) / l)
            lses.append(m + jnp.log2(l))
        o_ref[c * BAND:(c + 1) * BAND, :] = jnp.where(head0, outs[0], outs[1]).astype(BF16)
        lse_ref[c * BAND:(c + 1) * BAND, :] = jnp.where(head0, lses[0], lses[1])


def _band_segment(q_arr, k_arr, v_arr, col_of, *, bsz, seq, dil, tq):
    sub_rows = q_arr.shape[0]
    sub_len = seq // dil
    tq = min(tq, sub_len)
    nq = sub_len // tq
    band_per_tq = tq // BAND

    def cur(section):
        return pl.BlockSpec((tq, LANES), lambda b, r, p, i: (b * nq + i, col_of(section, r, p)))

    def prev(section):
        return pl.BlockSpec(
            (BAND, LANES),
            lambda b, r, p, i: (jnp.maximum((b * nq + i) * band_per_tq - 1, 0), col_of(section, r, p)))

    out_spec = pl.BlockSpec((tq, LANES), lambda b, r, p, i: (b * nq + i, r * PAIRS_PER_MIXER + p))
    return pl.pallas_call(
        functools.partial(_band_body, tq=tq),
        grid=(bsz, dil, PAIRS_PER_MIXER, nq),
        in_specs=[cur(0), cur(1), cur(2), prev(1), prev(2)],
        out_specs=[out_spec, out_spec],
        out_shape=[
            jax.ShapeDtypeStruct((sub_rows, dil * MIXER_WIDTH), BF16),
            jax.ShapeDtypeStruct((sub_rows, dil * MIXER_WIDTH), F32),
        ],
        compiler_params=pltpu.CompilerParams(
            dimension_semantics=("arbitrary",) * 4, vmem_limit_bytes=VMEM_LIMIT),
        name=f"band_d{dil}",
    )(q_arr, k_arr, v_arr, k_arr, v_arr)


def _softmax_step(carry, s, v_blk):
    m, l, acc = carry
    m_new = jnp.maximum(m, jnp.max(s, axis=1, keepdims=True))
    alpha = jnp.exp2(m - m_new)
    p = jnp.exp2(s - m_new)
    l = alpha * l + jnp.sum(p, axis=1, keepdims=True)
    acc = alpha * acc + _dot(p.astype(BF16), v_blk)
    return m_new, l, acc


def _attend_tile(q_aug, k_tiles, v_blk, state, key_offset=None):
    new = []
    for h in range(2):
        s = _dot_nt(q_aug[h], k_tiles[h])
        if key_offset is not None:
            row = lax.broadcasted_iota(jnp.int32, s.shape, 0)
            col = lax.broadcasted_iota(jnp.int32, s.shape, 1)
            s = jnp.where(col + key_offset <= row, s, NEG)
        new.append(_softmax_step(state[h], s, v_blk))
    return tuple(new)


def _attend_diagonal(tile, iq, tq, tk):
    state = (_softmax_init(tq), _softmax_init(tq))
    for d in range(tq // tk):
        state = tile(iq * (tq // tk) + d, state, d * tk)
    return state


def _softmax_init(tq):
    return (jnp.full((tq, 1), NEG, F32), jnp.zeros((tq, 1), F32), jnp.zeros((tq, LANES), F32))


def _merge_heads(outs):
    lane = lax.broadcasted_iota(jnp.int32, outs[0].shape, 1)
    return jnp.where(lane < HEAD_DIM, outs[0], outs[1])


def _seq_mixer_specs(*, mixer, nq, tq, seq):
    def colblk(section, p):
        return section * COL_BLOCKS_PER_SECTION + mixer * PAIRS_PER_MIXER + p

    q_spec = pl.BlockSpec((tq, LANES), lambda b, p, i: (b * nq + i, colblk(0, p)))
    k_spec = pl.BlockSpec((seq, LANES), lambda b, p, i: (b, colblk(1, p)))
    v_spec = pl.BlockSpec((seq, LANES), lambda b, p, i: (b, colblk(2, p)))
    o_spec = pl.BlockSpec((tq, LANES), lambda b, p, i: (b * nq + i, p))
    return q_spec, k_spec, v_spec, o_spec


_SEQ_PARAMS = pltpu.CompilerParams(
    dimension_semantics=("arbitrary",) * 3, vmem_limit_bytes=VMEM_LIMIT)


def _fox_augment(x, cum, head, h, key_side):
    lane = lax.broadcasted_iota(jnp.int32, x.shape, 1)
    g = jnp.broadcast_to(_lane_column(cum, head) * LOG2E, x.shape)
    g1, g2, g3 = (piece.astype(F32) for piece in _split3(g))
    one = jnp.ones_like(g1)
    pieces = (one, one, one, -g1, -g2, -g3) if key_side else (g1, g2, g3, one, one, one)
    base = HEAD_DIM * (1 - h)
    aug = jnp.zeros_like(g1)
    for n, piece in enumerate(pieces):
        aug = jnp.where(lane == base + n, piece, aug)
    return jnp.where(_head_lane_mask(x.shape, h), x.astype(F32), aug).astype(BF16)


def _fox_body(fend_ref, bound_ref, q_ref, k_ref, v_ref, cumq_ref, cumk_ref, o_ref, kaug_scr, *, tq, tk, seq):
    b = pl.program_id(0)
    p = pl.program_id(1)
    iq = pl.program_id(2)

    @pl.when(iq == 0)
    def _augment_keys():
        def chunk(c, _):
            rows = pl.ds(pl.multiple_of(c * tk, tk), tk)
            for h in range(2):
                kaug_scr[h, rows, :] = _fox_augment(k_ref[rows, :], cumk_ref[rows, :], 2 * p + h, h, True)
            return 0
        lax.fori_loop(0, seq // tk, chunk, 0)

    q = q_ref[...]
    cumq = cumq_ref[...]
    q_aug = [_fox_augment(q, cumq, 2 * p + h, h, False) for h in range(2)]

    def tile(j, state, key_offset=None):
        rows = pl.ds(pl.multiple_of(j * tk, tk), tk)
        return _attend_tile(q_aug, [kaug_scr[h, rows, :] for h in range(2)], v_ref[rows, :], state, key_offset)

    state = _attend_diagonal(tile, iq, tq, tk)

    first_past = iq * (tq // tk) - 1
    last = jnp.maximum(first_past, 0)

    def reaches(j):
        alive = [LOG2E * (fend_ref[b, 2 * p + h, last] - fend_ref[b, 2 * p + h, j])
                 + 2.0 * bound_ref[0] > UNDERFLOW_LOG2 for h in range(2)]
        return jnp.logical_or(alive[0], alive[1])

    def cond(loop):
        j, _ = loop
        return jnp.logical_and(j >= 0, reaches(jnp.maximum(j, 0)))

    def body(loop):
        j, state = loop
        return j - 1, tile(j, state)

    _, state = lax.while_loop(cond, body, (first_past, state))
    o_ref[...] = _merge_heads([acc / l for _, l, acc in state]).astype(BF16)


def _fox(proj, cum, tile_end_sums, score_bound, *, bsz, seq, tq, tk):
    rows = proj.shape[0]
    nq = seq // tq
    q_spec, k_spec, v_spec, o_spec = _seq_mixer_specs(mixer=3, nq=nq, tq=tq, seq=seq)
    smem = pl.BlockSpec(memory_space=pltpu.SMEM)
    return pl.pallas_call(
        functools.partial(_fox_body, tq=tq, tk=tk, seq=seq),
        grid=(bsz, PAIRS_PER_MIXER, nq),
        in_specs=[
            smem, smem, q_spec, k_spec, v_spec,
            pl.BlockSpec((tq, LANES), lambda b, p, i: (b * nq + i, 0)),
            pl.BlockSpec((seq, LANES), lambda b, p, i: (b, 0)),
        ],
        out_specs=o_spec,
        out_shape=jax.ShapeDtypeStruct((rows, MIXER_WIDTH), BF16),
        scratch_shapes=[pltpu.VMEM((2, seq, LANES), BF16)],
        compiler_params=_SEQ_PARAMS,
        name="fox",
    )(tile_end_sums, score_bound, proj, proj, proj, cum, cum)


def _moba_body(q_ref, k_ref, v_ref, o_ref, kmean_scr, kaug_scr, *, tq, tk, seq):
    iq = pl.program_id(2)
    nblk = seq // MOBA_BLOCK
    blocks_per_tile = tq // MOBA_BLOCK

    @pl.when(iq == 0)
    def _prepare_keys():
        r = lax.broadcasted_iota(jnp.int32, (LANES, seq), 0)
        c = lax.broadcasted_iota(jnp.int32, (LANES, seq), 1)
        member = jnp.where(c // MOBA_BLOCK == r % HEAD_DIM, 1.0 / MOBA_BLOCK, 0.0).astype(BF16)
        hi, lo = _split2(_dot(member, k_ref[...]))
        kmean_scr[0] = hi
        kmean_scr[1] = lo

        def chunk(c, _):
            rows = pl.ds(pl.multiple_of(c * tk, tk), tk)
            k = k_ref[rows, :].astype(F32)
            lane = lax.broadcasted_iota(jnp.int32, (tk, LANES), 1)
            blk = (c * tk + lax.broadcasted_iota(jnp.int32, (tk, LANES), 0)) // MOBA_BLOCK
            for h in range(2):
                onehot = jnp.where(lane - HEAD_DIM * (1 - h) == blk, 1.0, 0.0)
                kaug_scr[h, rows, :] = jnp.where(_head_lane_mask(k.shape, h), k, onehot).astype(BF16)
            return 0
        lax.fori_loop(0, seq // tk, chunk, 0)

    q = q_ref[...]
    slot = lax.broadcasted_iota(jnp.int32, (LANES, tq), 0)
    qblk = blocks_per_tile * iq + lax.broadcasted_iota(jnp.int32, (LANES, tq), 1) // MOBA_BLOCK
    q_aug = []
    for h in range(2):
        qh = _own_head(q, h)
        gate = _dot_nt(kmean_scr[0], qh) + _dot_nt(kmean_scr[1], qh)
        blk = slot - HEAD_DIM * (1 - h)
        past = jnp.logical_and(blk >= 0, blk < qblk)
        work = jnp.where(past, gate, NEG)
        bias = jnp.where(blk == qblk, 0.0, NEG)
        for _ in range(min(MOBA_TOPK, nblk)):
            best = jnp.max(work, axis=0, keepdims=True)
            idx = jnp.min(jnp.where(work == best, slot, LANES), axis=0, keepdims=True)
            hit = slot == idx
            bias = jnp.where(hit, jnp.where(past, 0.0, bias), bias)
            work = jnp.where(hit, -jnp.inf, work)
        q_aug.append(jnp.where(_head_lane_mask(q.shape, h), q.astype(F32), bias.T).astype(BF16))

    def tile(j, state, key_offset=None):
        rows = pl.ds(pl.multiple_of(j * tk, tk), tk)
        return _attend_tile(q_aug, [kaug_scr[h, rows, :] for h in range(2)], v_ref[rows, :], state, key_offset)

    state = _attend_diagonal(tile, iq, tq, tk)
    state = lax.fori_loop(0, iq * (tq // tk), tile, state)
    o_ref[...] = _merge_heads([acc / l for _, l, acc in state]).astype(BF16)


def _moba(proj, *, bsz, seq, tq, tk):
    assert tq % MOBA_BLOCK == 0 and seq // MOBA_BLOCK <= HEAD_DIM
    rows = proj.shape[0]
    nq = seq // tq
    q_spec, k_spec, v_spec, o_spec = _seq_mixer_specs(mixer=2, nq=nq, tq=tq, seq=seq)
    return pl.pallas_call(
        functools.partial(_moba_body, tq=tq, tk=tk, seq=seq),
        grid=(bsz, PAIRS_PER_MIXER, nq),
        in_specs=[q_spec, k_spec, v_spec],
        out_specs=o_spec,
        out_shape=jax.ShapeDtypeStruct((rows, MIXER_WIDTH), BF16),
        scratch_shapes=[pltpu.VMEM((2, LANES, LANES), BF16), pltpu.VMEM((2, seq, LANES), BF16)],
        compiler_params=_SEQ_PARAMS,
        name="moba",
    )(proj, proj, proj)


def _stick_body(q_ref, k_ref, v_ref, o_ref, *, tq):
    iq = pl.program_id(2)
    q = q_ref[...]
    q_own = [_own_head(q, h) for h in range(2)]
    row = lax.broadcasted_iota(jnp.int32, (tq, tq), 0)
    col = lax.broadcasted_iota(jnp.int32, (tq, tq), 1)
    strictly_past = col < row
    later = jnp.where(row > col, 1.0, 0.0).astype(BF16)

    def block(j, state, diagonal):
        rows = pl.ds(pl.multiple_of(j * tq, tq), tq)
        k_blk = k_ref[rows, :]
        v_blk = v_ref[rows, :]
        new = []
        for h in range(2):
            carry_sum, acc = state[h]
            z = _dot_nt(q_own[h], k_blk)
            softplus = jnp.maximum(z, 0.0) + jnp.log2(1.0 + jnp.exp2(-jnp.abs(z)))
            log_keep = -softplus
            if diagonal:
                log_keep = jnp.where(strictly_past, log_keep, 0.0)
            hi, lo = _split2(log_keep)
            after = _dot(hi, later) + _dot(lo, later) + carry_sum
            log_a = z - softplus + after
            if diagonal:
                log_a = jnp.where(strictly_past, log_a, NEG)
            acc = acc + _dot(jnp.exp2(log_a).astype(BF16), v_blk)
            new.append((carry_sum + jnp.sum(log_keep, axis=1, keepdims=True), acc))
        return tuple(new)

    zero = (jnp.zeros((tq, 1), F32), jnp.zeros((tq, LANES), F32))
    state = block(iq, (zero, zero), True)

    def cond(loop):
        j, state = loop
        alive = jnp.max(jnp.maximum(state[0][0], state[1][0])) > UNDERFLOW_LOG2
        return jnp.logical_and(j >= 0, alive)

    def body(loop):
        j, state = loop
        return j - 1, block(j, state, False)

    _, state = lax.while_loop(cond, body, (iq - 1, state))
    o_ref[...] = _merge_heads([acc for _, acc in state]).astype(BF16)


def _stick(proj, *, bsz, seq, tq):
    rows = proj.shape[0]
    nq = seq // tq
    q_spec, k_spec, v_spec, o_spec = _seq_mixer_specs(mixer=1, nq=nq, tq=tq, seq=seq)
    return pl.pallas_call(
        functools.partial(_stick_body, tq=tq),
        grid=(bsz, PAIRS_PER_MIXER, nq),
        in_specs=[q_spec, k_spec, v_spec],
        out_specs=o_spec,
        out_shape=jax.ShapeDtypeStruct((rows, MIXER_WIDTH), BF16),
        compiler_params=_SEQ_PARAMS,
        name="stick",
    )(proj, proj, proj)


def _outproj_body(o1_ref, o4_ref, o16_ref, l1_ref, l4_ref, l16_ref, yb_ref, yc_ref, yd_ref,
                  gate_ref, x_ref, w_ref, out_ref, y_scr, o_scr, l_scr, *, tm):
    j = pl.program_id(1)

    @pl.when(j == 0)
    def _gate():
        for slot, (dil, o_ref, l_ref) in enumerate(zip(STRIDED_DILATIONS, (o4_ref, o16_ref), (l4_ref, l16_ref))):
            n = tm // dil
            for r in range(dil):
                for c in range(LANE_BLOCKS_PER_MIXER):
                    cols = slice(r * MIXER_WIDTH + c * LANES, r * MIXER_WIDTH + (c + 1) * LANES)
                    o_scr[slot, c, pl.ds(r, n, stride=dil), :] = o_ref[:, cols].astype(F32)
                    l_scr[slot, c, pl.ds(r, n, stride=dil), :] = l_ref[:, cols]

        def natural(scr, slot):
            return jnp.concatenate([scr[slot, c] for c in range(LANE_BLOCKS_PER_MIXER)], axis=1)

        l1, l4, l16 = l1_ref[...], natural(l_scr, 0), natural(l_scr, 1)
        m = jnp.maximum(jnp.maximum(l1, l4), l16)
        e1, e4, e16 = jnp.exp2(l1 - m), jnp.exp2(l4 - m), jnp.exp2(l16 - m)
        ya = (e1 * o1_ref[...].astype(F32) + e4 * natural(o_scr, 0)
              + e16 * natural(o_scr, 1)) / (e1 + e4 + e16)
        parts = (ya, yb_ref[...].astype(F32), yc_ref[...].astype(F32), yd_ref[...].astype(F32))
        for mxr, y in enumerate(parts):
            g = gate_ref[:, mxr * MIXER_WIDTH:(mxr + 1) * MIXER_WIDTH].astype(F32)
            silu = g / (1.0 + jnp.exp(-g))
            y_scr[:, mxr * MIXER_WIDTH:(mxr + 1) * MIXER_WIDTH] = (y * silu).astype(BF16)

    out_ref[...] = x_ref[...] + _dot(y_scr[...], w_ref[...])


def _outproj(seg_o, seg_lse, yb, yc, yd, proj, x2, w_out, *, tm, tn):
    rows, d = x2.shape
    row_blk = lambda i, j: (i, 0)
    mix_spec = pl.BlockSpec((tm, MIXER_WIDTH), row_blk)
    seg_specs = [pl.BlockSpec((tm // dil, dil * MIXER_WIDTH), row_blk) for dil in DILATIONS]
    return pl.pallas_call(
        functools.partial(_outproj_body, tm=tm),
        grid=(rows // tm, d // tn),
        in_specs=seg_specs + seg_specs + [mix_spec] * 3 + [
            pl.BlockSpec((tm, MIX_WIDTH), lambda i, j: (i, N_SECTIONS - 1)),
            pl.BlockSpec((tm, tn), lambda i, j: (i, j)),
            pl.BlockSpec((MIX_WIDTH, tn), lambda i, j: (0, j)),
        ],
        out_specs=pl.BlockSpec((tm, tn), lambda i, j: (i, j)),
        out_shape=jax.ShapeDtypeStruct((rows, d), F32),
        scratch_shapes=[pltpu.VMEM((tm, MIX_WIDTH), BF16),
                        pltpu.VMEM((len(STRIDED_DILATIONS), LANE_BLOCKS_PER_MIXER, tm, LANES), F32),
                        pltpu.VMEM((len(STRIDED_DILATIONS), LANE_BLOCKS_PER_MIXER, tm, LANES), F32)],
        compiler_params=pltpu.CompilerParams(
            dimension_semantics=("arbitrary", "arbitrary"), vmem_limit_bytes=VMEM_LIMIT),
        name="outproj",
    )(*seg_o, *seg_lse, yb, yc, yd, proj, x2, w_out)


def _rope_tables(seq):
    inv = 1.0 / (ROPE_THETA ** (jnp.arange(0, HEAD_DIM, 2, dtype=F32) / HEAD_DIM))
    ang = jnp.arange(seq, dtype=F32)[:, None] * inv[None, :]
    cos, sin = jnp.cos(ang), jnp.sin(ang)
    reps = LANES // HEAD_DIM
    cos_t = jnp.tile(jnp.concatenate([cos, cos], axis=1), (1, reps))
    sin_t = jnp.tile(jnp.concatenate([-sin, sin], axis=1), (1, reps))
    return cos_t, sin_t


def _gain_table(qn, kn):
    ones = jnp.ones((HEAD_DIM,), F32)
    per_mixer_q = (qn[0], ones, qn[1], qn[2])
    per_mixer_k = (kn[0], ones, kn[1], kn[2])
    blocks = list(per_mixer_q) + list(per_mixer_k) + [ones] * (2 * N_MIXERS)
    tab = jnp.stack([jnp.tile(g.astype(F32), HEADS_PER_MIXER) for g in blocks])
    return tab[:, None, :]


def kernel(x, norm_gain, w_in, q_norm_gain, k_norm_gain, forget_bias, w_out):
    bsz, seq, d = x.shape
    depth = w_in.shape[0]
    rows = bsz * seq
    tm = min(512, seq)
    tq_seq = min(QUERY_TILE, seq)
    cos_t, sin_t = _rope_tables(seq)
    head_of_lane = jnp.arange(MIXER_WIDTH) // HEAD_DIM
    mavg = jnp.where(head_of_lane[:, None] == head_of_lane[None, :], 1.0 / HEAD_DIM, 0.0).astype(BF16)
    tri = jnp.tril(jnp.ones((256, 256), F32)).astype(BF16)

    def natural_col(section, r, p):
        return section * COL_BLOCKS_PER_SECTION + p

    def strided_col(section, r, p):
        return r * PAIRS_PER_MIXER + p

    x2 = x.reshape(rows, d)
    for layer in range(depth):
        w_l = w_in[layer]
        w_main = w_l[:, :PROJ_COLS].astype(BF16)
        wf = jnp.pad(w_l[:, PROJ_COLS:], ((0, 0), (0, LANES - HEADS_PER_MIXER)))
        wf_hi = wf.astype(BF16)
        wf_lo = (wf - wf_hi.astype(F32)).astype(BF16)
        wf_cat = jnp.concatenate([wf_hi, wf_lo], axis=1)
        fb_pad = jnp.pad(forget_bias[layer].astype(F32), (0, LANES - HEADS_PER_MIXER))[None, :]
        gain_tab = _gain_table(q_norm_gain[layer], k_norm_gain[layer])

        proj, cum, *strided = _inproj(x2, norm_gain[layer][None, :].astype(F32), w_main, wf_cat, fb_pad,
                                      gain_tab, cos_t, sin_t, mavg, tri, seq=seq, tm=tm)
        seg = [_band_segment(proj, proj, proj, natural_col, bsz=bsz, seq=seq, dil=1, tq=512)]
        for di, dil in enumerate(STRIDED_DILATIONS):
            qd, kd, vd = strided[3 * di:3 * di + 3]
            seg.append(_band_segment(qd, kd, vd, strided_col, bsz=bsz, seq=seq, dil=dil, tq=512))
        yb = _stick(proj, bsz=bsz, seq=seq, tq=STICK_TILE)
        yc = _moba(proj, bsz=bsz, seq=seq, tq=tq_seq, tk=KEY_TILE)
        nt = seq // KEY_TILE
        tile_end_sums = cum.reshape(bsz, nt, KEY_TILE, LANES)[:, :, KEY_TILE - 1, :HEADS_PER_MIXER]
        tile_end_sums = tile_end_sums.transpose(0, 2, 1)
        score_bound = (1.01 * HEAD_DIM * SCALE * LOG2E * jnp.max(jnp.abs(q_norm_gain[layer, 2]))
                       * jnp.max(jnp.abs(k_norm_gain[layer, 2]))).astype(F32).reshape(1)
        yd = _fox(proj, cum, tile_end_sums, score_bound, bsz=bsz, seq=seq, tq=tq_seq, tk=KEY_TILE)
        x2 = _outproj([s[0] for s in seg], [s[1] for s in seg], yb, yc, yd, proj, x2,
                      w_out[layer].astype(BF16), tm=tm, tn=min(1024, d))
    return x2.reshape(bsz, seq, d)
```

```python
import functools

import jax
import jax.numpy as jnp
from jax import lax
from jax.experimental import pallas as pl
from jax.experimental.pallas import tpu as pltpu

F32 = jnp.float32
BF16 = jnp.bfloat16

HEAD_DIM = 64
HALF_DIM = HEAD_DIM // 2
LANES = 128
N_MIXERS = 4
HEADS_PER_MIXER = 8
PAIRS_PER_MIXER = HEADS_PER_MIXER // 2
MIXER_WIDTH = HEADS_PER_MIXER * HEAD_DIM
LANE_BLOCKS_PER_MIXER = MIXER_WIDTH // LANES
MIX_WIDTH = N_MIXERS * MIXER_WIDTH
N_SECTIONS = 4
PROJ_COLS = N_SECTIONS * MIX_WIDTH
COL_BLOCKS_PER_SECTION = MIX_WIDTH // LANES
ROPE_THETA = 10000.0
RMS_EPS = 1e-6
SCALE = HEAD_DIM ** -0.5
LOG2E = 1.4426950408889634
NEG = -1e30
DILATIONS = (1, 4, 16)
STRIDED_DILATIONS = DILATIONS[1:]
BAND = 128
MOBA_BLOCK = 256
MOBA_TOPK = 3
QUERY_TILE = 1024
KEY_TILE = 1024
STICK_TILE = 256
OUT_ROW_TILE = 256
UNDERFLOW_LOG2 = -160.0
VMEM_LIMIT = 56 * 1024 * 1024

_NT = (((1,), (1,)), ((), ()))


def _dot(a, b):
    return jnp.dot(a, b, preferred_element_type=F32)


def _dot_nt(a, b):
    return lax.dot_general(a, b, _NT, preferred_element_type=F32)


def _split2(x):
    hi = x.astype(BF16)
    lo = (x - hi.astype(F32)).astype(BF16)
    return hi, lo


def _split3(x):
    b1 = x.astype(BF16)
    r1 = x - b1.astype(F32)
    b2 = r1.astype(BF16)
    r2 = r1 - b2.astype(F32)
    return b1, b2, r2.astype(BF16)


def _head_lane_mask(shape, h):
    lane = lax.broadcasted_iota(jnp.int32, shape, len(shape) - 1)
    return (lane >= h * HEAD_DIM) & (lane < (h + 1) * HEAD_DIM)


def _own_head(x, h):
    return jnp.where(_head_lane_mask(x.shape, h), x, jnp.zeros_like(x))


def _lane_column(x, n):
    lane = lax.broadcasted_iota(jnp.int32, x.shape, 1)
    return jnp.sum(jnp.where(lane == n, x, 0.0), axis=1, keepdims=True)


def _inproj_body(x_ref, g_ref, w_ref, wf_ref, fb_ref, gain_ref, cos_ref, sin_ref, mavg_ref, tri_ref,
                 proj_ref, cum_ref, *rest, blocks_per_batch, tm):
    dil_refs = rest[:3 * len(STRIDED_DILATIONS)]
    h_scr, carry_scr, dil_scr = rest[3 * len(STRIDED_DILATIONS):]
    i = pl.program_id(0)
    j = pl.program_id(1)

    @pl.when(j == 0)
    def _prologue():
        x = x_ref[...]
        ms = jnp.mean(x * x, axis=-1, keepdims=True)
        h = x * lax.rsqrt(ms + RMS_EPS) * g_ref[...]
        h_hi, h_lo = _split2(h)
        h_scr[...] = h_hi
        wf = wf_ref[...]
        t = _dot(h_hi, wf)
        u = _dot(h_lo, wf[:, :LANES])
        logit = t[:, :LANES] + t[:, LANES:] + u + fb_ref[...]
        lf = jnp.minimum(logit, 0.0) - jnp.log1p(jnp.exp(-jnp.abs(logit)))

        @pl.when(i % blocks_per_batch == 0)
        def _():
            carry_scr[...] = jnp.zeros_like(carry_scr)

        carry = carry_scr[...]
        tri = tri_ref[...]
        sub = tri.shape[0]
        for r in range(tm // sub):
            b1, b2, b3 = _split3(lf[r * sub:(r + 1) * sub])
            c = _dot(tri, jnp.concatenate([b1, b2, b3], axis=1))
            c = c[:, :LANES] + c[:, LANES:2 * LANES] + c[:, 2 * LANES:] + carry
            cum_ref[r * sub:(r + 1) * sub, :] = c
            carry = c[sub - 1:sub, :]
        carry_scr[...] = carry

    acc = _dot(h_scr[...], w_ref[...])
    sec = j // N_MIXERS
    mix = j % N_MIXERS
    normed = jnp.logical_and(sec < 2, mix != 1)
    roped = jnp.logical_or(mix == 0, mix == 2)
    scale = jnp.where(sec == 0, SCALE * LOG2E, 1.0).astype(F32)

    @pl.when(jnp.logical_not(normed))
    def _plain():
        proj_ref[...] = (acc * scale).astype(BF16)

    @pl.when(normed)
    def _normed():
        hi, lo = _split2(acc * acc)
        mavg = mavg_ref[...]
        ms = _dot(hi, mavg) + _dot(lo, mavg)
        t = acc * lax.rsqrt(ms + RMS_EPS) * (gain_ref[0] * scale)

        @pl.when(jnp.logical_not(roped))
        def _():
            proj_ref[...] = t.astype(BF16)

        @pl.when(roped)
        def _():
            cosv = cos_ref[...]
            sinv = sin_ref[...]
            lane = lax.broadcasted_iota(jnp.int32, (tm, LANES), 1)
            first_half = (lane % HEAD_DIM) < HALF_DIM
            for c in range(MIXER_WIDTH // LANES):
                tc = t[:, c * LANES:(c + 1) * LANES]
                partner = jnp.where(first_half,
                                    pltpu.roll(tc, LANES - HALF_DIM, 1),
                                    pltpu.roll(tc, HALF_DIM, 1))
                proj_ref[:, c * LANES:(c + 1) * LANES] = (tc * cosv + partner * sinv).astype(BF16)

    @pl.when(jnp.logical_and(mix == 0, sec < 3))
    def _dilated():
        for c in range(LANE_BLOCKS_PER_MIXER):
            dil_scr[c] = proj_ref[:, c * LANES:(c + 1) * LANES].astype(F32)
        for section in range(3):
            @pl.when(sec == section)
            def _():
                for di, dil in enumerate(STRIDED_DILATIONS):
                    ref = dil_refs[3 * di + section]
                    n = tm // dil
                    for r in range(dil):
                        for c in range(LANE_BLOCKS_PER_MIXER):
                            lo = r * MIXER_WIDTH + c * LANES
                            ref[:, lo:lo + LANES] = dil_scr[c, pl.ds(r, n, stride=dil), :].astype(BF16)


def _inproj(x2, g, w_main, wf_cat, fb_pad, gain_tab, cos_t, sin_t, mavg, tri, *, seq, tm):
    rows, d = x2.shape
    blocks_per_batch = seq // tm
    n_col = PROJ_COLS // MIXER_WIDTH
    body = functools.partial(_inproj_body, blocks_per_batch=blocks_per_batch, tm=tm)
    dil_specs, dil_shapes = [], []
    for dil in STRIDED_DILATIONS:
        for _ in range(3):
            dil_specs.append(pl.BlockSpec((tm // dil, dil * MIXER_WIDTH), lambda i, j: (i, 0)))
            dil_shapes.append(jax.ShapeDtypeStruct((rows // dil, dil * MIXER_WIDTH), BF16))
    return pl.pallas_call(
        body,
        grid=(rows // tm, n_col),
        in_specs=[
            pl.BlockSpec((tm, d), lambda i, j: (i, 0)),
            pl.BlockSpec((1, d), lambda i, j: (0, 0)),
            pl.BlockSpec((d, MIXER_WIDTH), lambda i, j: (0, j)),
            pl.BlockSpec((d, 2 * LANES), lambda i, j: (0, 0)),
            pl.BlockSpec((1, LANES), lambda i, j: (0, 0)),
            pl.BlockSpec((1, 1, MIXER_WIDTH), lambda i, j: (j, 0, 0)),
            pl.BlockSpec((tm, LANES), lambda i, j: (i % blocks_per_batch, 0)),
            pl.BlockSpec((tm, LANES), lambda i, j: (i % blocks_per_batch, 0)),
            pl.BlockSpec((MIXER_WIDTH, MIXER_WIDTH), lambda i, j: (0, 0)),
            pl.BlockSpec(tri.shape, lambda i, j: (0, 0)),
        ],
        out_specs=[
            pl.BlockSpec((tm, MIXER_WIDTH), lambda i, j: (i, j)),
            pl.BlockSpec((tm, LANES), lambda i, j: (i, 0)),
        ] + dil_specs,
        out_shape=[
            jax.ShapeDtypeStruct((rows, PROJ_COLS), BF16),
            jax.ShapeDtypeStruct((rows, LANES), F32),
        ] + dil_shapes,
        scratch_shapes=[pltpu.VMEM((tm, d), BF16), pltpu.VMEM((1, LANES), F32),
                        pltpu.VMEM((LANE_BLOCKS_PER_MIXER, tm, LANES), F32)],
        compiler_params=pltpu.CompilerParams(
            dimension_semantics=("arbitrary", "arbitrary"), vmem_limit_bytes=VMEM_LIMIT),
        name="inproj",
    )(x2, g, w_main, wf_cat, fb_pad, gain_tab, cos_t, sin_t, mavg, tri)


def _band_body(q_ref, k_ref, v_ref, kp_ref, vp_ref, o_ref, lse_ref, *, tq):
    first = pl.program_id(3) == 0
    q = q_ref[...]
    k = k_ref[...]
    v = v_ref[...]
    row = lax.broadcasted_iota(jnp.int32, (BAND, 2 * BAND), 0)
    col = lax.broadcasted_iota(jnp.int32, (BAND, 2 * BAND), 1)
    in_band = jnp.logical_or(jnp.logical_and(col < BAND, col >= row),
                             jnp.logical_and(col >= BAND, col - BAND <= row))
    lane = lax.broadcasted_iota(jnp.int32, (BAND, LANES), 1)
    head0 = lane < HEAD_DIM
    for c in range(tq // BAND):
        qc = q[c * BAND:(c + 1) * BAND]
        if c == 0:
            kw = jnp.concatenate([kp_ref[...], k[:BAND]], axis=0)
            vw = jnp.concatenate([vp_ref[...], v[:BAND]], axis=0)
            mask = jnp.logical_and(in_band, jnp.logical_or(col >= BAND, jnp.logical_not(first)))
        else:
            kw = k[(c - 1) * BAND:(c + 1) * BAND]
            vw = v[(c - 1) * BAND:(c + 1) * BAND]
            mask = in_band
        outs, lses = [], []
        for h in range(2):
            s = jnp.where(mask, _dot_nt(_own_head(qc, h), kw), NEG)
            m = jnp.max(s, axis=1, keepdims=True)
            p = jnp.exp2(s - m)
            l = jnp.sum(p, axis=1, keepdims=True)
            outs.append(_dot(p.astype(BF16), vw) / l)
            lses.append(m + jnp.log2(l))
        o_ref[c * BAND:(c + 1) * BAND, :] = jnp.where(head0, outs[0], outs[1]).astype(BF16)
        lse_ref[c * BAND:(c + 1) * BAND, :] = jnp.where(head0, lses[0], lses[1])


def _band_segment(q_arr, k_arr, v_arr, col_of, *, bsz, seq, dil, tq):
    sub_rows = q_arr.shape[0]
    sub_len = seq // dil
    tq = min(tq, sub_len)
    nq = sub_len // tq
    band_per_tq = tq // BAND

    def cur(section):
        return pl.BlockSpec((tq, LANES), lambda b, r, p, i: (b * nq + i, col_of(section, r, p)))

    def prev(section):
        return pl.BlockSpec(
            (BAND, LANES),
            lambda b, r, p, i: (jnp.maximum((b * nq + i) * band_per_tq - 1, 0), col_of(section, r, p)))

    out_spec = pl.BlockSpec((tq, LANES), lambda b, r, p, i: (b * nq + i, r * PAIRS_PER_MIXER + p))
    return pl.pallas_call(
        functools.partial(_band_body, tq=tq),
        grid=(bsz, dil, PAIRS_PER_MIXER, nq),
        in_specs=[cur(0), cur(1), cur(2), prev(1), prev(2)],
        out_specs=[out_spec, out_spec],
        out_shape=[
            jax.ShapeDtypeStruct((sub_rows, dil * MIXER_WIDTH), BF16),
            jax.ShapeDtypeStruct((sub_rows, dil * MIXER_WIDTH), F32),
        ],
        compiler_params=pltpu.CompilerParams(
            dimension_semantics=("arbitrary",) * 4, vmem_limit_bytes=VMEM_LIMIT),
        name=f"band_d{dil}",
    )(q_arr, k_arr, v_arr, k_arr, v_arr)


def _online_init(tq):
    return (jnp.full((tq, 1), NEG, F32), jnp.zeros((tq, 1), F32), jnp.zeros((tq, LANES), F32))


def _online_step(carry, s, v_blk):
    m, l, acc = carry
    m_new = jnp.maximum(m, jnp.max(s, axis=1, keepdims=True))
    alpha = jnp.exp2(m - m_new)
    p = jnp.exp2(s - m_new)
    l = alpha * l + jnp.sum(p, axis=1, keepdims=True)
    acc = alpha * acc + _dot(p.astype(BF16), v_blk)
    return m_new, l, acc


def _online_finish(carry):
    _, l, acc = carry
    return acc / l


def _shifted_init(tq):
    return (jnp.zeros((tq, LANES), F32), jnp.zeros((tq, LANES), F32))


def _shifted_step(carry, s, v_blk):
    lane_sums, acc = carry
    p = jnp.exp2(s)
    lane_sums = lane_sums + sum(p[:, c * LANES:(c + 1) * LANES] for c in range(p.shape[1] // LANES))
    return lane_sums, acc + _dot(p.astype(BF16), v_blk)


def _shifted_finish(carry):
    lane_sums, acc = carry
    return acc / jnp.sum(lane_sums, axis=1, keepdims=True)


_ONLINE = (_online_init, _online_step, _online_finish)
_SHIFTED = (_shifted_init, _shifted_step, _shifted_finish)
SHIFTED_MAX_SPAN = 100.0


def _attend_tile(q_aug, k_tiles, v_blk, state, step, key_offset=None):
    new = []
    for h in range(2):
        s = _dot_nt(q_aug[h], k_tiles[h])
        if key_offset is not None:
            row = lax.broadcasted_iota(jnp.int32, s.shape, 0)
            col = lax.broadcasted_iota(jnp.int32, s.shape, 1)
            s = jnp.where(col + key_offset <= row, s, NEG)
        new.append(step(state[h], s, v_blk))
    return tuple(new)


def _attend_diagonal(tile, iq, tq, tk, init):
    state = (init(tq), init(tq))
    for d in range(tq // tk):
        state = tile(iq * (tq // tk) + d, state, d * tk)
    return state


def _score_bound(q_gain, k_gain):
    bound = 1.02 * HEAD_DIM * SCALE * LOG2E * jnp.max(jnp.abs(q_gain)) * jnp.max(jnp.abs(k_gain))
    return bound.astype(BF16).astype(F32).reshape(1)


def _merge_heads(outs):
    lane = lax.broadcasted_iota(jnp.int32, outs[0].shape, 1)
    return jnp.where(lane < HEAD_DIM, outs[0], outs[1])


def _seq_mixer_specs(*, mixer, nq, tq, seq):
    def colblk(section, p):
        return section * COL_BLOCKS_PER_SECTION + mixer * PAIRS_PER_MIXER + p

    q_spec = pl.BlockSpec((tq, LANES), lambda b, p, i: (b * nq + i, colblk(0, p)))
    k_spec = pl.BlockSpec((seq, LANES), lambda b, p, i: (b, colblk(1, p)))
    v_spec = pl.BlockSpec((seq, LANES), lambda b, p, i: (b, colblk(2, p)))
    o_spec = pl.BlockSpec((tq, LANES), lambda b, p, i: (b * nq + i, p))
    return q_spec, k_spec, v_spec, o_spec


_SEQ_PARAMS = pltpu.CompilerParams(
    dimension_semantics=("arbitrary",) * 3, vmem_limit_bytes=VMEM_LIMIT)


def _fox_augment(x, cum, head, h, key_side, bound):
    lane = lax.broadcasted_iota(jnp.int32, x.shape, 1)
    g = jnp.broadcast_to(_lane_column(cum, head) * LOG2E, x.shape)
    g1, g2, g3 = (piece.astype(F32) for piece in _split3(g))
    one = jnp.ones_like(g1)
    pieces = (one, one, one, -g1, -g2, -g3, one) if key_side else (g1, g2, g3, one, one, one, -bound * one)
    base = HEAD_DIM * (1 - h)
    aug = jnp.zeros_like(g1)
    for n, piece in enumerate(pieces):
        aug = jnp.where(lane == base + n, piece, aug)
    return jnp.where(_head_lane_mask(x.shape, h), x.astype(F32), aug).astype(BF16)


def _fox_body(fend_ref, bound_ref, q_ref, k_ref, v_ref, cumq_ref, cumk_ref, o_ref, kaug_scr, *, tq, tk, seq):
    b = pl.program_id(0)
    p = pl.program_id(1)
    iq = pl.program_id(2)
    bound = bound_ref[0]

    @pl.when(iq == 0)
    def _augment_keys():
        def chunk(c, _):
            rows = pl.ds(pl.multiple_of(c * tk, tk), tk)
            for h in range(2):
                kaug_scr[h, rows, :] = _fox_augment(k_ref[rows, :], cumk_ref[rows, :], 2 * p + h, h, True, bound)
            return 0
        lax.fori_loop(0, seq // tk, chunk, 0)

    q = q_ref[...]
    cumq = cumq_ref[...]
    q_aug = [_fox_augment(q, cumq, 2 * p + h, h, False, bound) for h in range(2)]

    first_past = iq * (tq // tk) - 1
    last = jnp.maximum(first_past, 0)

    def reaches(j):
        alive = [LOG2E * (fend_ref[b, 2 * p + h, last] - fend_ref[b, 2 * p + h, j])
                 + 2.0 * bound > UNDERFLOW_LOG2 for h in range(2)]
        return jnp.logical_or(alive[0], alive[1])

    def attend(scheme):
        init, step, finish = scheme

        def tile(j, state, key_offset=None):
            rows = pl.ds(pl.multiple_of(j * tk, tk), tk)
            return _attend_tile(q_aug, [kaug_scr[h, rows, :] for h in range(2)], v_ref[rows, :], state,
                                step, key_offset)

        def cond(loop):
            j, _ = loop
            return jnp.logical_and(j >= 0, reaches(jnp.maximum(j, 0)))

        def body(loop):
            j, state = loop
            return j - 1, tile(j, state)

        _, state = lax.while_loop(cond, body, (first_past, _attend_diagonal(tile, iq, tq, tk, init)))
        o_ref[...] = _merge_heads([finish(st) for st in state]).astype(BF16)

    shifted_ok = 2.0 * bound <= SHIFTED_MAX_SPAN
    pl.when(shifted_ok)(functools.partial(attend, _SHIFTED))
    pl.when(jnp.logical_not(shifted_ok))(functools.partial(attend, _ONLINE))


def _fox(proj, cum, tile_end_sums, score_bound, *, bsz, seq, tq, tk):
    rows = proj.shape[0]
    nq = seq // tq
    q_spec, k_spec, v_spec, o_spec = _seq_mixer_specs(mixer=3, nq=nq, tq=tq, seq=seq)
    smem = pl.BlockSpec(memory_space=pltpu.SMEM)
    return pl.pallas_call(
        functools.partial(_fox_body, tq=tq, tk=tk, seq=seq),
        grid=(bsz, PAIRS_PER_MIXER, nq),
        in_specs=[
            smem, smem, q_spec, k_spec, v_spec,
            pl.BlockSpec((tq, LANES), lambda b, p, i: (b * nq + i, 0)),
            pl.BlockSpec((seq, LANES), lambda b, p, i: (b, 0)),
        ],
        out_specs=o_spec,
        out_shape=jax.ShapeDtypeStruct((rows, MIXER_WIDTH), BF16),
        scratch_shapes=[pltpu.VMEM((2, seq, LANES), BF16)],
        compiler_params=_SEQ_PARAMS,
        name="fox",
    )(tile_end_sums, score_bound, proj, proj, proj, cum, cum)


def _moba_body(bound_ref, q_ref, k_ref, v_ref, o_ref, kmean_scr, kaug_scr, *, tq, tk, seq):
    iq = pl.program_id(2)
    nblk = seq // MOBA_BLOCK
    blocks_per_tile = tq // MOBA_BLOCK
    bound = bound_ref[0]

    @pl.when(iq == 0)
    def _prepare_keys():
        r = lax.broadcasted_iota(jnp.int32, (LANES, seq), 0)
        c = lax.broadcasted_iota(jnp.int32, (LANES, seq), 1)
        member = jnp.where(c // MOBA_BLOCK == r % HEAD_DIM, 1.0 / MOBA_BLOCK, 0.0).astype(BF16)
        hi, lo = _split2(_dot(member, k_ref[...]))
        kmean_scr[0] = hi
        kmean_scr[1] = lo

        def chunk(c, _):
            rows = pl.ds(pl.multiple_of(c * tk, tk), tk)
            k = k_ref[rows, :].astype(F32)
            lane = lax.broadcasted_iota(jnp.int32, (tk, LANES), 1)
            blk = (c * tk + lax.broadcasted_iota(jnp.int32, (tk, LANES), 0)) // MOBA_BLOCK
            for h in range(2):
                spare = lane - HEAD_DIM * (1 - h)
                onehot = jnp.where(jnp.logical_or(spare == blk, spare == HEAD_DIM - 1), 1.0, 0.0)
                kaug_scr[h, rows, :] = jnp.where(_head_lane_mask(k.shape, h), k, onehot).astype(BF16)
            return 0
        lax.fori_loop(0, seq // tk, chunk, 0)

    q = q_ref[...]
    slot = lax.broadcasted_iota(jnp.int32, (LANES, tq), 0)
    qblk = blocks_per_tile * iq + lax.broadcasted_iota(jnp.int32, (LANES, tq), 1) // MOBA_BLOCK
    q_aug = []
    for h in range(2):
        qh = _own_head(q, h)
        gate = _dot_nt(kmean_scr[0], qh) + _dot_nt(kmean_scr[1], qh)
        blk = slot - HEAD_DIM * (1 - h)
        past = jnp.logical_and(blk >= 0, blk < qblk)
        work = jnp.where(past, gate, NEG)
        bias = jnp.where(blk == qblk, 0.0, NEG)
        for _ in range(min(MOBA_TOPK, nblk)):
            best = jnp.max(work, axis=0, keepdims=True)
            idx = jnp.min(jnp.where(work == best, slot, LANES), axis=0, keepdims=True)
            hit = slot == idx
            bias = jnp.where(hit, jnp.where(past, 0.0, bias), bias)
            work = jnp.where(hit, -jnp.inf, work)
        bias = jnp.where(blk == HEAD_DIM - 1, -bound, bias)
        q_aug.append(jnp.where(_head_lane_mask(q.shape, h), q.astype(F32), bias.T).astype(BF16))

    def attend(scheme):
        init, step, finish = scheme

        def tile(j, state, key_offset=None):
            rows = pl.ds(pl.multiple_of(j * tk, tk), tk)
            return _attend_tile(q_aug, [kaug_scr[h, rows, :] for h in range(2)], v_ref[rows, :], state,
                                step, key_offset)

        state = _attend_diagonal(tile, iq, tq, tk, init)
        state = lax.fori_loop(0, iq * (tq // tk), tile, state)
        o_ref[...] = _merge_heads([finish(st) for st in state]).astype(BF16)

    shifted_ok = 2.0 * bound <= SHIFTED_MAX_SPAN
    pl.when(shifted_ok)(functools.partial(attend, _SHIFTED))
    pl.when(jnp.logical_not(shifted_ok))(functools.partial(attend, _ONLINE))


def _moba(proj, score_bound, *, bsz, seq, tq, tk):
    assert tq % MOBA_BLOCK == 0 and seq // MOBA_BLOCK < HEAD_DIM
    rows = proj.shape[0]
    nq = seq // tq
    q_spec, k_spec, v_spec, o_spec = _seq_mixer_specs(mixer=2, nq=nq, tq=tq, seq=seq)
    return pl.pallas_call(
        functools.partial(_moba_body, tq=tq, tk=tk, seq=seq),
        grid=(bsz, PAIRS_PER_MIXER, nq),
        in_specs=[pl.BlockSpec(memory_space=pltpu.SMEM), q_spec, k_spec, v_spec],
        out_specs=o_spec,
        out_shape=jax.ShapeDtypeStruct((rows, MIXER_WIDTH), BF16),
        scratch_shapes=[pltpu.VMEM((2, LANES, LANES), BF16), pltpu.VMEM((2, seq, LANES), BF16)],
        compiler_params=_SEQ_PARAMS,
        name="moba",
    )(score_bound, proj, proj, proj)


def _stick_body(q_ref, k_ref, v_ref, o_ref, *, tq):
    iq = pl.program_id(2)
    q = q_ref[...]
    q_own = [_own_head(q, h) for h in range(2)]
    row = lax.broadcasted_iota(jnp.int32, (tq, tq), 0)
    col = lax.broadcasted_iota(jnp.int32, (tq, tq), 1)
    strictly_past = col < row
    later = jnp.where(row > col, 1.0, 0.0).astype(BF16)

    def block(j, state, diagonal):
        rows = pl.ds(pl.multiple_of(j * tq, tq), tq)
        k_blk = k_ref[rows, :]
        v_blk = v_ref[rows, :]
        new = []
        for h in range(2):
            carry_sum, acc = state[h]
            z = _dot_nt(q_own[h], k_blk)
            softplus = jnp.maximum(z, 0.0) + jnp.log2(1.0 + jnp.exp2(-jnp.abs(z)))
            log_keep = -softplus
            if diagonal:
                log_keep = jnp.where(strictly_past, log_keep, 0.0)
            hi, lo = _split2(log_keep)
            after = _dot(hi, later) + _dot(lo, later) + carry_sum
            log_a = z - softplus + after
            if diagonal:
                log_a = jnp.where(strictly_past, log_a, NEG)
            acc = acc + _dot(jnp.exp2(log_a).astype(BF16), v_blk)
            new.append((carry_sum + jnp.sum(log_keep, axis=1, keepdims=True), acc))
        return tuple(new)

    zero = (jnp.zeros((tq, 1), F32), jnp.zeros((tq, LANES), F32))
    state = block(iq, (zero, zero), True)

    def cond(loop):
        j, state = loop
        alive = jnp.max(jnp.maximum(state[0][0], state[1][0])) > UNDERFLOW_LOG2
        return jnp.logical_and(j >= 0, alive)

    def body(loop):
        j, state = loop
        return j - 1, block(j, state, False)

    _, state = lax.while_loop(cond, body, (iq - 1, state))
    o_ref[...] = _merge_heads([acc for _, acc in state]).astype(BF16)


def _stick(proj, *, bsz, seq, tq):
    rows = proj.shape[0]
    nq = seq // tq
    q_spec, k_spec, v_spec, o_spec = _seq_mixer_specs(mixer=1, nq=nq, tq=tq, seq=seq)
    return pl.pallas_call(
        functools.partial(_stick_body, tq=tq),
        grid=(bsz, PAIRS_PER_MIXER, nq),
        in_specs=[q_spec, k_spec, v_spec],
        out_specs=o_spec,
        out_shape=jax.ShapeDtypeStruct((rows, MIXER_WIDTH), BF16),
        compiler_params=_SEQ_PARAMS,
        name="stick",
    )(proj, proj, proj)


def _outproj_body(o1_ref, o4_ref, o16_ref, l1_ref, l4_ref, l16_ref, yb_ref, yc_ref, yd_ref,
                  gate_ref, x_ref, w_ref, out_ref, y_scr, o_scr, l_scr, *, tm):
    j = pl.program_id(1)

    @pl.when(j == 0)
    def _gate():
        for slot, (dil, o_ref, l_ref) in enumerate(zip(STRIDED_DILATIONS, (o4_ref, o16_ref), (l4_ref, l16_ref))):
            n = tm // dil
            for r in range(dil):
                for c in range(LANE_BLOCKS_PER_MIXER):
                    cols = slice(r * MIXER_WIDTH + c * LANES, r * MIXER_WIDTH + (c + 1) * LANES)
                    o_scr[slot, c, pl.ds(r, n, stride=dil), :] = o_ref[:, cols].astype(F32)
                    l_scr[slot, c, pl.ds(r, n, stride=dil), :] = l_ref[:, cols]

        def natural(scr, slot):
            return jnp.concatenate([scr[slot, c] for c in range(LANE_BLOCKS_PER_MIXER)], axis=1)

        l1, l4, l16 = l1_ref[...], natural(l_scr, 0), natural(l_scr, 1)
        m = jnp.maximum(jnp.maximum(l1, l4), l16)
        e1, e4, e16 = jnp.exp2(l1 - m), jnp.exp2(l4 - m), jnp.exp2(l16 - m)
        ya = (e1 * o1_ref[...].astype(F32) + e4 * natural(o_scr, 0)
              + e16 * natural(o_scr, 1)) / (e1 + e4 + e16)
        parts = (ya, yb_ref[...].astype(F32), yc_ref[...].astype(F32), yd_ref[...].astype(F32))
        for mxr, y in enumerate(parts):
            g = gate_ref[:, mxr * MIXER_WIDTH:(mxr + 1) * MIXER_WIDTH].astype(F32)
            silu = g / (1.0 + jnp.exp(-g))
            y_scr[:, mxr * MIXER_WIDTH:(mxr + 1) * MIXER_WIDTH] = (y * silu).astype(BF16)

    out_ref[...] = x_ref[...] + _dot(y_scr[...], w_ref[...])


def _outproj(seg_o, seg_lse, yb, yc, yd, proj, x2, w_out, *, tm, tn):
    rows, d = x2.shape
    row_blk = lambda i, j: (i, 0)
    mix_spec = pl.BlockSpec((tm, MIXER_WIDTH), row_blk)
    seg_specs = [pl.BlockSpec((tm // dil, dil * MIXER_WIDTH), row_blk) for dil in DILATIONS]
    return pl.pallas_call(
        functools.partial(_outproj_body, tm=tm),
        grid=(rows // tm, d // tn),
        in_specs=seg_specs + seg_specs + [mix_spec] * 3 + [
            pl.BlockSpec((tm, MIX_WIDTH), lambda i, j: (i, N_SECTIONS - 1)),
            pl.BlockSpec((tm, tn), lambda i, j: (i, j)),
            pl.BlockSpec((MIX_WIDTH, tn), lambda i, j: (0, j)),
        ],
        out_specs=pl.BlockSpec((tm, tn), lambda i, j: (i, j)),
        out_shape=jax.ShapeDtypeStruct((rows, d), F32),
        scratch_shapes=[pltpu.VMEM((tm, MIX_WIDTH), BF16),
                        pltpu.VMEM((len(STRIDED_DILATIONS), LANE_BLOCKS_PER_MIXER, tm, LANES), F32),
                        pltpu.VMEM((len(STRIDED_DILATIONS), LANE_BLOCKS_PER_MIXER, tm, LANES), F32)],
        compiler_params=pltpu.CompilerParams(
            dimension_semantics=("arbitrary", "arbitrary"), vmem_limit_bytes=VMEM_LIMIT),
        name="outproj",
    )(*seg_o, *seg_lse, yb, yc, yd, proj, x2, w_out)


def _rope_tables(seq):
    inv = 1.0 / (ROPE_THETA ** (jnp.arange(0, HEAD_DIM, 2, dtype=F32) / HEAD_DIM))
    ang = jnp.arange(seq, dtype=F32)[:, None] * inv[None, :]
    cos, sin = jnp.cos(ang), jnp.sin(ang)
    reps = LANES // HEAD_DIM
    cos_t = jnp.tile(jnp.concatenate([cos, cos], axis=1), (1, reps))
    sin_t = jnp.tile(jnp.concatenate([-sin, sin], axis=1), (1, reps))
    return cos_t, sin_t


def _gain_table(qn, kn):
    ones = jnp.ones((HEAD_DIM,), F32)
    per_mixer_q = (qn[0], ones, qn[1], qn[2])
    per_mixer_k = (kn[0], ones, kn[1], kn[2])
    blocks = list(per_mixer_q) + list(per_mixer_k) + [ones] * (2 * N_MIXERS)
    tab = jnp.stack([jnp.tile(g.astype(F32), HEADS_PER_MIXER) for g in blocks])
    return tab[:, None, :]


def kernel(x, norm_gain, w_in, q_norm_gain, k_norm_gain, forget_bias, w_out):
    bsz, seq, d = x.shape
    depth = w_in.shape[0]
    rows = bsz * seq
    tm = min(512, seq)
    tq_seq = min(QUERY_TILE, seq)
    cos_t, sin_t = _rope_tables(seq)
    head_of_lane = jnp.arange(MIXER_WIDTH) // HEAD_DIM
    mavg = jnp.where(head_of_lane[:, None] == head_of_lane[None, :], 1.0 / HEAD_DIM, 0.0).astype(BF16)
    tri = jnp.tril(jnp.ones((256, 256), F32)).astype(BF16)

    def natural_col(section, r, p):
        return section * COL_BLOCKS_PER_SECTION + p

    def strided_col(section, r, p):
        return r * PAIRS_PER_MIXER + p

    x2 = x.reshape(rows, d)
    for layer in range(depth):
        w_l = w_in[layer]
        w_main = w_l[:, :PROJ_COLS].astype(BF16)
        wf = jnp.pad(w_l[:, PROJ_COLS:], ((0, 0), (0, LANES - HEADS_PER_MIXER)))
        wf_hi = wf.astype(BF16)
        wf_lo = (wf - wf_hi.astype(F32)).astype(BF16)
        wf_cat = jnp.concatenate([wf_hi, wf_lo], axis=1)
        fb_pad = jnp.pad(forget_bias[layer].astype(F32), (0, LANES - HEADS_PER_MIXER))[None, :]
        gain_tab = _gain_table(q_norm_gain[layer], k_norm_gain[layer])

        proj, cum, *strided = _inproj(x2, norm_gain[layer][None, :].astype(F32), w_main, wf_cat, fb_pad,
                                      gain_tab, cos_t, sin_t, mavg, tri, seq=seq, tm=tm)
        seg = [_band_segment(proj, proj, proj, natural_col, bsz=bsz, seq=seq, dil=1, tq=512)]
        for di, dil in enumerate(STRIDED_DILATIONS):
            qd, kd, vd = strided[3 * di:3 * di + 3]
            seg.append(_band_segment(qd, kd, vd, strided_col, bsz=bsz, seq=seq, dil=dil, tq=512))
        yb = _stick(proj, bsz=bsz, seq=seq, tq=STICK_TILE)
        yc = _moba(proj, _score_bound(q_norm_gain[layer, 1], k_norm_gain[layer, 1]),
                   bsz=bsz, seq=seq, tq=tq_seq, tk=KEY_TILE)
        nt = seq // KEY_TILE
        tile_end_sums = cum.reshape(bsz, nt, KEY_TILE, LANES)[:, :, KEY_TILE - 1, :HEADS_PER_MIXER]
        tile_end_sums = tile_end_sums.transpose(0, 2, 1)
        yd = _fox(proj, cum, tile_end_sums, _score_bound(q_norm_gain[layer, 2], k_norm_gain[layer, 2]),
                  bsz=bsz, seq=seq, tq=tq_seq, tk=KEY_TILE)
        x2 = _outproj([s[0] for s in seg], [s[1] for s in seg], yb, yc, yd, proj, x2,
                      w_out[layer].astype(BF16), tm=min(OUT_ROW_TILE, seq), tn=d)
    return x2.reshape(bsz, seq, d)
```

```python
import functools

import jax
import jax.numpy as jnp
from jax import lax
from jax.experimental import pallas as pl
from jax.experimental.pallas import tpu as pltpu

F32 = jnp.float32
BF16 = jnp.bfloat16

HEAD_DIM = 64
HALF_DIM = HEAD_DIM // 2
LANES = 128
N_MIXERS = 4
HEADS_PER_MIXER = 8
PAIRS_PER_MIXER = HEADS_PER_MIXER // 2
MIXER_WIDTH = HEADS_PER_MIXER * HEAD_DIM
LANE_BLOCKS_PER_MIXER = MIXER_WIDTH // LANES
MIX_WIDTH = N_MIXERS * MIXER_WIDTH
N_SECTIONS = 4
PROJ_COLS = N_SECTIONS * MIX_WIDTH
COL_BLOCKS_PER_SECTION = MIX_WIDTH // LANES
ROPE_THETA = 10000.0
RMS_EPS = 1e-6
SCALE = HEAD_DIM ** -0.5
LOG2E = 1.4426950408889634
NEG = -1e30
DILATIONS = (1, 4, 16)
STRIDED_DILATIONS = DILATIONS[1:]
BAND = 128
MOBA_BLOCK = 256
MOBA_TOPK = 3
QUERY_TILE = 1024
KEY_TILE = 1024
STICK_QUERY_TILE = 512
STICK_KEY_TILE = 256
COL_BLOCKS_PER_STEP = 2
OUT_ROW_TILE = 256
UNDERFLOW_LOG2 = -160.0
VMEM_LIMIT = 56 * 1024 * 1024

_NT = (((1,), (1,)), ((), ()))


def _dot(a, b):
    return jnp.dot(a, b, preferred_element_type=F32)


def _dot_nt(a, b):
    return lax.dot_general(a, b, _NT, preferred_element_type=F32)


def _split2(x):
    hi = x.astype(BF16)
    lo = (x - hi.astype(F32)).astype(BF16)
    return hi, lo


def _split3(x):
    b1 = x.astype(BF16)
    r1 = x - b1.astype(F32)
    b2 = r1.astype(BF16)
    r2 = r1 - b2.astype(F32)
    return b1, b2, r2.astype(BF16)


def _head_lane_mask(shape, h):
    lane = lax.broadcasted_iota(jnp.int32, shape, len(shape) - 1)
    return (lane >= h * HEAD_DIM) & (lane < (h + 1) * HEAD_DIM)


def _own_head(x, h):
    return jnp.where(_head_lane_mask(x.shape, h), x, jnp.zeros_like(x))


def _lane_column(x, n):
    lane = lax.broadcasted_iota(jnp.int32, x.shape, 1)
    return jnp.sum(jnp.where(lane == n, x, 0.0), axis=1, keepdims=True)


def _inproj_body(x_ref, g_ref, w_ref, wf_ref, fb_ref, gain_ref, cos_ref, sin_ref, mavg_ref, tri_ref,
                 proj_ref, cum_ref, *rest, blocks_per_batch, tm):
    dil_refs = rest[:3 * len(STRIDED_DILATIONS)]
    h_scr, carry_scr, dil_scr = rest[3 * len(STRIDED_DILATIONS):]
    i = pl.program_id(0)
    j = pl.program_id(1)

    @pl.when(j == 0)
    def _prologue():
        x = x_ref[...]
        ms = jnp.mean(x * x, axis=-1, keepdims=True)
        h = x * lax.rsqrt(ms + RMS_EPS) * g_ref[...]
        h_hi, h_lo = _split2(h)
        h_scr[...] = h_hi
        wf = wf_ref[...]
        t = _dot(h_hi, wf)
        u = _dot(h_lo, wf[:, :LANES])
        logit = t[:, :LANES] + t[:, LANES:] + u + fb_ref[...]
        lf = jnp.minimum(logit, 0.0) - jnp.log1p(jnp.exp(-jnp.abs(logit)))

        @pl.when(i % blocks_per_batch == 0)
        def _():
            carry_scr[...] = jnp.zeros_like(carry_scr)

        carry = carry_scr[...]
        tri = tri_ref[...]
        sub = tri.shape[0]
        for r in range(tm // sub):
            b1, b2, b3 = _split3(lf[r * sub:(r + 1) * sub])
            c = _dot(tri, jnp.concatenate([b1, b2, b3], axis=1))
            c = c[:, :LANES] + c[:, LANES:2 * LANES] + c[:, 2 * LANES:] + carry
            cum_ref[r * sub:(r + 1) * sub, :] = c
            carry = c[sub - 1:sub, :]
        carry_scr[...] = carry

    for sub in range(COL_BLOCKS_PER_STEP):
        _inproj_column_block(j * COL_BLOCKS_PER_STEP + sub, sub * MIXER_WIDTH, h_scr, w_ref, gain_ref.at[sub],
                             cos_ref, sin_ref, mavg_ref, proj_ref, dil_refs, dil_scr, tm=tm)


def _inproj_column_block(jj, col0, h_scr, w_ref, gain_ref, cos_ref, sin_ref, mavg_ref, proj_ref, dil_refs,
                         dil_scr, *, tm):
    def out_cols(c):
        return slice(col0 + c * LANES, col0 + (c + 1) * LANES)

    acc = _dot(h_scr[...], w_ref[:, col0:col0 + MIXER_WIDTH])
    sec = jj // N_MIXERS
    mix = jj % N_MIXERS
    normed = jnp.logical_and(sec < 2, mix != 1)
    roped = jnp.logical_or(mix == 0, mix == 2)
    scale = jnp.where(sec == 0, SCALE * LOG2E, 1.0).astype(F32)

    @pl.when(jnp.logical_not(normed))
    def _plain():
        proj_ref[:, col0:col0 + MIXER_WIDTH] = (acc * scale).astype(BF16)

    @pl.when(normed)
    def _normed():
        hi, lo = _split2(acc * acc)
        mavg = mavg_ref[...]
        ms = _dot(hi, mavg) + _dot(lo, mavg)
        t = acc * lax.rsqrt(ms + RMS_EPS) * (gain_ref[0] * scale)

        @pl.when(jnp.logical_not(roped))
        def _():
            proj_ref[:, col0:col0 + MIXER_WIDTH] = t.astype(BF16)

        @pl.when(roped)
        def _():
            cosv = cos_ref[...]
            sinv = sin_ref[...]
            lane = lax.broadcasted_iota(jnp.int32, (tm, LANES), 1)
            first_half = (lane % HEAD_DIM) < HALF_DIM
            for c in range(LANE_BLOCKS_PER_MIXER):
                tc = t[:, c * LANES:(c + 1) * LANES]
                partner = jnp.where(first_half,
                                    pltpu.roll(tc, LANES - HALF_DIM, 1),
                                    pltpu.roll(tc, HALF_DIM, 1))
                proj_ref[:, out_cols(c)] = (tc * cosv + partner * sinv).astype(BF16)

    @pl.when(jnp.logical_and(mix == 0, sec < 3))
    def _dilated():
        for c in range(LANE_BLOCKS_PER_MIXER):
            dil_scr[c] = proj_ref[:, out_cols(c)].astype(F32)
        for section in range(3):
            @pl.when(sec == section)
            def _():
                for di, dil in enumerate(STRIDED_DILATIONS):
                    ref = dil_refs[3 * di + section]
                    n = tm // dil
                    for r in range(dil):
                        for c in range(LANE_BLOCKS_PER_MIXER):
                            lo = r * MIXER_WIDTH + c * LANES
                            ref[:, lo:lo + LANES] = dil_scr[c, pl.ds(r, n, stride=dil), :].astype(BF16)


def _inproj(x2, g, w_main, wf_cat, fb_pad, gain_tab, cos_t, sin_t, mavg, tri, *, seq, tm):
    rows, d = x2.shape
    blocks_per_batch = seq // tm
    step_cols = COL_BLOCKS_PER_STEP * MIXER_WIDTH
    body = functools.partial(_inproj_body, blocks_per_batch=blocks_per_batch, tm=tm)
    dil_specs, dil_shapes = [], []
    for dil in STRIDED_DILATIONS:
        for _ in range(3):
            dil_specs.append(pl.BlockSpec((tm // dil, dil * MIXER_WIDTH), lambda i, j: (i, 0)))
            dil_shapes.append(jax.ShapeDtypeStruct((rows // dil, dil * MIXER_WIDTH), BF16))
    return pl.pallas_call(
        body,
        grid=(rows // tm, PROJ_COLS // step_cols),
        in_specs=[
            pl.BlockSpec((tm, d), lambda i, j: (i, 0)),
            pl.BlockSpec((1, d), lambda i, j: (0, 0)),
            pl.BlockSpec((d, step_cols), lambda i, j: (0, j)),
            pl.BlockSpec((d, 2 * LANES), lambda i, j: (0, 0)),
            pl.BlockSpec((1, LANES), lambda i, j: (0, 0)),
            pl.BlockSpec((COL_BLOCKS_PER_STEP, 1, MIXER_WIDTH), lambda i, j: (j, 0, 0)),
            pl.BlockSpec((tm, LANES), lambda i, j: (i % blocks_per_batch, 0)),
            pl.BlockSpec((tm, LANES), lambda i, j: (i % blocks_per_batch, 0)),
            pl.BlockSpec((MIXER_WIDTH, MIXER_WIDTH), lambda i, j: (0, 0)),
            pl.BlockSpec(tri.shape, lambda i, j: (0, 0)),
        ],
        out_specs=[
            pl.BlockSpec((tm, step_cols), lambda i, j: (i, j)),
            pl.BlockSpec((tm, LANES), lambda i, j: (i, 0)),
        ] + dil_specs,
        out_shape=[
            jax.ShapeDtypeStruct((rows, PROJ_COLS), BF16),
            jax.ShapeDtypeStruct((rows, LANES), F32),
        ] + dil_shapes,
        scratch_shapes=[pltpu.VMEM((tm, d), BF16), pltpu.VMEM((1, LANES), F32),
                        pltpu.VMEM((LANE_BLOCKS_PER_MIXER, tm, LANES), F32)],
        compiler_params=pltpu.CompilerParams(
            dimension_semantics=("arbitrary", "arbitrary"), vmem_limit_bytes=VMEM_LIMIT),
        name="inproj",
    )(x2, g, w_main, wf_cat, fb_pad, gain_tab, cos_t, sin_t, mavg, tri)


def _band_body(q_ref, k_ref, v_ref, kp_ref, vp_ref, o_ref, lse_ref, *, tq):
    first = pl.program_id(2) == 0
    row = lax.broadcasted_iota(jnp.int32, (BAND, 2 * BAND), 0)
    col = lax.broadcasted_iota(jnp.int32, (BAND, 2 * BAND), 1)
    in_band = jnp.logical_or(jnp.logical_and(col < BAND, col >= row),
                             jnp.logical_and(col >= BAND, col - BAND <= row))
    first_mask = jnp.logical_and(in_band, jnp.logical_or(col >= BAND, jnp.logical_not(first)))
    lane = lax.broadcasted_iota(jnp.int32, (BAND, LANES), 1)
    head0 = lane < HEAD_DIM
    for pair in range(PAIRS_PER_MIXER):
        lanes = slice(pair * LANES, (pair + 1) * LANES)
        for c in range(tq // BAND):
            own = slice(c * BAND, (c + 1) * BAND)
            qc = q_ref[own, lanes]
            if c == 0:
                kw = jnp.concatenate([kp_ref[:, lanes], k_ref[own, lanes]], axis=0)
                vw = jnp.concatenate([vp_ref[:, lanes], v_ref[own, lanes]], axis=0)
                mask = first_mask
            else:
                window = slice((c - 1) * BAND, (c + 1) * BAND)
                kw = k_ref[window, lanes]
                vw = v_ref[window, lanes]
                mask = in_band
            outs, lses = [], []
            for h in range(2):
                s = jnp.where(mask, _dot_nt(_own_head(qc, h), kw), NEG)
                m = jnp.max(s, axis=1, keepdims=True)
                p = jnp.exp2(s - m)
                l = jnp.sum(p, axis=1, keepdims=True)
                outs.append(_dot(p.astype(BF16), vw) / l)
                lses.append(m + jnp.log2(l))
            o_ref[own, lanes] = jnp.where(head0, outs[0], outs[1]).astype(BF16)
            lse_ref[own, lanes] = jnp.where(head0, lses[0], lses[1])


def _band_segment(q_arr, k_arr, v_arr, col_of, *, bsz, seq, dil, tq):
    sub_rows = q_arr.shape[0]
    sub_len = seq // dil
    tq = min(tq, sub_len)
    nq = sub_len // tq
    band_per_tq = tq // BAND

    def cur(section):
        return pl.BlockSpec((tq, MIXER_WIDTH), lambda b, r, i: (b * nq + i, col_of(section, r)))

    def prev(section):
        return pl.BlockSpec(
            (BAND, MIXER_WIDTH),
            lambda b, r, i: (jnp.maximum((b * nq + i) * band_per_tq - 1, 0), col_of(section, r)))

    out_spec = pl.BlockSpec((tq, MIXER_WIDTH), lambda b, r, i: (b * nq + i, r))
    return pl.pallas_call(
        functools.partial(_band_body, tq=tq),
        grid=(bsz, dil, nq),
        in_specs=[cur(0), cur(1), cur(2), prev(1), prev(2)],
        out_specs=[out_spec, out_spec],
        out_shape=[
            jax.ShapeDtypeStruct((sub_rows, dil * MIXER_WIDTH), BF16),
            jax.ShapeDtypeStruct((sub_rows, dil * MIXER_WIDTH), F32),
        ],
        compiler_params=pltpu.CompilerParams(
            dimension_semantics=("arbitrary",) * 3, vmem_limit_bytes=VMEM_LIMIT),
        name=f"band_d{dil}",
    )(q_arr, k_arr, v_arr, k_arr, v_arr)


def _online_init(tq):
    return (jnp.full((tq, 1), NEG, F32), jnp.zeros((tq, 1), F32), jnp.zeros((tq, LANES), F32))


def _online_step(carry, s, v_blk):
    m, l, acc = carry
    m_new = jnp.maximum(m, jnp.max(s, axis=1, keepdims=True))
    alpha = jnp.exp2(m - m_new)
    p = jnp.exp2(s - m_new)
    l = alpha * l + jnp.sum(p, axis=1, keepdims=True)
    acc = alpha * acc + _dot(p.astype(BF16), v_blk)
    return m_new, l, acc


def _online_finish(carry):
    _, l, acc = carry
    return acc / l


def _shifted_init(tq):
    return (jnp.zeros((tq, LANES), F32), jnp.zeros((tq, LANES), F32))


def _shifted_step(carry, s, v_blk):
    lane_sums, acc = carry
    p = jnp.exp2(s)
    lane_sums = lane_sums + sum(p[:, c * LANES:(c + 1) * LANES] for c in range(p.shape[1] // LANES))
    return lane_sums, acc + _dot(p.astype(BF16), v_blk)


def _shifted_finish(carry):
    lane_sums, acc = carry
    return acc / jnp.sum(lane_sums, axis=1, keepdims=True)


_ONLINE = (_online_init, _online_step, _online_finish)
_SHIFTED = (_shifted_init, _shifted_step, _shifted_finish)
SHIFTED_MAX_SPAN = 100.0


def _attend_tile(q_aug, k_tiles, v_blk, state, step, key_offset=None):
    new = []
    for h in range(2):
        s = _dot_nt(q_aug[h], k_tiles[h])
        if key_offset is not None:
            row = lax.broadcasted_iota(jnp.int32, s.shape, 0)
            col = lax.broadcasted_iota(jnp.int32, s.shape, 1)
            s = jnp.where(col + key_offset <= row, s, NEG)
        new.append(step(state[h], s, v_blk))
    return tuple(new)


def _attend_diagonal(tile, iq, tq, tk, init):
    state = (init(tq), init(tq))
    for d in range(tq // tk):
        state = tile(iq * (tq // tk) + d, state, d * tk)
    return state


def _score_bound(q_gain, k_gain):
    bound = 1.02 * HEAD_DIM * SCALE * LOG2E * jnp.max(jnp.abs(q_gain)) * jnp.max(jnp.abs(k_gain))
    return bound.astype(BF16).astype(F32).reshape(1)


def _merge_heads(outs):
    lane = lax.broadcasted_iota(jnp.int32, outs[0].shape, 1)
    return jnp.where(lane < HEAD_DIM, outs[0], outs[1])


def _seq_mixer_specs(*, mixer, nq, tq, seq):
    def colblk(section, p):
        return section * COL_BLOCKS_PER_SECTION + mixer * PAIRS_PER_MIXER + p

    q_spec = pl.BlockSpec((tq, LANES), lambda b, p, i: (b * nq + i, colblk(0, p)))
    k_spec = pl.BlockSpec((seq, LANES), lambda b, p, i: (b, colblk(1, p)))
    v_spec = pl.BlockSpec((seq, LANES), lambda b, p, i: (b, colblk(2, p)))
    o_spec = pl.BlockSpec((tq, LANES), lambda b, p, i: (b * nq + i, p))
    return q_spec, k_spec, v_spec, o_spec


_SEQ_PARAMS = pltpu.CompilerParams(
    dimension_semantics=("arbitrary",) * 3, vmem_limit_bytes=VMEM_LIMIT)


def _fox_augment(x, cum, head, h, key_side, bound):
    lane = lax.broadcasted_iota(jnp.int32, x.shape, 1)
    g = jnp.broadcast_to(_lane_column(cum, head) * LOG2E, x.shape)
    g1, g2, g3 = (piece.astype(F32) for piece in _split3(g))
    one = jnp.ones_like(g1)
    pieces = (one, one, one, -g1, -g2, -g3, one) if key_side else (g1, g2, g3, one, one, one, -bound * one)
    base = HEAD_DIM * (1 - h)
    aug = jnp.zeros_like(g1)
    for n, piece in enumerate(pieces):
        aug = jnp.where(lane == base + n, piece, aug)
    return jnp.where(_head_lane_mask(x.shape, h), x.astype(F32), aug).astype(BF16)


def _fox_body(fend_ref, bound_ref, q_ref, k_ref, v_ref, cumq_ref, cumk_ref, o_ref, kaug_scr, *, tq, tk, seq):
    b = pl.program_id(0)
    p = pl.program_id(1)
    iq = pl.program_id(2)
    bound = bound_ref[0]

    @pl.when(iq == 0)
    def _augment_keys():
        def chunk(c, _):
            rows = pl.ds(pl.multiple_of(c * tk, tk), tk)
            for h in range(2):
                kaug_scr[h, rows, :] = _fox_augment(k_ref[rows, :], cumk_ref[rows, :], 2 * p + h, h, True, bound)
            return 0
        lax.fori_loop(0, seq // tk, chunk, 0)

    q = q_ref[...]
    cumq = cumq_ref[...]
    q_aug = [_fox_augment(q, cumq, 2 * p + h, h, False, bound) for h in range(2)]

    first_past = iq * (tq // tk) - 1
    last = jnp.maximum(first_past, 0)

    def reaches(j):
        alive = [LOG2E * (fend_ref[b, 2 * p + h, last] - fend_ref[b, 2 * p + h, j])
                 + 2.0 * bound > UNDERFLOW_LOG2 for h in range(2)]
        return jnp.logical_or(alive[0], alive[1])

    def attend(scheme):
        init, step, finish = scheme

        def tile(j, state, key_offset=None):
            rows = pl.ds(pl.multiple_of(j * tk, tk), tk)
            return _attend_tile(q_aug, [kaug_scr[h, rows, :] for h in range(2)], v_ref[rows, :], state,
                                step, key_offset)

        def cond(loop):
            j, _ = loop
            return jnp.logical_and(j >= 0, reaches(jnp.maximum(j, 0)))

        def body(loop):
            j, state = loop
            return j - 1, tile(j, state)

        _, state = lax.while_loop(cond, body, (first_past, _attend_diagonal(tile, iq, tq, tk, init)))
        o_ref[...] = _merge_heads([finish(st) for st in state]).astype(BF16)

    shifted_ok = 2.0 * bound <= SHIFTED_MAX_SPAN
    pl.when(shifted_ok)(functools.partial(attend, _SHIFTED))
    pl.when(jnp.logical_not(shifted_ok))(functools.partial(attend, _ONLINE))


def _fox(proj, cum, tile_end_sums, score_bound, *, bsz, seq, tq, tk):
    rows = proj.shape[0]
    nq = seq // tq
    q_spec, k_spec, v_spec, o_spec = _seq_mixer_specs(mixer=3, nq=nq, tq=tq, seq=seq)
    smem = pl.BlockSpec(memory_space=pltpu.SMEM)
    return pl.pallas_call(
        functools.partial(_fox_body, tq=tq, tk=tk, seq=seq),
        grid=(bsz, PAIRS_PER_MIXER, nq),
        in_specs=[
            smem, smem, q_spec, k_spec, v_spec,
            pl.BlockSpec((tq, LANES), lambda b, p, i: (b * nq + i, 0)),
            pl.BlockSpec((seq, LANES), lambda b, p, i: (b, 0)),
        ],
        out_specs=o_spec,
        out_shape=jax.ShapeDtypeStruct((rows, MIXER_WIDTH), BF16),
        scratch_shapes=[pltpu.VMEM((2, seq, LANES), BF16)],
        compiler_params=_SEQ_PARAMS,
        name="fox",
    )(tile_end_sums, score_bound, proj, proj, proj, cum, cum)


def _moba_body(bound_ref, q_ref, k_ref, v_ref, o_ref, kmean_scr, kaug_scr, *, tq, tk, seq):
    iq = pl.program_id(2)
    nblk = seq // MOBA_BLOCK
    blocks_per_tile = tq // MOBA_BLOCK
    bound = bound_ref[0]

    @pl.when(iq == 0)
    def _prepare_keys():
        r = lax.broadcasted_iota(jnp.int32, (LANES, seq), 0)
        c = lax.broadcasted_iota(jnp.int32, (LANES, seq), 1)
        member = jnp.where(c // MOBA_BLOCK == r % HEAD_DIM, 1.0 / MOBA_BLOCK, 0.0).astype(BF16)
        hi, lo = _split2(_dot(member, k_ref[...]))
        kmean_scr[0] = hi
        kmean_scr[1] = lo

        def chunk(c, _):
            rows = pl.ds(pl.multiple_of(c * tk, tk), tk)
            k = k_ref[rows, :].astype(F32)
            lane = lax.broadcasted_iota(jnp.int32, (tk, LANES), 1)
            blk = (c * tk + lax.broadcasted_iota(jnp.int32, (tk, LANES), 0)) // MOBA_BLOCK
            for h in range(2):
                spare = lane - HEAD_DIM * (1 - h)
                onehot = jnp.where(jnp.logical_or(spare == blk, spare == HEAD_DIM - 1), 1.0, 0.0)
                kaug_scr[h, rows, :] = jnp.where(_head_lane_mask(k.shape, h), k, onehot).astype(BF16)
            return 0
        lax.fori_loop(0, seq // tk, chunk, 0)

    q = q_ref[...]
    slot = lax.broadcasted_iota(jnp.int32, (LANES, tq), 0)
    qblk = blocks_per_tile * iq + lax.broadcasted_iota(jnp.int32, (LANES, tq), 1) // MOBA_BLOCK
    q_aug = []
    for h in range(2):
        qh = _own_head(q, h)
        gate = _dot_nt(kmean_scr[0], qh) + _dot_nt(kmean_scr[1], qh)
        blk = slot - HEAD_DIM * (1 - h)
        past = jnp.logical_and(blk >= 0, blk < qblk)
        work = jnp.where(past, gate, NEG)
        bias = jnp.where(blk == qblk, 0.0, NEG)
        for _ in range(min(MOBA_TOPK, nblk)):
            best = jnp.max(work, axis=0, keepdims=True)
            idx = jnp.min(jnp.where(work == best, slot, LANES), axis=0, keepdims=True)
            hit = slot == idx
            bias = jnp.where(hit, jnp.where(past, 0.0, bias), bias)
            work = jnp.where(hit, -jnp.inf, work)
        bias = jnp.where(blk == HEAD_DIM - 1, -bound, bias)
        q_aug.append(jnp.where(_head_lane_mask(q.shape, h), q.astype(F32), bias.T).astype(BF16))

    def attend(scheme):
        init, step, finish = scheme

        def tile(j, state, key_offset=None):
            rows = pl.ds(pl.multiple_of(j * tk, tk), tk)
            return _attend_tile(q_aug, [kaug_scr[h, rows, :] for h in range(2)], v_ref[rows, :], state,
                                step, key_offset)

        state = _attend_diagonal(tile, iq, tq, tk, init)
        state = lax.fori_loop(0, iq * (tq // tk), tile, state)
        o_ref[...] = _merge_heads([finish(st) for st in state]).astype(BF16)

    shifted_ok = 2.0 * bound <= SHIFTED_MAX_SPAN
    pl.when(shifted_ok)(functools.partial(attend, _SHIFTED))
    pl.when(jnp.logical_not(shifted_ok))(functools.partial(attend, _ONLINE))


def _moba(proj, score_bound, *, bsz, seq, tq, tk):
    assert tq % MOBA_BLOCK == 0 and seq // MOBA_BLOCK < HEAD_DIM
    rows = proj.shape[0]
    nq = seq // tq
    q_spec, k_spec, v_spec, o_spec = _seq_mixer_specs(mixer=2, nq=nq, tq=tq, seq=seq)
    return pl.pallas_call(
        functools.partial(_moba_body, tq=tq, tk=tk, seq=seq),
        grid=(bsz, PAIRS_PER_MIXER, nq),
        in_specs=[pl.BlockSpec(memory_space=pltpu.SMEM), q_spec, k_spec, v_spec],
        out_specs=o_spec,
        out_shape=jax.ShapeDtypeStruct((rows, MIXER_WIDTH), BF16),
        scratch_shapes=[pltpu.VMEM((2, LANES, LANES), BF16), pltpu.VMEM((2, seq, LANES), BF16)],
        compiler_params=_SEQ_PARAMS,
        name="moba",
    )(score_bound, proj, proj, proj)


def _stick_body(q_ref, k_ref, v_ref, o_ref, *, tq, tk):
    iq = pl.program_id(2)
    blocks_per_tile = tq // tk
    q = q_ref[...]
    q_own = [_own_head(q, h) for h in range(2)]
    later_row = lax.broadcasted_iota(jnp.int32, (tk, tk), 0)
    later_col = lax.broadcasted_iota(jnp.int32, (tk, tk), 1)
    later = jnp.where(later_row > later_col, 1.0, 0.0).astype(BF16)

    def block(j, state, first_row=None):
        keys = pl.ds(pl.multiple_of(j * tk, tk), tk)
        k_blk = k_ref[keys, :]
        v_blk = v_ref[keys, :]
        rows = slice(first_row or 0, tq)
        n = tq - rows.start
        if first_row is not None:
            strictly_past = (lax.broadcasted_iota(jnp.int32, (n, tk), 1)
                             < lax.broadcasted_iota(jnp.int32, (n, tk), 0))
        new = []
        for h in range(2):
            carry_sum, acc = state[h]
            z = _dot_nt(q_own[h][rows], k_blk)
            softplus = jnp.maximum(z, 0.0) + jnp.log2(1.0 + jnp.exp2(-jnp.abs(z)))
            log_keep = -softplus
            if first_row is not None:
                log_keep = jnp.where(strictly_past, log_keep, 0.0)
            hi, lo = _split2(log_keep)
            after = _dot(hi, later) + _dot(lo, later) + carry_sum[rows]
            log_a = z - softplus + after
            if first_row is not None:
                log_a = jnp.where(strictly_past, log_a, NEG)
            acc_rows = acc[rows] + _dot(jnp.exp2(log_a).astype(BF16), v_blk)
            sum_rows = carry_sum[rows] + jnp.sum(log_keep, axis=1, keepdims=True)
            if rows.start:
                acc_rows = jnp.concatenate([acc[:rows.start], acc_rows], axis=0)
                sum_rows = jnp.concatenate([carry_sum[:rows.start], sum_rows], axis=0)
            new.append((sum_rows, acc_rows))
        return tuple(new)

    zero = (jnp.zeros((tq, 1), F32), jnp.zeros((tq, LANES), F32))
    state = (zero, zero)
    for d in reversed(range(blocks_per_tile)):
        state = block(iq * blocks_per_tile + d, state, d * tk)

    def cond(loop):
        j, state = loop
        alive = jnp.max(jnp.maximum(state[0][0], state[1][0])) > UNDERFLOW_LOG2
        return jnp.logical_and(j >= 0, alive)

    def body(loop):
        j, state = loop
        return j - 1, block(j, state)

    _, state = lax.while_loop(cond, body, (iq * blocks_per_tile - 1, state))
    o_ref[...] = _merge_heads([acc for _, acc in state]).astype(BF16)


def _stick(proj, *, bsz, seq, tq, tk):
    rows = proj.shape[0]
    nq = seq // tq
    q_spec, k_spec, v_spec, o_spec = _seq_mixer_specs(mixer=1, nq=nq, tq=tq, seq=seq)
    return pl.pallas_call(
        functools.partial(_stick_body, tq=tq, tk=tk),
        grid=(bsz, PAIRS_PER_MIXER, nq),
        in_specs=[q_spec, k_spec, v_spec],
        out_specs=o_spec,
        out_shape=jax.ShapeDtypeStruct((rows, MIXER_WIDTH), BF16),
        compiler_params=_SEQ_PARAMS,
        name="stick",
    )(proj, proj, proj)


def _outproj_body(o1_ref, o4_ref, o16_ref, l1_ref, l4_ref, l16_ref, yb_ref, yc_ref, yd_ref,
                  gate_ref, x_ref, w_ref, out_ref, y_scr, o_scr, l_scr, *, tm):
    j = pl.program_id(1)

    @pl.when(j == 0)
    def _gate():
        for slot, (dil, o_ref, l_ref) in enumerate(zip(STRIDED_DILATIONS, (o4_ref, o16_ref), (l4_ref, l16_ref))):
            n = tm // dil
            for r in range(dil):
                for c in range(LANE_BLOCKS_PER_MIXER):
                    cols = slice(r * MIXER_WIDTH + c * LANES, r * MIXER_WIDTH + (c + 1) * LANES)
                    o_scr[slot, c, pl.ds(r, n, stride=dil), :] = o_ref[:, cols].astype(F32)
                    l_scr[slot, c, pl.ds(r, n, stride=dil), :] = l_ref[:, cols]

        def natural(scr, slot):
            return jnp.concatenate([scr[slot, c] for c in range(LANE_BLOCKS_PER_MIXER)], axis=1)

        l1, l4, l16 = l1_ref[...], natural(l_scr, 0), natural(l_scr, 1)
        m = jnp.maximum(jnp.maximum(l1, l4), l16)
        e1, e4, e16 = jnp.exp2(l1 - m), jnp.exp2(l4 - m), jnp.exp2(l16 - m)
        ya = (e1 * o1_ref[...].astype(F32) + e4 * natural(o_scr, 0)
              + e16 * natural(o_scr, 1)) / (e1 + e4 + e16)
        parts = (ya, yb_ref[...].astype(F32), yc_ref[...].astype(F32), yd_ref[...].astype(F32))
        for mxr, y in enumerate(parts):
            g = gate_ref[:, mxr * MIXER_WIDTH:(mxr + 1) * MIXER_WIDTH].astype(F32)
            silu = g / (1.0 + jnp.exp(-g))
            y_scr[:, mxr * MIXER_WIDTH:(mxr + 1) * MIXER_WIDTH] = (y * silu).astype(BF16)

    out_ref[...] = x_ref[...] + _dot(y_scr[...], w_ref[...])


def _outproj(seg_o, seg_lse, yb, yc, yd, proj, x2, w_out, *, tm, tn):
    rows, d = x2.shape
    row_blk = lambda i, j: (i, 0)
    mix_spec = pl.BlockSpec((tm, MIXER_WIDTH), row_blk)
    seg_specs = [pl.BlockSpec((tm // dil, dil * MIXER_WIDTH), row_blk) for dil in DILATIONS]
    return pl.pallas_call(
        functools.partial(_outproj_body, tm=tm),
        grid=(rows // tm, d // tn),
        in_specs=seg_specs + seg_specs + [mix_spec] * 3 + [
            pl.BlockSpec((tm, MIX_WIDTH), lambda i, j: (i, N_SECTIONS - 1)),
            pl.BlockSpec((tm, tn), lambda i, j: (i, j)),
            pl.BlockSpec((MIX_WIDTH, tn), lambda i, j: (0, j)),
        ],
        out_specs=pl.BlockSpec((tm, tn), lambda i, j: (i, j)),
        out_shape=jax.ShapeDtypeStruct((rows, d), F32),
        scratch_shapes=[pltpu.VMEM((tm, MIX_WIDTH), BF16),
                        pltpu.VMEM((len(STRIDED_DILATIONS), LANE_BLOCKS_PER_MIXER, tm, LANES), F32),
                        pltpu.VMEM((len(STRIDED_DILATIONS), LANE_BLOCKS_PER_MIXER, tm, LANES), F32)],
        compiler_params=pltpu.CompilerParams(
            dimension_semantics=("arbitrary", "arbitrary"), vmem_limit_bytes=VMEM_LIMIT),
        name="outproj",
    )(*seg_o, *seg_lse, yb, yc, yd, proj, x2, w_out)


def _rope_tables(seq):
    inv = 1.0 / (ROPE_THETA ** (jnp.arange(0, HEAD_DIM, 2, dtype=F32) / HEAD_DIM))
    ang = jnp.arange(seq, dtype=F32)[:, None] * inv[None, :]
    cos, sin = jnp.cos(ang), jnp.sin(ang)
    reps = LANES // HEAD_DIM
    cos_t = jnp.tile(jnp.concatenate([cos, cos], axis=1), (1, reps))
    sin_t = jnp.tile(jnp.concatenate([-sin, sin], axis=1), (1, reps))
    return cos_t, sin_t


def _gain_table(qn, kn):
    ones = jnp.ones((HEAD_DIM,), F32)
    per_mixer_q = (qn[0], ones, qn[1], qn[2])
    per_mixer_k = (kn[0], ones, kn[1], kn[2])
    blocks = list(per_mixer_q) + list(per_mixer_k) + [ones] * (2 * N_MIXERS)
    tab = jnp.stack([jnp.tile(g.astype(F32), HEADS_PER_MIXER) for g in blocks])
    return tab[:, None, :]


def kernel(x, norm_gain, w_in, q_norm_gain, k_norm_gain, forget_bias, w_out):
    bsz, seq, d = x.shape
    depth = w_in.shape[0]
    rows = bsz * seq
    tm = min(512, seq)
    tq_seq = min(QUERY_TILE, seq)
    cos_t, sin_t = _rope_tables(seq)
    head_of_lane = jnp.arange(MIXER_WIDTH) // HEAD_DIM
    mavg = jnp.where(head_of_lane[:, None] == head_of_lane[None, :], 1.0 / HEAD_DIM, 0.0).astype(BF16)
    tri = jnp.tril(jnp.ones((256, 256), F32)).astype(BF16)

    def natural_col(section, r):
        return section * N_MIXERS

    def strided_col(section, r):
        return r

    x2 = x.reshape(rows, d)
    for layer in range(depth):
        w_l = w_in[layer]
        w_main = w_l[:, :PROJ_COLS].astype(BF16)
        wf = jnp.pad(w_l[:, PROJ_COLS:], ((0, 0), (0, LANES - HEADS_PER_MIXER)))
        wf_hi = wf.astype(BF16)
        wf_lo = (wf - wf_hi.astype(F32)).astype(BF16)
        wf_cat = jnp.concatenate([wf_hi, wf_lo], axis=1)
        fb_pad = jnp.pad(forget_bias[layer].astype(F32), (0, LANES - HEADS_PER_MIXER))[None, :]
        gain_tab = _gain_table(q_norm_gain[layer], k_norm_gain[layer])

        proj, cum, *strided = _inproj(x2, norm_gain[layer][None, :].astype(F32), w_main, wf_cat, fb_pad,
                                      gain_tab, cos_t, sin_t, mavg, tri, seq=seq, tm=tm)
        seg = [_band_segment(proj, proj, proj, natural_col, bsz=bsz, seq=seq, dil=1, tq=512)]
        for di, dil in enumerate(STRIDED_DILATIONS):
            qd, kd, vd = strided[3 * di:3 * di + 3]
            seg.append(_band_segment(qd, kd, vd, strided_col, bsz=bsz, seq=seq, dil=dil, tq=512))
        yb = _stick(proj, bsz=bsz, seq=seq, tq=min(STICK_QUERY_TILE, seq), tk=STICK_KEY_TILE)
        yc = _moba(proj, _score_bound(q_norm_gain[layer, 1], k_norm_gain[layer, 1]),
                   bsz=bsz, seq=seq, tq=tq_seq, tk=KEY_TILE)
        nt = seq // KEY_TILE
        tile_end_sums = cum.reshape(bsz, nt, KEY_TILE, LANES)[:, :, KEY_TILE - 1, :HEADS_PER_MIXER]
        tile_end_sums = tile_end_sums.transpose(0, 2, 1)
        yd = _fox(proj, cum, tile_end_sums, _score_bound(q_norm_gain[layer, 2], k_norm_gain[layer, 2]),
                  bsz=bsz, seq=seq, tq=tq_seq, tk=KEY_TILE)
        x2 = _outproj([s[0] for s in seg], [s[1] for s in seg], yb, yc, yd, proj, x2,
                      w_out[layer].astype(BF16), tm=min(OUT_ROW_TILE, seq), tn=d)
    return x2.reshape(bsz, seq, d)
```

```python
import functools

import jax
import jax.numpy as jnp
from jax import lax
from jax.experimental import pallas as pl
from jax.experimental.pallas import tpu as pltpu

F32 = jnp.float32
BF16 = jnp.bfloat16

HEAD_DIM = 64
HALF_DIM = HEAD_DIM // 2
LANES = 128
N_MIXERS = 4
HEADS_PER_MIXER = 8
PAIRS_PER_MIXER = HEADS_PER_MIXER // 2
MIXER_WIDTH = HEADS_PER_MIXER * HEAD_DIM
LANE_BLOCKS_PER_MIXER = MIXER_WIDTH // LANES
MIX_WIDTH = N_MIXERS * MIXER_WIDTH
N_SECTIONS = 4
PROJ_COLS = N_SECTIONS * MIX_WIDTH
COL_BLOCKS_PER_SECTION = MIX_WIDTH // LANES
ROPE_THETA = 10000.0
RMS_EPS = 1e-6
SCALE = HEAD_DIM ** -0.5
LOG2E = 1.4426950408889634
NEG = -1e30
DILATIONS = (1, 4, 16)
STRIDED_DILATIONS = DILATIONS[1:]
BAND = 128
MOBA_BLOCK = 256
MOBA_TOPK = 3
QUERY_TILE = 1024
KEY_TILE = 1024
STICK_QUERY_TILE = 512
STICK_KEY_TILE = 256
COL_BLOCKS_PER_STEP = 4
PROJ_BLOCK_ORDER = ((0, 0), (1, 0), (0, 2), (1, 2),
                    (0, 3), (1, 3), (0, 1), (1, 1),
                    (2, 0), (2, 1), (2, 2), (2, 3),
                    (3, 0), (3, 1), (3, 2), (3, 3))
PROJ_BLOCK_POS = {block: pos for pos, block in enumerate(PROJ_BLOCK_ORDER)}
GATE_GROUP = PROJ_BLOCK_POS[(3, 0)] // N_MIXERS
OUT_ROW_TILE = 256
UNDERFLOW_LOG2 = -160.0
VMEM_LIMIT = 56 * 1024 * 1024

_NT = (((1,), (1,)), ((), ()))


def _dot(a, b):
    return jnp.dot(a, b, preferred_element_type=F32)


def _dot_nt(a, b):
    return lax.dot_general(a, b, _NT, preferred_element_type=F32)


def _split2(x):
    hi = x.astype(BF16)
    lo = (x - hi.astype(F32)).astype(BF16)
    return hi, lo


def _split3(x):
    b1 = x.astype(BF16)
    r1 = x - b1.astype(F32)
    b2 = r1.astype(BF16)
    r2 = r1 - b2.astype(F32)
    return b1, b2, r2.astype(BF16)


def _head_lane_mask(shape, h):
    lane = lax.broadcasted_iota(jnp.int32, shape, len(shape) - 1)
    return (lane >= h * HEAD_DIM) & (lane < (h + 1) * HEAD_DIM)


def _own_head(x, h):
    return jnp.where(_head_lane_mask(x.shape, h), x, jnp.zeros_like(x))


def _lane_column(x, n):
    lane = lax.broadcasted_iota(jnp.int32, x.shape, 1)
    return jnp.sum(jnp.where(lane == n, x, 0.0), axis=1, keepdims=True)


def _inproj_body(x_ref, g_ref, w_ref, wf_ref, fb_ref, gain_ref, cos_ref, sin_ref, mavg_ref, tri_ref,
                 proj_ref, cum_ref, *rest, blocks_per_batch, tm):
    dil_refs = rest[:3 * len(STRIDED_DILATIONS)]
    h_scr, carry_scr, dil_scr = rest[3 * len(STRIDED_DILATIONS):]
    i = pl.program_id(0)
    j = pl.program_id(1)

    @pl.when(j == 0)
    def _prologue():
        x = x_ref[...]
        ms = jnp.mean(x * x, axis=-1, keepdims=True)
        h = x * lax.rsqrt(ms + RMS_EPS) * g_ref[...]
        h_hi, h_lo = _split2(h)
        h_scr[...] = h_hi
        wf = wf_ref[...]
        t = _dot(h_hi, wf)
        u = _dot(h_lo, wf[:, :LANES])
        logit = t[:, :LANES] + t[:, LANES:] + u + fb_ref[...]
        lf = jnp.minimum(logit, 0.0) - jnp.log1p(jnp.exp(-jnp.abs(logit)))

        @pl.when(i % blocks_per_batch == 0)
        def _():
            carry_scr[...] = jnp.zeros_like(carry_scr)

        carry = carry_scr[...]
        tri = tri_ref[...]
        sub = tri.shape[0]
        for r in range(tm // sub):
            b1, b2, b3 = _split3(lf[r * sub:(r + 1) * sub])
            c = _dot(tri, jnp.concatenate([b1, b2, b3], axis=1))
            c = c[:, :LANES] + c[:, LANES:2 * LANES] + c[:, 2 * LANES:] + carry
            cum_ref[r * sub:(r + 1) * sub, :] = c
            carry = c[sub - 1:sub, :]
        carry_scr[...] = carry

    for group in range(len(PROJ_BLOCK_ORDER) // COL_BLOCKS_PER_STEP):
        @pl.when(j == group)
        def _():
            for sub in range(COL_BLOCKS_PER_STEP):
                section, mixer = PROJ_BLOCK_ORDER[group * COL_BLOCKS_PER_STEP + sub]
                _inproj_column_block(section, mixer, sub * MIXER_WIDTH, h_scr, w_ref, gain_ref.at[sub],
                                     cos_ref, sin_ref, mavg_ref, proj_ref, dil_refs, dil_scr, tm=tm)


def _inproj_column_block(section, mixer, col0, h_scr, w_ref, gain_ref, cos_ref, sin_ref, mavg_ref, proj_ref,
                         dil_refs, dil_scr, *, tm):
    def out_cols(c):
        return slice(col0 + c * LANES, col0 + (c + 1) * LANES)

    acc = _dot(h_scr[...], w_ref[:, col0:col0 + MIXER_WIDTH])
    normed = section < 2 and mixer != 1
    roped = normed and mixer in (0, 2)
    scale = SCALE * LOG2E if section == 0 else 1.0

    if not normed:
        proj_ref[:, col0:col0 + MIXER_WIDTH] = (acc * scale if section == 0 else acc).astype(BF16)
    else:
        hi, lo = _split2(acc * acc)
        mavg = mavg_ref[...]
        ms = _dot(hi, mavg) + _dot(lo, mavg)
        t = acc * lax.rsqrt(ms + RMS_EPS) * (gain_ref[0] * scale)
        if not roped:
            proj_ref[:, col0:col0 + MIXER_WIDTH] = t.astype(BF16)
        else:
            cosv = cos_ref[...]
            sinv = sin_ref[...]
            lane = lax.broadcasted_iota(jnp.int32, (tm, LANES), 1)
            first_half = (lane % HEAD_DIM) < HALF_DIM
            for c in range(LANE_BLOCKS_PER_MIXER):
                tc = t[:, c * LANES:(c + 1) * LANES]
                partner = jnp.where(first_half,
                                    pltpu.roll(tc, LANES - HALF_DIM, 1),
                                    pltpu.roll(tc, HALF_DIM, 1))
                proj_ref[:, out_cols(c)] = (tc * cosv + partner * sinv).astype(BF16)

    if mixer == 0 and section < 3:
        for c in range(LANE_BLOCKS_PER_MIXER):
            dil_scr[c] = proj_ref[:, out_cols(c)].astype(F32)
        for di, dil in enumerate(STRIDED_DILATIONS):
            ref = dil_refs[3 * di + section]
            n = tm // dil
            for r in range(dil):
                for c in range(LANE_BLOCKS_PER_MIXER):
                    lo = r * MIXER_WIDTH + c * LANES
                    ref[:, lo:lo + LANES] = dil_scr[c, pl.ds(r, n, stride=dil), :].astype(BF16)


def _inproj(x2, g, w_main, wf_cat, fb_pad, gain_tab, cos_t, sin_t, mavg, tri, *, seq, tm):
    rows, d = x2.shape
    blocks_per_batch = seq // tm
    step_cols = COL_BLOCKS_PER_STEP * MIXER_WIDTH
    body = functools.partial(_inproj_body, blocks_per_batch=blocks_per_batch, tm=tm)
    dil_specs, dil_shapes = [], []
    for dil in STRIDED_DILATIONS:
        for _ in range(3):
            dil_specs.append(pl.BlockSpec((tm // dil, dil * MIXER_WIDTH), lambda i, j: (i, 0)))
            dil_shapes.append(jax.ShapeDtypeStruct((rows // dil, dil * MIXER_WIDTH), BF16))
    return pl.pallas_call(
        body,
        grid=(rows // tm, PROJ_COLS // step_cols),
        in_specs=[
            pl.BlockSpec((tm, d), lambda i, j: (i, 0)),
            pl.BlockSpec((1, d), lambda i, j: (0, 0)),
            pl.BlockSpec((d, step_cols), lambda i, j: (0, j)),
            pl.BlockSpec((d, 2 * LANES), lambda i, j: (0, 0)),
            pl.BlockSpec((1, LANES), lambda i, j: (0, 0)),
            pl.BlockSpec((COL_BLOCKS_PER_STEP, 1, MIXER_WIDTH), lambda i, j: (j, 0, 0)),
            pl.BlockSpec((tm, LANES), lambda i, j: (i % blocks_per_batch, 0)),
            pl.BlockSpec((tm, LANES), lambda i, j: (i % blocks_per_batch, 0)),
            pl.BlockSpec((MIXER_WIDTH, MIXER_WIDTH), lambda i, j: (0, 0)),
            pl.BlockSpec(tri.shape, lambda i, j: (0, 0)),
        ],
        out_specs=[
            pl.BlockSpec((tm, step_cols), lambda i, j: (i, j)),
            pl.BlockSpec((tm, LANES), lambda i, j: (i, 0)),
        ] + dil_specs,
        out_shape=[
            jax.ShapeDtypeStruct((rows, PROJ_COLS), BF16),
            jax.ShapeDtypeStruct((rows, LANES), F32),
        ] + dil_shapes,
        scratch_shapes=[pltpu.VMEM((tm, d), BF16), pltpu.VMEM((1, LANES), F32),
                        pltpu.VMEM((LANE_BLOCKS_PER_MIXER, tm, LANES), F32)],
        compiler_params=pltpu.CompilerParams(
            dimension_semantics=("arbitrary", "arbitrary"), vmem_limit_bytes=VMEM_LIMIT),
        name="inproj",
    )(x2, g, w_main, wf_cat, fb_pad, gain_tab, cos_t, sin_t, mavg, tri)


def _band_body(q_ref, k_ref, v_ref, kp_ref, vp_ref, o_ref, lse_ref, *, tq):
    first = pl.program_id(2) == 0
    row = lax.broadcasted_iota(jnp.int32, (BAND, 2 * BAND), 0)
    col = lax.broadcasted_iota(jnp.int32, (BAND, 2 * BAND), 1)
    in_band = jnp.logical_or(jnp.logical_and(col < BAND, col >= row),
                             jnp.logical_and(col >= BAND, col - BAND <= row))
    first_mask = jnp.logical_and(in_band, jnp.logical_or(col >= BAND, jnp.logical_not(first)))
    lane = lax.broadcasted_iota(jnp.int32, (BAND, LANES), 1)
    head0 = lane < HEAD_DIM
    for pair in range(PAIRS_PER_MIXER):
        lanes = slice(pair * LANES, (pair + 1) * LANES)
        for c in range(tq // BAND):
            own = slice(c * BAND, (c + 1) * BAND)
            qc = q_ref[own, lanes]
            if c == 0:
                kw = jnp.concatenate([kp_ref[:, lanes], k_ref[own, lanes]], axis=0)
                vw = jnp.concatenate([vp_ref[:, lanes], v_ref[own, lanes]], axis=0)
                mask = first_mask
            else:
                window = slice((c - 1) * BAND, (c + 1) * BAND)
                kw = k_ref[window, lanes]
                vw = v_ref[window, lanes]
                mask = in_band
            outs, lses = [], []
            for h in range(2):
                s = jnp.where(mask, _dot_nt(_own_head(qc, h), kw), NEG)
                m = jnp.max(s, axis=1, keepdims=True)
                p = jnp.exp2(s - m)
                l = jnp.sum(p, axis=1, keepdims=True)
                outs.append(_dot(p.astype(BF16), vw) / l)
                lses.append(m + jnp.log2(l))
            o_ref[own, lanes] = jnp.where(head0, outs[0], outs[1]).astype(BF16)
            lse_ref[own, lanes] = jnp.where(head0, lses[0], lses[1])


def _band_segment(q_arr, k_arr, v_arr, col_of, *, bsz, seq, dil, tq):
    sub_rows = q_arr.shape[0]
    sub_len = seq // dil
    tq = min(tq, sub_len)
    nq = sub_len // tq
    band_per_tq = tq // BAND

    def cur(section):
        return pl.BlockSpec((tq, MIXER_WIDTH), lambda b, r, i: (b * nq + i, col_of(section, r)))

    def prev(section):
        return pl.BlockSpec(
            (BAND, MIXER_WIDTH),
            lambda b, r, i: (jnp.maximum((b * nq + i) * band_per_tq - 1, 0), col_of(section, r)))

    out_spec = pl.BlockSpec((tq, MIXER_WIDTH), lambda b, r, i: (b * nq + i, r))
    return pl.pallas_call(
        functools.partial(_band_body, tq=tq),
        grid=(bsz, dil, nq),
        in_specs=[cur(0), cur(1), cur(2), prev(1), prev(2)],
        out_specs=[out_spec, out_spec],
        out_shape=[
            jax.ShapeDtypeStruct((sub_rows, dil * MIXER_WIDTH), BF16),
            jax.ShapeDtypeStruct((sub_rows, dil * MIXER_WIDTH), F32),
        ],
        compiler_params=pltpu.CompilerParams(
            dimension_semantics=("arbitrary",) * 3, vmem_limit_bytes=VMEM_LIMIT),
        name=f"band_d{dil}",
    )(q_arr, k_arr, v_arr, k_arr, v_arr)


def _online_init(tq):
    return (jnp.full((tq, 1), NEG, F32), jnp.zeros((tq, 1), F32), jnp.zeros((tq, LANES), F32))


def _online_step(carry, s, v_blk):
    m, l, acc = carry
    m_new = jnp.maximum(m, jnp.max(s, axis=1, keepdims=True))
    alpha = jnp.exp2(m - m_new)
    p = jnp.exp2(s - m_new)
    l = alpha * l + jnp.sum(p, axis=1, keepdims=True)
    acc = alpha * acc + _dot(p.astype(BF16), v_blk)
    return m_new, l, acc


def _online_finish(carry):
    _, l, acc = carry
    return acc / l


def _shifted_init(tq):
    return (jnp.zeros((tq, LANES), F32), jnp.zeros((tq, LANES), F32))


def _shifted_step(carry, s, v_blk):
    lane_sums, acc = carry
    p = jnp.exp2(s)
    lane_sums = lane_sums + sum(p[:, c * LANES:(c + 1) * LANES] for c in range(p.shape[1] // LANES))
    return lane_sums, acc + _dot(p.astype(BF16), v_blk)


def _shifted_finish(carry):
    lane_sums, acc = carry
    return acc / jnp.sum(lane_sums, axis=1, keepdims=True)


_ONLINE = (_online_init, _online_step, _online_finish)
_SHIFTED = (_shifted_init, _shifted_step, _shifted_finish)
SHIFTED_MAX_SPAN = 100.0


def _attend_tile(q_aug, k_tile, v_blk, state, step, key_offset=None, heads=(0, 1)):
    new = list(state)
    for h in heads:
        s = _dot_nt(q_aug[h], k_tile(h))
        if key_offset is not None:
            row = lax.broadcasted_iota(jnp.int32, s.shape, 0)
            col = lax.broadcasted_iota(jnp.int32, s.shape, 1)
            s = jnp.where(col + key_offset <= row, s, NEG)
        new[h] = step(state[h], s, v_blk)
    return tuple(new)


def _attend_diagonal(tile, iq, tq, tk, init):
    state = (init(tq), init(tq))
    for d in range(tq // tk):
        state = tile(iq * (tq // tk) + d, state, d * tk)
    return state


def _score_bound(q_gain, k_gain):
    bound = 1.02 * HEAD_DIM * SCALE * LOG2E * jnp.max(jnp.abs(q_gain)) * jnp.max(jnp.abs(k_gain))
    return bound.astype(BF16).astype(F32).reshape(1)


def _merge_heads(outs):
    lane = lax.broadcasted_iota(jnp.int32, outs[0].shape, 1)
    return jnp.where(lane < HEAD_DIM, outs[0], outs[1])


def _seq_mixer_specs(*, mixer, nq, tq, seq):
    def colblk(section, p):
        return PROJ_BLOCK_POS[(section, mixer)] * PAIRS_PER_MIXER + p

    q_spec = pl.BlockSpec((tq, LANES), lambda b, p, i: (b * nq + i, colblk(0, p)))
    k_spec = pl.BlockSpec((seq, LANES), lambda b, p, i: (b, colblk(1, p)))
    v_spec = pl.BlockSpec((seq, LANES), lambda b, p, i: (b, colblk(2, p)))
    o_spec = pl.BlockSpec((tq, LANES), lambda b, p, i: (b * nq + i, p))
    return q_spec, k_spec, v_spec, o_spec


_SEQ_PARAMS = pltpu.CompilerParams(
    dimension_semantics=("arbitrary",) * 3, vmem_limit_bytes=VMEM_LIMIT)


def _fox_augment(x, cum, head, h, key_side, bound):
    lane = lax.broadcasted_iota(jnp.int32, x.shape, 1)
    g = jnp.broadcast_to(_lane_column(cum, head) * LOG2E, x.shape)
    g1, g2, g3 = (piece.astype(F32) for piece in _split3(g))
    one = jnp.ones_like(g1)
    pieces = (one, one, one, -g1, -g2, -g3, one) if key_side else (g1, g2, g3, one, one, one, -bound * one)
    base = HEAD_DIM * (1 - h)
    aug = jnp.zeros_like(g1)
    for n, piece in enumerate(pieces):
        aug = jnp.where(lane == base + n, piece, aug)
    return jnp.where(_head_lane_mask(x.shape, h), x.astype(F32), aug).astype(BF16)


def _fox_body(fend_ref, bound_ref, q_ref, k_ref, v_ref, cumq_ref, cumk_ref, o_ref, kaug_scr, *, tq, tk, seq):
    b = pl.program_id(0)
    p = pl.program_id(1)
    iq = pl.program_id(2)
    bound = bound_ref[0]

    @pl.when(iq == 0)
    def _augment_keys():
        def chunk(c, _):
            rows = pl.ds(pl.multiple_of(c * tk, tk), tk)
            for h in range(2):
                kaug_scr[h, rows, :] = _fox_augment(k_ref[rows, :], cumk_ref[rows, :], 2 * p + h, h, True, bound)
            return 0
        lax.fori_loop(0, seq // tk, chunk, 0)

    q = q_ref[...]
    cumq = cumq_ref[...]
    q_aug = [_fox_augment(q, cumq, 2 * p + h, h, False, bound) for h in range(2)]

    first_past = iq * (tq // tk) - 1
    last = jnp.maximum(first_past, 0)

    def reaches(j, heads):
        j = jnp.maximum(j, 0)
        alive = [LOG2E * (fend_ref[b, 2 * p + h, last] - fend_ref[b, 2 * p + h, j])
                 + 2.0 * bound > UNDERFLOW_LOG2 for h in heads]
        return functools.reduce(jnp.logical_and, alive)

    def attend(scheme):
        init, step, finish = scheme

        def tile(j, state, key_offset=None, heads=(0, 1)):
            rows = pl.ds(pl.multiple_of(j * tk, tk), tk)
            return _attend_tile(q_aug, lambda h: kaug_scr[h, rows, :], v_ref[rows, :], state, step,
                                key_offset, heads)

        loop = (first_past, _attend_diagonal(tile, iq, tq, tk, init))
        for heads in ((0, 1), (0,), (1,)):
            loop = lax.while_loop(
                lambda lp: jnp.logical_and(lp[0] >= 0, reaches(lp[0], heads)),
                lambda lp: (lp[0] - 1, tile(lp[0], lp[1], heads=heads)),
                loop)
        o_ref[...] = _merge_heads([finish(st) for st in loop[1]]).astype(BF16)

    shifted_ok = 2.0 * bound <= SHIFTED_MAX_SPAN
    pl.when(shifted_ok)(functools.partial(attend, _SHIFTED))
    pl.when(jnp.logical_not(shifted_ok))(functools.partial(attend, _ONLINE))


def _fox(proj, cum, tile_end_sums, score_bound, *, bsz, seq, tq, tk):
    rows = proj.shape[0]
    nq = seq // tq
    q_spec, k_spec, v_spec, o_spec = _seq_mixer_specs(mixer=3, nq=nq, tq=tq, seq=seq)
    smem = pl.BlockSpec(memory_space=pltpu.SMEM)
    return pl.pallas_call(
        functools.partial(_fox_body, tq=tq, tk=tk, seq=seq),
        grid=(bsz, PAIRS_PER_MIXER, nq),
        in_specs=[
            smem, smem, q_spec, k_spec, v_spec,
            pl.BlockSpec((tq, LANES), lambda b, p, i: (b * nq + i, 0)),
            pl.BlockSpec((seq, LANES), lambda b, p, i: (b, 0)),
        ],
        out_specs=o_spec,
        out_shape=jax.ShapeDtypeStruct((rows, MIXER_WIDTH), BF16),
        scratch_shapes=[pltpu.VMEM((2, seq, LANES), BF16)],
        compiler_params=_SEQ_PARAMS,
        name="fox",
    )(tile_end_sums, score_bound, proj, proj, proj, cum, cum)


def _moba_body(bound_ref, q_ref, k_ref, v_ref, o_ref, kmean_scr, kaug_scr, *, tq, tk, seq):
    iq = pl.program_id(2)
    nblk = seq // MOBA_BLOCK
    blocks_per_tile = tq // MOBA_BLOCK
    bound = bound_ref[0]

    @pl.when(iq == 0)
    def _prepare_keys():
        r = lax.broadcasted_iota(jnp.int32, (LANES, seq), 0)
        c = lax.broadcasted_iota(jnp.int32, (LANES, seq), 1)
        member = jnp.where(c // MOBA_BLOCK == r % HEAD_DIM, 1.0 / MOBA_BLOCK, 0.0).astype(BF16)
        hi, lo = _split2(_dot(member, k_ref[...]))
        kmean_scr[0] = hi
        kmean_scr[1] = lo

        def chunk(c, _):
            rows = pl.ds(pl.multiple_of(c * tk, tk), tk)
            k = k_ref[rows, :].astype(F32)
            lane = lax.broadcasted_iota(jnp.int32, (tk, LANES), 1)
            blk = (c * tk + lax.broadcasted_iota(jnp.int32, (tk, LANES), 0)) // MOBA_BLOCK
            for h in range(2):
                spare = lane - HEAD_DIM * (1 - h)
                onehot = jnp.where(jnp.logical_or(spare == blk, spare == HEAD_DIM - 1), 1.0, 0.0)
                kaug_scr[h, rows, :] = jnp.where(_head_lane_mask(k.shape, h), k, onehot).astype(BF16)
            return 0
        lax.fori_loop(0, seq // tk, chunk, 0)

    q = q_ref[...]
    slot = lax.broadcasted_iota(jnp.int32, (LANES, tq), 0)
    qblk = blocks_per_tile * iq + lax.broadcasted_iota(jnp.int32, (LANES, tq), 1) // MOBA_BLOCK
    q_aug = []
    for h in range(2):
        qh = _own_head(q, h)
        gate = _dot_nt(kmean_scr[0], qh) + _dot_nt(kmean_scr[1], qh)
        blk = slot - HEAD_DIM * (1 - h)
        past = jnp.logical_and(blk >= 0, blk < qblk)
        work = jnp.where(past, gate, NEG)
        bias = jnp.where(blk == qblk, 0.0, NEG)
        for _ in range(min(MOBA_TOPK, nblk)):
            best = jnp.max(work, axis=0, keepdims=True)
            idx = jnp.min(jnp.where(work == best, slot, LANES), axis=0, keepdims=True)
            hit = slot == idx
            bias = jnp.where(hit, jnp.where(past, 0.0, bias), bias)
            work = jnp.where(hit, -jnp.inf, work)
        bias = jnp.where(blk == HEAD_DIM - 1, -bound, bias)
        q_aug.append(jnp.where(_head_lane_mask(q.shape, h), q.astype(F32), bias.T).astype(BF16))

    def attend(scheme):
        init, step, finish = scheme

        def tile(j, state, key_offset=None):
            rows = pl.ds(pl.multiple_of(j * tk, tk), tk)
            return _attend_tile(q_aug, lambda h: kaug_scr[h, rows, :], v_ref[rows, :], state, step, key_offset)

        state = _attend_diagonal(tile, iq, tq, tk, init)
        state = lax.fori_loop(0, iq * (tq // tk), tile, state)
        o_ref[...] = _merge_heads([finish(st) for st in state]).astype(BF16)

    shifted_ok = 2.0 * bound <= SHIFTED_MAX_SPAN
    pl.when(shifted_ok)(functools.partial(attend, _SHIFTED))
    pl.when(jnp.logical_not(shifted_ok))(functools.partial(attend, _ONLINE))


def _moba(proj, score_bound, *, bsz, seq, tq, tk):
    assert tq % MOBA_BLOCK == 0 and seq // MOBA_BLOCK < HEAD_DIM
    rows = proj.shape[0]
    nq = seq // tq
    q_spec, k_spec, v_spec, o_spec = _seq_mixer_specs(mixer=2, nq=nq, tq=tq, seq=seq)
    return pl.pallas_call(
        functools.partial(_moba_body, tq=tq, tk=tk, seq=seq),
        grid=(bsz, PAIRS_PER_MIXER, nq),
        in_specs=[pl.BlockSpec(memory_space=pltpu.SMEM), q_spec, k_spec, v_spec],
        out_specs=o_spec,
        out_shape=jax.ShapeDtypeStruct((rows, MIXER_WIDTH), BF16),
        scratch_shapes=[pltpu.VMEM((2, LANES, LANES), BF16), pltpu.VMEM((2, seq, LANES), BF16)],
        compiler_params=_SEQ_PARAMS,
        name="moba",
    )(score_bound, proj, proj, proj)


def _stick_body(q_ref, k_ref, v_ref, o_ref, *, tq, tk):
    iq = pl.program_id(2)
    blocks_per_tile = tq // tk
    q = q_ref[...]
    q_own = [_own_head(q, h) for h in range(2)]
    later_row = lax.broadcasted_iota(jnp.int32, (tk, tk), 0)
    later_col = lax.broadcasted_iota(jnp.int32, (tk, tk), 1)
    later = jnp.where(later_row > later_col, 1.0, 0.0).astype(BF16)

    def block(j, state, first_row=None):
        keys = pl.ds(pl.multiple_of(j * tk, tk), tk)
        k_blk = k_ref[keys, :]
        v_blk = v_ref[keys, :]
        rows = slice(first_row or 0, tq)
        n = tq - rows.start
        if first_row is not None:
            strictly_past = (lax.broadcasted_iota(jnp.int32, (n, tk), 1)
                             < lax.broadcasted_iota(jnp.int32, (n, tk), 0))
        new = []
        for h in range(2):
            carry_sum, acc = state[h]
            z = _dot_nt(q_own[h][rows], k_blk)
            softplus = jnp.maximum(z, 0.0) + jnp.log2(1.0 + jnp.exp2(-jnp.abs(z)))
            log_keep = -softplus
            if first_row is not None:
                log_keep = jnp.where(strictly_past, log_keep, 0.0)
            hi, lo = _split2(log_keep)
            after = _dot(hi, later) + _dot(lo, later) + carry_sum[rows]
            log_a = z - softplus + after
            if first_row is not None:
                log_a = jnp.where(strictly_past, log_a, NEG)
            acc_rows = acc[rows] + _dot(jnp.exp2(log_a).astype(BF16), v_blk)
            sum_rows = carry_sum[rows] + jnp.sum(log_keep, axis=1, keepdims=True)
            if rows.start:
                acc_rows = jnp.concatenate([acc[:rows.start], acc_rows], axis=0)
                sum_rows = jnp.concatenate([carry_sum[:rows.start], sum_rows], axis=0)
            new.append((sum_rows, acc_rows))
        return tuple(new)

    zero = (jnp.zeros((tq, 1), F32), jnp.zeros((tq, LANES), F32))
    state = (zero, zero)
    for d in reversed(range(blocks_per_tile)):
        state = block(iq * blocks_per_tile + d, state, d * tk)

    def cond(loop):
        j, state = loop
        alive = jnp.max(jnp.maximum(state[0][0], state[1][0])) > UNDERFLOW_LOG2
        return jnp.logical_and(j >= 0, alive)

    def body(loop):
        j, state = loop
        return j - 1, block(j, state)

    _, state = lax.while_loop(cond, body, (iq * blocks_per_tile - 1, state))
    o_ref[...] = _merge_heads([acc for _, acc in state]).astype(BF16)


def _stick(proj, *, bsz, seq, tq, tk):
    rows = proj.shape[0]
    nq = seq // tq
    q_spec, k_spec, v_spec, o_spec = _seq_mixer_specs(mixer=1, nq=nq, tq=tq, seq=seq)
    return pl.pallas_call(
        functools.partial(_stick_body, tq=tq, tk=tk),
        grid=(bsz, PAIRS_PER_MIXER, nq),
        in_specs=[q_spec, k_spec, v_spec],
        out_specs=o_spec,
        out_shape=jax.ShapeDtypeStruct((rows, MIXER_WIDTH), BF16),
        compiler_params=_SEQ_PARAMS,
        name="stick",
    )(proj, proj, proj)


def _outproj_body(o1_ref, o4_ref, o16_ref, l1_ref, l4_ref, l16_ref, yb_ref, yc_ref, yd_ref,
                  gate_ref, x_ref, w_ref, out_ref, y_scr, o_scr, l_scr, *, tm):
    j = pl.program_id(1)

    @pl.when(j == 0)
    def _gate():
        for slot, (dil, o_ref, l_ref) in enumerate(zip(STRIDED_DILATIONS, (o4_ref, o16_ref), (l4_ref, l16_ref))):
            n = tm // dil
            for r in range(dil):
                for c in range(LANE_BLOCKS_PER_MIXER):
                    cols = slice(r * MIXER_WIDTH + c * LANES, r * MIXER_WIDTH + (c + 1) * LANES)
                    o_scr[slot, c, pl.ds(r, n, stride=dil), :] = o_ref[:, cols].astype(F32)
                    l_scr[slot, c, pl.ds(r, n, stride=dil), :] = l_ref[:, cols]

        def natural(scr, slot):
            return jnp.concatenate([scr[slot, c] for c in range(LANE_BLOCKS_PER_MIXER)], axis=1)

        l1, l4, l16 = l1_ref[...], natural(l_scr, 0), natural(l_scr, 1)
        m = jnp.maximum(jnp.maximum(l1, l4), l16)
        e1, e4, e16 = jnp.exp2(l1 - m), jnp.exp2(l4 - m), jnp.exp2(l16 - m)
        ya = (e1 * o1_ref[...].astype(F32) + e4 * natural(o_scr, 0)
              + e16 * natural(o_scr, 1)) / (e1 + e4 + e16)
        parts = (ya, yb_ref[...].astype(F32), yc_ref[...].astype(F32), yd_ref[...].astype(F32))
        for mxr, y in enumerate(parts):
            g = gate_ref[:, mxr * MIXER_WIDTH:(mxr + 1) * MIXER_WIDTH].astype(F32)
            silu = g / (1.0 + jnp.exp(-g))
            y_scr[:, mxr * MIXER_WIDTH:(mxr + 1) * MIXER_WIDTH] = (y * silu).astype(BF16)

    out_ref[...] = x_ref[...] + _dot(y_scr[...], w_ref[...])


def _outproj(seg_o, seg_lse, yb, yc, yd, proj, x2, w_out, *, tm, tn):
    rows, d = x2.shape
    row_blk = lambda i, j: (i, 0)
    mix_spec = pl.BlockSpec((tm, MIXER_WIDTH), row_blk)
    seg_specs = [pl.BlockSpec((tm // dil, dil * MIXER_WIDTH), row_blk) for dil in DILATIONS]
    return pl.pallas_call(
        functools.partial(_outproj_body, tm=tm),
        grid=(rows // tm, d // tn),
        in_specs=seg_specs + seg_specs + [mix_spec] * 3 + [
            pl.BlockSpec((tm, MIX_WIDTH), lambda i, j: (i, GATE_GROUP)),
            pl.BlockSpec((tm, tn), lambda i, j: (i, j)),
            pl.BlockSpec((MIX_WIDTH, tn), lambda i, j: (0, j)),
        ],
        out_specs=pl.BlockSpec((tm, tn), lambda i, j: (i, j)),
        out_shape=jax.ShapeDtypeStruct((rows, d), F32),
        scratch_shapes=[pltpu.VMEM((tm, MIX_WIDTH), BF16),
                        pltpu.VMEM((len(STRIDED_DILATIONS), LANE_BLOCKS_PER_MIXER, tm, LANES), F32),
                        pltpu.VMEM((len(STRIDED_DILATIONS), LANE_BLOCKS_PER_MIXER, tm, LANES), F32)],
        compiler_params=pltpu.CompilerParams(
            dimension_semantics=("arbitrary", "arbitrary"), vmem_limit_bytes=VMEM_LIMIT),
        name="outproj",
    )(*seg_o, *seg_lse, yb, yc, yd, proj, x2, w_out)


def _rope_tables(seq):
    inv = 1.0 / (ROPE_THETA ** (jnp.arange(0, HEAD_DIM, 2, dtype=F32) / HEAD_DIM))
    ang = jnp.arange(seq, dtype=F32)[:, None] * inv[None, :]
    cos, sin = jnp.cos(ang), jnp.sin(ang)
    reps = LANES // HEAD_DIM
    cos_t = jnp.tile(jnp.concatenate([cos, cos], axis=1), (1, reps))
    sin_t = jnp.tile(jnp.concatenate([-sin, sin], axis=1), (1, reps))
    return cos_t, sin_t


def _gain_table(qn, kn):
    ones = jnp.ones((HEAD_DIM,), F32)
    per_mixer = {0: (qn[0], ones, qn[1], qn[2]), 1: (kn[0], ones, kn[1], kn[2])}
    blocks = [per_mixer[section][mixer] if section < 2 else ones for section, mixer in PROJ_BLOCK_ORDER]
    tab = jnp.stack([jnp.tile(g.astype(F32), HEADS_PER_MIXER) for g in blocks])
    return tab[:, None, :]


def kernel(x, norm_gain, w_in, q_norm_gain, k_norm_gain, forget_bias, w_out):
    bsz, seq, d = x.shape
    depth = w_in.shape[0]
    rows = bsz * seq
    tm = min(512, seq)
    tq_seq = min(QUERY_TILE, seq)
    cos_t, sin_t = _rope_tables(seq)
    head_of_lane = jnp.arange(MIXER_WIDTH) // HEAD_DIM
    mavg = jnp.where(head_of_lane[:, None] == head_of_lane[None, :], 1.0 / HEAD_DIM, 0.0).astype(BF16)
    tri = jnp.tril(jnp.ones((256, 256), F32)).astype(BF16)

    def natural_col(section, r):
        return PROJ_BLOCK_POS[(section, 0)]

    def strided_col(section, r):
        return r

    x2 = x.reshape(rows, d)
    for layer in range(depth):
        w_l = w_in[layer]
        w_main = jnp.concatenate(
            [w_l[:, section * MIX_WIDTH + mixer * MIXER_WIDTH:][:, :MIXER_WIDTH] for section, mixer in PROJ_BLOCK_ORDER],
            axis=1).astype(BF16)
        wf = jnp.pad(w_l[:, PROJ_COLS:], ((0, 0), (0, LANES - HEADS_PER_MIXER)))
        wf_hi = wf.astype(BF16)
        wf_lo = (wf - wf_hi.astype(F32)).astype(BF16)
        wf_cat = jnp.concatenate([wf_hi, wf_lo], axis=1)
        fb_pad = jnp.pad(forget_bias[layer].astype(F32), (0, LANES - HEADS_PER_MIXER))[None, :]
        gain_tab = _gain_table(q_norm_gain[layer], k_norm_gain[layer])

        proj, cum, *strided = _inproj(x2, norm_gain[layer][None, :].astype(F32), w_main, wf_cat, fb_pad,
                                      gain_tab, cos_t, sin_t, mavg, tri, seq=seq, tm=tm)
        seg = [_band_segment(proj, proj, proj, natural_col, bsz=bsz, seq=seq, dil=1, tq=512)]
        for di, dil in enumerate(STRIDED_DILATIONS):
            qd, kd, vd = strided[3 * di:3 * di + 3]
            seg.append(_band_segment(qd, kd, vd, strided_col, bsz=bsz, seq=seq, dil=dil, tq=512))
        yb = _stick(proj, bsz=bsz, seq=seq, tq=min(STICK_QUERY_TILE, seq), tk=STICK_KEY_TILE)
        yc = _moba(proj, _score_bound(q_norm_gain[layer, 1], k_norm_gain[layer, 1]),
                   bsz=bsz, seq=seq, tq=tq_seq, tk=KEY_TILE)
        nt = seq // KEY_TILE
        tile_end_sums = cum.reshape(bsz, nt, KEY_TILE, LANES)[:, :, KEY_TILE - 1, :HEADS_PER_MIXER]
        tile_end_sums = tile_end_sums.transpose(0, 2, 1)
        yd = _fox(proj, cum, tile_end_sums, _score_bound(q_norm_gain[layer, 2], k_norm_gain[layer, 2]),
                  bsz=bsz, seq=seq, tq=tq_seq, tk=KEY_TILE)
        x2 = _outproj([s[0] for s in seg], [s[1] for s in seg], yb, yc, yd, proj, x2,
                      w_out[layer].astype(BF16), tm=min(OUT_ROW_TILE, seq), tn=d)
    return x2.reshape(bsz, seq, d)
```

```python
import functools

import jax
import jax.numpy as jnp
from jax import lax
from jax.experimental import pallas as pl
from jax.experimental.pallas import tpu as pltpu

F32 = jnp.float32
BF16 = jnp.bfloat16

HEAD_DIM = 64
HALF_DIM = HEAD_DIM // 2
LANES = 128
N_MIXERS = 4
HEADS_PER_MIXER = 8
PAIRS_PER_MIXER = HEADS_PER_MIXER // 2
MIXER_WIDTH = HEADS_PER_MIXER * HEAD_DIM
LANE_BLOCKS_PER_MIXER = MIXER_WIDTH // LANES
MIX_WIDTH = N_MIXERS * MIXER_WIDTH
N_SECTIONS = 4
PROJ_COLS = N_SECTIONS * MIX_WIDTH
COL_BLOCKS_PER_SECTION = MIX_WIDTH // LANES
ROPE_THETA = 10000.0
RMS_EPS = 1e-6
SCALE = HEAD_DIM ** -0.5
LOG2E = 1.4426950408889634
NEG = -1e30
DILATIONS = (1, 4, 16)
STRIDED_DILATIONS = DILATIONS[1:]
BAND = 128
MOBA_BLOCK = 256
MOBA_TOPK = 3
QUERY_TILE = 1024
KEY_TILE = 1024
STICK_QUERY_TILE = 512
STICK_KEY_TILE = 256
COL_BLOCKS_PER_STEP = 4
PROJ_BLOCK_ORDER = ((0, 0), (1, 0), (0, 2), (1, 2),
                    (0, 3), (1, 3), (0, 1), (1, 1),
                    (2, 0), (2, 1), (2, 2), (2, 3),
                    (3, 0), (3, 1), (3, 2), (3, 3))
PROJ_BLOCK_POS = {block: pos for pos, block in enumerate(PROJ_BLOCK_ORDER)}
GATE_GROUP = PROJ_BLOCK_POS[(3, 0)] // N_MIXERS
OUT_ROW_TILE = 256
UNDERFLOW_LOG2 = -160.0
VMEM_LIMIT = 56 * 1024 * 1024

_NT = (((1,), (1,)), ((), ()))


def _dot(a, b):
    return jnp.dot(a, b, preferred_element_type=F32)


def _dot_nt(a, b):
    return lax.dot_general(a, b, _NT, preferred_element_type=F32)


def _split2(x):
    hi = x.astype(BF16)
    lo = (x - hi.astype(F32)).astype(BF16)
    return hi, lo


def _split3(x):
    b1 = x.astype(BF16)
    r1 = x - b1.astype(F32)
    b2 = r1.astype(BF16)
    r2 = r1 - b2.astype(F32)
    return b1, b2, r2.astype(BF16)


def _head_lane_mask(shape, h):
    lane = lax.broadcasted_iota(jnp.int32, shape, len(shape) - 1)
    return (lane >= h * HEAD_DIM) & (lane < (h + 1) * HEAD_DIM)


def _own_head(x, h):
    return jnp.where(_head_lane_mask(x.shape, h), x, jnp.zeros_like(x))


def _lane_column(x, n):
    lane = lax.broadcasted_iota(jnp.int32, x.shape, 1)
    return jnp.sum(jnp.where(lane == n, x, 0.0), axis=1, keepdims=True)


def _inproj_body(x_ref, g_ref, wf_ref, fb_ref, gain_ref, cos_ref, sin_ref, mavg_ref, tri_ref, *rest,
                 blocks_per_batch, tm):
    w_refs, (proj_ref, cum_ref), rest = rest[:COL_BLOCKS_PER_STEP], rest[COL_BLOCKS_PER_STEP:][:2], \
        rest[COL_BLOCKS_PER_STEP + 2:]
    dil_refs = rest[:3 * len(STRIDED_DILATIONS)]
    h_scr, carry_scr, dil_scr = rest[3 * len(STRIDED_DILATIONS):]
    i = pl.program_id(0)
    j = pl.program_id(1)

    @pl.when(j == 0)
    def _prologue():
        x = x_ref[...]
        ms = jnp.mean(x * x, axis=-1, keepdims=True)
        h = x * lax.rsqrt(ms + RMS_EPS) * g_ref[...]
        h_hi, h_lo = _split2(h)
        h_scr[...] = h_hi
        wf = wf_ref[...]
        t = _dot(h_hi, wf)
        u = _dot(h_lo, wf[:, :LANES])
        logit = t[:, :LANES] + t[:, LANES:] + u + fb_ref[...]
        lf = jnp.minimum(logit, 0.0) - jnp.log1p(jnp.exp(-jnp.abs(logit)))

        @pl.when(i % blocks_per_batch == 0)
        def _():
            carry_scr[...] = jnp.zeros_like(carry_scr)

        carry = carry_scr[...]
        tri = tri_ref[...]
        sub = tri.shape[0]
        for r in range(tm // sub):
            b1, b2, b3 = _split3(lf[r * sub:(r + 1) * sub])
            c = _dot(tri, jnp.concatenate([b1, b2, b3], axis=1))
            c = c[:, :LANES] + c[:, LANES:2 * LANES] + c[:, 2 * LANES:] + carry
            cum_ref[r * sub:(r + 1) * sub, :] = c
            carry = c[sub - 1:sub, :]
        carry_scr[...] = carry

    for group in range(len(PROJ_BLOCK_ORDER) // COL_BLOCKS_PER_STEP):
        @pl.when(j == group)
        def _():
            for sub in range(COL_BLOCKS_PER_STEP):
                section, mixer = PROJ_BLOCK_ORDER[group * COL_BLOCKS_PER_STEP + sub]
                _inproj_column_block(section, mixer, sub * MIXER_WIDTH, h_scr, w_refs[sub], gain_ref.at[sub],
                                     cos_ref, sin_ref, mavg_ref, proj_ref, dil_refs, dil_scr, tm=tm)


def _inproj_column_block(section, mixer, col0, h_scr, w_ref, gain_ref, cos_ref, sin_ref, mavg_ref, proj_ref,
                         dil_refs, dil_scr, *, tm):
    def out_cols(c):
        return slice(col0 + c * LANES, col0 + (c + 1) * LANES)

    acc = _dot(h_scr[...], w_ref[...])
    normed = section < 2 and mixer != 1
    roped = normed and mixer in (0, 2)
    scale = SCALE * LOG2E if section == 0 else 1.0

    if not normed:
        proj_ref[:, col0:col0 + MIXER_WIDTH] = (acc * scale if section == 0 else acc).astype(BF16)
    else:
        hi, lo = _split2(acc * acc)
        mavg = mavg_ref[...]
        ms = _dot(hi, mavg) + _dot(lo, mavg)
        t = acc * lax.rsqrt(ms + RMS_EPS) * (gain_ref[0] * scale)
        if not roped:
            proj_ref[:, col0:col0 + MIXER_WIDTH] = t.astype(BF16)
        else:
            cosv = cos_ref[...]
            sinv = sin_ref[...]
            lane = lax.broadcasted_iota(jnp.int32, (tm, LANES), 1)
            first_half = (lane % HEAD_DIM) < HALF_DIM
            for c in range(LANE_BLOCKS_PER_MIXER):
                tc = t[:, c * LANES:(c + 1) * LANES]
                partner = jnp.where(first_half,
                                    pltpu.roll(tc, LANES - HALF_DIM, 1),
                                    pltpu.roll(tc, HALF_DIM, 1))
                proj_ref[:, out_cols(c)] = (tc * cosv + partner * sinv).astype(BF16)

    if mixer == 0 and section < 3:
        for c in range(LANE_BLOCKS_PER_MIXER):
            dil_scr[c] = proj_ref[:, out_cols(c)].astype(F32)
        for di, dil in enumerate(STRIDED_DILATIONS):
            ref = dil_refs[3 * di + section]
            n = tm // dil
            for r in range(dil):
                for c in range(LANE_BLOCKS_PER_MIXER):
                    lo = r * MIXER_WIDTH + c * LANES
                    ref[:, lo:lo + LANES] = dil_scr[c, pl.ds(r, n, stride=dil), :].astype(BF16)


def _inproj(x2, g, w_main, wf_cat, fb_pad, gain_tab, cos_t, sin_t, mavg, tri, *, seq, tm):
    rows, d = x2.shape
    blocks_per_batch = seq // tm
    step_cols = COL_BLOCKS_PER_STEP * MIXER_WIDTH
    body = functools.partial(_inproj_body, blocks_per_batch=blocks_per_batch, tm=tm)

    def weight_block(sub):
        origin = [section * N_MIXERS + mixer for section, mixer in PROJ_BLOCK_ORDER[sub::COL_BLOCKS_PER_STEP]]

        def index(i, j):
            blk = origin[0]
            for group in range(1, len(origin)):
                blk = jnp.where(j == group, origin[group], blk)
            return 0, blk
        return index

    dil_specs, dil_shapes = [], []
    for dil in STRIDED_DILATIONS:
        for _ in range(3):
            dil_specs.append(pl.BlockSpec((tm // dil, dil * MIXER_WIDTH), lambda i, j: (i, 0)))
            dil_shapes.append(jax.ShapeDtypeStruct((rows // dil, dil * MIXER_WIDTH), BF16))
    return pl.pallas_call(
        body,
        grid=(rows // tm, PROJ_COLS // step_cols),
        in_specs=[
            pl.BlockSpec((tm, d), lambda i, j: (i, 0)),
            pl.BlockSpec((1, d), lambda i, j: (0, 0)),
            pl.BlockSpec((d, 2 * LANES), lambda i, j: (0, 0)),
            pl.BlockSpec((1, LANES), lambda i, j: (0, 0)),
            pl.BlockSpec((COL_BLOCKS_PER_STEP, 1, MIXER_WIDTH), lambda i, j: (j, 0, 0)),
            pl.BlockSpec((tm, LANES), lambda i, j: (i % blocks_per_batch, 0)),
            pl.BlockSpec((tm, LANES), lambda i, j: (i % blocks_per_batch, 0)),
            pl.BlockSpec((MIXER_WIDTH, MIXER_WIDTH), lambda i, j: (0, 0)),
            pl.BlockSpec(tri.shape, lambda i, j: (0, 0)),
        ] + [pl.BlockSpec((d, MIXER_WIDTH), weight_block(sub)) for sub in range(COL_BLOCKS_PER_STEP)],
        out_specs=[
            pl.BlockSpec((tm, step_cols), lambda i, j: (i, j)),
            pl.BlockSpec((tm, LANES), lambda i, j: (i, 0)),
        ] + dil_specs,
        out_shape=[
            jax.ShapeDtypeStruct((rows, PROJ_COLS), BF16),
            jax.ShapeDtypeStruct((rows, LANES), F32),
        ] + dil_shapes,
        scratch_shapes=[pltpu.VMEM((tm, d), BF16), pltpu.VMEM((1, LANES), F32),
                        pltpu.VMEM((LANE_BLOCKS_PER_MIXER, tm, LANES), F32)],
        compiler_params=pltpu.CompilerParams(
            dimension_semantics=("arbitrary", "arbitrary"), vmem_limit_bytes=VMEM_LIMIT),
        name="inproj",
    )(x2, g, wf_cat, fb_pad, gain_tab, cos_t, sin_t, mavg, tri, *([w_main] * COL_BLOCKS_PER_STEP))


def _band_body(q_ref, k_ref, v_ref, kp_ref, vp_ref, o_ref, lse_ref, *, tq):
    first = pl.program_id(2) == 0
    row = lax.broadcasted_iota(jnp.int32, (BAND, 2 * BAND), 0)
    col = lax.broadcasted_iota(jnp.int32, (BAND, 2 * BAND), 1)
    in_band = jnp.logical_or(jnp.logical_and(col < BAND, col >= row),
                             jnp.logical_and(col >= BAND, col - BAND <= row))
    first_mask = jnp.logical_and(in_band, jnp.logical_or(col >= BAND, jnp.logical_not(first)))
    lane = lax.broadcasted_iota(jnp.int32, (BAND, LANES), 1)
    head0 = lane < HEAD_DIM
    for pair in range(PAIRS_PER_MIXER):
        lanes = slice(pair * LANES, (pair + 1) * LANES)
        for c in range(tq // BAND):
            own = slice(c * BAND, (c + 1) * BAND)
            qc = q_ref[own, lanes]
            if c == 0:
                kw = jnp.concatenate([kp_ref[:, lanes], k_ref[own, lanes]], axis=0)
                vw = jnp.concatenate([vp_ref[:, lanes], v_ref[own, lanes]], axis=0)
                mask = first_mask
            else:
                window = slice((c - 1) * BAND, (c + 1) * BAND)
                kw = k_ref[window, lanes]
                vw = v_ref[window, lanes]
                mask = in_band
            outs, lses = [], []
            for h in range(2):
                s = jnp.where(mask, _dot_nt(_own_head(qc, h), kw), NEG)
                m = jnp.max(s, axis=1, keepdims=True)
                p = jnp.exp2(s - m)
                l = jnp.sum(p, axis=1, keepdims=True)
                outs.append(_dot(p.astype(BF16), vw) / l)
                lses.append(m + jnp.log2(l))
            o_ref[own, lanes] = jnp.where(head0, outs[0], outs[1]).astype(BF16)
            lse_ref[own, lanes] = jnp.where(head0, lses[0], lses[1])


def _band_segment(q_arr, k_arr, v_arr, col_of, *, bsz, seq, dil, tq):
    sub_rows = q_arr.shape[0]
    sub_len = seq // dil
    tq = min(tq, sub_len)
    nq = sub_len // tq
    band_per_tq = tq // BAND

    def cur(section):
        return pl.BlockSpec((tq, MIXER_WIDTH), lambda b, r, i: (b * nq + i, col_of(section, r)))

    def prev(section):
        return pl.BlockSpec(
            (BAND, MIXER_WIDTH),
            lambda b, r, i: (jnp.maximum((b * nq + i) * band_per_tq - 1, 0), col_of(section, r)))

    out_spec = pl.BlockSpec((tq, MIXER_WIDTH), lambda b, r, i: (b * nq + i, r))
    return pl.pallas_call(
        functools.partial(_band_body, tq=tq),
        grid=(bsz, dil, nq),
        in_specs=[cur(0), cur(1), cur(2), prev(1), prev(2)],
        out_specs=[out_spec, out_spec],
        out_shape=[
            jax.ShapeDtypeStruct((sub_rows, dil * MIXER_WIDTH), BF16),
            jax.ShapeDtypeStruct((sub_rows, dil * MIXER_WIDTH), F32),
        ],
        compiler_params=pltpu.CompilerParams(
            dimension_semantics=("arbitrary",) * 3, vmem_limit_bytes=VMEM_LIMIT),
        name=f"band_d{dil}",
    )(q_arr, k_arr, v_arr, k_arr, v_arr)


def _online_init(tq):
    return (jnp.full((tq, 1), NEG, F32), jnp.zeros((tq, 1), F32), jnp.zeros((tq, LANES), F32))


def _online_step(carry, s, v_blk):
    m, l, acc = carry
    m_new = jnp.maximum(m, jnp.max(s, axis=1, keepdims=True))
    alpha = jnp.exp2(m - m_new)
    p = jnp.exp2(s - m_new)
    l = alpha * l + jnp.sum(p, axis=1, keepdims=True)
    acc = alpha * acc + _dot(p.astype(BF16), v_blk)
    return m_new, l, acc


def _online_finish(carry):
    _, l, acc = carry
    return acc / l


def _shifted_init(tq):
    return (jnp.zeros((tq, LANES), F32), jnp.zeros((tq, LANES), F32))


def _shifted_step(carry, s, v_blk):
    lane_sums, acc = carry
    p = jnp.exp2(s)
    lane_sums = lane_sums + sum(p[:, c * LANES:(c + 1) * LANES] for c in range(p.shape[1] // LANES))
    return lane_sums, acc + _dot(p.astype(BF16), v_blk)


def _shifted_finish(carry):
    lane_sums, acc = carry
    return acc / jnp.sum(lane_sums, axis=1, keepdims=True)


_ONLINE = (_online_init, _online_step, _online_finish, 1)
_SHIFTED = (_shifted_init, _shifted_step, _shifted_finish, 2)
SHIFTED_MAX_SPAN = 100.0


def _attend_tile(q_aug, k_tile, v_blk, state, step, heads=(0, 1), first_row=None):
    new = list(state)
    for h in heads:
        start = first_row or 0
        s = _dot_nt(q_aug[h][start:] if start else q_aug[h], k_tile(h))
        if first_row is not None:
            row = lax.broadcasted_iota(jnp.int32, s.shape, 0)
            col = lax.broadcasted_iota(jnp.int32, s.shape, 1)
            s = jnp.where(col <= row, s, NEG)
        if start:
            part = step(tuple(a[start:] for a in state[h]), s, v_blk)
            new[h] = tuple(jnp.concatenate([a[:start], b], axis=0) for a, b in zip(state[h], part))
        else:
            new[h] = step(state[h], s, v_blk)
    return tuple(new)


def _attend_diagonal(tile_part, tq, init, parts):
    state = (init(tq), init(tq))
    for d in range(parts):
        state = tile_part(state, d * (tq // parts), tq // parts)
    return state


def _score_bound(q_gain, k_gain):
    bound = 1.02 * HEAD_DIM * SCALE * LOG2E * jnp.max(jnp.abs(q_gain)) * jnp.max(jnp.abs(k_gain))
    return bound.astype(BF16).astype(F32).reshape(1)


def _merge_heads(outs):
    lane = lax.broadcasted_iota(jnp.int32, outs[0].shape, 1)
    return jnp.where(lane < HEAD_DIM, outs[0], outs[1])


def _seq_mixer_specs(*, mixer, nq, tq, seq):
    def colblk(section, p):
        return PROJ_BLOCK_POS[(section, mixer)] * PAIRS_PER_MIXER + p

    q_spec = pl.BlockSpec((tq, LANES), lambda b, p, i: (b * nq + i, colblk(0, p)))
    k_spec = pl.BlockSpec((seq, LANES), lambda b, p, i: (b, colblk(1, p)))
    v_spec = pl.BlockSpec((seq, LANES), lambda b, p, i: (b, colblk(2, p)))
    o_spec = pl.BlockSpec((tq, LANES), lambda b, p, i: (b * nq + i, p))
    return q_spec, k_spec, v_spec, o_spec


_SEQ_PARAMS = pltpu.CompilerParams(
    dimension_semantics=("arbitrary",) * 3, vmem_limit_bytes=VMEM_LIMIT)


def _fox_augment(x, cum, head, h, key_side, bound):
    lane = lax.broadcasted_iota(jnp.int32, x.shape, 1)
    g = jnp.broadcast_to(_lane_column(cum, head) * LOG2E, x.shape)
    g1, g2, g3 = (piece.astype(F32) for piece in _split3(g))
    one = jnp.ones_like(g1)
    pieces = (one, one, one, -g1, -g2, -g3, one) if key_side else (g1, g2, g3, one, one, one, -bound * one)
    base = HEAD_DIM * (1 - h)
    aug = jnp.zeros_like(g1)
    for n, piece in enumerate(pieces):
        aug = jnp.where(lane == base + n, piece, aug)
    return jnp.where(_head_lane_mask(x.shape, h), x.astype(F32), aug).astype(BF16)


def _fox_body(fend_ref, bound_ref, q_ref, k_ref, v_ref, cumq_ref, cumk_ref, o_ref, kaug_scr, *, tq, tk, seq):
    b = pl.program_id(0)
    p = pl.program_id(1)
    iq = pl.program_id(2)
    bound = bound_ref[0]

    @pl.when(iq == 0)
    def _augment_keys():
        def chunk(c, _):
            rows = pl.ds(pl.multiple_of(c * tk, tk), tk)
            for h in range(2):
                kaug_scr[h, rows, :] = _fox_augment(k_ref[rows, :], cumk_ref[rows, :], 2 * p + h, h, True, bound)
            return 0
        lax.fori_loop(0, seq // tk, chunk, 0)

    q = q_ref[...]
    cumq = cumq_ref[...]
    q_aug = [_fox_augment(q, cumq, 2 * p + h, h, False, bound) for h in range(2)]

    first_past = iq - 1
    last = jnp.maximum(first_past, 0)

    def reaches(j, heads):
        j = jnp.maximum(j, 0)
        alive = [LOG2E * (fend_ref[b, 2 * p + h, last] - fend_ref[b, 2 * p + h, j])
                 + 2.0 * bound > UNDERFLOW_LOG2 for h in heads]
        return functools.reduce(jnp.logical_and, alive)

    def attend(scheme):
        init, step, finish, diagonal_parts = scheme

        def tile(j, state, heads=(0, 1)):
            rows = pl.ds(pl.multiple_of(j * tk, tk), tk)
            return _attend_tile(q_aug, lambda h: kaug_scr[h, rows, :], v_ref[rows, :], state, step, heads)

        def diagonal_part(state, key0, nkeys):
            rows = pl.ds(pl.multiple_of(iq * tk + key0, nkeys), nkeys)
            return _attend_tile(q_aug, lambda h: kaug_scr[h, rows, :], v_ref[rows, :], state, step,
                                first_row=key0)

        loop = (first_past, _attend_diagonal(diagonal_part, tq, init, diagonal_parts))
        for heads in ((0, 1), (0,), (1,)):
            loop = lax.while_loop(
                lambda lp: jnp.logical_and(lp[0] >= 0, reaches(lp[0], heads)),
                lambda lp: (lp[0] - 1, tile(lp[0], lp[1], heads)),
                loop)
        o_ref[...] = _merge_heads([finish(st) for st in loop[1]]).astype(BF16)

    shifted_ok = 2.0 * bound <= SHIFTED_MAX_SPAN
    pl.when(shifted_ok)(functools.partial(attend, _SHIFTED))
    pl.when(jnp.logical_not(shifted_ok))(functools.partial(attend, _ONLINE))


def _fox(proj, cum, tile_end_sums, score_bound, *, bsz, seq, tq, tk):
    rows = proj.shape[0]
    nq = seq // tq
    q_spec, k_spec, v_spec, o_spec = _seq_mixer_specs(mixer=3, nq=nq, tq=tq, seq=seq)
    smem = pl.BlockSpec(memory_space=pltpu.SMEM)
    return pl.pallas_call(
        functools.partial(_fox_body, tq=tq, tk=tk, seq=seq),
        grid=(bsz, PAIRS_PER_MIXER, nq),
        in_specs=[
            smem, smem, q_spec, k_spec, v_spec,
            pl.BlockSpec((tq, LANES), lambda b, p, i: (b * nq + i, 0)),
            pl.BlockSpec((seq, LANES), lambda b, p, i: (b, 0)),
        ],
        out_specs=o_spec,
        out_shape=jax.ShapeDtypeStruct((rows, MIXER_WIDTH), BF16),
        scratch_shapes=[pltpu.VMEM((2, seq, LANES), BF16)],
        compiler_params=_SEQ_PARAMS,
        name="fox",
    )(tile_end_sums, score_bound, proj, proj, proj, cum, cum)


def _moba_body(bound_ref, q_ref, k_ref, v_ref, o_ref, kmean_scr, kaug_scr, *, tq, tk, seq):
    iq = pl.program_id(2)
    nblk = seq // MOBA_BLOCK
    blocks_per_tile = tq // MOBA_BLOCK
    bound = bound_ref[0]

    @pl.when(iq == 0)
    def _prepare_keys():
        r = lax.broadcasted_iota(jnp.int32, (LANES, seq), 0)
        c = lax.broadcasted_iota(jnp.int32, (LANES, seq), 1)
        member = jnp.where(c // MOBA_BLOCK == r % HEAD_DIM, 1.0 / MOBA_BLOCK, 0.0).astype(BF16)
        hi, lo = _split2(_dot(member, k_ref[...]))
        kmean_scr[0] = hi
        kmean_scr[1] = lo

        def chunk(c, _):
            rows = pl.ds(pl.multiple_of(c * tk, tk), tk)
            k = k_ref[rows, :].astype(F32)
            lane = lax.broadcasted_iota(jnp.int32, (tk, LANES), 1)
            blk = (c * tk + lax.broadcasted_iota(jnp.int32, (tk, LANES), 0)) // MOBA_BLOCK
            for h in range(2):
                spare = lane - HEAD_DIM * (1 - h)
                onehot = jnp.where(jnp.logical_or(spare == blk, spare == HEAD_DIM - 1), 1.0, 0.0)
                kaug_scr[h, rows, :] = jnp.where(_head_lane_mask(k.shape, h), k, onehot).astype(BF16)
            return 0
        lax.fori_loop(0, seq // tk, chunk, 0)

    q = q_ref[...]
    nslot = -(-nblk // 8) * 8
    blk = lax.broadcasted_iota(jnp.int32, (nslot, tq), 0)
    qblk = blocks_per_tile * iq + lax.broadcasted_iota(jnp.int32, (nslot, tq), 1) // MOBA_BLOCK
    q_aug = []
    tail_row = lax.broadcasted_iota(jnp.int32, (HEAD_DIM - nslot, tq), 0)
    tail = jnp.where(tail_row == HEAD_DIM - nslot - 1, -bound, 0.0)
    for h in range(2):
        qh = _own_head(q, h)
        spare = slice(HEAD_DIM * (1 - h), HEAD_DIM * (1 - h) + nslot)
        gate = _dot_nt(kmean_scr[0, spare, :], qh) + _dot_nt(kmean_scr[1, spare, :], qh)
        past = blk < qblk
        work = jnp.where(past, gate, NEG)
        bias = jnp.where(blk == qblk, 0.0, NEG)
        for _ in range(min(MOBA_TOPK, nblk)):
            best = jnp.max(work, axis=0, keepdims=True)
            idx = jnp.min(jnp.where(work == best, blk, nslot), axis=0, keepdims=True)
            hit = blk == idx
            bias = jnp.where(hit, jnp.where(past, 0.0, bias), bias)
            work = jnp.where(hit, -jnp.inf, work)
        own_half = jnp.zeros((HEAD_DIM, tq), F32)
        halves = [bias, tail, own_half] if h == 1 else [own_half, bias, tail]
        q_aug.append(jnp.where(_head_lane_mask(q.shape, h), q.astype(F32),
                               jnp.concatenate(halves, axis=0).T).astype(BF16))

    def attend(scheme):
        init, step, finish, diagonal_parts = scheme

        def tile(j, state):
            rows = pl.ds(pl.multiple_of(j * tk, tk), tk)
            return _attend_tile(q_aug, lambda h: kaug_scr[h, rows, :], v_ref[rows, :], state, step)

        def diagonal_part(state, key0, nkeys):
            rows = pl.ds(pl.multiple_of(iq * tk + key0, nkeys), nkeys)
            return _attend_tile(q_aug, lambda h: kaug_scr[h, rows, :], v_ref[rows, :], state, step,
                                first_row=key0)

        state = _attend_diagonal(diagonal_part, tq, init, diagonal_parts)
        state = lax.fori_loop(0, iq, tile, state)
        o_ref[...] = _merge_heads([finish(st) for st in state]).astype(BF16)

    shifted_ok = 2.0 * bound <= SHIFTED_MAX_SPAN
    pl.when(shifted_ok)(functools.partial(attend, _SHIFTED))
    pl.when(jnp.logical_not(shifted_ok))(functools.partial(attend, _ONLINE))


def _moba(proj, score_bound, *, bsz, seq, tq, tk):
    assert tq == tk and tq % MOBA_BLOCK == 0
    assert -(-(seq // MOBA_BLOCK) // 8) * 8 < HEAD_DIM
    rows = proj.shape[0]
    nq = seq // tq
    q_spec, k_spec, v_spec, o_spec = _seq_mixer_specs(mixer=2, nq=nq, tq=tq, seq=seq)
    return pl.pallas_call(
        functools.partial(_moba_body, tq=tq, tk=tk, seq=seq),
        grid=(bsz, PAIRS_PER_MIXER, nq),
        in_specs=[pl.BlockSpec(memory_space=pltpu.SMEM), q_spec, k_spec, v_spec],
        out_specs=o_spec,
        out_shape=jax.ShapeDtypeStruct((rows, MIXER_WIDTH), BF16),
        scratch_shapes=[pltpu.VMEM((2, LANES, LANES), BF16), pltpu.VMEM((2, seq, LANES), BF16)],
        compiler_params=_SEQ_PARAMS,
        name="moba",
    )(score_bound, proj, proj, proj)


def _stick_body(q_ref, k_ref, v_ref, o_ref, *, tq, tk):
    iq = pl.program_id(2)
    blocks_per_tile = tq // tk
    q = q_ref[...]
    q_own = [_own_head(q, h) for h in range(2)]
    later_row = lax.broadcasted_iota(jnp.int32, (tk, tk), 0)
    later_col = lax.broadcasted_iota(jnp.int32, (tk, tk), 1)
    later = jnp.where(later_row > later_col, 1.0, 0.0).astype(BF16)

    def block(j, state, first_row=None):
        keys = pl.ds(pl.multiple_of(j * tk, tk), tk)
        k_blk = k_ref[keys, :]
        v_blk = v_ref[keys, :]
        rows = slice(first_row or 0, tq)
        n = tq - rows.start
        if first_row is not None:
            strictly_past = (lax.broadcasted_iota(jnp.int32, (n, tk), 1)
                             < lax.broadcasted_iota(jnp.int32, (n, tk), 0))
        new = []
        for h in range(2):
            carry_sum, acc = state[h]
            z = _dot_nt(q_own[h][rows], k_blk)
            softplus = jnp.maximum(z, 0.0) + jnp.log2(1.0 + jnp.exp2(-jnp.abs(z)))
            log_keep = -softplus
            if first_row is not None:
                log_keep = jnp.where(strictly_past, log_keep, 0.0)
            hi, lo = _split2(log_keep)
            after = _dot(hi, later) + _dot(lo, later) + carry_sum[rows]
            log_a = z - softplus + after
            if first_row is not None:
                log_a = jnp.where(strictly_past, log_a, NEG)
            acc_rows = acc[rows] + _dot(jnp.exp2(log_a).astype(BF16), v_blk)
            sum_rows = carry_sum[rows] + jnp.sum(log_keep, axis=1, keepdims=True)
            if rows.start:
                acc_rows = jnp.concatenate([acc[:rows.start], acc_rows], axis=0)
                sum_rows = jnp.concatenate([carry_sum[:rows.start], sum_rows], axis=0)
            new.append((sum_rows, acc_rows))
        return tuple(new)

    zero = (jnp.zeros((tq, 1), F32), jnp.zeros((tq, LANES), F32))
    state = (zero, zero)
    for d in reversed(range(blocks_per_tile)):
        state = block(iq * blocks_per_tile + d, state, d * tk)

    def cond(loop):
        j, state = loop
        alive = jnp.max(jnp.maximum(state[0][0], state[1][0])) > UNDERFLOW_LOG2
        return jnp.logical_and(j >= 0, alive)

    def body(loop):
        j, state = loop
        return j - 1, block(j, state)

    _, state = lax.while_loop(cond, body, (iq * blocks_per_tile - 1, state))
    o_ref[...] = _merge_heads([acc for _, acc in state]).astype(BF16)


def _stick(proj, *, bsz, seq, tq, tk):
    rows = proj.shape[0]
    nq = seq // tq
    q_spec, k_spec, v_spec, o_spec = _seq_mixer_specs(mixer=1, nq=nq, tq=tq, seq=seq)
    return pl.pallas_call(
        functools.partial(_stick_body, tq=tq, tk=tk),
        grid=(bsz, PAIRS_PER_MIXER, nq),
        in_specs=[q_spec, k_spec, v_spec],
        out_specs=o_spec,
        out_shape=jax.ShapeDtypeStruct((rows, MIXER_WIDTH), BF16),
        compiler_params=_SEQ_PARAMS,
        name="stick",
    )(proj, proj, proj)


def _outproj_body(o1_ref, o4_ref, o16_ref, l1_ref, l4_ref, l16_ref, yb_ref, yc_ref, yd_ref,
                  gate_ref, x_ref, w_ref, out_ref, y_scr, o_scr, l_scr, *, tm):
    j = pl.program_id(1)

    @pl.when(j == 0)
    def _gate():
        for slot, (dil, o_ref, l_ref) in enumerate(zip(STRIDED_DILATIONS, (o4_ref, o16_ref), (l4_ref, l16_ref))):
            n = tm // dil
            for r in range(dil):
                for c in range(LANE_BLOCKS_PER_MIXER):
                    cols = slice(r * MIXER_WIDTH + c * LANES, r * MIXER_WIDTH + (c + 1) * LANES)
                    o_scr[slot, c, pl.ds(r, n, stride=dil), :] = o_ref[:, cols].astype(F32)
                    l_scr[slot, c, pl.ds(r, n, stride=dil), :] = l_ref[:, cols]

        def natural(scr, slot):
            return jnp.concatenate([scr[slot, c] for c in range(LANE_BLOCKS_PER_MIXER)], axis=1)

        l1, l4, l16 = l1_ref[...], natural(l_scr, 0), natural(l_scr, 1)
        m = jnp.maximum(jnp.maximum(l1, l4), l16)
        e1, e4, e16 = jnp.exp2(l1 - m), jnp.exp2(l4 - m), jnp.exp2(l16 - m)
        ya = (e1 * o1_ref[...].astype(F32) + e4 * natural(o_scr, 0)
              + e16 * natural(o_scr, 1)) / (e1 + e4 + e16)
        parts = (ya, yb_ref[...].astype(F32), yc_ref[...].astype(F32), yd_ref[...].astype(F32))
        for mxr, y in enumerate(parts):
            g = gate_ref[:, mxr * MIXER_WIDTH:(mxr + 1) * MIXER_WIDTH].astype(F32)
            silu = g / (1.0 + jnp.exp(-g))
            y_scr[:, mxr * MIXER_WIDTH:(mxr + 1) * MIXER_WIDTH] = (y * silu).astype(BF16)

    out_ref[...] = x_ref[...] + _dot(y_scr[...], w_ref[...])


def _outproj(seg_o, seg_lse, yb, yc, yd, proj, x2, w_out, *, tm, tn):
    rows, d = x2.shape
    row_blk = lambda i, j: (i, 0)
    mix_spec = pl.BlockSpec((tm, MIXER_WIDTH), row_blk)
    seg_specs = [pl.BlockSpec((tm // dil, dil * MIXER_WIDTH), row_blk) for dil in DILATIONS]
    return pl.pallas_call(
        functools.partial(_outproj_body, tm=tm),
        grid=(rows // tm, d // tn),
        in_specs=seg_specs + seg_specs + [mix_spec] * 3 + [
            pl.BlockSpec((tm, MIX_WIDTH), lambda i, j: (i, GATE_GROUP)),
            pl.BlockSpec((tm, tn), lambda i, j: (i, j)),
            pl.BlockSpec((MIX_WIDTH, tn), lambda i, j: (0, j)),
        ],
        out_specs=pl.BlockSpec((tm, tn), lambda i, j: (i, j)),
        out_shape=jax.ShapeDtypeStruct((rows, d), F32),
        scratch_shapes=[pltpu.VMEM((tm, MIX_WIDTH), BF16),
                        pltpu.VMEM((len(STRIDED_DILATIONS), LANE_BLOCKS_PER_MIXER, tm, LANES), F32),
                        pltpu.VMEM((len(STRIDED_DILATIONS), LANE_BLOCKS_PER_MIXER, tm, LANES), F32)],
        compiler_params=pltpu.CompilerParams(
            dimension_semantics=("arbitrary", "arbitrary"), vmem_limit_bytes=VMEM_LIMIT),
        name="outproj",
    )(*seg_o, *seg_lse, yb, yc, yd, proj, x2, w_out)


def _rope_tables(seq):
    inv = 1.0 / (ROPE_THETA ** (jnp.arange(0, HEAD_DIM, 2, dtype=F32) / HEAD_DIM))
    ang = jnp.arange(seq, dtype=F32)[:, None] * inv[None, :]
    cos, sin = jnp.cos(ang), jnp.sin(ang)
    reps = LANES // HEAD_DIM
    cos_t = jnp.tile(jnp.concatenate([cos, cos], axis=1), (1, reps))
    sin_t = jnp.tile(jnp.concatenate([-sin, sin], axis=1), (1, reps))
    return cos_t, sin_t


def _gain_table(qn, kn):
    ones = jnp.ones((HEAD_DIM,), F32)
    per_mixer = {0: (qn[0], ones, qn[1], qn[2]), 1: (kn[0], ones, kn[1], kn[2])}
    blocks = [per_mixer[section][mixer] if section < 2 else ones for section, mixer in PROJ_BLOCK_ORDER]
    tab = jnp.stack([jnp.tile(g.astype(F32), HEADS_PER_MIXER) for g in blocks])
    return tab[:, None, :]


def kernel(x, norm_gain, w_in, q_norm_gain, k_norm_gain, forget_bias, w_out):
    bsz, seq, d = x.shape
    depth = w_in.shape[0]
    rows = bsz * seq
    tm = min(512, seq)
    tq_seq = min(QUERY_TILE, seq)
    cos_t, sin_t = _rope_tables(seq)
    head_of_lane = jnp.arange(MIXER_WIDTH) // HEAD_DIM
    mavg = jnp.where(head_of_lane[:, None] == head_of_lane[None, :], 1.0 / HEAD_DIM, 0.0).astype(BF16)
    tri = jnp.tril(jnp.ones((256, 256), F32)).astype(BF16)

    def natural_col(section, r):
        return PROJ_BLOCK_POS[(section, 0)]

    def strided_col(section, r):
        return r

    x2 = x.reshape(rows, d)
    for layer in range(depth):
        w_l = w_in[layer]
        w_main = w_l[:, :PROJ_COLS].astype(BF16)
        wf = jnp.pad(w_l[:, PROJ_COLS:], ((0, 0), (0, LANES - HEADS_PER_MIXER)))
        wf_hi = wf.astype(BF16)
        wf_lo = (wf - wf_hi.astype(F32)).astype(BF16)
        wf_cat = jnp.concatenate([wf_hi, wf_lo], axis=1)
        fb_pad = jnp.pad(forget_bias[layer].astype(F32), (0, LANES - HEADS_PER_MIXER))[None, :]
        gain_tab = _gain_table(q_norm_gain[layer], k_norm_gain[layer])

        proj, cum, *strided = _inproj(x2, norm_gain[layer][None, :].astype(F32), w_main, wf_cat, fb_pad,
                                      gain_tab, cos_t, sin_t, mavg, tri, seq=seq, tm=tm)
        seg = [_band_segment(proj, proj, proj, natural_col, bsz=bsz, seq=seq, dil=1, tq=512)]
        for di, dil in enumerate(STRIDED_DILATIONS):
            qd, kd, vd = strided[3 * di:3 * di + 3]
            seg.append(_band_segment(qd, kd, vd, strided_col, bsz=bsz, seq=seq, dil=dil, tq=512))
        yb = _stick(proj, bsz=bsz, seq=seq, tq=min(STICK_QUERY_TILE, seq), tk=STICK_KEY_TILE)
        yc = _moba(proj, _score_bound(q_norm_gain[layer, 1], k_norm_gain[layer, 1]),
                   bsz=bsz, seq=seq, tq=tq_seq, tk=KEY_TILE)
        nt = seq // KEY_TILE
        tile_end_sums = cum.reshape(bsz, nt, KEY_TILE, LANES)[:, :, KEY_TILE - 1, :HEADS_PER_MIXER]
        tile_end_sums = tile_end_sums.transpose(0, 2, 1)
        yd = _fox(proj, cum, tile_end_sums, _score_bound(q_norm_gain[layer, 2], k_norm_gain[layer, 2]),
                  bsz=bsz, seq=seq, tq=tq_seq, tk=KEY_TILE)
        x2 = _outproj([s[0] for s in seg], [s[1] for s in seg], yb, yc, yd, proj, x2,
                      w_out[layer].astype(BF16), tm=min(OUT_ROW_TILE, seq), tn=d)
    return x2.reshape(bsz, seq, d)
```

```python
import functools

import jax
import jax.numpy as jnp
from jax import lax
from jax.experimental import pallas as pl
from jax.experimental.pallas import tpu as pltpu

F32 = jnp.float32
BF16 = jnp.bfloat16

HEAD_DIM = 64
HALF_DIM = HEAD_DIM // 2
LANES = 128
N_MIXERS = 4
HEADS_PER_MIXER = 8
PAIRS_PER_MIXER = HEADS_PER_MIXER // 2
MIXER_WIDTH = HEADS_PER_MIXER * HEAD_DIM
LANE_BLOCKS_PER_MIXER = MIXER_WIDTH // LANES
MIX_WIDTH = N_MIXERS * MIXER_WIDTH
N_SECTIONS = 4
PROJ_COLS = N_SECTIONS * MIX_WIDTH
COL_BLOCKS_PER_SECTION = MIX_WIDTH // LANES
ROPE_THETA = 10000.0
RMS_EPS = 1e-6
SCALE = HEAD_DIM ** -0.5
LOG2E = 1.4426950408889634
NEG = -1e30
DILATIONS = (1, 4, 16)
STRIDED_DILATIONS = DILATIONS[1:]
BAND = 128
MOBA_BLOCK = 256
MOBA_TOPK = 3
QUERY_TILE = 1024
KEY_TILE = 1024
STICK_QUERY_TILE = 512
STICK_KEY_TILE = 256
COL_BLOCKS_PER_STEP = 4
PROJ_BLOCK_ORDER = ((0, 0), (1, 0), (0, 2), (1, 2),
                    (0, 3), (1, 3), (0, 1), (1, 1),
                    (2, 0), (2, 1), (2, 2), (2, 3),
                    (3, 0), (3, 1), (3, 2), (3, 3))
PROJ_BLOCK_POS = {block: pos for pos, block in enumerate(PROJ_BLOCK_ORDER)}
GATE_GROUP = PROJ_BLOCK_POS[(3, 0)] // N_MIXERS
OUT_ROW_TILE = 256
UNDERFLOW_LOG2 = -160.0
VMEM_LIMIT = 56 * 1024 * 1024

_NT = (((1,), (1,)), ((), ()))


def _dot(a, b):
    return jnp.dot(a, b, preferred_element_type=F32)


def _dot_nt(a, b):
    return lax.dot_general(a, b, _NT, preferred_element_type=F32)


def _split2(x):
    hi = x.astype(BF16)
    lo = (x - hi.astype(F32)).astype(BF16)
    return hi, lo


def _split3(x):
    b1 = x.astype(BF16)
    r1 = x - b1.astype(F32)
    b2 = r1.astype(BF16)
    r2 = r1 - b2.astype(F32)
    return b1, b2, r2.astype(BF16)


def _head_lane_mask(shape, h):
    lane = lax.broadcasted_iota(jnp.int32, shape, len(shape) - 1)
    return (lane >= h * HEAD_DIM) & (lane < (h + 1) * HEAD_DIM)


def _own_head(x, h):
    return jnp.where(_head_lane_mask(x.shape, h), x, jnp.zeros_like(x))


def _lane_column(x, n):
    lane = lax.broadcasted_iota(jnp.int32, x.shape, 1)
    return jnp.sum(jnp.where(lane == n, x, 0.0), axis=1, keepdims=True)


def _inproj_body(x_ref, g_ref, wf_ref, fb_ref, gain_ref, cos_ref, sin_ref, mavg_ref, tri_ref, *rest,
                 blocks_per_batch, tm):
    w_refs, (proj_ref, cum_ref), rest = rest[:COL_BLOCKS_PER_STEP], rest[COL_BLOCKS_PER_STEP:][:2], \
        rest[COL_BLOCKS_PER_STEP + 2:]
    dil_refs = rest[:3 * len(STRIDED_DILATIONS)]
    h_scr, carry_scr, dil_scr = rest[3 * len(STRIDED_DILATIONS):]
    i = pl.program_id(0)
    j = pl.program_id(1)

    @pl.when(j == 0)
    def _prologue():
        x = x_ref[...]
        ms = jnp.mean(x * x, axis=-1, keepdims=True)
        h = x * lax.rsqrt(ms + RMS_EPS) * g_ref[...]
        h_hi, h_lo = _split2(h)
        h_scr[...] = h_hi
        wf = wf_ref[...]
        t = _dot(h_hi, wf)
        u = _dot(h_lo, wf[:, :LANES])
        logit = t[:, :LANES] + t[:, LANES:] + u + fb_ref[...]
        lf = jnp.minimum(logit, 0.0) - jnp.log1p(jnp.exp(-jnp.abs(logit)))

        @pl.when(i % blocks_per_batch == 0)
        def _():
            carry_scr[...] = jnp.zeros_like(carry_scr)

        carry = carry_scr[...]
        tri = tri_ref[...]
        sub = tri.shape[0]
        for r in range(tm // sub):
            b1, b2, b3 = _split3(lf[r * sub:(r + 1) * sub])
            c = _dot(tri, jnp.concatenate([b1, b2, b3], axis=1))
            c = c[:, :LANES] + c[:, LANES:2 * LANES] + c[:, 2 * LANES:] + carry
            cum_ref[r * sub:(r + 1) * sub, :] = c
            carry = c[sub - 1:sub, :]
        carry_scr[...] = carry

    for group in range(len(PROJ_BLOCK_ORDER) // COL_BLOCKS_PER_STEP):
        @pl.when(j == group)
        def _():
            for sub in range(COL_BLOCKS_PER_STEP):
                section, mixer = PROJ_BLOCK_ORDER[group * COL_BLOCKS_PER_STEP + sub]
                _inproj_column_block(section, mixer, sub * MIXER_WIDTH, h_scr, w_refs[sub], gain_ref.at[sub],
                                     cos_ref, sin_ref, mavg_ref, proj_ref, dil_refs, dil_scr, tm=tm)


def _inproj_column_block(section, mixer, col0, h_scr, w_ref, gain_ref, cos_ref, sin_ref, mavg_ref, proj_ref,
                         dil_refs, dil_scr, *, tm):
    def out_cols(c):
        return slice(col0 + c * LANES, col0 + (c + 1) * LANES)

    acc = _dot(h_scr[...], w_ref[...])
    normed = section < 2 and mixer != 1
    roped = normed and mixer in (0, 2)
    scale = SCALE * LOG2E if section == 0 else 1.0

    if not normed:
        proj_ref[:, col0:col0 + MIXER_WIDTH] = (acc * scale if section == 0 else acc).astype(BF16)
    else:
        ms = _dot((acc * acc).astype(BF16), mavg_ref[...])
        t = acc * lax.rsqrt(ms + RMS_EPS) * (gain_ref[0] * scale)
        if not roped:
            proj_ref[:, col0:col0 + MIXER_WIDTH] = t.astype(BF16)
        else:
            cosv = cos_ref[...]
            sinv = sin_ref[...]
            lane = lax.broadcasted_iota(jnp.int32, (tm, LANES), 1)
            first_half = (lane % HEAD_DIM) < HALF_DIM
            for c in range(LANE_BLOCKS_PER_MIXER):
                tc = t[:, c * LANES:(c + 1) * LANES]
                partner = jnp.where(first_half,
                                    pltpu.roll(tc, LANES - HALF_DIM, 1),
                                    pltpu.roll(tc, HALF_DIM, 1))
                proj_ref[:, out_cols(c)] = (tc * cosv + partner * sinv).astype(BF16)

    if mixer == 0 and section < 3:
        for c in range(LANE_BLOCKS_PER_MIXER):
            dil_scr[c] = proj_ref[:, out_cols(c)].astype(F32)
        for di, dil in enumerate(STRIDED_DILATIONS):
            ref = dil_refs[3 * di + section]
            n = tm // dil
            for r in range(dil):
                for c in range(LANE_BLOCKS_PER_MIXER):
                    lo = r * MIXER_WIDTH + c * LANES
                    ref[:, lo:lo + LANES] = dil_scr[c, pl.ds(r, n, stride=dil), :].astype(BF16)


def _inproj(x2, g, w_main, wf_cat, fb_pad, gain_tab, cos_t, sin_t, mavg, tri, *, seq, tm):
    rows, d = x2.shape
    blocks_per_batch = seq // tm
    step_cols = COL_BLOCKS_PER_STEP * MIXER_WIDTH
    body = functools.partial(_inproj_body, blocks_per_batch=blocks_per_batch, tm=tm)

    def weight_block(sub):
        origin = [section * N_MIXERS + mixer for section, mixer in PROJ_BLOCK_ORDER[sub::COL_BLOCKS_PER_STEP]]

        def index(i, j):
            blk = origin[0]
            for group in range(1, len(origin)):
                blk = jnp.where(j == group, origin[group], blk)
            return 0, blk
        return index

    dil_specs, dil_shapes = [], []
    for dil in STRIDED_DILATIONS:
        for _ in range(3):
            dil_specs.append(pl.BlockSpec((tm // dil, dil * MIXER_WIDTH), lambda i, j: (i, 0)))
            dil_shapes.append(jax.ShapeDtypeStruct((rows // dil, dil * MIXER_WIDTH), BF16))
    return pl.pallas_call(
        body,
        grid=(rows // tm, PROJ_COLS // step_cols),
        in_specs=[
            pl.BlockSpec((tm, d), lambda i, j: (i, 0)),
            pl.BlockSpec((1, d), lambda i, j: (0, 0)),
            pl.BlockSpec((d, 2 * LANES), lambda i, j: (0, 0)),
            pl.BlockSpec((1, LANES), lambda i, j: (0, 0)),
            pl.BlockSpec((COL_BLOCKS_PER_STEP, 1, MIXER_WIDTH), lambda i, j: (j, 0, 0)),
            pl.BlockSpec((tm, LANES), lambda i, j: (i % blocks_per_batch, 0)),
            pl.BlockSpec((tm, LANES), lambda i, j: (i % blocks_per_batch, 0)),
            pl.BlockSpec((MIXER_WIDTH, MIXER_WIDTH), lambda i, j: (0, 0)),
            pl.BlockSpec(tri.shape, lambda i, j: (0, 0)),
        ] + [pl.BlockSpec((d, MIXER_WIDTH), weight_block(sub)) for sub in range(COL_BLOCKS_PER_STEP)],
        out_specs=[
            pl.BlockSpec((tm, step_cols), lambda i, j: (i, j)),
            pl.BlockSpec((tm, LANES), lambda i, j: (i, 0)),
        ] + dil_specs,
        out_shape=[
            jax.ShapeDtypeStruct((rows, PROJ_COLS), BF16),
            jax.ShapeDtypeStruct((rows, LANES), F32),
        ] + dil_shapes,
        scratch_shapes=[pltpu.VMEM((tm, d), BF16), pltpu.VMEM((1, LANES), F32),
                        pltpu.VMEM((LANE_BLOCKS_PER_MIXER, tm, LANES), F32)],
        compiler_params=pltpu.CompilerParams(
            dimension_semantics=("arbitrary", "arbitrary"), vmem_limit_bytes=VMEM_LIMIT),
        name="inproj",
    )(x2, g, wf_cat, fb_pad, gain_tab, cos_t, sin_t, mavg, tri, *([w_main] * COL_BLOCKS_PER_STEP))


def _band_body(q_ref, k_ref, v_ref, kp_ref, vp_ref, o_ref, lse_ref, *, tq):
    first = pl.program_id(2) == 0
    row = lax.broadcasted_iota(jnp.int32, (BAND, 2 * BAND), 0)
    col = lax.broadcasted_iota(jnp.int32, (BAND, 2 * BAND), 1)
    in_band = jnp.logical_or(jnp.logical_and(col < BAND, col >= row),
                             jnp.logical_and(col >= BAND, col - BAND <= row))
    first_mask = jnp.logical_and(in_band, jnp.logical_or(col >= BAND, jnp.logical_not(first)))
    lane = lax.broadcasted_iota(jnp.int32, (BAND, LANES), 1)
    head0 = lane < HEAD_DIM
    for pair in range(PAIRS_PER_MIXER):
        lanes = slice(pair * LANES, (pair + 1) * LANES)
        for c in range(tq // BAND):
            own = slice(c * BAND, (c + 1) * BAND)
            qc = q_ref[own, lanes]
            if c == 0:
                kw = jnp.concatenate([kp_ref[:, lanes], k_ref[own, lanes]], axis=0)
                vw = jnp.concatenate([vp_ref[:, lanes], v_ref[own, lanes]], axis=0)
                mask = first_mask
            else:
                window = slice((c - 1) * BAND, (c + 1) * BAND)
                kw = k_ref[window, lanes]
                vw = v_ref[window, lanes]
                mask = in_band
            outs, lses = [], []
            for h in range(2):
                s = jnp.where(mask, _dot_nt(_own_head(qc, h), kw), NEG)
                m = jnp.max(s, axis=1, keepdims=True)
                p = jnp.exp2(s - m)
                l = jnp.sum(p, axis=1, keepdims=True)
                outs.append(_dot(p.astype(BF16), vw) / l)
                lses.append(m + jnp.log2(l))
            o_ref[own, lanes] = jnp.where(head0, outs[0], outs[1]).astype(BF16)
            lse_ref[own, lanes] = jnp.where(head0, lses[0], lses[1])


def _band_segment(q_arr, k_arr, v_arr, col_of, *, bsz, seq, dil, tq):
    sub_rows = q_arr.shape[0]
    sub_len = seq // dil
    tq = min(tq, sub_len)
    nq = sub_len // tq
    band_per_tq = tq // BAND

    def cur(section):
        return pl.BlockSpec((tq, MIXER_WIDTH), lambda b, r, i: (b * nq + i, col_of(section, r)))

    def prev(section):
        return pl.BlockSpec(
            (BAND, MIXER_WIDTH),
            lambda b, r, i: (jnp.maximum((b * nq + i) * band_per_tq - 1, 0), col_of(section, r)))

    out_spec = pl.BlockSpec((tq, MIXER_WIDTH), lambda b, r, i: (b * nq + i, r))
    return pl.pallas_call(
        functools.partial(_band_body, tq=tq),
        grid=(bsz, dil, nq),
        in_specs=[cur(0), cur(1), cur(2), prev(1), prev(2)],
        out_specs=[out_spec, out_spec],
        out_shape=[
            jax.ShapeDtypeStruct((sub_rows, dil * MIXER_WIDTH), BF16),
            jax.ShapeDtypeStruct((sub_rows, dil * MIXER_WIDTH), F32),
        ],
        compiler_params=pltpu.CompilerParams(
            dimension_semantics=("arbitrary",) * 3, vmem_limit_bytes=VMEM_LIMIT),
        name=f"band_d{dil}",
    )(q_arr, k_arr, v_arr, k_arr, v_arr)


def _online_init(tq):
    return (jnp.full((tq, 1), NEG, F32), jnp.zeros((tq, 1), F32), jnp.zeros((tq, LANES), F32))


def _online_step(carry, s, v_blk):
    m, l, acc = carry
    m_new = jnp.maximum(m, jnp.max(s, axis=1, keepdims=True))
    alpha = jnp.exp2(m - m_new)
    p = jnp.exp2(s - m_new)
    l = alpha * l + jnp.sum(p, axis=1, keepdims=True)
    acc = alpha * acc + _dot(p.astype(BF16), v_blk)
    return m_new, l, acc


def _online_finish(carry, h):
    _, l, acc = carry
    return acc / l


def _shifted_init(tq):
    return (jnp.zeros((tq, LANES), F32),)


def _shifted_step(carry, s, v_blk):
    acc, = carry
    return (acc + _dot(jnp.exp2(s).astype(BF16), v_blk),)


def _shifted_finish(carry, h):
    acc, = carry
    return acc / _lane_column(acc, _ones_lane(h))


_ONLINE = (_online_init, _online_step, _online_finish, 1)
_SHIFTED = (_shifted_init, _shifted_step, _shifted_finish, 2)
SHIFTED_MAX_SPAN = 100.0


def _ones_lane(h):
    return HEAD_DIM * (1 - h)


def _augment_values(v, h):
    lane = lax.broadcasted_iota(jnp.int32, v.shape, 1)
    ones = jnp.where(lane == _ones_lane(h), 1.0, 0.0)
    return jnp.where(_head_lane_mask(v.shape, h), v.astype(F32), ones).astype(BF16)


def _attend_tile(q_aug, k_tile, v_tile, state, step, heads=(0, 1), first_row=None):
    new = list(state)
    for h in heads:
        start = first_row or 0
        s = _dot_nt(q_aug[h][start:] if start else q_aug[h], k_tile(h))
        if first_row is not None:
            row = lax.broadcasted_iota(jnp.int32, s.shape, 0)
            col = lax.broadcasted_iota(jnp.int32, s.shape, 1)
            s = jnp.where(col <= row, s, NEG)
        if start:
            part = step(tuple(a[start:] for a in state[h]), s, v_tile(h))
            new[h] = tuple(jnp.concatenate([a[:start], b], axis=0) for a, b in zip(state[h], part))
        else:
            new[h] = step(state[h], s, v_tile(h))
    return tuple(new)


def _attend_diagonal(tile_part, tq, init, parts):
    state = (init(tq), init(tq))
    for d in range(parts):
        state = tile_part(state, d * (tq // parts), tq // parts)
    return state


def _score_bound(q_gain, k_gain):
    bound = 1.02 * HEAD_DIM * SCALE * LOG2E * jnp.max(jnp.abs(q_gain)) * jnp.max(jnp.abs(k_gain))
    return bound.astype(BF16).astype(F32).reshape(1)


def _merge_heads(outs):
    lane = lax.broadcasted_iota(jnp.int32, outs[0].shape, 1)
    return jnp.where(lane < HEAD_DIM, outs[0], outs[1])


def _seq_mixer_specs(*, mixer, nq, tq, seq):
    def colblk(section, p):
        return PROJ_BLOCK_POS[(section, mixer)] * PAIRS_PER_MIXER + p

    q_spec = pl.BlockSpec((tq, LANES), lambda b, p, i: (b * nq + i, colblk(0, p)))
    k_spec = pl.BlockSpec((seq, LANES), lambda b, p, i: (b, colblk(1, p)))
    v_spec = pl.BlockSpec((seq, LANES), lambda b, p, i: (b, colblk(2, p)))
    o_spec = pl.BlockSpec((tq, LANES), lambda b, p, i: (b * nq + i, p))
    return q_spec, k_spec, v_spec, o_spec


_SEQ_PARAMS = pltpu.CompilerParams(
    dimension_semantics=("arbitrary",) * 3, vmem_limit_bytes=VMEM_LIMIT)


def _fox_augment(x, cum, head, h, key_side, bound):
    lane = lax.broadcasted_iota(jnp.int32, x.shape, 1)
    g = jnp.broadcast_to(_lane_column(cum, head) * LOG2E, x.shape)
    g1, g2, g3 = (piece.astype(F32) for piece in _split3(g))
    one = jnp.ones_like(g1)
    pieces = (one, one, one, -g1, -g2, -g3, one) if key_side else (g1, g2, g3, one, one, one, -bound * one)
    base = HEAD_DIM * (1 - h)
    aug = jnp.zeros_like(g1)
    for n, piece in enumerate(pieces):
        aug = jnp.where(lane == base + n, piece, aug)
    return jnp.where(_head_lane_mask(x.shape, h), x.astype(F32), aug).astype(BF16)


def _fox_body(fend_ref, bound_ref, q_ref, k_ref, v_ref, cumq_ref, cumk_ref, o_ref, kaug_scr, vaug_scr, *,
              tq, tk, seq):
    b = pl.program_id(0)
    p = pl.program_id(1)
    iq = pl.program_id(2)
    bound = bound_ref[0]

    @pl.when(iq == 0)
    def _augment_keys():
        def chunk(c, _):
            rows = pl.ds(pl.multiple_of(c * tk, tk), tk)
            for h in range(2):
                kaug_scr[h, rows, :] = _fox_augment(k_ref[rows, :], cumk_ref[rows, :], 2 * p + h, h, True, bound)
                vaug_scr[h, rows, :] = _augment_values(v_ref[rows, :], h)
            return 0
        lax.fori_loop(0, seq // tk, chunk, 0)

    q = q_ref[...]
    cumq = cumq_ref[...]
    q_aug = [_fox_augment(q, cumq, 2 * p + h, h, False, bound) for h in range(2)]

    first_past = iq - 1
    last = jnp.maximum(first_past, 0)

    def reaches(j, heads):
        j = jnp.maximum(j, 0)
        alive = [LOG2E * (fend_ref[b, 2 * p + h, last] - fend_ref[b, 2 * p + h, j])
                 + 2.0 * bound > UNDERFLOW_LOG2 for h in heads]
        return functools.reduce(jnp.logical_and, alive)

    def attend(scheme):
        init, step, finish, diagonal_parts = scheme

        def tile(j, state, heads=(0, 1)):
            rows = pl.ds(pl.multiple_of(j * tk, tk), tk)
            return _attend_tile(q_aug, lambda h: kaug_scr[h, rows, :], lambda h: vaug_scr[h, rows, :], state,
                                step, heads)

        def diagonal_part(state, key0, nkeys):
            rows = pl.ds(pl.multiple_of(iq * tk + key0, nkeys), nkeys)
            return _attend_tile(q_aug, lambda h: kaug_scr[h, rows, :], lambda h: vaug_scr[h, rows, :], state,
                                step, first_row=key0)

        loop = (first_past, _attend_diagonal(diagonal_part, tq, init, diagonal_parts))
        for heads in ((0, 1), (0,), (1,)):
            loop = lax.while_loop(
                lambda lp: jnp.logical_and(lp[0] >= 0, reaches(lp[0], heads)),
                lambda lp: (lp[0] - 1, tile(lp[0], lp[1], heads)),
                loop)
        o_ref[...] = _merge_heads([finish(st, h) for h, st in enumerate(loop[1])]).astype(BF16)

    shifted_ok = 2.0 * bound <= SHIFTED_MAX_SPAN
    pl.when(shifted_ok)(functools.partial(attend, _SHIFTED))
    pl.when(jnp.logical_not(shifted_ok))(functools.partial(attend, _ONLINE))


def _fox(proj, cum, tile_end_sums, score_bound, *, bsz, seq, tq, tk):
    rows = proj.shape[0]
    nq = seq // tq
    q_spec, k_spec, v_spec, o_spec = _seq_mixer_specs(mixer=3, nq=nq, tq=tq, seq=seq)
    smem = pl.BlockSpec(memory_space=pltpu.SMEM)
    return pl.pallas_call(
        functools.partial(_fox_body, tq=tq, tk=tk, seq=seq),
        grid=(bsz, PAIRS_PER_MIXER, nq),
        in_specs=[
            smem, smem, q_spec, k_spec, v_spec,
            pl.BlockSpec((tq, LANES), lambda b, p, i: (b * nq + i, 0)),
            pl.BlockSpec((seq, LANES), lambda b, p, i: (b, 0)),
        ],
        out_specs=o_spec,
        out_shape=jax.ShapeDtypeStruct((rows, MIXER_WIDTH), BF16),
        scratch_shapes=[pltpu.VMEM((2, seq, LANES), BF16), pltpu.VMEM((2, seq, LANES), BF16)],
        compiler_params=_SEQ_PARAMS,
        name="fox",
    )(tile_end_sums, score_bound, proj, proj, proj, cum, cum)


def _moba_body(bound_ref, q_ref, k_ref, v_ref, o_ref, kmean_scr, kaug_scr, vaug_scr, *, tq, tk, seq):
    iq = pl.program_id(2)
    nblk = seq // MOBA_BLOCK
    blocks_per_tile = tq // MOBA_BLOCK
    bound = bound_ref[0]

    @pl.when(iq == 0)
    def _prepare_keys():
        r = lax.broadcasted_iota(jnp.int32, (LANES, seq), 0)
        c = lax.broadcasted_iota(jnp.int32, (LANES, seq), 1)
        member = jnp.where(c // MOBA_BLOCK == r % HEAD_DIM, 1.0 / MOBA_BLOCK, 0.0).astype(BF16)
        hi, lo = _split2(_dot(member, k_ref[...]))
        kmean_scr[0] = hi
        kmean_scr[1] = lo

        def chunk(c, _):
            rows = pl.ds(pl.multiple_of(c * tk, tk), tk)
            k = k_ref[rows, :].astype(F32)
            lane = lax.broadcasted_iota(jnp.int32, (tk, LANES), 1)
            blk = (c * tk + lax.broadcasted_iota(jnp.int32, (tk, LANES), 0)) // MOBA_BLOCK
            for h in range(2):
                spare = lane - HEAD_DIM * (1 - h)
                onehot = jnp.where(jnp.logical_or(spare == blk, spare == HEAD_DIM - 1), 1.0, 0.0)
                kaug_scr[h, rows, :] = jnp.where(_head_lane_mask(k.shape, h), k, onehot).astype(BF16)
                vaug_scr[h, rows, :] = _augment_values(v_ref[rows, :], h)
            return 0
        lax.fori_loop(0, seq // tk, chunk, 0)

    q = q_ref[...]
    nslot = -(-nblk // 8) * 8
    blk = lax.broadcasted_iota(jnp.int32, (nslot, tq), 0)
    qblk = blocks_per_tile * iq + lax.broadcasted_iota(jnp.int32, (nslot, tq), 1) // MOBA_BLOCK
    q_aug = []
    tail_row = lax.broadcasted_iota(jnp.int32, (HEAD_DIM - nslot, tq), 0)
    tail = jnp.where(tail_row == HEAD_DIM - nslot - 1, -bound, 0.0)
    for h in range(2):
        qh = _own_head(q, h)
        spare = slice(HEAD_DIM * (1 - h), HEAD_DIM * (1 - h) + nslot)
        gate = _dot_nt(kmean_scr[0, spare, :], qh) + _dot_nt(kmean_scr[1, spare, :], qh)
        past = blk < qblk
        work = jnp.where(past, gate, NEG)
        bias = jnp.where(blk == qblk, 0.0, NEG)
        for _ in range(min(MOBA_TOPK, nblk)):
            best = jnp.max(work, axis=0, keepdims=True)
            idx = jnp.min(jnp.where(work == best, blk, nslot), axis=0, keepdims=True)
            hit = blk == idx
            bias = jnp.where(hit, jnp.where(past, 0.0, bias), bias)
            work = jnp.where(hit, -jnp.inf, work)
        own_half = jnp.zeros((HEAD_DIM, tq), F32)
        halves = [bias, tail, own_half] if h == 1 else [own_half, bias, tail]
        q_aug.append(jnp.where(_head_lane_mask(q.shape, h), q.astype(F32),
                               jnp.concatenate(halves, axis=0).T).astype(BF16))

    def attend(scheme):
        init, step, finish, diagonal_parts = scheme

        def tile(j, state):
            rows = pl.ds(pl.multiple_of(j * tk, tk), tk)
            return _attend_tile(q_aug, lambda h: kaug_scr[h, rows, :], lambda h: vaug_scr[h, rows, :], state, step)

        def diagonal_part(state, key0, nkeys):
            rows = pl.ds(pl.multiple_of(iq * tk + key0, nkeys), nkeys)
            return _attend_tile(q_aug, lambda h: kaug_scr[h, rows, :], lambda h: vaug_scr[h, rows, :], state,
                                step, first_row=key0)

        state = _attend_diagonal(diagonal_part, tq, init, diagonal_parts)
        state = lax.fori_loop(0, iq, tile, state)
        o_ref[...] = _merge_heads([finish(st, h) for h, st in enumerate(state)]).astype(BF16)

    shifted_ok = 2.0 * bound <= SHIFTED_MAX_SPAN
    pl.when(shifted_ok)(functools.partial(attend, _SHIFTED))
    pl.when(jnp.logical_not(shifted_ok))(functools.partial(attend, _ONLINE))


def _moba(proj, score_bound, *, bsz, seq, tq, tk):
    assert tq == tk and tq % MOBA_BLOCK == 0
    assert -(-(seq // MOBA_BLOCK) // 8) * 8 < HEAD_DIM
    rows = proj.shape[0]
    nq = seq // tq
    q_spec, k_spec, v_spec, o_spec = _seq_mixer_specs(mixer=2, nq=nq, tq=tq, seq=seq)
    return pl.pallas_call(
        functools.partial(_moba_body, tq=tq, tk=tk, seq=seq),
        grid=(bsz, PAIRS_PER_MIXER, nq),
        in_specs=[pl.BlockSpec(memory_space=pltpu.SMEM), q_spec, k_spec, v_spec],
        out_specs=o_spec,
        out_shape=jax.ShapeDtypeStruct((rows, MIXER_WIDTH), BF16),
        scratch_shapes=[pltpu.VMEM((2, LANES, LANES), BF16), pltpu.VMEM((2, seq, LANES), BF16),
                        pltpu.VMEM((2, seq, LANES), BF16)],
        compiler_params=_SEQ_PARAMS,
        name="moba",
    )(score_bound, proj, proj, proj)


def _stick_body(q_ref, k_ref, v_ref, o_ref, *, tq, tk):
    iq = pl.program_id(2)
    blocks_per_tile = tq // tk
    q = q_ref[...]
    q_own = [_own_head(q, h) for h in range(2)]
    later_row = lax.broadcasted_iota(jnp.int32, (tk, tk), 0)
    later_col = lax.broadcasted_iota(jnp.int32, (tk, tk), 1)
    later = jnp.where(later_row > later_col, 1.0, 0.0).astype(BF16)

    def block(j, state, first_row=None):
        keys = pl.ds(pl.multiple_of(j * tk, tk), tk)
        k_blk = k_ref[keys, :]
        v_blk = v_ref[keys, :]
        rows = slice(first_row or 0, tq)
        n = tq - rows.start
        if first_row is not None:
            strictly_past = (lax.broadcasted_iota(jnp.int32, (n, tk), 1)
                             < lax.broadcasted_iota(jnp.int32, (n, tk), 0))
        new = []
        for h in range(2):
            carry_sum, acc = state[h]
            z = _dot_nt(q_own[h][rows], k_blk)
            softplus = jnp.maximum(z, 0.0) + jnp.log2(1.0 + jnp.exp2(-jnp.abs(z)))
            log_keep = -softplus
            if first_row is not None:
                log_keep = jnp.where(strictly_past, log_keep, 0.0)
            hi, lo = _split2(log_keep)
            after = _dot(hi, later) + _dot(lo, later) + carry_sum[rows]
            log_a = z - softplus + after
            if first_row is not None:
                log_a = jnp.where(strictly_past, log_a, NEG)
            acc_rows = acc[rows] + _dot(jnp.exp2(log_a).astype(BF16), v_blk)
            sum_rows = carry_sum[rows] + jnp.sum(log_keep, axis=1, keepdims=True)
            if rows.start:
                acc_rows = jnp.concatenate([acc[:rows.start], acc_rows], axis=0)
                sum_rows = jnp.concatenate([carry_sum[:rows.start], sum_rows], axis=0)
            new.append((sum_rows, acc_rows))
        return tuple(new)

    zero = (jnp.zeros((tq, 1), F32), jnp.zeros((tq, LANES), F32))
    state = (zero, zero)
    for d in reversed(range(blocks_per_tile)):
        state = block(iq * blocks_per_tile + d, state, d * tk)

    def cond(loop):
        j, state = loop
        alive = jnp.max(jnp.maximum(state[0][0], state[1][0])) > UNDERFLOW_LOG2
        return jnp.logical_and(j >= 0, alive)

    def body(loop):
        j, state = loop
        return j - 1, block(j, state)

    _, state = lax.while_loop(cond, body, (iq * blocks_per_tile - 1, state))
    o_ref[...] = _merge_heads([acc for _, acc in state]).astype(BF16)


def _stick(proj, *, bsz, seq, tq, tk):
    rows = proj.shape[0]
    nq = seq // tq
    q_spec, k_spec, v_spec, o_spec = _seq_mixer_specs(mixer=1, nq=nq, tq=tq, seq=seq)
    return pl.pallas_call(
        functools.partial(_stick_body, tq=tq, tk=tk),
        grid=(bsz, PAIRS_PER_MIXER, nq),
        in_specs=[q_spec, k_spec, v_spec],
        out_specs=o_spec,
        out_shape=jax.ShapeDtypeStruct((rows, MIXER_WIDTH), BF16),
        compiler_params=_SEQ_PARAMS,
        name="stick",
    )(proj, proj, proj)


def _outproj_body(o1_ref, o4_ref, o16_ref, l1_ref, l4_ref, l16_ref, yb_ref, yc_ref, yd_ref,
                  gate_ref, x_ref, w_ref, out_ref, y_scr, o_scr, l_scr, *, tm):
    j = pl.program_id(1)

    @pl.when(j == 0)
    def _gate():
        for slot, (dil, o_ref, l_ref) in enumerate(zip(STRIDED_DILATIONS, (o4_ref, o16_ref), (l4_ref, l16_ref))):
            n = tm // dil
            for r in range(dil):
                for c in range(LANE_BLOCKS_PER_MIXER):
                    cols = slice(r * MIXER_WIDTH + c * LANES, r * MIXER_WIDTH + (c + 1) * LANES)
                    o_scr[slot, c, pl.ds(r, n, stride=dil), :] = o_ref[:, cols].astype(F32)
                    l_scr[slot, c, pl.ds(r, n, stride=dil), :] = l_ref[:, cols]

        def natural(scr, slot):
            return jnp.concatenate([scr[slot, c] for c in range(LANE_BLOCKS_PER_MIXER)], axis=1)

        l1, l4, l16 = l1_ref[...], natural(l_scr, 0), natural(l_scr, 1)
        m = jnp.maximum(jnp.maximum(l1, l4), l16)
        e1, e4, e16 = jnp.exp2(l1 - m), jnp.exp2(l4 - m), jnp.exp2(l16 - m)
        ya = (e1 * o1_ref[...].astype(F32) + e4 * natural(o_scr, 0)
              + e16 * natural(o_scr, 1)) / (e1 + e4 + e16)
        parts = (ya, yb_ref[...].astype(F32), yc_ref[...].astype(F32), yd_ref[...].astype(F32))
        for mxr, y in enumerate(parts):
            g = gate_ref[:, mxr * MIXER_WIDTH:(mxr + 1) * MIXER_WIDTH].astype(F32)
            silu = g / (1.0 + jnp.exp(-g))
            y_scr[:, mxr * MIXER_WIDTH:(mxr + 1) * MIXER_WIDTH] = (y * silu).astype(BF16)

    out_ref[...] = x_ref[...] + _dot(y_scr[...], w_ref[...])


def _outproj(seg_o, seg_lse, yb, yc, yd, proj, x2, w_out, *, tm, tn):
    rows, d = x2.shape
    row_blk = lambda i, j: (i, 0)
    mix_spec = pl.BlockSpec((tm, MIXER_WIDTH), row_blk)
    seg_specs = [pl.BlockSpec((tm // dil, dil * MIXER_WIDTH), row_blk) for dil in DILATIONS]
    return pl.pallas_call(
        functools.partial(_outproj_body, tm=tm),
        grid=(rows // tm, d // tn),
        in_specs=seg_specs + seg_specs + [mix_spec] * 3 + [
            pl.BlockSpec((tm, MIX_WIDTH), lambda i, j: (i, GATE_GROUP)),
            pl.BlockSpec((tm, tn), lambda i, j: (i, j)),
            pl.BlockSpec((MIX_WIDTH, tn), lambda i, j: (0, j)),
        ],
        out_specs=pl.BlockSpec((tm, tn), lambda i, j: (i, j)),
        out_shape=jax.ShapeDtypeStruct((rows, d), F32),
        scratch_shapes=[pltpu.VMEM((tm, MIX_WIDTH), BF16),
                        pltpu.VMEM((len(STRIDED_DILATIONS), LANE_BLOCKS_PER_MIXER, tm, LANES), F32),
                        pltpu.VMEM((len(STRIDED_DILATIONS), LANE_BLOCKS_PER_MIXER, tm, LANES), F32)],
        compiler_params=pltpu.CompilerParams(
            dimension_semantics=("arbitrary", "arbitrary"), vmem_limit_bytes=VMEM_LIMIT),
        name="outproj",
    )(*seg_o, *seg_lse, yb, yc, yd, proj, x2, w_out)


def _rope_tables(seq):
    inv = 1.0 / (ROPE_THETA ** (jnp.arange(0, HEAD_DIM, 2, dtype=F32) / HEAD_DIM))
    ang = jnp.arange(seq, dtype=F32)[:, None] * inv[None, :]
    cos, sin = jnp.cos(ang), jnp.sin(ang)
    reps = LANES // HEAD_DIM
    cos_t = jnp.tile(jnp.concatenate([cos, cos], axis=1), (1, reps))
    sin_t = jnp.tile(jnp.concatenate([-sin, sin], axis=1), (1, reps))
    return cos_t, sin_t


def _gain_table(qn, kn):
    ones = jnp.ones((HEAD_DIM,), F32)
    per_mixer = {0: (qn[0], ones, qn[1], qn[2]), 1: (kn[0], ones, kn[1], kn[2])}
    blocks = [per_mixer[section][mixer] if section < 2 else ones for section, mixer in PROJ_BLOCK_ORDER]
    tab = jnp.stack([jnp.tile(g.astype(F32), HEADS_PER_MIXER) for g in blocks])
    return tab[:, None, :]


def kernel(x, norm_gain, w_in, q_norm_gain, k_norm_gain, forget_bias, w_out):
    bsz, seq, d = x.shape
    depth = w_in.shape[0]
    rows = bsz * seq
    tm = min(512, seq)
    tq_seq = min(QUERY_TILE, seq)
    cos_t, sin_t = _rope_tables(seq)
    head_of_lane = jnp.arange(MIXER_WIDTH) // HEAD_DIM
    mavg = jnp.where(head_of_lane[:, None] == head_of_lane[None, :], 1.0 / HEAD_DIM, 0.0).astype(BF16)
    tri = jnp.tril(jnp.ones((256, 256), F32)).astype(BF16)

    def natural_col(section, r):
        return PROJ_BLOCK_POS[(section, 0)]

    def strided_col(section, r):
        return r

    w_main = w_in[:, :, :PROJ_COLS].astype(BF16)
    wf = jnp.pad(w_in[:, :, PROJ_COLS:], ((0, 0), (0, 0), (0, LANES - HEADS_PER_MIXER)))
    wf_hi = wf.astype(BF16)
    wf_cat = jnp.concatenate([wf_hi, (wf - wf_hi.astype(F32)).astype(BF16)], axis=2)
    w_out_bf16 = w_out.astype(BF16)

    x2 = x.reshape(rows, d)
    for layer in range(depth):
        fb_pad = jnp.pad(forget_bias[layer].astype(F32), (0, LANES - HEADS_PER_MIXER))[None, :]
        gain_tab = _gain_table(q_norm_gain[layer], k_norm_gain[layer])

        proj, cum, *strided = _inproj(x2, norm_gain[layer][None, :].astype(F32), w_main[layer], wf_cat[layer],
                                      fb_pad, gain_tab, cos_t, sin_t, mavg, tri, seq=seq, tm=tm)
        seg = [_band_segment(proj, proj, proj, natural_col, bsz=bsz, seq=seq, dil=1, tq=512)]
        for di, dil in enumerate(STRIDED_DILATIONS):
            qd, kd, vd = strided[3 * di:3 * di + 3]
            seg.append(_band_segment(qd, kd, vd, strided_col, bsz=bsz, seq=seq, dil=dil, tq=512))
        yb = _stick(proj, bsz=bsz, seq=seq, tq=min(STICK_QUERY_TILE, seq), tk=STICK_KEY_TILE)
        yc = _moba(proj, _score_bound(q_norm_gain[layer, 1], k_norm_gain[layer, 1]),
                   bsz=bsz, seq=seq, tq=tq_seq, tk=KEY_TILE)
        nt = seq // KEY_TILE
        tile_end_sums = cum.reshape(bsz, nt, KEY_TILE, LANES)[:, :, KEY_TILE - 1, :HEADS_PER_MIXER]
        tile_end_sums = tile_end_sums.transpose(0, 2, 1)
        yd = _fox(proj, cum, tile_end_sums, _score_bound(q_norm_gain[layer, 2], k_norm_gain[layer, 2]),
                  bsz=bsz, seq=seq, tq=tq_seq, tk=KEY_TILE)
        x2 = _outproj([s[0] for s in seg], [s[1] for s in seg], yb, yc, yd, proj, x2,
                      w_out_bf16[layer], tm=min(OUT_ROW_TILE, seq), tn=d)
    return x2.reshape(bsz, seq, d)
```

```python
import functools

import jax
import jax.numpy as jnp
from jax import lax
from jax.experimental import pallas as pl
from jax.experimental.pallas import tpu as pltpu

F32 = jnp.float32
BF16 = jnp.bfloat16

HEAD_DIM = 64
HALF_DIM = HEAD_DIM // 2
LANES = 128
N_MIXERS = 4
HEADS_PER_MIXER = 8
PAIRS_PER_MIXER = HEADS_PER_MIXER // 2
MIXER_WIDTH = HEADS_PER_MIXER * HEAD_DIM
LANE_BLOCKS_PER_MIXER = MIXER_WIDTH // LANES
MIX_WIDTH = N_MIXERS * MIXER_WIDTH
N_SECTIONS = 4
PROJ_COLS = N_SECTIONS * MIX_WIDTH
COL_BLOCKS_PER_SECTION = MIX_WIDTH // LANES
ROPE_THETA = 10000.0
RMS_EPS = 1e-6
SCALE = HEAD_DIM ** -0.5
LOG2E = 1.4426950408889634
NEG = -1e30
DILATIONS = (1, 4, 16)
STRIDED_DILATIONS = DILATIONS[1:]
BAND = 128
MOBA_BLOCK = 256
MOBA_TOPK = 3
QUERY_TILE = 1024
KEY_TILE = 1024
STICK_QUERY_TILE = 512
STICK_KEY_TILE = 256
COL_BLOCKS_PER_STEP = 4
PROJ_BLOCK_ORDER = ((0, 0), (1, 0), (0, 2), (1, 2),
                    (0, 3), (1, 3), (0, 1), (1, 1),
                    (2, 0), (2, 1), (2, 2), (2, 3),
                    (3, 0), (3, 1), (3, 2), (3, 3))
PROJ_BLOCK_POS = {block: pos for pos, block in enumerate(PROJ_BLOCK_ORDER)}
GATE_GROUP = PROJ_BLOCK_POS[(3, 0)] // N_MIXERS
CAST_COLS = 1024
OUT_ROW_TILE = 256
UNDERFLOW_LOG2 = -160.0
VMEM_LIMIT = 56 * 1024 * 1024

_NT = (((1,), (1,)), ((), ()))


def _dot(a, b):
    return jnp.dot(a, b, preferred_element_type=F32)


def _dot_nt(a, b):
    return lax.dot_general(a, b, _NT, preferred_element_type=F32)


def _split2(x):
    hi = x.astype(BF16)
    lo = (x - hi.astype(F32)).astype(BF16)
    return hi, lo


def _split3(x):
    b1 = x.astype(BF16)
    r1 = x - b1.astype(F32)
    b2 = r1.astype(BF16)
    r2 = r1 - b2.astype(F32)
    return b1, b2, r2.astype(BF16)


def _head_lane_mask(shape, h):
    lane = lax.broadcasted_iota(jnp.int32, shape, len(shape) - 1)
    return (lane >= h * HEAD_DIM) & (lane < (h + 1) * HEAD_DIM)


def _own_head(x, h):
    return jnp.where(_head_lane_mask(x.shape, h), x, jnp.zeros_like(x))


def _lane_column(x, n):
    lane = lax.broadcasted_iota(jnp.int32, x.shape, 1)
    return jnp.sum(jnp.where(lane == n, x, 0.0), axis=1, keepdims=True)


def _inproj_body(x_ref, g_ref, wf_ref, fb_ref, gain_ref, cos_ref, sin_ref, mavg_ref, tri_ref, *rest,
                 blocks_per_batch, tm):
    w_refs, (proj_ref, cum_ref), rest = rest[:COL_BLOCKS_PER_STEP], rest[COL_BLOCKS_PER_STEP:][:2], \
        rest[COL_BLOCKS_PER_STEP + 2:]
    dil_refs = rest[:3 * len(STRIDED_DILATIONS)]
    h_scr, carry_scr, dil_scr = rest[3 * len(STRIDED_DILATIONS):]
    i = pl.program_id(0)
    j = pl.program_id(1)

    @pl.when(j == 0)
    def _prologue():
        x = x_ref[...]
        ms = jnp.mean(x * x, axis=-1, keepdims=True)
        h = x * lax.rsqrt(ms + RMS_EPS) * g_ref[...]
        h_hi, h_lo = _split2(h)
        h_scr[...] = h_hi
        wf = wf_ref[...]
        t = _dot(h_hi, wf)
        u = _dot(h_lo, wf[:, :LANES])
        logit = t[:, :LANES] + t[:, LANES:] + u + fb_ref[...]
        lf = jnp.minimum(logit, 0.0) - jnp.log1p(jnp.exp(-jnp.abs(logit)))

        @pl.when(i % blocks_per_batch == 0)
        def _():
            carry_scr[...] = jnp.zeros_like(carry_scr)

        carry = carry_scr[...]
        tri = tri_ref[...]
        sub = tri.shape[0]
        for r in range(tm // sub):
            b1, b2, b3 = _split3(lf[r * sub:(r + 1) * sub])
            c = _dot(tri, jnp.concatenate([b1, b2, b3], axis=1))
            c = c[:, :LANES] + c[:, LANES:2 * LANES] + c[:, 2 * LANES:] + carry
            cum_ref[r * sub:(r + 1) * sub, :] = c
            carry = c[sub - 1:sub, :]
        carry_scr[...] = carry

    for group in range(len(PROJ_BLOCK_ORDER) // COL_BLOCKS_PER_STEP):
        @pl.when(j == group)
        def _():
            for sub in range(COL_BLOCKS_PER_STEP):
                section, mixer = PROJ_BLOCK_ORDER[group * COL_BLOCKS_PER_STEP + sub]
                _inproj_column_block(section, mixer, sub * MIXER_WIDTH, h_scr, w_refs[sub], gain_ref.at[sub],
                                     cos_ref, sin_ref, mavg_ref, proj_ref, dil_refs, dil_scr, tm=tm)


def _inproj_column_block(section, mixer, col0, h_scr, w_ref, gain_ref, cos_ref, sin_ref, mavg_ref, proj_ref,
                         dil_refs, dil_scr, *, tm):
    def out_cols(c):
        return slice(col0 + c * LANES, col0 + (c + 1) * LANES)

    acc = _dot(h_scr[...], w_ref[...])
    normed = section < 2 and mixer != 1
    roped = normed and mixer in (0, 2)
    scale = SCALE * LOG2E if section == 0 else 1.0

    if not normed:
        proj_ref[:, col0:col0 + MIXER_WIDTH] = (acc * scale if section == 0 else acc).astype(BF16)
    else:
        ms = _dot((acc * acc).astype(BF16), mavg_ref[...])
        t = acc * lax.rsqrt(ms + RMS_EPS) * (gain_ref[0] * scale)
        if not roped:
            proj_ref[:, col0:col0 + MIXER_WIDTH] = t.astype(BF16)
        else:
            cosv = cos_ref[...]
            sinv = sin_ref[...]
            lane = lax.broadcasted_iota(jnp.int32, (tm, LANES), 1)
            first_half = (lane % HEAD_DIM) < HALF_DIM
            for c in range(LANE_BLOCKS_PER_MIXER):
                tc = t[:, c * LANES:(c + 1) * LANES]
                partner = jnp.where(first_half,
                                    pltpu.roll(tc, LANES - HALF_DIM, 1),
                                    pltpu.roll(tc, HALF_DIM, 1))
                proj_ref[:, out_cols(c)] = (tc * cosv + partner * sinv).astype(BF16)

    if mixer == 0 and section < 3:
        for c in range(LANE_BLOCKS_PER_MIXER):
            dil_scr[c] = proj_ref[:, out_cols(c)].astype(F32)
        for di, dil in enumerate(STRIDED_DILATIONS):
            ref = dil_refs[3 * di + section]
            n = tm // dil
            for r in range(dil):
                for c in range(LANE_BLOCKS_PER_MIXER):
                    lo = r * MIXER_WIDTH + c * LANES
                    ref[:, lo:lo + LANES] = dil_scr[c, pl.ds(r, n, stride=dil), :].astype(BF16)


def _inproj(x2, g, w_main, wf_cat, fb_pad, gain_tab, cos_t, sin_t, mavg, tri, *, seq, tm):
    rows, d = x2.shape
    blocks_per_batch = seq // tm
    step_cols = COL_BLOCKS_PER_STEP * MIXER_WIDTH
    body = functools.partial(_inproj_body, blocks_per_batch=blocks_per_batch, tm=tm)

    def weight_block(sub):
        origin = [section * N_MIXERS + mixer for section, mixer in PROJ_BLOCK_ORDER[sub::COL_BLOCKS_PER_STEP]]

        def index(i, j):
            blk = origin[0]
            for group in range(1, len(origin)):
                blk = jnp.where(j == group, origin[group], blk)
            return 0, blk
        return index

    dil_specs, dil_shapes = [], []
    for dil in STRIDED_DILATIONS:
        for _ in range(3):
            dil_specs.append(pl.BlockSpec((tm // dil, dil * MIXER_WIDTH), lambda i, j: (i, 0)))
            dil_shapes.append(jax.ShapeDtypeStruct((rows // dil, dil * MIXER_WIDTH), BF16))
    return pl.pallas_call(
        body,
        grid=(rows // tm, PROJ_COLS // step_cols),
        in_specs=[
            pl.BlockSpec((tm, d), lambda i, j: (i, 0)),
            pl.BlockSpec((1, d), lambda i, j: (0, 0)),
            pl.BlockSpec((d, 2 * LANES), lambda i, j: (0, 0)),
            pl.BlockSpec((1, LANES), lambda i, j: (0, 0)),
            pl.BlockSpec((COL_BLOCKS_PER_STEP, 1, MIXER_WIDTH), lambda i, j: (j, 0, 0)),
            pl.BlockSpec((tm, LANES), lambda i, j: (i % blocks_per_batch, 0)),
            pl.BlockSpec((tm, LANES), lambda i, j: (i % blocks_per_batch, 0)),
            pl.BlockSpec((MIXER_WIDTH, MIXER_WIDTH), lambda i, j: (0, 0)),
            pl.BlockSpec(tri.shape, lambda i, j: (0, 0)),
        ] + [pl.BlockSpec((d, MIXER_WIDTH), weight_block(sub)) for sub in range(COL_BLOCKS_PER_STEP)],
        out_specs=[
            pl.BlockSpec((tm, step_cols), lambda i, j: (i, j)),
            pl.BlockSpec((tm, LANES), lambda i, j: (i, 0)),
        ] + dil_specs,
        out_shape=[
            jax.ShapeDtypeStruct((rows, PROJ_COLS), BF16),
            jax.ShapeDtypeStruct((rows, LANES), F32),
        ] + dil_shapes,
        scratch_shapes=[pltpu.VMEM((tm, d), BF16), pltpu.VMEM((1, LANES), F32),
                        pltpu.VMEM((LANE_BLOCKS_PER_MIXER, tm, LANES), F32)],
        compiler_params=pltpu.CompilerParams(
            dimension_semantics=("arbitrary", "arbitrary"), vmem_limit_bytes=VMEM_LIMIT),
        name="inproj",
    )(x2, g, wf_cat, fb_pad, gain_tab, cos_t, sin_t, mavg, tri, *([w_main] * COL_BLOCKS_PER_STEP))


def _band_body(q_ref, k_ref, v_ref, kp_ref, vp_ref, o_ref, lse_ref, *, tq):
    first = pl.program_id(2) == 0
    row = lax.broadcasted_iota(jnp.int32, (BAND, 2 * BAND), 0)
    col = lax.broadcasted_iota(jnp.int32, (BAND, 2 * BAND), 1)
    in_band = jnp.logical_or(jnp.logical_and(col < BAND, col >= row),
                             jnp.logical_and(col >= BAND, col - BAND <= row))
    first_mask = jnp.logical_and(in_band, jnp.logical_or(col >= BAND, jnp.logical_not(first)))
    lane = lax.broadcasted_iota(jnp.int32, (BAND, LANES), 1)
    head0 = lane < HEAD_DIM
    for pair in range(PAIRS_PER_MIXER):
        lanes = slice(pair * LANES, (pair + 1) * LANES)
        for c in range(tq // BAND):
            own = slice(c * BAND, (c + 1) * BAND)
            qc = q_ref[own, lanes]
            if c == 0:
                kw = jnp.concatenate([kp_ref[:, lanes], k_ref[own, lanes]], axis=0)
                vw = jnp.concatenate([vp_ref[:, lanes], v_ref[own, lanes]], axis=0)
                mask = first_mask
            else:
                window = slice((c - 1) * BAND, (c + 1) * BAND)
                kw = k_ref[window, lanes]
                vw = v_ref[window, lanes]
                mask = in_band
            outs, lses = [], []
            for h in range(2):
                s = jnp.where(mask, _dot_nt(_own_head(qc, h), kw), NEG)
                m = jnp.max(s, axis=1, keepdims=True)
                p = jnp.exp2(s - m)
                l = jnp.sum(p, axis=1, keepdims=True)
                outs.append(_dot(p.astype(BF16), vw) / l)
                lses.append(m + jnp.log2(l))
            o_ref[own, lanes] = jnp.where(head0, outs[0], outs[1]).astype(BF16)
            lse_ref[own, lanes] = jnp.where(head0, lses[0], lses[1])


def _band_segment(q_arr, k_arr, v_arr, col_of, *, bsz, seq, dil, tq):
    sub_rows = q_arr.shape[0]
    sub_len = seq // dil
    tq = min(tq, sub_len)
    nq = sub_len // tq
    band_per_tq = tq // BAND

    def cur(section):
        return pl.BlockSpec((tq, MIXER_WIDTH), lambda b, r, i: (b * nq + i, col_of(section, r)))

    def prev(section):
        return pl.BlockSpec(
            (BAND, MIXER_WIDTH),
            lambda b, r, i: (jnp.maximum((b * nq + i) * band_per_tq - 1, 0), col_of(section, r)))

    out_spec = pl.BlockSpec((tq, MIXER_WIDTH), lambda b, r, i: (b * nq + i, r))
    return pl.pallas_call(
        functools.partial(_band_body, tq=tq),
        grid=(bsz, dil, nq),
        in_specs=[cur(0), cur(1), cur(2), prev(1), prev(2)],
        out_specs=[out_spec, out_spec],
        out_shape=[
            jax.ShapeDtypeStruct((sub_rows, dil * MIXER_WIDTH), BF16),
            jax.ShapeDtypeStruct((sub_rows, dil * MIXER_WIDTH), F32),
        ],
        compiler_params=pltpu.CompilerParams(
            dimension_semantics=("arbitrary",) * 3, vmem_limit_bytes=VMEM_LIMIT),
        name=f"band_d{dil}",
    )(q_arr, k_arr, v_arr, k_arr, v_arr)


def _online_init(tq):
    return (jnp.full((tq, 1), NEG, F32), jnp.zeros((tq, 1), F32), jnp.zeros((tq, LANES), F32))


def _online_step(carry, s, v_blk):
    m, l, acc = carry
    m_new = jnp.maximum(m, jnp.max(s, axis=1, keepdims=True))
    alpha = jnp.exp2(m - m_new)
    p = jnp.exp2(s - m_new)
    l = alpha * l + jnp.sum(p, axis=1, keepdims=True)
    acc = alpha * acc + _dot(p.astype(BF16), v_blk)
    return m_new, l, acc


def _online_finish(carry, h):
    _, l, acc = carry
    return acc / l


def _shifted_init(tq):
    return (jnp.zeros((tq, LANES), F32),)


def _shifted_step(carry, s, v_blk):
    acc, = carry
    return (acc + _dot(jnp.exp2(s).astype(BF16), v_blk),)


def _shifted_finish(carry, h):
    acc, = carry
    return acc / _lane_column(acc, _ones_lane(h))


_ONLINE = (_online_init, _online_step, _online_finish, 1)
_SHIFTED = (_shifted_init, _shifted_step, _shifted_finish, 2)
SHIFTED_MAX_SPAN = 100.0


def _ones_lane(h):
    return HEAD_DIM * (1 - h)


def _augment_values(v, h):
    lane = lax.broadcasted_iota(jnp.int32, v.shape, 1)
    ones = jnp.where(lane == _ones_lane(h), 1.0, 0.0)
    return jnp.where(_head_lane_mask(v.shape, h), v.astype(F32), ones).astype(BF16)


def _attend_tile(q_aug, k_tile, v_tile, state, step, heads=(0, 1), first_row=None):
    new = list(state)
    for h in heads:
        start = first_row or 0
        s = _dot_nt(q_aug[h][start:] if start else q_aug[h], k_tile(h))
        if first_row is not None:
            row = lax.broadcasted_iota(jnp.int32, s.shape, 0)
            col = lax.broadcasted_iota(jnp.int32, s.shape, 1)
            s = jnp.where(col <= row, s, NEG)
        if start:
            part = step(tuple(a[start:] for a in state[h]), s, v_tile(h))
            new[h] = tuple(jnp.concatenate([a[:start], b], axis=0) for a, b in zip(state[h], part))
        else:
            new[h] = step(state[h], s, v_tile(h))
    return tuple(new)


def _attend_diagonal(tile_part, tq, init, parts):
    state = (init(tq), init(tq))
    for d in range(parts):
        state = tile_part(state, d * (tq // parts), tq // parts)
    return state


def _score_bound(q_gain, k_gain):
    bound = 1.02 * HEAD_DIM * SCALE * LOG2E * jnp.max(jnp.abs(q_gain)) * jnp.max(jnp.abs(k_gain))
    return bound.astype(BF16).astype(F32).reshape(1)


def _merge_heads(outs):
    lane = lax.broadcasted_iota(jnp.int32, outs[0].shape, 1)
    return jnp.where(lane < HEAD_DIM, outs[0], outs[1])


def _seq_mixer_specs(*, mixer, nq, tq, seq):
    def colblk(section, p):
        return PROJ_BLOCK_POS[(section, mixer)] * PAIRS_PER_MIXER + p

    q_spec = pl.BlockSpec((tq, LANES), lambda b, p, i: (b * nq + i, colblk(0, p)))
    k_spec = pl.BlockSpec((seq, LANES), lambda b, p, i: (b, colblk(1, p)))
    v_spec = pl.BlockSpec((seq, LANES), lambda b, p, i: (b, colblk(2, p)))
    o_spec = pl.BlockSpec((tq, LANES), lambda b, p, i: (b * nq + i, p))
    return q_spec, k_spec, v_spec, o_spec


_SEQ_PARAMS = pltpu.CompilerParams(
    dimension_semantics=("arbitrary",) * 3, vmem_limit_bytes=VMEM_LIMIT)


def _fox_augment(x, cum, head, h, key_side, bound):
    lane = lax.broadcasted_iota(jnp.int32, x.shape, 1)
    g = jnp.broadcast_to(_lane_column(cum, head) * LOG2E, x.shape)
    g1, g2, g3 = (piece.astype(F32) for piece in _split3(g))
    one = jnp.ones_like(g1)
    pieces = (one, one, one, -g1, -g2, -g3, one) if key_side else (g1, g2, g3, one, one, one, -bound * one)
    base = HEAD_DIM * (1 - h)
    aug = jnp.zeros_like(g1)
    for n, piece in enumerate(pieces):
        aug = jnp.where(lane == base + n, piece, aug)
    return jnp.where(_head_lane_mask(x.shape, h), x.astype(F32), aug).astype(BF16)


def _fox_body(fend_ref, bound_ref, q_ref, k_ref, v_ref, cumq_ref, cumk_ref, o_ref, kaug_scr, vaug_scr, *,
              tq, tk, seq):
    b = pl.program_id(0)
    p = pl.program_id(1)
    iq = pl.program_id(2)
    bound = bound_ref[0]

    @pl.when(iq == 0)
    def _augment_keys():
        def chunk(c, _):
            rows = pl.ds(pl.multiple_of(c * tk, tk), tk)
            for h in range(2):
                kaug_scr[h, rows, :] = _fox_augment(k_ref[rows, :], cumk_ref[rows, :], 2 * p + h, h, True, bound)
                vaug_scr[h, rows, :] = _augment_values(v_ref[rows, :], h)
            return 0
        lax.fori_loop(0, seq // tk, chunk, 0)

    q = q_ref[...]
    cumq = cumq_ref[...]
    q_aug = [_fox_augment(q, cumq, 2 * p + h, h, False, bound) for h in range(2)]

    first_past = iq - 1
    last = jnp.maximum(first_past, 0)

    def reaches(j, heads):
        j = jnp.maximum(j, 0)
        alive = [LOG2E * (fend_ref[b, 2 * p + h, last] - fend_ref[b, 2 * p + h, j])
                 + 2.0 * bound > UNDERFLOW_LOG2 for h in heads]
        return functools.reduce(jnp.logical_and, alive)

    def attend(scheme):
        init, step, finish, diagonal_parts = scheme

        def tile(j, state, heads=(0, 1)):
            rows = pl.ds(pl.multiple_of(j * tk, tk), tk)
            return _attend_tile(q_aug, lambda h: kaug_scr[h, rows, :], lambda h: vaug_scr[h, rows, :], state,
                                step, heads)

        def diagonal_part(state, key0, nkeys):
            rows = pl.ds(pl.multiple_of(iq * tk + key0, nkeys), nkeys)
            return _attend_tile(q_aug, lambda h: kaug_scr[h, rows, :], lambda h: vaug_scr[h, rows, :], state,
                                step, first_row=key0)

        loop = (first_past, _attend_diagonal(diagonal_part, tq, init, diagonal_parts))
        for heads in ((0, 1), (0,), (1,)):
            loop = lax.while_loop(
                lambda lp: jnp.logical_and(lp[0] >= 0, reaches(lp[0], heads)),
                lambda lp: (lp[0] - 1, tile(lp[0], lp[1], heads)),
                loop)
        o_ref[...] = _merge_heads([finish(st, h) for h, st in enumerate(loop[1])]).astype(BF16)

    shifted_ok = 2.0 * bound <= SHIFTED_MAX_SPAN
    pl.when(shifted_ok)(functools.partial(attend, _SHIFTED))
    pl.when(jnp.logical_not(shifted_ok))(functools.partial(attend, _ONLINE))


def _fox(proj, cum, tile_end_sums, score_bound, *, bsz, seq, tq, tk):
    rows = proj.shape[0]
    nq = seq // tq
    q_spec, k_spec, v_spec, o_spec = _seq_mixer_specs(mixer=3, nq=nq, tq=tq, seq=seq)
    smem = pl.BlockSpec(memory_space=pltpu.SMEM)
    return pl.pallas_call(
        functools.partial(_fox_body, tq=tq, tk=tk, seq=seq),
        grid=(bsz, PAIRS_PER_MIXER, nq),
        in_specs=[
            smem, smem, q_spec, k_spec, v_spec,
            pl.BlockSpec((tq, LANES), lambda b, p, i: (b * nq + i, 0)),
            pl.BlockSpec((seq, LANES), lambda b, p, i: (b, 0)),
        ],
        out_specs=o_spec,
        out_shape=jax.ShapeDtypeStruct((rows, MIXER_WIDTH), BF16),
        scratch_shapes=[pltpu.VMEM((2, seq, LANES), BF16), pltpu.VMEM((2, seq, LANES), BF16)],
        compiler_params=_SEQ_PARAMS,
        name="fox",
    )(tile_end_sums, score_bound, proj, proj, proj, cum, cum)


def _moba_body(bound_ref, q_ref, k_ref, v_ref, o_ref, kmean_scr, kaug_scr, vaug_scr, *, tq, tk, seq):
    iq = pl.program_id(2)
    nblk = seq // MOBA_BLOCK
    blocks_per_tile = tq // MOBA_BLOCK
    bound = bound_ref[0]

    @pl.when(iq == 0)
    def _prepare_keys():
        r = lax.broadcasted_iota(jnp.int32, (LANES, seq), 0)
        c = lax.broadcasted_iota(jnp.int32, (LANES, seq), 1)
        member = jnp.where(c // MOBA_BLOCK == r % HEAD_DIM, 1.0 / MOBA_BLOCK, 0.0).astype(BF16)
        hi, lo = _split2(_dot(member, k_ref[...]))
        kmean_scr[0] = hi
        kmean_scr[1] = lo

        def chunk(c, _):
            rows = pl.ds(pl.multiple_of(c * tk, tk), tk)
            k = k_ref[rows, :].astype(F32)
            lane = lax.broadcasted_iota(jnp.int32, (tk, LANES), 1)
            blk = (c * tk + lax.broadcasted_iota(jnp.int32, (tk, LANES), 0)) // MOBA_BLOCK
            for h in range(2):
                spare = lane - HEAD_DIM * (1 - h)
                onehot = jnp.where(jnp.logical_or(spare == blk, spare == HEAD_DIM - 1), 1.0, 0.0)
                kaug_scr[h, rows, :] = jnp.where(_head_lane_mask(k.shape, h), k, onehot).astype(BF16)
                vaug_scr[h, rows, :] = _augment_values(v_ref[rows, :], h)
            return 0
        lax.fori_loop(0, seq // tk, chunk, 0)

    q = q_ref[...]
    nslot = -(-nblk // 8) * 8
    blk = lax.broadcasted_iota(jnp.int32, (nslot, tq), 0)
    qblk = blocks_per_tile * iq + lax.broadcasted_iota(jnp.int32, (nslot, tq), 1) // MOBA_BLOCK
    q_aug = []
    tail_row = lax.broadcasted_iota(jnp.int32, (HEAD_DIM - nslot, tq), 0)
    tail = jnp.where(tail_row == HEAD_DIM - nslot - 1, -bound, 0.0)
    for h in range(2):
        qh = _own_head(q, h)
        spare = slice(HEAD_DIM * (1 - h), HEAD_DIM * (1 - h) + nslot)
        gate = _dot_nt(kmean_scr[0, spare, :], qh) + _dot_nt(kmean_scr[1, spare, :], qh)
        past = blk < qblk
        work = jnp.where(past, gate, NEG)
        bias = jnp.where(blk == qblk, 0.0, NEG)
        for _ in range(min(MOBA_TOPK, nblk)):
            best = jnp.max(work, axis=0, keepdims=True)
            idx = jnp.min(jnp.where(work == best, blk, nslot), axis=0, keepdims=True)
            hit = blk == idx
            bias = jnp.where(hit, jnp.where(past, 0.0, bias), bias)
            work = jnp.where(hit, -jnp.inf, work)
        own_half = jnp.zeros((HEAD_DIM, tq), F32)
        halves = [bias, tail, own_half] if h == 1 else [own_half, bias, tail]
        q_aug.append(jnp.where(_head_lane_mask(q.shape, h), q.astype(F32),
                               jnp.concatenate(halves, axis=0).T).astype(BF16))

    def attend(scheme):
        init, step, finish, diagonal_parts = scheme

        def tile(j, state):
            rows = pl.ds(pl.multiple_of(j * tk, tk), tk)
            return _attend_tile(q_aug, lambda h: kaug_scr[h, rows, :], lambda h: vaug_scr[h, rows, :], state, step)

        def diagonal_part(state, key0, nkeys):
            rows = pl.ds(pl.multiple_of(iq * tk + key0, nkeys), nkeys)
            return _attend_tile(q_aug, lambda h: kaug_scr[h, rows, :], lambda h: vaug_scr[h, rows, :], state,
                                step, first_row=key0)

        state = _attend_diagonal(diagonal_part, tq, init, diagonal_parts)
        state = lax.fori_loop(0, iq, tile, state)
        o_ref[...] = _merge_heads([finish(st, h) for h, st in enumerate(state)]).astype(BF16)

    shifted_ok = 2.0 * bound <= SHIFTED_MAX_SPAN
    pl.when(shifted_ok)(functools.partial(attend, _SHIFTED))
    pl.when(jnp.logical_not(shifted_ok))(functools.partial(attend, _ONLINE))


def _moba(proj, score_bound, *, bsz, seq, tq, tk):
    assert tq == tk and tq % MOBA_BLOCK == 0
    assert -(-(seq // MOBA_BLOCK) // 8) * 8 < HEAD_DIM
    rows = proj.shape[0]
    nq = seq // tq
    q_spec, k_spec, v_spec, o_spec = _seq_mixer_specs(mixer=2, nq=nq, tq=tq, seq=seq)
    return pl.pallas_call(
        functools.partial(_moba_body, tq=tq, tk=tk, seq=seq),
        grid=(bsz, PAIRS_PER_MIXER, nq),
        in_specs=[pl.BlockSpec(memory_space=pltpu.SMEM), q_spec, k_spec, v_spec],
        out_specs=o_spec,
        out_shape=jax.ShapeDtypeStruct((rows, MIXER_WIDTH), BF16),
        scratch_shapes=[pltpu.VMEM((2, LANES, LANES), BF16), pltpu.VMEM((2, seq, LANES), BF16),
                        pltpu.VMEM((2, seq, LANES), BF16)],
        compiler_params=_SEQ_PARAMS,
        name="moba",
    )(score_bound, proj, proj, proj)


def _stick_body(q_ref, k_ref, v_ref, o_ref, *, tq, tk):
    iq = pl.program_id(2)
    blocks_per_tile = tq // tk
    q = q_ref[...]
    q_own = [_own_head(q, h) for h in range(2)]
    later_row = lax.broadcasted_iota(jnp.int32, (tk, tk), 0)
    later_col = lax.broadcasted_iota(jnp.int32, (tk, tk), 1)
    later = jnp.where(later_row > later_col, 1.0, 0.0).astype(BF16)

    def block(j, state, first_row=None):
        keys = pl.ds(pl.multiple_of(j * tk, tk), tk)
        k_blk = k_ref[keys, :]
        v_blk = v_ref[keys, :]
        rows = slice(first_row or 0, tq)
        n = tq - rows.start
        if first_row is not None:
            strictly_past = (lax.broadcasted_iota(jnp.int32, (n, tk), 1)
                             < lax.broadcasted_iota(jnp.int32, (n, tk), 0))
        new = []
        for h in range(2):
            carry_sum, acc = state[h]
            z = _dot_nt(q_own[h][rows], k_blk)
            softplus = jnp.maximum(z, 0.0) + jnp.log2(1.0 + jnp.exp2(-jnp.abs(z)))
            log_keep = -softplus
            if first_row is not None:
                log_keep = jnp.where(strictly_past, log_keep, 0.0)
            hi, lo = _split2(log_keep)
            after = _dot(hi, later) + _dot(lo, later) + carry_sum[rows]
            log_a = z - softplus + after
            if first_row is not None:
                log_a = jnp.where(strictly_past, log_a, NEG)
            acc_rows = acc[rows] + _dot(jnp.exp2(log_a).astype(BF16), v_blk)
            sum_rows = carry_sum[rows] + jnp.sum(log_keep, axis=1, keepdims=True)
            if rows.start:
                acc_rows = jnp.concatenate([acc[:rows.start], acc_rows], axis=0)
                sum_rows = jnp.concatenate([carry_sum[:rows.start], sum_rows], axis=0)
            new.append((sum_rows, acc_rows))
        return tuple(new)

    zero = (jnp.zeros((tq, 1), F32), jnp.zeros((tq, LANES), F32))
    state = (zero, zero)
    for d in reversed(range(blocks_per_tile)):
        state = block(iq * blocks_per_tile + d, state, d * tk)

    def cond(loop):
        j, state = loop
        alive = jnp.max(jnp.maximum(state[0][0], state[1][0])) > UNDERFLOW_LOG2
        return jnp.logical_and(j >= 0, alive)

    def body(loop):
        j, state = loop
        return j - 1, block(j, state)

    _, state = lax.while_loop(cond, body, (iq * blocks_per_tile - 1, state))
    o_ref[...] = _merge_heads([acc for _, acc in state]).astype(BF16)


def _stick(proj, *, bsz, seq, tq, tk):
    rows = proj.shape[0]
    nq = seq // tq
    q_spec, k_spec, v_spec, o_spec = _seq_mixer_specs(mixer=1, nq=nq, tq=tq, seq=seq)
    return pl.pallas_call(
        functools.partial(_stick_body, tq=tq, tk=tk),
        grid=(bsz, PAIRS_PER_MIXER, nq),
        in_specs=[q_spec, k_spec, v_spec],
        out_specs=o_spec,
        out_shape=jax.ShapeDtypeStruct((rows, MIXER_WIDTH), BF16),
        compiler_params=_SEQ_PARAMS,
        name="stick",
    )(proj, proj, proj)


def _outproj_body(o1_ref, o4_ref, o16_ref, l1_ref, l4_ref, l16_ref, yb_ref, yc_ref, yd_ref,
                  gate_ref, x_ref, w_ref, out_ref, y_scr, o_scr, l_scr, *, tm):
    j = pl.program_id(1)

    @pl.when(j == 0)
    def _gate():
        for slot, (dil, o_ref, l_ref) in enumerate(zip(STRIDED_DILATIONS, (o4_ref, o16_ref), (l4_ref, l16_ref))):
            n = tm // dil
            for r in range(dil):
                for c in range(LANE_BLOCKS_PER_MIXER):
                    cols = slice(r * MIXER_WIDTH + c * LANES, r * MIXER_WIDTH + (c + 1) * LANES)
                    o_scr[slot, c, pl.ds(r, n, stride=dil), :] = o_ref[:, cols].astype(F32)
                    l_scr[slot, c, pl.ds(r, n, stride=dil), :] = l_ref[:, cols]

        def natural(scr, slot):
            return jnp.concatenate([scr[slot, c] for c in range(LANE_BLOCKS_PER_MIXER)], axis=1)

        l1, l4, l16 = l1_ref[...], natural(l_scr, 0), natural(l_scr, 1)
        m = jnp.maximum(jnp.maximum(l1, l4), l16)
        e1, e4, e16 = jnp.exp2(l1 - m), jnp.exp2(l4 - m), jnp.exp2(l16 - m)
        ya = (e1 * o1_ref[...].astype(F32) + e4 * natural(o_scr, 0)
              + e16 * natural(o_scr, 1)) / (e1 + e4 + e16)
        parts = (ya, yb_ref[...].astype(F32), yc_ref[...].astype(F32), yd_ref[...].astype(F32))
        for mxr, y in enumerate(parts):
            g = gate_ref[:, mxr * MIXER_WIDTH:(mxr + 1) * MIXER_WIDTH].astype(F32)
            silu = g / (1.0 + jnp.exp(-g))
            y_scr[:, mxr * MIXER_WIDTH:(mxr + 1) * MIXER_WIDTH] = (y * silu).astype(BF16)

    out_ref[...] = x_ref[...] + _dot(y_scr[...], w_ref[...])


def _outproj(seg_o, seg_lse, yb, yc, yd, proj, x2, w_out, *, tm, tn):
    rows, d = x2.shape
    row_blk = lambda i, j: (i, 0)
    mix_spec = pl.BlockSpec((tm, MIXER_WIDTH), row_blk)
    seg_specs = [pl.BlockSpec((tm // dil, dil * MIXER_WIDTH), row_blk) for dil in DILATIONS]
    return pl.pallas_call(
        functools.partial(_outproj_body, tm=tm),
        grid=(rows // tm, d // tn),
        in_specs=seg_specs + seg_specs + [mix_spec] * 3 + [
            pl.BlockSpec((tm, MIX_WIDTH), lambda i, j: (i, GATE_GROUP)),
            pl.BlockSpec((tm, tn), lambda i, j: (i, j)),
            pl.BlockSpec((MIX_WIDTH, tn), lambda i, j: (0, j)),
        ],
        out_specs=pl.BlockSpec((tm, tn), lambda i, j: (i, j)),
        out_shape=jax.ShapeDtypeStruct((rows, d), F32),
        scratch_shapes=[pltpu.VMEM((tm, MIX_WIDTH), BF16),
                        pltpu.VMEM((len(STRIDED_DILATIONS), LANE_BLOCKS_PER_MIXER, tm, LANES), F32),
                        pltpu.VMEM((len(STRIDED_DILATIONS), LANE_BLOCKS_PER_MIXER, tm, LANES), F32)],
        compiler_params=pltpu.CompilerParams(
            dimension_semantics=("arbitrary", "arbitrary"), vmem_limit_bytes=VMEM_LIMIT),
        name="outproj",
    )(*seg_o, *seg_lse, yb, yc, yd, proj, x2, w_out)


def _cast_body(w_ref, o_ref):
    o_ref[...] = w_ref[...].astype(BF16)


def _cast_columns(w, cols):
    depth, d, _ = w.shape
    step = min(CAST_COLS, cols)
    spec = pl.BlockSpec((1, d, step), lambda layer, j: (layer, 0, j))
    return pl.pallas_call(
        _cast_body,
        grid=(depth, cols // step),
        in_specs=[spec],
        out_specs=spec,
        out_shape=jax.ShapeDtypeStruct((depth, d, cols), BF16),
        compiler_params=pltpu.CompilerParams(
            dimension_semantics=("arbitrary", "arbitrary"), vmem_limit_bytes=VMEM_LIMIT),
        name="cast_weights",
    )(w)


def _rope_tables(seq):
    inv = 1.0 / (ROPE_THETA ** (jnp.arange(0, HEAD_DIM, 2, dtype=F32) / HEAD_DIM))
    ang = jnp.arange(seq, dtype=F32)[:, None] * inv[None, :]
    cos, sin = jnp.cos(ang), jnp.sin(ang)
    reps = LANES // HEAD_DIM
    cos_t = jnp.tile(jnp.concatenate([cos, cos], axis=1), (1, reps))
    sin_t = jnp.tile(jnp.concatenate([-sin, sin], axis=1), (1, reps))
    return cos_t, sin_t


def _gain_table(qn, kn):
    ones = jnp.ones((HEAD_DIM,), F32)
    per_mixer = {0: (qn[0], ones, qn[1], qn[2]), 1: (kn[0], ones, kn[1], kn[2])}
    blocks = [per_mixer[section][mixer] if section < 2 else ones for section, mixer in PROJ_BLOCK_ORDER]
    tab = jnp.stack([jnp.tile(g.astype(F32), HEADS_PER_MIXER) for g in blocks])
    return tab[:, None, :]


def kernel(x, norm_gain, w_in, q_norm_gain, k_norm_gain, forget_bias, w_out):
    bsz, seq, d = x.shape
    depth = w_in.shape[0]
    rows = bsz * seq
    tm = min(512, seq)
    tq_seq = min(QUERY_TILE, seq)
    cos_t, sin_t = _rope_tables(seq)
    head_of_lane = jnp.arange(MIXER_WIDTH) // HEAD_DIM
    mavg = jnp.where(head_of_lane[:, None] == head_of_lane[None, :], 1.0 / HEAD_DIM, 0.0).astype(BF16)
    tri = jnp.tril(jnp.ones((256, 256), F32)).astype(BF16)

    def natural_col(section, r):
        return PROJ_BLOCK_POS[(section, 0)]

    def strided_col(section, r):
        return r

    w_main = _cast_columns(w_in, PROJ_COLS)
    wf = jnp.pad(w_in[:, :, PROJ_COLS:], ((0, 0), (0, 0), (0, LANES - HEADS_PER_MIXER)))
    wf_hi = wf.astype(BF16)
    wf_cat = jnp.concatenate([wf_hi, (wf - wf_hi.astype(F32)).astype(BF16)], axis=2)
    w_out_bf16 = _cast_columns(w_out, w_out.shape[2])

    x2 = x.reshape(rows, d)
    for layer in range(depth):
        fb_pad = jnp.pad(forget_bias[layer].astype(F32), (0, LANES - HEADS_PER_MIXER))[None, :]
        gain_tab = _gain_table(q_norm_gain[layer], k_norm_gain[layer])

        proj, cum, *strided = _inproj(x2, norm_gain[layer][None, :].astype(F32), w_main[layer], wf_cat[layer],
                                      fb_pad, gain_tab, cos_t, sin_t, mavg, tri, seq=seq, tm=tm)
        seg = [_band_segment(proj, proj, proj, natural_col, bsz=bsz, seq=seq, dil=1, tq=512)]
        for di, dil in enumerate(STRIDED_DILATIONS):
            qd, kd, vd = strided[3 * di:3 * di + 3]
            seg.append(_band_segment(qd, kd, vd, strided_col, bsz=bsz, seq=seq, dil=dil, tq=512))
        yb = _stick(proj, bsz=bsz, seq=seq, tq=min(STICK_QUERY_TILE, seq), tk=STICK_KEY_TILE)
        yc = _moba(proj, _score_bound(q_norm_gain[layer, 1], k_norm_gain[layer, 1]),
                   bsz=bsz, seq=seq, tq=tq_seq, tk=KEY_TILE)
        nt = seq // KEY_TILE
        tile_end_sums = cum.reshape(bsz, nt, KEY_TILE, LANES)[:, :, KEY_TILE - 1, :HEADS_PER_MIXER]
        tile_end_sums = tile_end_sums.transpose(0, 2, 1)
        yd = _fox(proj, cum, tile_end_sums, _score_bound(q_norm_gain[layer, 2], k_norm_gain[layer, 2]),
                  bsz=bsz, seq=seq, tq=tq_seq, tk=KEY_TILE)
        x2 = _outproj([s[0] for s in seg], [s[1] for s in seg], yb, yc, yd, proj, x2,
                      w_out_bf16[layer], tm=min(OUT_ROW_TILE, seq), tn=d)
    return x2.reshape(bsz, seq, d)
```

```python
import functools

import jax
import jax.numpy as jnp
from jax import lax
from jax.experimental import pallas as pl
from jax.experimental.pallas import tpu as pltpu

F32 = jnp.float32
BF16 = jnp.bfloat16

HEAD_DIM = 64
HALF_DIM = HEAD_DIM // 2
LANES = 128
N_MIXERS = 4
HEADS_PER_MIXER = 8
PAIRS_PER_MIXER = HEADS_PER_MIXER // 2
MIXER_WIDTH = HEADS_PER_MIXER * HEAD_DIM
LANE_BLOCKS_PER_MIXER = MIXER_WIDTH // LANES
MIX_WIDTH = N_MIXERS * MIXER_WIDTH
N_SECTIONS = 4
PROJ_COLS = N_SECTIONS * MIX_WIDTH
COL_BLOCKS_PER_SECTION = MIX_WIDTH // LANES
ROPE_THETA = 10000.0
RMS_EPS = 1e-6
SCALE = HEAD_DIM ** -0.5
LOG2E = 1.4426950408889634
NEG = -1e30
DILATIONS = (1, 4, 16)
STRIDED_DILATIONS = DILATIONS[1:]
BAND = 128
MOBA_BLOCK = 256
MOBA_TOPK = 3
QUERY_TILE = 1024
KEY_TILE = 1024
STICK_QUERY_TILE = 512
STICK_KEY_TILE = 256
COL_BLOCKS_PER_STEP = 4
PROJ_BLOCK_ORDER = ((0, 0), (1, 0), (0, 2), (1, 2),
                    (0, 3), (1, 3), (0, 1), (1, 1),
                    (2, 0), (2, 1), (2, 2), (2, 3),
                    (3, 0), (3, 1), (3, 2), (3, 3))
PROJ_BLOCK_POS = {block: pos for pos, block in enumerate(PROJ_BLOCK_ORDER)}
GATE_GROUP = PROJ_BLOCK_POS[(3, 0)] // N_MIXERS
CAST_COLS = 1024
OUT_ROW_TILE = 256
UNDERFLOW_LOG2 = -160.0
VMEM_LIMIT = 56 * 1024 * 1024

_NT = (((1,), (1,)), ((), ()))


def _dot(a, b):
    return jnp.dot(a, b, preferred_element_type=F32)


def _dot_nt(a, b):
    return lax.dot_general(a, b, _NT, preferred_element_type=F32)


def _split2(x):
    hi = x.astype(BF16)
    lo = (x - hi.astype(F32)).astype(BF16)
    return hi, lo


def _split3(x):
    b1 = x.astype(BF16)
    r1 = x - b1.astype(F32)
    b2 = r1.astype(BF16)
    r2 = r1 - b2.astype(F32)
    return b1, b2, r2.astype(BF16)


def _head_lane_mask(shape, h):
    lane = lax.broadcasted_iota(jnp.int32, shape, len(shape) - 1)
    return (lane >= h * HEAD_DIM) & (lane < (h + 1) * HEAD_DIM)


def _own_head(x, h):
    return jnp.where(_head_lane_mask(x.shape, h), x, jnp.zeros_like(x))


def _lane_column(x, n):
    lane = lax.broadcasted_iota(jnp.int32, x.shape, 1)
    return jnp.sum(jnp.where(lane == n, x, 0.0), axis=1, keepdims=True)


def _inproj_body(x_ref, g_ref, wf_ref, fb_ref, gain_ref, cos_ref, sin_ref, mavg_ref, tri_ref, *rest,
                 blocks_per_batch, tm):
    w_refs, (proj_ref, cum_ref), rest = rest[:COL_BLOCKS_PER_STEP], rest[COL_BLOCKS_PER_STEP:][:2], \
        rest[COL_BLOCKS_PER_STEP + 2:]
    dil_refs = rest[:3 * len(STRIDED_DILATIONS)]
    h_scr, carry_scr, dil_scr = rest[3 * len(STRIDED_DILATIONS):]
    i = pl.program_id(0)
    j = pl.program_id(1)

    @pl.when(j == 0)
    def _prologue():
        x = x_ref[...]
        ms = jnp.mean(x * x, axis=-1, keepdims=True)
        h = x * lax.rsqrt(ms + RMS_EPS) * g_ref[...]
        h_hi, h_lo = _split2(h)
        h_scr[...] = h_hi
        wf = wf_ref[...]
        t = _dot(h_hi, wf)
        u = _dot(h_lo, wf[:, :LANES])
        logit = t[:, :LANES] + t[:, LANES:] + u + fb_ref[...]
        lf = jnp.minimum(logit, 0.0) - jnp.log1p(jnp.exp(-jnp.abs(logit)))

        @pl.when(i % blocks_per_batch == 0)
        def _():
            carry_scr[...] = jnp.zeros_like(carry_scr)

        carry = carry_scr[...]
        tri = tri_ref[...]
        sub = tri.shape[0]
        for r in range(tm // sub):
            b1, b2, b3 = _split3(lf[r * sub:(r + 1) * sub])
            c = _dot(tri, jnp.concatenate([b1, b2, b3], axis=1))
            c = c[:, :LANES] + c[:, LANES:2 * LANES] + c[:, 2 * LANES:] + carry
            cum_ref[r * sub:(r + 1) * sub, :] = c
            carry = c[sub - 1:sub, :]
        carry_scr[...] = carry

    for group in range(len(PROJ_BLOCK_ORDER) // COL_BLOCKS_PER_STEP):
        @pl.when(j == group)
        def _():
            for sub in range(COL_BLOCKS_PER_STEP):
                section, mixer = PROJ_BLOCK_ORDER[group * COL_BLOCKS_PER_STEP + sub]
                _inproj_column_block(section, mixer, sub * MIXER_WIDTH, h_scr, w_refs[sub], gain_ref.at[sub],
                                     cos_ref, sin_ref, mavg_ref, proj_ref, dil_refs, dil_scr, tm=tm)


def _inproj_column_block(section, mixer, col0, h_scr, w_ref, gain_ref, cos_ref, sin_ref, mavg_ref, proj_ref,
                         dil_refs, dil_scr, *, tm):
    def out_cols(c):
        return slice(col0 + c * LANES, col0 + (c + 1) * LANES)

    acc = _dot(h_scr[...], w_ref[...])
    normed = section < 2 and mixer != 1
    roped = normed and mixer in (0, 2)
    scale = SCALE * LOG2E if section == 0 else 1.0

    if not normed:
        proj_ref[:, col0:col0 + MIXER_WIDTH] = (acc * scale if section == 0 else acc).astype(BF16)
    else:
        ms = _dot((acc * acc).astype(BF16), mavg_ref[...])
        t = acc * lax.rsqrt(ms + RMS_EPS) * (gain_ref[0] * scale)
        if not roped:
            proj_ref[:, col0:col0 + MIXER_WIDTH] = t.astype(BF16)
        else:
            cosv = cos_ref[...]
            sinv = sin_ref[...]
            lane = lax.broadcasted_iota(jnp.int32, (tm, LANES), 1)
            first_half = (lane % HEAD_DIM) < HALF_DIM
            for c in range(LANE_BLOCKS_PER_MIXER):
                tc = t[:, c * LANES:(c + 1) * LANES]
                partner = jnp.where(first_half,
                                    pltpu.roll(tc, LANES - HALF_DIM, 1),
                                    pltpu.roll(tc, HALF_DIM, 1))
                proj_ref[:, out_cols(c)] = (tc * cosv + partner * sinv).astype(BF16)

    if mixer == 0 and section < 3:
        for c in range(LANE_BLOCKS_PER_MIXER):
            dil_scr[c] = proj_ref[:, out_cols(c)].astype(F32)
        for di, dil in enumerate(STRIDED_DILATIONS):
            ref = dil_refs[3 * di + section]
            n = tm // dil
            for r in range(dil):
                for c in range(LANE_BLOCKS_PER_MIXER):
                    lo = r * MIXER_WIDTH + c * LANES
                    ref[:, lo:lo + LANES] = dil_scr[c, pl.ds(r, n, stride=dil), :].astype(BF16)


def _inproj(x2, g, w_main, wf_cat, fb_pad, gain_tab, cos_t, sin_t, mavg, tri, *, seq, tm):
    rows, d = x2.shape
    blocks_per_batch = seq // tm
    step_cols = COL_BLOCKS_PER_STEP * MIXER_WIDTH
    body = functools.partial(_inproj_body, blocks_per_batch=blocks_per_batch, tm=tm)

    def weight_block(sub):
        origin = [section * N_MIXERS + mixer for section, mixer in PROJ_BLOCK_ORDER[sub::COL_BLOCKS_PER_STEP]]

        def index(i, j):
            blk = origin[0]
            for group in range(1, len(origin)):
                blk = jnp.where(j == group, origin[group], blk)
            return 0, blk
        return index

    dil_specs, dil_shapes = [], []
    for dil in STRIDED_DILATIONS:
        for _ in range(3):
            dil_specs.append(pl.BlockSpec((tm // dil, dil * MIXER_WIDTH), lambda i, j: (i, 0)))
            dil_shapes.append(jax.ShapeDtypeStruct((rows // dil, dil * MIXER_WIDTH), BF16))
    return pl.pallas_call(
        body,
        grid=(rows // tm, PROJ_COLS // step_cols),
        in_specs=[
            pl.BlockSpec((tm, d), lambda i, j: (i, 0)),
            pl.BlockSpec((1, d), lambda i, j: (0, 0)),
            pl.BlockSpec((d, 2 * LANES), lambda i, j: (0, 0)),
            pl.BlockSpec((1, LANES), lambda i, j: (0, 0)),
            pl.BlockSpec((COL_BLOCKS_PER_STEP, 1, MIXER_WIDTH), lambda i, j: (j, 0, 0)),
            pl.BlockSpec((tm, LANES), lambda i, j: (i % blocks_per_batch, 0)),
            pl.BlockSpec((tm, LANES), lambda i, j: (i % blocks_per_batch, 0)),
            pl.BlockSpec((MIXER_WIDTH, MIXER_WIDTH), lambda i, j: (0, 0)),
            pl.BlockSpec(tri.shape, lambda i, j: (0, 0)),
        ] + [pl.BlockSpec((d, MIXER_WIDTH), weight_block(sub)) for sub in range(COL_BLOCKS_PER_STEP)],
        out_specs=[
            pl.BlockSpec((tm, step_cols), lambda i, j: (i, j)),
            pl.BlockSpec((tm, LANES), lambda i, j: (i, 0)),
        ] + dil_specs,
        out_shape=[
            jax.ShapeDtypeStruct((rows, PROJ_COLS), BF16),
            jax.ShapeDtypeStruct((rows, LANES), F32),
        ] + dil_shapes,
        scratch_shapes=[pltpu.VMEM((tm, d), BF16), pltpu.VMEM((1, LANES), F32),
                        pltpu.VMEM((LANE_BLOCKS_PER_MIXER, tm, LANES), F32)],
        compiler_params=pltpu.CompilerParams(
            dimension_semantics=("arbitrary", "arbitrary"), vmem_limit_bytes=VMEM_LIMIT),
        name="inproj",
    )(x2, g, wf_cat, fb_pad, gain_tab, cos_t, sin_t, mavg, tri, *([w_main] * COL_BLOCKS_PER_STEP))


def _band_body(q_ref, k_ref, v_ref, kp_ref, vp_ref, o_ref, lse_ref, *, tq):
    first = pl.program_id(2) == 0
    row = lax.broadcasted_iota(jnp.int32, (BAND, 2 * BAND), 0)
    col = lax.broadcasted_iota(jnp.int32, (BAND, 2 * BAND), 1)
    in_band = jnp.logical_or(jnp.logical_and(col < BAND, col >= row),
                             jnp.logical_and(col >= BAND, col - BAND <= row))
    first_mask = jnp.logical_and(in_band, jnp.logical_or(col >= BAND, jnp.logical_not(first)))
    lane = lax.broadcasted_iota(jnp.int32, (BAND, LANES), 1)
    head0 = lane < HEAD_DIM
    for pair in range(PAIRS_PER_MIXER):
        lanes = slice(pair * LANES, (pair + 1) * LANES)
        for c in range(tq // BAND):
            own = slice(c * BAND, (c + 1) * BAND)
            qc = q_ref[own, lanes]
            if c == 0:
                kw = jnp.concatenate([kp_ref[:, lanes], k_ref[own, lanes]], axis=0)
                vw = jnp.concatenate([vp_ref[:, lanes], v_ref[own, lanes]], axis=0)
                mask = first_mask
            else:
                window = slice((c - 1) * BAND, (c + 1) * BAND)
                kw = k_ref[window, lanes]
                vw = v_ref[window, lanes]
                mask = in_band
            outs, lses = [], []
            for h in range(2):
                s = jnp.where(mask, _dot_nt(_own_head(qc, h), kw), NEG)
                m = jnp.max(s, axis=1, keepdims=True)
                p = jnp.exp2(s - m)
                l = jnp.sum(p, axis=1, keepdims=True)
                outs.append(_dot(p.astype(BF16), vw) / l)
                lses.append(m + jnp.log2(l))
            o_ref[own, lanes] = jnp.where(head0, outs[0], outs[1]).astype(BF16)
            lse_ref[own, lanes] = jnp.where(head0, lses[0], lses[1])


def _band_segment(q_arr, k_arr, v_arr, col_of, *, bsz, seq, dil, tq):
    sub_rows = q_arr.shape[0]
    sub_len = seq // dil
    tq = min(tq, sub_len)
    nq = sub_len // tq
    band_per_tq = tq // BAND

    def cur(section):
        return pl.BlockSpec((tq, MIXER_WIDTH), lambda b, r, i: (b * nq + i, col_of(section, r)))

    def prev(section):
        return pl.BlockSpec(
            (BAND, MIXER_WIDTH),
            lambda b, r, i: (jnp.maximum((b * nq + i) * band_per_tq - 1, 0), col_of(section, r)))

    out_spec = pl.BlockSpec((tq, MIXER_WIDTH), lambda b, r, i: (b * nq + i, r))
    return pl.pallas_call(
        functools.partial(_band_body, tq=tq),
        grid=(bsz, dil, nq),
        in_specs=[cur(0), cur(1), cur(2), prev(1), prev(2)],
        out_specs=[out_spec, out_spec],
        out_shape=[
            jax.ShapeDtypeStruct((sub_rows, dil * MIXER_WIDTH), BF16),
            jax.ShapeDtypeStruct((sub_rows, dil * MIXER_WIDTH), F32),
        ],
        compiler_params=pltpu.CompilerParams(
            dimension_semantics=("arbitrary",) * 3, vmem_limit_bytes=VMEM_LIMIT),
        name=f"band_d{dil}",
    )(q_arr, k_arr, v_arr, k_arr, v_arr)


def _online_init(tq):
    return (jnp.full((tq, 1), NEG, F32), jnp.zeros((tq, 1), F32), jnp.zeros((tq, LANES), F32))


def _online_step(carry, s, v_blk):
    m, l, acc = carry
    m_new = jnp.maximum(m, jnp.max(s, axis=1, keepdims=True))
    alpha = jnp.exp2(m - m_new)
    p = jnp.exp2(s - m_new)
    l = alpha * l + jnp.sum(p, axis=1, keepdims=True)
    acc = alpha * acc + _dot(p.astype(BF16), v_blk)
    return m_new, l, acc


def _online_finish(carry, h):
    _, l, acc = carry
    return acc / l


def _shifted_init(tq):
    return (jnp.zeros((tq, LANES), F32),)


def _shifted_step(carry, s, v_blk):
    acc, = carry
    return (acc + _dot(jnp.exp2(s).astype(BF16), v_blk),)


def _shifted_finish(carry, h):
    acc, = carry
    return acc / _lane_column(acc, _ones_lane(h))


_ONLINE = (_online_init, _online_step, _online_finish, 1)
_SHIFTED = (_shifted_init, _shifted_step, _shifted_finish, 2)
SHIFTED_MAX_SPAN = 100.0


def _ones_lane(h):
    return HEAD_DIM * (1 - h)


def _augment_values(v, h):
    lane = lax.broadcasted_iota(jnp.int32, v.shape, 1)
    ones = jnp.where(lane == _ones_lane(h), 1.0, 0.0)
    return jnp.where(_head_lane_mask(v.shape, h), v.astype(F32), ones).astype(BF16)


def _attend_tile(q_aug, k_tile, v_tile, state, step, heads=(0, 1), first_row=None):
    new = list(state)
    for h in heads:
        start = first_row or 0
        s = _dot_nt(q_aug[h][start:] if start else q_aug[h], k_tile(h))
        if first_row is not None:
            row = lax.broadcasted_iota(jnp.int32, s.shape, 0)
            col = lax.broadcasted_iota(jnp.int32, s.shape, 1)
            s = jnp.where(col <= row, s, NEG)
        if start:
            part = step(tuple(a[start:] for a in state[h]), s, v_tile(h))
            new[h] = tuple(jnp.concatenate([a[:start], b], axis=0) for a, b in zip(state[h], part))
        else:
            new[h] = step(state[h], s, v_tile(h))
    return tuple(new)


def _attend_diagonal(tile_part, tq, init, parts):
    state = (init(tq), init(tq))
    for d in range(parts):
        state = tile_part(state, d * (tq // parts), tq // parts)
    return state


def _score_bound(q_gain, k_gain):
    bound = 1.02 * HEAD_DIM * SCALE * LOG2E * jnp.max(jnp.abs(q_gain)) * jnp.max(jnp.abs(k_gain))
    return bound.astype(BF16).astype(F32).reshape(1)


def _merge_heads(outs):
    lane = lax.broadcasted_iota(jnp.int32, outs[0].shape, 1)
    return jnp.where(lane < HEAD_DIM, outs[0], outs[1])


def _seq_mixer_specs(*, mixer, nq, tq, seq):
    def colblk(section, p):
        return PROJ_BLOCK_POS[(section, mixer)] * PAIRS_PER_MIXER + p

    q_spec = pl.BlockSpec((tq, LANES), lambda b, p, i: (b * nq + i, colblk(0, p)))
    k_spec = pl.BlockSpec((seq, LANES), lambda b, p, i: (b, colblk(1, p)))
    v_spec = pl.BlockSpec((seq, LANES), lambda b, p, i: (b, colblk(2, p)))
    o_spec = pl.BlockSpec((tq, LANES), lambda b, p, i: (b * nq + i, p))
    return q_spec, k_spec, v_spec, o_spec


_SEQ_PARAMS = pltpu.CompilerParams(
    dimension_semantics=("arbitrary",) * 3, vmem_limit_bytes=VMEM_LIMIT)


def _fox_augment(x, cum, head, h, key_side, bound):
    lane = lax.broadcasted_iota(jnp.int32, x.shape, 1)
    g = jnp.broadcast_to(_lane_column(cum, head) * LOG2E, x.shape)
    g1, g2, g3 = (piece.astype(F32) for piece in _split3(g))
    one = jnp.ones_like(g1)
    pieces = (one, one, one, -g1, -g2, -g3, one) if key_side else (g1, g2, g3, one, one, one, -bound * one)
    base = HEAD_DIM * (1 - h)
    aug = jnp.zeros_like(g1)
    for n, piece in enumerate(pieces):
        aug = jnp.where(lane == base + n, piece, aug)
    return jnp.where(_head_lane_mask(x.shape, h), x.astype(F32), aug).astype(BF16)


def _fox_body(fend_ref, bound_ref, q_ref, k_ref, v_ref, cumq_ref, cumk_ref, o_ref, kaug_scr, vaug_scr, *,
              tq, tk, seq):
    b = pl.program_id(0)
    p = pl.program_id(1)
    iq = pl.program_id(2)
    bound = bound_ref[0]

    @pl.when(iq == 0)
    def _augment_keys():
        def chunk(c, _):
            rows = pl.ds(pl.multiple_of(c * tk, tk), tk)
            for h in range(2):
                kaug_scr[h, rows, :] = _fox_augment(k_ref[rows, :], cumk_ref[rows, :], 2 * p + h, h, True, bound)
                vaug_scr[h, rows, :] = _augment_values(v_ref[rows, :], h)
            return 0
        lax.fori_loop(0, seq // tk, chunk, 0)

    q = q_ref[...]
    cumq = cumq_ref[...]
    q_aug = [_fox_augment(q, cumq, 2 * p + h, h, False, bound) for h in range(2)]

    first_past = iq - 1
    last = jnp.maximum(first_past, 0)

    def reaches(j, heads):
        j = jnp.maximum(j, 0)
        alive = [LOG2E * (fend_ref[b, 2 * p + h, last] - fend_ref[b, 2 * p + h, j])
                 + 2.0 * bound > UNDERFLOW_LOG2 for h in heads]
        return functools.reduce(jnp.logical_and, alive)

    def attend(scheme):
        init, step, finish, diagonal_parts = scheme

        def tile(j, state, heads=(0, 1)):
            rows = pl.ds(pl.multiple_of(j * tk, tk), tk)
            return _attend_tile(q_aug, lambda h: kaug_scr[h, rows, :], lambda h: vaug_scr[h, rows, :], state,
                                step, heads)

        def diagonal_part(state, key0, nkeys):
            rows = pl.ds(pl.multiple_of(iq * tk + key0, nkeys), nkeys)
            return _attend_tile(q_aug, lambda h: kaug_scr[h, rows, :], lambda h: vaug_scr[h, rows, :], state,
                                step, first_row=key0)

        loop = (first_past, _attend_diagonal(diagonal_part, tq, init, diagonal_parts))
        for heads in ((0, 1), (0,), (1,)):
            loop = lax.while_loop(
                lambda lp: jnp.logical_and(lp[0] >= 0, reaches(lp[0], heads)),
                lambda lp: (lp[0] - 1, tile(lp[0], lp[1], heads)),
                loop)
        o_ref[...] = _merge_heads([finish(st, h) for h, st in enumerate(loop[1])]).astype(BF16)

    shifted_ok = 2.0 * bound <= SHIFTED_MAX_SPAN
    pl.when(shifted_ok)(functools.partial(attend, _SHIFTED))
    pl.when(jnp.logical_not(shifted_ok))(functools.partial(attend, _ONLINE))


def _fox(proj, cum, tile_end_sums, score_bound, *, bsz, seq, tq, tk):
    rows = proj.shape[0]
    nq = seq // tq
    q_spec, k_spec, v_spec, o_spec = _seq_mixer_specs(mixer=3, nq=nq, tq=tq, seq=seq)
    smem = pl.BlockSpec(memory_space=pltpu.SMEM)
    return pl.pallas_call(
        functools.partial(_fox_body, tq=tq, tk=tk, seq=seq),
        grid=(bsz, PAIRS_PER_MIXER, nq),
        in_specs=[
            smem, smem, q_spec, k_spec, v_spec,
            pl.BlockSpec((tq, LANES), lambda b, p, i: (b * nq + i, 0)),
            pl.BlockSpec((seq, LANES), lambda b, p, i: (b, 0)),
        ],
        out_specs=o_spec,
        out_shape=jax.ShapeDtypeStruct((rows, MIXER_WIDTH), BF16),
        scratch_shapes=[pltpu.VMEM((2, seq, LANES), BF16), pltpu.VMEM((2, seq, LANES), BF16)],
        compiler_params=_SEQ_PARAMS,
        name="fox",
    )(tile_end_sums, score_bound, proj, proj, proj, cum, cum)


def _moba_body(bound_ref, q_ref, k_ref, v_ref, o_ref, kmean_scr, kaug_scr, vaug_scr, *, tq, tk, seq):
    iq = pl.program_id(2)
    nblk = seq // MOBA_BLOCK
    blocks_per_tile = tq // MOBA_BLOCK
    bound = bound_ref[0]

    @pl.when(iq == 0)
    def _prepare_keys():
        r = lax.broadcasted_iota(jnp.int32, (LANES, seq), 0)
        c = lax.broadcasted_iota(jnp.int32, (LANES, seq), 1)
        member = jnp.where(c // MOBA_BLOCK == r % HEAD_DIM, 1.0 / MOBA_BLOCK, 0.0).astype(BF16)
        hi, lo = _split2(_dot(member, k_ref[...]))
        kmean_scr[0] = hi
        kmean_scr[1] = lo

        def chunk(c, _):
            rows = pl.ds(pl.multiple_of(c * tk, tk), tk)
            k = k_ref[rows, :].astype(F32)
            lane = lax.broadcasted_iota(jnp.int32, (tk, LANES), 1)
            blk = (c * tk + lax.broadcasted_iota(jnp.int32, (tk, LANES), 0)) // MOBA_BLOCK
            for h in range(2):
                spare = lane - HEAD_DIM * (1 - h)
                onehot = jnp.where(jnp.logical_or(spare == blk, spare == HEAD_DIM - 1), 1.0, 0.0)
                kaug_scr[h, rows, :] = jnp.where(_head_lane_mask(k.shape, h), k, onehot).astype(BF16)
                vaug_scr[h, rows, :] = _augment_values(v_ref[rows, :], h)
            return 0
        lax.fori_loop(0, seq // tk, chunk, 0)

    q = q_ref[...]
    nslot = -(-nblk // 8) * 8
    blk = lax.broadcasted_iota(jnp.int32, (nslot, tq), 0)
    qblk = blocks_per_tile * iq + lax.broadcasted_iota(jnp.int32, (nslot, tq), 1) // MOBA_BLOCK
    q_aug = []
    tail_row = lax.broadcasted_iota(jnp.int32, (HEAD_DIM - nslot, tq), 0)
    tail = jnp.where(tail_row == HEAD_DIM - nslot - 1, -bound, 0.0)
    for h in range(2):
        qh = _own_head(q, h)
        spare = slice(HEAD_DIM * (1 - h), HEAD_DIM * (1 - h) + nslot)
        gate = _dot_nt(kmean_scr[0, spare, :], qh) + _dot_nt(kmean_scr[1, spare, :], qh)
        past = blk < qblk
        work = jnp.where(past, gate, NEG)
        bias = jnp.where(blk == qblk, 0.0, NEG)
        for _ in range(min(MOBA_TOPK, nblk)):
            best = jnp.max(work, axis=0, keepdims=True)
            idx = jnp.min(jnp.where(work == best, blk, nslot), axis=0, keepdims=True)
            hit = blk == idx
            bias = jnp.where(hit, jnp.where(past, 0.0, bias), bias)
            work = jnp.where(hit, -jnp.inf, work)
        own_half = jnp.zeros((HEAD_DIM, tq), F32)
        halves = [bias, tail, own_half] if h == 1 else [own_half, bias, tail]
        q_aug.append(jnp.where(_head_lane_mask(q.shape, h), q.astype(F32),
                               jnp.concatenate(halves, axis=0).T).astype(BF16))

    def attend(scheme):
        init, step, finish, diagonal_parts = scheme

        def tile(j, state):
            rows = pl.ds(pl.multiple_of(j * tk, tk), tk)
            return _attend_tile(q_aug, lambda h: kaug_scr[h, rows, :], lambda h: vaug_scr[h, rows, :], state, step)

        def diagonal_part(state, key0, nkeys):
            rows = pl.ds(pl.multiple_of(iq * tk + key0, nkeys), nkeys)
            return _attend_tile(q_aug, lambda h: kaug_scr[h, rows, :], lambda h: vaug_scr[h, rows, :], state,
                                step, first_row=key0)

        state = _attend_diagonal(diagonal_part, tq, init, diagonal_parts)
        state = lax.fori_loop(0, iq, tile, state)
        o_ref[...] = _merge_heads([finish(st, h) for h, st in enumerate(state)]).astype(BF16)

    shifted_ok = 2.0 * bound <= SHIFTED_MAX_SPAN
    pl.when(shifted_ok)(functools.partial(attend, _SHIFTED))
    pl.when(jnp.logical_not(shifted_ok))(functools.partial(attend, _ONLINE))


def _moba(proj, score_bound, *, bsz, seq, tq, tk):
    assert tq == tk and tq % MOBA_BLOCK == 0
    assert -(-(seq // MOBA_BLOCK) // 8) * 8 < HEAD_DIM
    rows = proj.shape[0]
    nq = seq // tq
    q_spec, k_spec, v_spec, o_spec = _seq_mixer_specs(mixer=2, nq=nq, tq=tq, seq=seq)
    return pl.pallas_call(
        functools.partial(_moba_body, tq=tq, tk=tk, seq=seq),
        grid=(bsz, PAIRS_PER_MIXER, nq),
        in_specs=[pl.BlockSpec(memory_space=pltpu.SMEM), q_spec, k_spec, v_spec],
        out_specs=o_spec,
        out_shape=jax.ShapeDtypeStruct((rows, MIXER_WIDTH), BF16),
        scratch_shapes=[pltpu.VMEM((2, LANES, LANES), BF16), pltpu.VMEM((2, seq, LANES), BF16),
                        pltpu.VMEM((2, seq, LANES), BF16)],
        compiler_params=_SEQ_PARAMS,
        name="moba",
    )(score_bound, proj, proj, proj)


def _stick_body(q_ref, k_ref, v_ref, o_ref, *, tq, tk):
    iq = pl.program_id(2)
    blocks_per_tile = tq // tk
    q = q_ref[...]
    q_own = [_own_head(q, h) for h in range(2)]
    later_row = lax.broadcasted_iota(jnp.int32, (tk, tk), 0)
    later_col = lax.broadcasted_iota(jnp.int32, (tk, tk), 1)
    later = jnp.where(later_row > later_col, 1.0, 0.0).astype(BF16)

    def block(j, state, first_row=None):
        keys = pl.ds(pl.multiple_of(j * tk, tk), tk)
        k_blk = k_ref[keys, :]
        v_blk = v_ref[keys, :]
        rows = slice(first_row or 0, tq)
        n = tq - rows.start
        if first_row is not None:
            strictly_past = (lax.broadcasted_iota(jnp.int32, (n, tk), 1)
                             < lax.broadcasted_iota(jnp.int32, (n, tk), 0))
        new = []
        for h in range(2):
            carry_sum, acc = state[h]
            z = _dot_nt(q_own[h][rows], k_blk)
            softplus = jnp.maximum(z, 0.0) + jnp.log2(1.0 + jnp.exp2(-jnp.abs(z)))
            log_keep = -softplus
            if first_row is not None:
                log_keep = jnp.where(strictly_past, log_keep, 0.0)
            hi, lo = _split2(log_keep)
            after = _dot(hi, later) + _dot(lo, later) + carry_sum[rows]
            log_a = z - softplus + after
            if first_row is not None:
                log_a = jnp.where(strictly_past, log_a, NEG)
            acc_rows = acc[rows] + _dot(jnp.exp2(log_a).astype(BF16), v_blk)
            sum_rows = carry_sum[rows] + jnp.sum(log_keep, axis=1, keepdims=True)
            if rows.start:
                acc_rows = jnp.concatenate([acc[:rows.start], acc_rows], axis=0)
                sum_rows = jnp.concatenate([carry_sum[:rows.start], sum_rows], axis=0)
            new.append((sum_rows, acc_rows))
        return tuple(new)

    zero = (jnp.zeros((tq, 1), F32), jnp.zeros((tq, LANES), F32))
    state = (zero, zero)
    for d in reversed(range(blocks_per_tile)):
        state = block(iq * blocks_per_tile + d, state, d * tk)

    def cond(loop):
        j, state = loop
        alive = jnp.max(jnp.maximum(state[0][0], state[1][0])) > UNDERFLOW_LOG2
        return jnp.logical_and(j >= 0, alive)

    def body(loop):
        j, state = loop
        return j - 1, block(j, state)

    _, state = lax.while_loop(cond, body, (iq * blocks_per_tile - 1, state))
    o_ref[...] = _merge_heads([acc for _, acc in state]).astype(BF16)


def _stick(proj, *, bsz, seq, tq, tk):
    rows = proj.shape[0]
    nq = seq // tq
    q_spec, k_spec, v_spec, o_spec = _seq_mixer_specs(mixer=1, nq=nq, tq=tq, seq=seq)
    return pl.pallas_call(
        functools.partial(_stick_body, tq=tq, tk=tk),
        grid=(bsz, PAIRS_PER_MIXER, nq),
        in_specs=[q_spec, k_spec, v_spec],
        out_specs=o_spec,
        out_shape=jax.ShapeDtypeStruct((rows, MIXER_WIDTH), BF16),
        compiler_params=_SEQ_PARAMS,
        name="stick",
    )(proj, proj, proj)


def _outproj_body(o1_ref, o4_ref, o16_ref, l1_ref, l4_ref, l16_ref, yb_ref, yc_ref, yd_ref,
                  gate_ref, x_ref, w_ref, out_ref, y_scr, o_scr, l_scr, *, tm):
    j = pl.program_id(1)

    @pl.when(j == 0)
    def _gate():
        for slot, (dil, o_ref, l_ref) in enumerate(zip(STRIDED_DILATIONS, (o4_ref, o16_ref), (l4_ref, l16_ref))):
            n = tm // dil
            for r in range(dil):
                for c in range(LANE_BLOCKS_PER_MIXER):
                    cols = slice(r * MIXER_WIDTH + c * LANES, r * MIXER_WIDTH + (c + 1) * LANES)
                    o_scr[slot, c, pl.ds(r, n, stride=dil), :] = o_ref[:, cols].astype(F32)
                    l_scr[slot, c, pl.ds(r, n, stride=dil), :] = l_ref[:, cols]

        def natural(scr, slot):
            return jnp.concatenate([scr[slot, c] for c in range(LANE_BLOCKS_PER_MIXER)], axis=1)

        l1, l4, l16 = l1_ref[...], natural(l_scr, 0), natural(l_scr, 1)
        m = jnp.maximum(jnp.maximum(l1, l4), l16)
        e1, e4, e16 = jnp.exp2(l1 - m), jnp.exp2(l4 - m), jnp.exp2(l16 - m)
        ya = (e1 * o1_ref[...].astype(F32) + e4 * natural(o_scr, 0)
              + e16 * natural(o_scr, 1)) / (e1 + e4 + e16)
        parts = (ya, yb_ref[...].astype(F32), yc_ref[...].astype(F32), yd_ref[...].astype(F32))
        for mxr, y in enumerate(parts):
            g = gate_ref[:, mxr * MIXER_WIDTH:(mxr + 1) * MIXER_WIDTH].astype(F32)
            silu = g / (1.0 + jnp.exp(-g))
            y_scr[:, mxr * MIXER_WIDTH:(mxr + 1) * MIXER_WIDTH] = (y * silu).astype(BF16)

    out_ref[...] = x_ref[...] + _dot(y_scr[...], w_ref[...])


def _outproj(seg_o, seg_lse, yb, yc, yd, proj, x2, w_out, *, tm, tn):
    rows, d = x2.shape
    row_blk = lambda i, j: (i, 0)
    mix_spec = pl.BlockSpec((tm, MIXER_WIDTH), row_blk)
    seg_specs = [pl.BlockSpec((tm // dil, dil * MIXER_WIDTH), row_blk) for dil in DILATIONS]
    return pl.pallas_call(
        functools.partial(_outproj_body, tm=tm),
        grid=(rows // tm, d // tn),
        in_specs=seg_specs + seg_specs + [mix_spec] * 3 + [
            pl.BlockSpec((tm, MIX_WIDTH), lambda i, j: (i, GATE_GROUP)),
            pl.BlockSpec((tm, tn), lambda i, j: (i, j)),
            pl.BlockSpec((MIX_WIDTH, tn), lambda i, j: (0, j)),
        ],
        out_specs=pl.BlockSpec((tm, tn), lambda i, j: (i, j)),
        out_shape=jax.ShapeDtypeStruct((rows, d), F32),
        scratch_shapes=[pltpu.VMEM((tm, MIX_WIDTH), BF16),
                        pltpu.VMEM((len(STRIDED_DILATIONS), LANE_BLOCKS_PER_MIXER, tm, LANES), F32),
                        pltpu.VMEM((len(STRIDED_DILATIONS), LANE_BLOCKS_PER_MIXER, tm, LANES), F32)],
        compiler_params=pltpu.CompilerParams(
            dimension_semantics=("arbitrary", "arbitrary"), vmem_limit_bytes=VMEM_LIMIT),
        name="outproj",
    )(*seg_o, *seg_lse, yb, yc, yd, proj, x2, w_out)


def _cast_body(w_ref, o_ref):
    o_ref[...] = w_ref[...].astype(BF16)


def _cast_columns(w):
    depth, d, n = w.shape
    step = min(CAST_COLS, n)
    spec = pl.BlockSpec((1, d, step), lambda layer, j: (layer, 0, j))
    return pl.pallas_call(
        _cast_body,
        grid=(depth, n // step),
        in_specs=[spec],
        out_specs=spec,
        out_shape=jax.ShapeDtypeStruct((depth, d, n), BF16),
        compiler_params=pltpu.CompilerParams(
            dimension_semantics=("arbitrary", "arbitrary"), vmem_limit_bytes=VMEM_LIMIT),
        name="cast_w_out",
    )(w)


def _cast_w_in_body(w_ref, main_ref, gate_ref, *, main_steps):
    j = pl.program_id(1)

    @pl.when(j < main_steps)
    def _():
        main_ref[...] = w_ref[...].astype(BF16)

    @pl.when(j == main_steps)
    def _():
        w = w_ref[0, :, :LANES]
        lane = lax.broadcasted_iota(jnp.int32, w.shape, 1)
        hi, lo = _split2(jnp.where(lane < HEADS_PER_MIXER, w, 0.0))
        gate_ref[0] = jnp.concatenate([hi, lo], axis=1)


def _cast_w_in(w_in):
    depth, d, n = w_in.shape
    assert n == PROJ_COLS + HEADS_PER_MIXER
    main_steps = PROJ_COLS // CAST_COLS
    return pl.pallas_call(
        functools.partial(_cast_w_in_body, main_steps=main_steps),
        grid=(depth, main_steps + 1),
        in_specs=[pl.BlockSpec((1, d, CAST_COLS), lambda layer, j: (layer, 0, j))],
        out_specs=[
            pl.BlockSpec((1, d, CAST_COLS), lambda layer, j: (layer, 0, jnp.minimum(j, main_steps - 1))),
            pl.BlockSpec((1, d, 2 * LANES), lambda layer, j: (layer, 0, 0)),
        ],
        out_shape=[
            jax.ShapeDtypeStruct((depth, d, PROJ_COLS), BF16),
            jax.ShapeDtypeStruct((depth, d, 2 * LANES), BF16),
        ],
        compiler_params=pltpu.CompilerParams(
            dimension_semantics=("arbitrary", "arbitrary"), vmem_limit_bytes=VMEM_LIMIT),
        name="cast_w_in",
    )(w_in)


def _rope_tables(seq):
    inv = 1.0 / (ROPE_THETA ** (jnp.arange(0, HEAD_DIM, 2, dtype=F32) / HEAD_DIM))
    ang = jnp.arange(seq, dtype=F32)[:, None] * inv[None, :]
    cos, sin = jnp.cos(ang), jnp.sin(ang)
    reps = LANES // HEAD_DIM
    cos_t = jnp.tile(jnp.concatenate([cos, cos], axis=1), (1, reps))
    sin_t = jnp.tile(jnp.concatenate([-sin, sin], axis=1), (1, reps))
    return cos_t, sin_t


def _gain_table(qn, kn):
    ones = jnp.ones((HEAD_DIM,), F32)
    per_mixer = {0: (qn[0], ones, qn[1], qn[2]), 1: (kn[0], ones, kn[1], kn[2])}
    blocks = [per_mixer[section][mixer] if section < 2 else ones for section, mixer in PROJ_BLOCK_ORDER]
    tab = jnp.stack([jnp.tile(g.astype(F32), HEADS_PER_MIXER) for g in blocks])
    return tab[:, None, :]


def kernel(x, norm_gain, w_in, q_norm_gain, k_norm_gain, forget_bias, w_out):
    bsz, seq, d = x.shape
    depth = w_in.shape[0]
    rows = bsz * seq
    tm = min(512, seq)
    tq_seq = min(QUERY_TILE, seq)
    cos_t, sin_t = _rope_tables(seq)
    head_of_lane = jnp.arange(MIXER_WIDTH) // HEAD_DIM
    mavg = jnp.where(head_of_lane[:, None] == head_of_lane[None, :], 1.0 / HEAD_DIM, 0.0).astype(BF16)
    tri = jnp.tril(jnp.ones((256, 256), F32)).astype(BF16)

    def natural_col(section, r):
        return PROJ_BLOCK_POS[(section, 0)]

    def strided_col(section, r):
        return r

    w_main, wf_cat = _cast_w_in(w_in)
    w_out_bf16 = _cast_columns(w_out)

    x2 = x.reshape(rows, d)
    for layer in range(depth):
        fb_pad = jnp.pad(forget_bias[layer].astype(F32), (0, LANES - HEADS_PER_MIXER))[None, :]
        gain_tab = _gain_table(q_norm_gain[layer], k_norm_gain[layer])

        proj, cum, *strided = _inproj(x2, norm_gain[layer][None, :].astype(F32), w_main[layer], wf_cat[layer],
                                      fb_pad, gain_tab, cos_t, sin_t, mavg, tri, seq=seq, tm=tm)
        seg = [_band_segment(proj, proj, proj, natural_col, bsz=bsz, seq=seq, dil=1, tq=512)]
        for di, dil in enumerate(STRIDED_DILATIONS):
            qd, kd, vd = strided[3 * di:3 * di + 3]
            seg.append(_band_segment(qd, kd, vd, strided_col, bsz=bsz, seq=seq, dil=dil, tq=512))
        yb = _stick(proj, bsz=bsz, seq=seq, tq=min(STICK_QUERY_TILE, seq), tk=STICK_KEY_TILE)
        yc = _moba(proj, _score_bound(q_norm_gain[layer, 1], k_norm_gain[layer, 1]),
                   bsz=bsz, seq=seq, tq=tq_seq, tk=KEY_TILE)
        nt = seq // KEY_TILE
        tile_end_sums = cum.reshape(bsz, nt, KEY_TILE, LANES)[:, :, KEY_TILE - 1, :HEADS_PER_MIXER]
        tile_end_sums = tile_end_sums.transpose(0, 2, 1)
        yd = _fox(proj, cum, tile_end_sums, _score_bound(q_norm_gain[layer, 2], k_norm_gain[layer, 2]),
                  bsz=bsz, seq=seq, tq=tq_seq, tk=KEY_TILE)
        x2 = _outproj([s[0] for s in seg], [s[1] for s in seg], yb, yc, yd, proj, x2,
                      w_out_bf16[layer], tm=min(OUT_ROW_TILE, seq), tn=d)
    return x2.reshape(bsz, seq, d)
```

```python
import functools

import jax
import jax.numpy as jnp
from jax import lax
from jax.experimental import pallas as pl
from jax.experimental.pallas import tpu as pltpu

F32 = jnp.float32
BF16 = jnp.bfloat16

HEAD_DIM = 64
HALF_DIM = HEAD_DIM // 2
LANES = 128
N_MIXERS = 4
HEADS_PER_MIXER = 8
PAIRS_PER_MIXER = HEADS_PER_MIXER // 2
MIXER_WIDTH = HEADS_PER_MIXER * HEAD_DIM
LANE_BLOCKS_PER_MIXER = MIXER_WIDTH // LANES
MIX_WIDTH = N_MIXERS * MIXER_WIDTH
N_SECTIONS = 4
PROJ_COLS = N_SECTIONS * MIX_WIDTH
COL_BLOCKS_PER_SECTION = MIX_WIDTH // LANES
ROPE_THETA = 10000.0
RMS_EPS = 1e-6
SCALE = HEAD_DIM ** -0.5
LOG2E = 1.4426950408889634
NEG = -1e30
DILATIONS = (1, 4, 16)
STRIDED_DILATIONS = DILATIONS[1:]
BAND = 128
MOBA_BLOCK = 256
MOBA_TOPK = 3
QUERY_TILE = 1024
KEY_TILE = 1024
STICK_QUERY_TILE = 512
STICK_KEY_TILE = 256
COL_BLOCKS_PER_STEP = 4
PROJ_BLOCK_ORDER = ((0, 0), (1, 0), (0, 2), (1, 2),
                    (0, 3), (1, 3), (0, 1), (1, 1),
                    (2, 0), (2, 1), (2, 2), (2, 3),
                    (3, 0), (3, 1), (3, 2), (3, 3))
PROJ_BLOCK_POS = {block: pos for pos, block in enumerate(PROJ_BLOCK_ORDER)}
GATE_GROUP = PROJ_BLOCK_POS[(3, 0)] // N_MIXERS
CAST_COLS = 1024
OUT_ROW_TILE = 256
UNDERFLOW_LOG2 = -160.0
VMEM_LIMIT = 56 * 1024 * 1024

_NT = (((1,), (1,)), ((), ()))


def _dot(a, b):
    return jnp.dot(a, b, preferred_element_type=F32)


def _dot_nt(a, b):
    return lax.dot_general(a, b, _NT, preferred_element_type=F32)


def _split2(x):
    hi = x.astype(BF16)
    lo = (x - hi.astype(F32)).astype(BF16)
    return hi, lo


def _split3(x):
    b1 = x.astype(BF16)
    r1 = x - b1.astype(F32)
    b2 = r1.astype(BF16)
    r2 = r1 - b2.astype(F32)
    return b1, b2, r2.astype(BF16)


def _head_lane_mask(shape, h):
    lane = lax.broadcasted_iota(jnp.int32, shape, len(shape) - 1)
    return (lane >= h * HEAD_DIM) & (lane < (h + 1) * HEAD_DIM)


def _own_head(x, h):
    return jnp.where(_head_lane_mask(x.shape, h), x, jnp.zeros_like(x))


def _lane_column(x, n):
    lane = lax.broadcasted_iota(jnp.int32, x.shape, 1)
    return jnp.sum(jnp.where(lane == n, x, 0.0), axis=1, keepdims=True)


def _inproj_body(x_ref, g_ref, wf_ref, fb_ref, gain_ref, cos_ref, sin_ref, mavg_ref, tri_ref, *rest,
                 blocks_per_batch, tm):
    w_refs, (proj_ref, cum_ref), rest = rest[:COL_BLOCKS_PER_STEP], rest[COL_BLOCKS_PER_STEP:][:2], \
        rest[COL_BLOCKS_PER_STEP + 2:]
    dil_refs = rest[:3 * len(STRIDED_DILATIONS)]
    h_scr, carry_scr, dil_scr = rest[3 * len(STRIDED_DILATIONS):]
    i = pl.program_id(0)
    j = pl.program_id(1)

    @pl.when(j == 0)
    def _prologue():
        x = x_ref[...]
        ms = jnp.mean(x * x, axis=-1, keepdims=True)
        h = x * lax.rsqrt(ms + RMS_EPS) * g_ref[...]
        h_hi, h_lo = _split2(h)
        h_scr[...] = h_hi
        wf = wf_ref[...]
        t = _dot(h_hi, wf)
        u = _dot(h_lo, wf[:, :LANES])
        logit = t[:, :LANES] + t[:, LANES:] + u + fb_ref[...]
        lf = jnp.minimum(logit, 0.0) - jnp.log1p(jnp.exp(-jnp.abs(logit)))

        @pl.when(i % blocks_per_batch == 0)
        def _():
            carry_scr[...] = jnp.zeros_like(carry_scr)

        carry = carry_scr[...]
        tri = tri_ref[...]
        sub = tri.shape[0]
        for r in range(tm // sub):
            b1, b2, b3 = _split3(lf[r * sub:(r + 1) * sub])
            c = _dot(tri, jnp.concatenate([b1, b2, b3], axis=1))
            c = c[:, :LANES] + c[:, LANES:2 * LANES] + c[:, 2 * LANES:] + carry
            cum_ref[r * sub:(r + 1) * sub, :] = c
            carry = c[sub - 1:sub, :]
        carry_scr[...] = carry

    for group in range(len(PROJ_BLOCK_ORDER) // COL_BLOCKS_PER_STEP):
        @pl.when(j == group)
        def _():
            for sub in range(COL_BLOCKS_PER_STEP):
                section, mixer = PROJ_BLOCK_ORDER[group * COL_BLOCKS_PER_STEP + sub]
                _inproj_column_block(section, mixer, sub * MIXER_WIDTH, h_scr, w_refs[sub], gain_ref.at[sub],
                                     cos_ref, sin_ref, mavg_ref, proj_ref, dil_refs, dil_scr, tm=tm)


def _inproj_column_block(section, mixer, col0, h_scr, w_ref, gain_ref, cos_ref, sin_ref, mavg_ref, proj_ref,
                         dil_refs, dil_scr, *, tm):
    def out_cols(c):
        return slice(col0 + c * LANES, col0 + (c + 1) * LANES)

    acc = _dot(h_scr[...], w_ref[...])
    normed = section < 2 and mixer != 1
    roped = normed and mixer in (0, 2)
    scale = SCALE * LOG2E if section == 0 else 1.0

    if not normed:
        proj_ref[:, col0:col0 + MIXER_WIDTH] = (acc * scale if section == 0 else acc).astype(BF16)
    else:
        ms = _dot((acc * acc).astype(BF16), mavg_ref[...])
        t = acc * lax.rsqrt(ms + RMS_EPS) * (gain_ref[0] * scale)
        if not roped:
            proj_ref[:, col0:col0 + MIXER_WIDTH] = t.astype(BF16)
        else:
            cosv = cos_ref[...]
            sinv = sin_ref[...]
            lane = lax.broadcasted_iota(jnp.int32, (tm, LANES), 1)
            first_half = (lane % HEAD_DIM) < HALF_DIM
            for c in range(LANE_BLOCKS_PER_MIXER):
                tc = t[:, c * LANES:(c + 1) * LANES]
                partner = jnp.where(first_half,
                                    pltpu.roll(tc, LANES - HALF_DIM, 1),
                                    pltpu.roll(tc, HALF_DIM, 1))
                proj_ref[:, out_cols(c)] = (tc * cosv + partner * sinv).astype(BF16)

    if mixer == 0 and section < 3:
        for c in range(LANE_BLOCKS_PER_MIXER):
            dil_scr[c] = proj_ref[:, out_cols(c)].astype(F32)
        for di, dil in enumerate(STRIDED_DILATIONS):
            ref = dil_refs[3 * di + section]
            n = tm // dil
            for r in range(dil):
                for c in range(LANE_BLOCKS_PER_MIXER):
                    lo = r * MIXER_WIDTH + c * LANES
                    ref[:, lo:lo + LANES] = dil_scr[c, pl.ds(r, n, stride=dil), :].astype(BF16)


def _inproj(x2, g, w_main, wf_cat, fb_pad, gain_tab, cos_t, sin_t, mavg, tri, *, seq, tm):
    rows, d = x2.shape
    blocks_per_batch = seq // tm
    step_cols = COL_BLOCKS_PER_STEP * MIXER_WIDTH
    body = functools.partial(_inproj_body, blocks_per_batch=blocks_per_batch, tm=tm)

    def weight_block(sub):
        origin = [section * N_MIXERS + mixer for section, mixer in PROJ_BLOCK_ORDER[sub::COL_BLOCKS_PER_STEP]]

        def index(i, j):
            blk = origin[0]
            for group in range(1, len(origin)):
                blk = jnp.where(j == group, origin[group], blk)
            return 0, blk
        return index

    dil_specs, dil_shapes = [], []
    for dil in STRIDED_DILATIONS:
        for _ in range(3):
            dil_specs.append(pl.BlockSpec((tm // dil, dil * MIXER_WIDTH), lambda i, j: (i, 0)))
            dil_shapes.append(jax.ShapeDtypeStruct((rows // dil, dil * MIXER_WIDTH), BF16))
    return pl.pallas_call(
        body,
        grid=(rows // tm, PROJ_COLS // step_cols),
        in_specs=[
            pl.BlockSpec((tm, d), lambda i, j: (i, 0)),
            pl.BlockSpec((1, d), lambda i, j: (0, 0)),
            pl.BlockSpec((d, 2 * LANES), lambda i, j: (0, 0)),
            pl.BlockSpec((1, LANES), lambda i, j: (0, 0)),
            pl.BlockSpec((COL_BLOCKS_PER_STEP, 1, MIXER_WIDTH), lambda i, j: (j, 0, 0)),
            pl.BlockSpec((tm, LANES), lambda i, j: (i % blocks_per_batch, 0)),
            pl.BlockSpec((tm, LANES), lambda i, j: (i % blocks_per_batch, 0)),
            pl.BlockSpec((MIXER_WIDTH, MIXER_WIDTH), lambda i, j: (0, 0)),
            pl.BlockSpec(tri.shape, lambda i, j: (0, 0)),
        ] + [pl.BlockSpec((d, MIXER_WIDTH), weight_block(sub)) for sub in range(COL_BLOCKS_PER_STEP)],
        out_specs=[
            pl.BlockSpec((tm, step_cols), lambda i, j: (i, j)),
            pl.BlockSpec((tm, LANES), lambda i, j: (i, 0)),
        ] + dil_specs,
        out_shape=[
            jax.ShapeDtypeStruct((rows, PROJ_COLS), BF16),
            jax.ShapeDtypeStruct((rows, LANES), F32),
        ] + dil_shapes,
        scratch_shapes=[pltpu.VMEM((tm, d), BF16), pltpu.VMEM((1, LANES), F32),
                        pltpu.VMEM((LANE_BLOCKS_PER_MIXER, tm, LANES), F32)],
        compiler_params=pltpu.CompilerParams(
            dimension_semantics=("arbitrary", "arbitrary"), vmem_limit_bytes=VMEM_LIMIT),
        name="inproj",
    )(x2, g, wf_cat, fb_pad, gain_tab, cos_t, sin_t, mavg, tri, *([w_main] * COL_BLOCKS_PER_STEP))


def _band_body(q_ref, k_ref, v_ref, kp_ref, vp_ref, o_ref, lse_ref, *, tq):
    first = pl.program_id(2) == 0
    row = lax.broadcasted_iota(jnp.int32, (BAND, 2 * BAND), 0)
    col = lax.broadcasted_iota(jnp.int32, (BAND, 2 * BAND), 1)
    in_band = jnp.logical_or(jnp.logical_and(col < BAND, col >= row),
                             jnp.logical_and(col >= BAND, col - BAND <= row))
    first_mask = jnp.logical_and(in_band, jnp.logical_or(col >= BAND, jnp.logical_not(first)))
    lane = lax.broadcasted_iota(jnp.int32, (BAND, LANES), 1)
    head0 = lane < HEAD_DIM
    for pair in range(PAIRS_PER_MIXER):
        lanes = slice(pair * LANES, (pair + 1) * LANES)
        for c in range(tq // BAND):
            own = slice(c * BAND, (c + 1) * BAND)
            qc = q_ref[own, lanes]
            if c == 0:
                kw = jnp.concatenate([kp_ref[:, lanes], k_ref[own, lanes]], axis=0)
                vw = jnp.concatenate([vp_ref[:, lanes], v_ref[own, lanes]], axis=0)
                mask = first_mask
            else:
                window = slice((c - 1) * BAND, (c + 1) * BAND)
                kw = k_ref[window, lanes]
                vw = v_ref[window, lanes]
                mask = in_band
            outs, lses = [], []
            for h in range(2):
                s = jnp.where(mask, _dot_nt(_own_head(qc, h), kw), NEG)
                m = jnp.max(s, axis=1, keepdims=True)
                p = jnp.exp2(s - m)
                l = jnp.sum(p, axis=1, keepdims=True)
                outs.append(_dot(p.astype(BF16), vw) / l)
                lses.append(m + jnp.log2(l))
            o_ref[own, lanes] = jnp.where(head0, outs[0], outs[1]).astype(BF16)
            lse_ref[own, lanes] = jnp.where(head0, lses[0], lses[1])


def _band_segment(q_arr, k_arr, v_arr, col_of, *, bsz, seq, dil, tq):
    sub_rows = q_arr.shape[0]
    sub_len = seq // dil
    tq = min(tq, sub_len)
    nq = sub_len // tq
    band_per_tq = tq // BAND

    def cur(section):
        return pl.BlockSpec((tq, MIXER_WIDTH), lambda b, r, i: (b * nq + i, col_of(section, r)))

    def prev(section):
        return pl.BlockSpec(
            (BAND, MIXER_WIDTH),
            lambda b, r, i: (jnp.maximum((b * nq + i) * band_per_tq - 1, 0), col_of(section, r)))

    out_spec = pl.BlockSpec((tq, MIXER_WIDTH), lambda b, r, i: (b * nq + i, r))
    return pl.pallas_call(
        functools.partial(_band_body, tq=tq),
        grid=(bsz, dil, nq),
        in_specs=[cur(0), cur(1), cur(2), prev(1), prev(2)],
        out_specs=[out_spec, out_spec],
        out_shape=[
            jax.ShapeDtypeStruct((sub_rows, dil * MIXER_WIDTH), BF16),
            jax.ShapeDtypeStruct((sub_rows, dil * MIXER_WIDTH), F32),
        ],
        compiler_params=pltpu.CompilerParams(
            dimension_semantics=("arbitrary",) * 3, vmem_limit_bytes=VMEM_LIMIT),
        name=f"band_d{dil}",
    )(q_arr, k_arr, v_arr, k_arr, v_arr)


def _online_init(tq):
    return (jnp.full((tq, 1), NEG, F32), jnp.zeros((tq, 1), F32), jnp.zeros((tq, LANES), F32))


def _online_step(carry, s, v_blk):
    m, l, acc = carry
    m_new = jnp.maximum(m, jnp.max(s, axis=1, keepdims=True))
    alpha = jnp.exp2(m - m_new)
    p = jnp.exp2(s - m_new)
    l = alpha * l + jnp.sum(p, axis=1, keepdims=True)
    acc = alpha * acc + _dot(p.astype(BF16), v_blk)
    return m_new, l, acc


def _online_finish(carry, h):
    _, l, acc = carry
    return acc / l


def _shifted_init(tq):
    return (jnp.zeros((tq, LANES), F32),)


def _shifted_step(carry, s, v_blk):
    acc, = carry
    return (acc + _dot(jnp.exp2(s).astype(BF16), v_blk),)


def _shifted_finish(carry, h):
    acc, = carry
    return acc / _lane_column(acc, _ones_lane(h))


_ONLINE = (_online_init, _online_step, _online_finish, 1)
_SHIFTED = (_shifted_init, _shifted_step, _shifted_finish, 2)
SHIFTED_MAX_SPAN = 100.0


def _ones_lane(h):
    return HEAD_DIM * (1 - h)


def _augment_values(v, h):
    lane = lax.broadcasted_iota(jnp.int32, v.shape, 1)
    ones = jnp.where(lane == _ones_lane(h), 1.0, 0.0)
    return jnp.where(_head_lane_mask(v.shape, h), v.astype(F32), ones).astype(BF16)


def _attend_tile(q_aug, k_tile, v_tile, state, step, heads=(0, 1), first_row=None):
    new = list(state)
    for h in heads:
        start = first_row or 0
        s = _dot_nt(q_aug[h][start:] if start else q_aug[h], k_tile(h))
        if first_row is not None:
            row = lax.broadcasted_iota(jnp.int32, s.shape, 0)
            col = lax.broadcasted_iota(jnp.int32, s.shape, 1)
            s = jnp.where(col <= row, s, NEG)
        if start:
            part = step(tuple(a[start:] for a in state[h]), s, v_tile(h))
            new[h] = tuple(jnp.concatenate([a[:start], b], axis=0) for a, b in zip(state[h], part))
        else:
            new[h] = step(state[h], s, v_tile(h))
    return tuple(new)


def _attend_diagonal(tile_part, tq, init, parts):
    state = (init(tq), init(tq))
    for d in range(parts):
        state = tile_part(state, d * (tq // parts), tq // parts)
    return state


def _score_bound(q_gain, k_gain):
    bound = 1.02 * HEAD_DIM * SCALE * LOG2E * jnp.max(jnp.abs(q_gain)) * jnp.max(jnp.abs(k_gain))
    return bound.astype(BF16).astype(F32).reshape(1)


def _merge_heads(outs):
    lane = lax.broadcasted_iota(jnp.int32, outs[0].shape, 1)
    return jnp.where(lane < HEAD_DIM, outs[0], outs[1])


def _seq_mixer_specs(*, mixer, nq, tq, seq):
    def colblk(section, p):
        return PROJ_BLOCK_POS[(section, mixer)] * PAIRS_PER_MIXER + p

    q_spec = pl.BlockSpec((tq, LANES), lambda b, p, i: (b * nq + i, colblk(0, p)))
    k_spec = pl.BlockSpec((seq, LANES), lambda b, p, i: (b, colblk(1, p)))
    v_spec = pl.BlockSpec((seq, LANES), lambda b, p, i: (b, colblk(2, p)))
    o_spec = pl.BlockSpec((tq, LANES), lambda b, p, i: (b * nq + i, p))
    return q_spec, k_spec, v_spec, o_spec


_SEQ_PARAMS = pltpu.CompilerParams(
    dimension_semantics=("arbitrary",) * 3, vmem_limit_bytes=VMEM_LIMIT)


def _fox_augment(x, cum, head, h, key_side, bound):
    lane = lax.broadcasted_iota(jnp.int32, x.shape, 1)
    g = jnp.broadcast_to(_lane_column(cum, head) * LOG2E, x.shape)
    g1, g2, g3 = (piece.astype(F32) for piece in _split3(g))
    one = jnp.ones_like(g1)
    pieces = (one, one, one, -g1, -g2, -g3, one) if key_side else (g1, g2, g3, one, one, one, -bound * one)
    base = HEAD_DIM * (1 - h)
    aug = jnp.zeros_like(g1)
    for n, piece in enumerate(pieces):
        aug = jnp.where(lane == base + n, piece, aug)
    return jnp.where(_head_lane_mask(x.shape, h), x.astype(F32), aug).astype(BF16)


def _fox_body(fend_ref, bound_ref, q_ref, k_ref, v_ref, cumq_ref, cumk_ref, o_ref, kaug_scr, vaug_scr, *,
              tq, tk, seq):
    b = pl.program_id(0)
    p = pl.program_id(1)
    iq = pl.program_id(2)
    bound = bound_ref[0]

    @pl.when(iq == 0)
    def _augment_keys():
        def chunk(c, _):
            rows = pl.ds(pl.multiple_of(c * tk, tk), tk)
            for h in range(2):
                kaug_scr[h, rows, :] = _fox_augment(k_ref[rows, :], cumk_ref[rows, :], 2 * p + h, h, True, bound)
                vaug_scr[h, rows, :] = _augment_values(v_ref[rows, :], h)
            return 0
        lax.fori_loop(0, seq // tk, chunk, 0)

    q = q_ref[...]
    cumq = cumq_ref[...]
    q_aug = [_fox_augment(q, cumq, 2 * p + h, h, False, bound) for h in range(2)]

    first_past = iq - 1
    last = jnp.maximum(first_past, 0)

    def reaches(j, heads):
        j = jnp.maximum(j, 0)
        alive = [LOG2E * (fend_ref[b, 2 * p + h, last] - fend_ref[b, 2 * p + h, j])
                 + 2.0 * bound > UNDERFLOW_LOG2 for h in heads]
        return functools.reduce(jnp.logical_and, alive)

    def attend(scheme):
        init, step, finish, diagonal_parts = scheme

        def tile(j, state, heads=(0, 1)):
            rows = pl.ds(pl.multiple_of(j * tk, tk), tk)
            return _attend_tile(q_aug, lambda h: kaug_scr[h, rows, :], lambda h: vaug_scr[h, rows, :], state,
                                step, heads)

        def diagonal_part(state, key0, nkeys):
            rows = pl.ds(pl.multiple_of(iq * tk + key0, nkeys), nkeys)
            return _attend_tile(q_aug, lambda h: kaug_scr[h, rows, :], lambda h: vaug_scr[h, rows, :], state,
                                step, first_row=key0)

        loop = (first_past, _attend_diagonal(diagonal_part, tq, init, diagonal_parts))
        for heads in ((0, 1), (0,), (1,)):
            loop = lax.while_loop(
                lambda lp: jnp.logical_and(lp[0] >= 0, reaches(lp[0], heads)),
                lambda lp: (lp[0] - 1, tile(lp[0], lp[1], heads)),
                loop)
        o_ref[...] = _merge_heads([finish(st, h) for h, st in enumerate(loop[1])]).astype(BF16)

    shifted_ok = 2.0 * bound <= SHIFTED_MAX_SPAN
    pl.when(shifted_ok)(functools.partial(attend, _SHIFTED))
    pl.when(jnp.logical_not(shifted_ok))(functools.partial(attend, _ONLINE))


def _fox(proj, cum, tile_end_sums, score_bound, *, bsz, seq, tq, tk):
    rows = proj.shape[0]
    nq = seq // tq
    q_spec, k_spec, v_spec, o_spec = _seq_mixer_specs(mixer=3, nq=nq, tq=tq, seq=seq)
    smem = pl.BlockSpec(memory_space=pltpu.SMEM)
    return pl.pallas_call(
        functools.partial(_fox_body, tq=tq, tk=tk, seq=seq),
        grid=(bsz, PAIRS_PER_MIXER, nq),
        in_specs=[
            smem, smem, q_spec, k_spec, v_spec,
            pl.BlockSpec((tq, LANES), lambda b, p, i: (b * nq + i, 0)),
            pl.BlockSpec((seq, LANES), lambda b, p, i: (b, 0)),
        ],
        out_specs=o_spec,
        out_shape=jax.ShapeDtypeStruct((rows, MIXER_WIDTH), BF16),
        scratch_shapes=[pltpu.VMEM((2, seq, LANES), BF16), pltpu.VMEM((2, seq, LANES), BF16)],
        compiler_params=_SEQ_PARAMS,
        name="fox",
    )(tile_end_sums, score_bound, proj, proj, proj, cum, cum)


def _moba_body(bound_ref, q_ref, k_ref, v_ref, o_ref, kmean_scr, kaug_scr, vaug_scr, *, tq, tk, seq):
    iq = pl.program_id(2)
    nblk = seq // MOBA_BLOCK
    blocks_per_tile = tq // MOBA_BLOCK
    bound = bound_ref[0]

    @pl.when(iq == 0)
    def _prepare_keys():
        r = lax.broadcasted_iota(jnp.int32, (LANES, seq), 0)
        c = lax.broadcasted_iota(jnp.int32, (LANES, seq), 1)
        member = jnp.where(c // MOBA_BLOCK == r % HEAD_DIM, 1.0 / MOBA_BLOCK, 0.0).astype(BF16)
        hi, lo = _split2(_dot(member, k_ref[...]))
        kmean_scr[0] = hi
        kmean_scr[1] = lo

        def chunk(c, _):
            rows = pl.ds(pl.multiple_of(c * tk, tk), tk)
            k = k_ref[rows, :].astype(F32)
            lane = lax.broadcasted_iota(jnp.int32, (tk, LANES), 1)
            blk = (c * tk + lax.broadcasted_iota(jnp.int32, (tk, LANES), 0)) // MOBA_BLOCK
            for h in range(2):
                spare = lane - HEAD_DIM * (1 - h)
                onehot = jnp.where(jnp.logical_or(spare == blk, spare == HEAD_DIM - 1), 1.0, 0.0)
                kaug_scr[h, rows, :] = jnp.where(_head_lane_mask(k.shape, h), k, onehot).astype(BF16)
                vaug_scr[h, rows, :] = _augment_values(v_ref[rows, :], h)
            return 0
        lax.fori_loop(0, seq // tk, chunk, 0)

    q = q_ref[...]
    nslot = -(-nblk // 8) * 8
    blk = lax.broadcasted_iota(jnp.int32, (nslot, tq), 0)
    qblk = blocks_per_tile * iq + lax.broadcasted_iota(jnp.int32, (nslot, tq), 1) // MOBA_BLOCK
    q_aug = []
    tail_row = lax.broadcasted_iota(jnp.int32, (HEAD_DIM - nslot, tq), 0)
    tail = jnp.where(tail_row == HEAD_DIM - nslot - 1, -bound, 0.0)
    for h in range(2):
        qh = _own_head(q, h)
        spare = slice(HEAD_DIM * (1 - h), HEAD_DIM * (1 - h) + nslot)
        gate = _dot_nt(kmean_scr[0, spare, :], qh) + _dot_nt(kmean_scr[1, spare, :], qh)
        past = blk < qblk
        work = jnp.where(past, gate, NEG)
        bias = jnp.where(blk == qblk, 0.0, NEG)
        for _ in range(min(MOBA_TOPK, nblk)):
            best = jnp.max(work, axis=0, keepdims=True)
            idx = jnp.min(jnp.where(work == best, blk, nslot), axis=0, keepdims=True)
            hit = blk == idx
            bias = jnp.where(hit, jnp.where(past, 0.0, bias), bias)
            work = jnp.where(hit, -jnp.inf, work)
        own_half = jnp.zeros((HEAD_DIM, tq), F32)
        halves = [bias, tail, own_half] if h == 1 else [own_half, bias, tail]
        q_aug.append(jnp.where(_head_lane_mask(q.shape, h), q.astype(F32),
                               jnp.concatenate(halves, axis=0).T).astype(BF16))

    def attend(scheme):
        init, step, finish, diagonal_parts = scheme

        def tile(j, state):
            rows = pl.ds(pl.multiple_of(j * tk, tk), tk)
            return _attend_tile(q_aug, lambda h: kaug_scr[h, rows, :], lambda h: vaug_scr[h, rows, :], state, step)

        def diagonal_part(state, key0, nkeys):
            rows = pl.ds(pl.multiple_of(iq * tk + key0, nkeys), nkeys)
            return _attend_tile(q_aug, lambda h: kaug_scr[h, rows, :], lambda h: vaug_scr[h, rows, :], state,
                                step, first_row=key0)

        state = _attend_diagonal(diagonal_part, tq, init, diagonal_parts)
        state = lax.fori_loop(0, iq, tile, state)
        o_ref[...] = _merge_heads([finish(st, h) for h, st in enumerate(state)]).astype(BF16)

    shifted_ok = 2.0 * bound <= SHIFTED_MAX_SPAN
    pl.when(shifted_ok)(functools.partial(attend, _SHIFTED))
    pl.when(jnp.logical_not(shifted_ok))(functools.partial(attend, _ONLINE))


def _moba(proj, score_bound, *, bsz, seq, tq, tk):
    assert tq == tk and tq % MOBA_BLOCK == 0
    assert -(-(seq // MOBA_BLOCK) // 8) * 8 < HEAD_DIM
    rows = proj.shape[0]
    nq = seq // tq
    q_spec, k_spec, v_spec, o_spec = _seq_mixer_specs(mixer=2, nq=nq, tq=tq, seq=seq)
    return pl.pallas_call(
        functools.partial(_moba_body, tq=tq, tk=tk, seq=seq),
        grid=(bsz, PAIRS_PER_MIXER, nq),
        in_specs=[pl.BlockSpec(memory_space=pltpu.SMEM), q_spec, k_spec, v_spec],
        out_specs=o_spec,
        out_shape=jax.ShapeDtypeStruct((rows, MIXER_WIDTH), BF16),
        scratch_shapes=[pltpu.VMEM((2, LANES, LANES), BF16), pltpu.VMEM((2, seq, LANES), BF16),
                        pltpu.VMEM((2, seq, LANES), BF16)],
        compiler_params=_SEQ_PARAMS,
        name="moba",
    )(score_bound, proj, proj, proj)


def _stick_body(q_ref, k_ref, v_ref, o_ref, *, tq, tk):
    iq = pl.program_id(2)
    groups = [slice(g * tk, (g + 1) * tk) for g in range(tq // tk)]
    q = q_ref[...]
    q_own = [_own_head(q, h) for h in range(2)]
    later_row = lax.broadcasted_iota(jnp.int32, (tk, tk), 0)
    later_col = lax.broadcasted_iota(jnp.int32, (tk, tk), 1)
    later = jnp.where(later_row > later_col, 1.0, 0.0).astype(BF16)

    def attend(z, weigh_values, carry_sum, strictly_past=None):
        softplus = jnp.maximum(z, 0.0) + jnp.log2(1.0 + jnp.exp2(-jnp.abs(z)))
        log_keep = -softplus
        if strictly_past is not None:
            log_keep = jnp.where(strictly_past, log_keep, 0.0)
        afters = []
        newer = carry_sum
        for c in reversed(range(z.shape[1] // tk)):
            blk = log_keep[:, c * tk:(c + 1) * tk]
            hi, lo = _split2(blk)
            afters.append(_dot(hi, later) + _dot(lo, later) + newer)
            newer = newer + jnp.sum(blk, axis=1, keepdims=True)
        log_a = z - softplus + jnp.concatenate(afters[::-1], axis=1)
        if strictly_past is not None:
            log_a = jnp.where(strictly_past, log_a, NEG)
        return newer, weigh_values(jnp.exp2(log_a).astype(BF16))

    first_blk, k_win, v_win, masks = [], [], [], []
    row = lax.broadcasted_iota(jnp.int32, (tk, 2 * tk), 0)
    col = lax.broadcasted_iota(jnp.int32, (tk, 2 * tk), 1)
    for g in range(len(groups)):
        own_blk = iq * len(groups) + g
        first_blk.append(jnp.maximum(own_blk - 1, 0))
        keys = pl.ds(pl.multiple_of(first_blk[g] * tk, tk), 2 * tk)
        k_win.append(k_ref[keys, :])
        v_win.append(v_ref[keys, :])
        masks.append(col + (first_blk[g] - own_blk) * tk < row)
    strictly_past = jnp.concatenate(masks, axis=0)
    state = []
    for h in range(2):
        z = jnp.concatenate([_dot_nt(q_own[h][rows], k_win[g]) for g, rows in enumerate(groups)], axis=0)
        state.append(attend(
            z, lambda a: jnp.concatenate([_dot(a[rows], v_win[g]) for g, rows in enumerate(groups)], axis=0),
            jnp.zeros((tq, 1), F32), strictly_past))

    outs = [[None] * len(groups) for _ in range(2)]
    for g, rows in enumerate(groups):
        def cond(loop):
            j, tail = loop
            alive = jnp.max(jnp.maximum(tail[0][0], tail[1][0])) > UNDERFLOW_LOG2
            return jnp.logical_and(j >= 0, alive)

        def body(loop):
            j, tail = loop
            keys = pl.ds(pl.multiple_of(j * tk, tk), tk)
            k_blk = k_ref[keys, :]
            v_blk = v_ref[keys, :]
            new = []
            for h in range(2):
                carry_sum, acc = tail[h]
                carry_sum, add = attend(_dot_nt(q_own[h][rows], k_blk), lambda a: _dot(a, v_blk), carry_sum)
                new.append((carry_sum, acc + add))
            return j - 1, tuple(new)

        _, tail = lax.while_loop(cond, body, (first_blk[g] - 1, tuple((s[rows], a[rows]) for s, a in state)))
        for h in range(2):
            outs[h][g] = tail[h][1]
    o_ref[...] = _merge_heads([jnp.concatenate(outs[h], axis=0) for h in range(2)]).astype(BF16)


def _stick(proj, *, bsz, seq, tq, tk):
    rows = proj.shape[0]
    nq = seq // tq
    q_spec, k_spec, v_spec, o_spec = _seq_mixer_specs(mixer=1, nq=nq, tq=tq, seq=seq)
    return pl.pallas_call(
        functools.partial(_stick_body, tq=tq, tk=tk),
        grid=(bsz, PAIRS_PER_MIXER, nq),
        in_specs=[q_spec, k_spec, v_spec],
        out_specs=o_spec,
        out_shape=jax.ShapeDtypeStruct((rows, MIXER_WIDTH), BF16),
        compiler_params=_SEQ_PARAMS,
        name="stick",
    )(proj, proj, proj)


def _outproj_body(o1_ref, o4_ref, o16_ref, l1_ref, l4_ref, l16_ref, yb_ref, yc_ref, yd_ref,
                  gate_ref, x_ref, w_ref, out_ref, y_scr, o_scr, l_scr, *, tm):
    j = pl.program_id(1)

    @pl.when(j == 0)
    def _gate():
        for slot, (dil, o_ref, l_ref) in enumerate(zip(STRIDED_DILATIONS, (o4_ref, o16_ref), (l4_ref, l16_ref))):
            n = tm // dil
            for r in range(dil):
                for c in range(LANE_BLOCKS_PER_MIXER):
                    cols = slice(r * MIXER_WIDTH + c * LANES, r * MIXER_WIDTH + (c + 1) * LANES)
                    o_scr[slot, c, pl.ds(r, n, stride=dil), :] = o_ref[:, cols].astype(F32)
                    l_scr[slot, c, pl.ds(r, n, stride=dil), :] = l_ref[:, cols]

        def natural(scr, slot):
            return jnp.concatenate([scr[slot, c] for c in range(LANE_BLOCKS_PER_MIXER)], axis=1)

        l1, l4, l16 = l1_ref[...], natural(l_scr, 0), natural(l_scr, 1)
        m = jnp.maximum(jnp.maximum(l1, l4), l16)
        e1, e4, e16 = jnp.exp2(l1 - m), jnp.exp2(l4 - m), jnp.exp2(l16 - m)
        ya = (e1 * o1_ref[...].astype(F32) + e4 * natural(o_scr, 0)
              + e16 * natural(o_scr, 1)) / (e1 + e4 + e16)
        parts = (ya, yb_ref[...].astype(F32), yc_ref[...].astype(F32), yd_ref[...].astype(F32))
        for mxr, y in enumerate(parts):
            g = gate_ref[:, mxr * MIXER_WIDTH:(mxr + 1) * MIXER_WIDTH].astype(F32)
            silu = g / (1.0 + jnp.exp(-g))
            y_scr[:, mxr * MIXER_WIDTH:(mxr + 1) * MIXER_WIDTH] = (y * silu).astype(BF16)

    out_ref[...] = x_ref[...] + _dot(y_scr[...], w_ref[...])


def _outproj(seg_o, seg_lse, yb, yc, yd, proj, x2, w_out, *, tm, tn):
    rows, d = x2.shape
    row_blk = lambda i, j: (i, 0)
    mix_spec = pl.BlockSpec((tm, MIXER_WIDTH), row_blk)
    seg_specs = [pl.BlockSpec((tm // dil, dil * MIXER_WIDTH), row_blk) for dil in DILATIONS]
    return pl.pallas_call(
        functools.partial(_outproj_body, tm=tm),
        grid=(rows // tm, d // tn),
        in_specs=seg_specs + seg_specs + [mix_spec] * 3 + [
            pl.BlockSpec((tm, MIX_WIDTH), lambda i, j: (i, GATE_GROUP)),
            pl.BlockSpec((tm, tn), lambda i, j: (i, j)),
            pl.BlockSpec((MIX_WIDTH, tn), lambda i, j: (0, j)),
        ],
        out_specs=pl.BlockSpec((tm, tn), lambda i, j: (i, j)),
        out_shape=jax.ShapeDtypeStruct((rows, d), F32),
        scratch_shapes=[pltpu.VMEM((tm, MIX_WIDTH), BF16),
                        pltpu.VMEM((len(STRIDED_DILATIONS), LANE_BLOCKS_PER_MIXER, tm, LANES), F32),
                        pltpu.VMEM((len(STRIDED_DILATIONS), LANE_BLOCKS_PER_MIXER, tm, LANES), F32)],
        compiler_params=pltpu.CompilerParams(
            dimension_semantics=("arbitrary", "arbitrary"), vmem_limit_bytes=VMEM_LIMIT),
        name="outproj",
    )(*seg_o, *seg_lse, yb, yc, yd, proj, x2, w_out)


def _cast_body(w_ref, o_ref):
    o_ref[...] = w_ref[...].astype(BF16)


def _cast_columns(w):
    depth, d, n = w.shape
    step = min(CAST_COLS, n)
    spec = pl.BlockSpec((1, d, step), lambda layer, j: (layer, 0, j))
    return pl.pallas_call(
        _cast_body,
        grid=(depth, n // step),
        in_specs=[spec],
        out_specs=spec,
        out_shape=jax.ShapeDtypeStruct((depth, d, n), BF16),
        compiler_params=pltpu.CompilerParams(
            dimension_semantics=("arbitrary", "arbitrary"), vmem_limit_bytes=VMEM_LIMIT),
        name="cast_w_out",
    )(w)


def _cast_w_in_body(w_ref, main_ref, gate_ref, *, main_steps):
    j = pl.program_id(1)

    @pl.when(j < main_steps)
    def _():
        main_ref[...] = w_ref[...].astype(BF16)

    @pl.when(j == main_steps)
    def _():
        w = w_ref[0, :, :LANES]
        lane = lax.broadcasted_iota(jnp.int32, w.shape, 1)
        hi, lo = _split2(jnp.where(lane < HEADS_PER_MIXER, w, 0.0))
        gate_ref[0] = jnp.concatenate([hi, lo], axis=1)


def _cast_w_in(w_in):
    depth, d, n = w_in.shape
    assert n == PROJ_COLS + HEADS_PER_MIXER
    main_steps = PROJ_COLS // CAST_COLS
    return pl.pallas_call(
        functools.partial(_cast_w_in_body, main_steps=main_steps),
        grid=(depth, main_steps + 1),
        in_specs=[pl.BlockSpec((1, d, CAST_COLS), lambda layer, j: (layer, 0, j))],
        out_specs=[
            pl.BlockSpec((1, d, CAST_COLS), lambda layer, j: (layer, 0, jnp.minimum(j, main_steps - 1))),
            pl.BlockSpec((1, d, 2 * LANES), lambda layer, j: (layer, 0, 0)),
        ],
        out_shape=[
            jax.ShapeDtypeStruct((depth, d, PROJ_COLS), BF16),
            jax.ShapeDtypeStruct((depth, d, 2 * LANES), BF16),
        ],
        compiler_params=pltpu.CompilerParams(
            dimension_semantics=("arbitrary", "arbitrary"), vmem_limit_bytes=VMEM_LIMIT),
        name="cast_w_in",
    )(w_in)


def _rope_tables(seq):
    inv = 1.0 / (ROPE_THETA ** (jnp.arange(0, HEAD_DIM, 2, dtype=F32) / HEAD_DIM))
    ang = jnp.arange(seq, dtype=F32)[:, None] * inv[None, :]
    cos, sin = jnp.cos(ang), jnp.sin(ang)
    reps = LANES // HEAD_DIM
    cos_t = jnp.tile(jnp.concatenate([cos, cos], axis=1), (1, reps))
    sin_t = jnp.tile(jnp.concatenate([-sin, sin], axis=1), (1, reps))
    return cos_t, sin_t


def _gain_table(qn, kn):
    ones = jnp.ones((HEAD_DIM,), F32)
    per_mixer = {0: (qn[0], ones, qn[1], qn[2]), 1: (kn[0], ones, kn[1], kn[2])}
    blocks = [per_mixer[section][mixer] if section < 2 else ones for section, mixer in PROJ_BLOCK_ORDER]
    tab = jnp.stack([jnp.tile(g.astype(F32), HEADS_PER_MIXER) for g in blocks])
    return tab[:, None, :]


def kernel(x, norm_gain, w_in, q_norm_gain, k_norm_gain, forget_bias, w_out):
    bsz, seq, d = x.shape
    depth = w_in.shape[0]
    rows = bsz * seq
    tm = min(512, seq)
    tq_seq = min(QUERY_TILE, seq)
    cos_t, sin_t = _rope_tables(seq)
    head_of_lane = jnp.arange(MIXER_WIDTH) // HEAD_DIM
    mavg = jnp.where(head_of_lane[:, None] == head_of_lane[None, :], 1.0 / HEAD_DIM, 0.0).astype(BF16)
    tri = jnp.tril(jnp.ones((256, 256), F32)).astype(BF16)

    def natural_col(section, r):
        return PROJ_BLOCK_POS[(section, 0)]

    def strided_col(section, r):
        return r

    w_main, wf_cat = _cast_w_in(w_in)
    w_out_bf16 = _cast_columns(w_out)

    x2 = x.reshape(rows, d)
    for layer in range(depth):
        fb_pad = jnp.pad(forget_bias[layer].astype(F32), (0, LANES - HEADS_PER_MIXER))[None, :]
        gain_tab = _gain_table(q_norm_gain[layer], k_norm_gain[layer])

        proj, cum, *strided = _inproj(x2, norm_gain[layer][None, :].astype(F32), w_main[layer], wf_cat[layer],
                                      fb_pad, gain_tab, cos_t, sin_t, mavg, tri, seq=seq, tm=tm)
        seg = [_band_segment(proj, proj, proj, natural_col, bsz=bsz, seq=seq, dil=1, tq=512)]
        for di, dil in enumerate(STRIDED_DILATIONS):
            qd, kd, vd = strided[3 * di:3 * di + 3]
            seg.append(_band_segment(qd, kd, vd, strided_col, bsz=bsz, seq=seq, dil=dil, tq=512))
        yb = _stick(proj, bsz=bsz, seq=seq, tq=min(STICK_QUERY_TILE, seq), tk=STICK_KEY_TILE)
        yc = _moba(proj, _score_bound(q_norm_gain[layer, 1], k_norm_gain[layer, 1]),
                   bsz=bsz, seq=seq, tq=tq_seq, tk=KEY_TILE)
        nt = seq // KEY_TILE
        tile_end_sums = cum.reshape(bsz, nt, KEY_TILE, LANES)[:, :, KEY_TILE - 1, :HEADS_PER_MIXER]
        tile_end_sums = tile_end_sums.transpose(0, 2, 1)
        yd = _fox(proj, cum, tile_end_sums, _score_bound(q_norm_gain[layer, 2], k_norm_gain[layer, 2]),
                  bsz=bsz, seq=seq, tq=tq_seq, tk=KEY_TILE)
        x2 = _outproj([s[0] for s in seg], [s[1] for s in seg], yb, yc, yd, proj, x2,
                      w_out_bf16[layer], tm=min(OUT_ROW_TILE, seq), tn=d)
    return x2.reshape(bsz, seq, d)
```

```python
import functools

import jax
import jax.numpy as jnp
from jax import lax
from jax.experimental import pallas as pl
from jax.experimental.pallas import tpu as pltpu

F32 = jnp.float32
BF16 = jnp.bfloat16

HEAD_DIM = 64
HALF_DIM = HEAD_DIM // 2
LANES = 128
N_MIXERS = 4
HEADS_PER_MIXER = 8
PAIRS_PER_MIXER = HEADS_PER_MIXER // 2
MIXER_WIDTH = HEADS_PER_MIXER * HEAD_DIM
LANE_BLOCKS_PER_MIXER = MIXER_WIDTH // LANES
MIX_WIDTH = N_MIXERS * MIXER_WIDTH
N_SECTIONS = 4
PROJ_COLS = N_SECTIONS * MIX_WIDTH
COL_BLOCKS_PER_SECTION = MIX_WIDTH // LANES
ROPE_THETA = 10000.0
RMS_EPS = 1e-6
SCALE = HEAD_DIM ** -0.5
LOG2E = 1.4426950408889634
NEG = -1e30
DILATIONS = (1, 4, 16)
STRIDED_DILATIONS = DILATIONS[1:]
BAND = 128
MOBA_BLOCK = 256
MOBA_TOPK = 3
QUERY_TILE = 1024
KEY_TILE = 1024
STICK_QUERY_TILE = 1024
STICK_KEY_TILE = 256
COL_BLOCKS_PER_STEP = 4
PROJ_BLOCK_ORDER = ((0, 0), (1, 0), (0, 2), (1, 2),
                    (0, 3), (1, 3), (0, 1), (1, 1),
                    (2, 0), (2, 1), (2, 2), (2, 3),
                    (3, 0), (3, 1), (3, 2), (3, 3))
PROJ_BLOCK_POS = {block: pos for pos, block in enumerate(PROJ_BLOCK_ORDER)}
GATE_GROUP = PROJ_BLOCK_POS[(3, 0)] // N_MIXERS
CAST_COLS = 1024
OUT_ROW_TILE = 256
UNDERFLOW_LOG2 = -160.0
VMEM_LIMIT = 56 * 1024 * 1024

_NT = (((1,), (1,)), ((), ()))


def _dot(a, b):
    return jnp.dot(a, b, preferred_element_type=F32)


def _dot_nt(a, b):
    return lax.dot_general(a, b, _NT, preferred_element_type=F32)


def _split2(x):
    hi = x.astype(BF16)
    lo = (x - hi.astype(F32)).astype(BF16)
    return hi, lo


def _split3(x):
    b1 = x.astype(BF16)
    r1 = x - b1.astype(F32)
    b2 = r1.astype(BF16)
    r2 = r1 - b2.astype(F32)
    return b1, b2, r2.astype(BF16)


def _head_lane_mask(shape, h):
    lane = lax.broadcasted_iota(jnp.int32, shape, len(shape) - 1)
    return (lane >= h * HEAD_DIM) & (lane < (h + 1) * HEAD_DIM)


def _own_head(x, h):
    return jnp.where(_head_lane_mask(x.shape, h), x, jnp.zeros_like(x))


def _lane_column(x, n):
    lane = lax.broadcasted_iota(jnp.int32, x.shape, 1)
    return jnp.sum(jnp.where(lane == n, x, 0.0), axis=1, keepdims=True)


def _inproj_body(x_ref, g_ref, wf_ref, fb_ref, gain_ref, cos_ref, sin_ref, mavg_ref, tri_ref, *rest,
                 blocks_per_batch, tm):
    w_refs, (proj_ref, cum_ref), rest = rest[:COL_BLOCKS_PER_STEP], rest[COL_BLOCKS_PER_STEP:][:2], \
        rest[COL_BLOCKS_PER_STEP + 2:]
    dil_refs = rest[:3 * len(STRIDED_DILATIONS)]
    h_scr, carry_scr, dil_scr = rest[3 * len(STRIDED_DILATIONS):]
    i = pl.program_id(0)
    j = pl.program_id(1)

    @pl.when(j == 0)
    def _prologue():
        x = x_ref[...]
        ms = jnp.mean(x * x, axis=-1, keepdims=True)
        h = x * lax.rsqrt(ms + RMS_EPS) * g_ref[...]
        h_hi, h_lo = _split2(h)
        h_scr[...] = h_hi
        wf = wf_ref[...]
        t = _dot(h_hi, wf)
        u = _dot(h_lo, wf[:, :LANES])
        logit = t[:, :LANES] + t[:, LANES:] + u + fb_ref[...]
        lf = jnp.minimum(logit, 0.0) - jnp.log1p(jnp.exp(-jnp.abs(logit)))

        @pl.when(i % blocks_per_batch == 0)
        def _():
            carry_scr[...] = jnp.zeros_like(carry_scr)

        carry = carry_scr[...]
        tri = tri_ref[...]
        sub = tri.shape[0]
        for r in range(tm // sub):
            b1, b2, b3 = _split3(lf[r * sub:(r + 1) * sub])
            c = _dot(tri, jnp.concatenate([b1, b2, b3], axis=1))
            c = c[:, :LANES] + c[:, LANES:2 * LANES] + c[:, 2 * LANES:] + carry
            cum_ref[r * sub:(r + 1) * sub, :] = c
            carry = c[sub - 1:sub, :]
        carry_scr[...] = carry

    for group in range(len(PROJ_BLOCK_ORDER) // COL_BLOCKS_PER_STEP):
        @pl.when(j == group)
        def _():
            for sub in range(COL_BLOCKS_PER_STEP):
                section, mixer = PROJ_BLOCK_ORDER[group * COL_BLOCKS_PER_STEP + sub]
                _inproj_column_block(section, mixer, sub * MIXER_WIDTH, h_scr, w_refs[sub], gain_ref.at[sub],
                                     cos_ref, sin_ref, mavg_ref, proj_ref, dil_refs, dil_scr, tm=tm)


def _inproj_column_block(section, mixer, col0, h_scr, w_ref, gain_ref, cos_ref, sin_ref, mavg_ref, proj_ref,
                         dil_refs, dil_scr, *, tm):
    def out_cols(c):
        return slice(col0 + c * LANES, col0 + (c + 1) * LANES)

    acc = _dot(h_scr[...], w_ref[...])
    normed = section < 2 and mixer != 1
    roped = normed and mixer in (0, 2)
    scale = SCALE * LOG2E if section == 0 else 1.0

    if not normed:
        proj_ref[:, col0:col0 + MIXER_WIDTH] = (acc * scale if section == 0 else acc).astype(BF16)
    else:
        ms = _dot((acc * acc).astype(BF16), mavg_ref[...])
        t = acc * lax.rsqrt(ms + RMS_EPS) * (gain_ref[0] * scale)
        if not roped:
            proj_ref[:, col0:col0 + MIXER_WIDTH] = t.astype(BF16)
        else:
            cosv = cos_ref[...]
            sinv = sin_ref[...]
            lane = lax.broadcasted_iota(jnp.int32, (tm, LANES), 1)
            first_half = (lane % HEAD_DIM) < HALF_DIM
            for c in range(LANE_BLOCKS_PER_MIXER):
                tc = t[:, c * LANES:(c + 1) * LANES]
                partner = jnp.where(first_half,
                                    pltpu.roll(tc, LANES - HALF_DIM, 1),
                                    pltpu.roll(tc, HALF_DIM, 1))
                proj_ref[:, out_cols(c)] = (tc * cosv + partner * sinv).astype(BF16)

    if mixer == 0 and section < 3:
        for c in range(LANE_BLOCKS_PER_MIXER):
            dil_scr[c] = proj_ref[:, out_cols(c)].astype(F32)
        for di, dil in enumerate(STRIDED_DILATIONS):
            ref = dil_refs[3 * di + section]
            n = tm // dil
            for r in range(dil):
                for c in range(LANE_BLOCKS_PER_MIXER):
                    lo = r * MIXER_WIDTH + c * LANES
                    ref[:, lo:lo + LANES] = dil_scr[c, pl.ds(r, n, stride=dil), :].astype(BF16)


def _inproj(x2, g, w_main, wf_cat, fb_pad, gain_tab, cos_t, sin_t, mavg, tri, *, seq, tm):
    rows, d = x2.shape
    blocks_per_batch = seq // tm
    step_cols = COL_BLOCKS_PER_STEP * MIXER_WIDTH
    body = functools.partial(_inproj_body, blocks_per_batch=blocks_per_batch, tm=tm)

    def weight_block(sub):
        origin = [section * N_MIXERS + mixer for section, mixer in PROJ_BLOCK_ORDER[sub::COL_BLOCKS_PER_STEP]]

        def index(i, j):
            blk = origin[0]
            for group in range(1, len(origin)):
                blk = jnp.where(j == group, origin[group], blk)
            return 0, blk
        return index

    dil_specs, dil_shapes = [], []
    for dil in STRIDED_DILATIONS:
        for _ in range(3):
            dil_specs.append(pl.BlockSpec((tm // dil, dil * MIXER_WIDTH), lambda i, j: (i, 0)))
            dil_shapes.append(jax.ShapeDtypeStruct((rows // dil, dil * MIXER_WIDTH), BF16))
    return pl.pallas_call(
        body,
        grid=(rows // tm, PROJ_COLS // step_cols),
        in_specs=[
            pl.BlockSpec((tm, d), lambda i, j: (i, 0)),
            pl.BlockSpec((1, d), lambda i, j: (0, 0)),
            pl.BlockSpec((d, 2 * LANES), lambda i, j: (0, 0)),
            pl.BlockSpec((1, LANES), lambda i, j: (0, 0)),
            pl.BlockSpec((COL_BLOCKS_PER_STEP, 1, MIXER_WIDTH), lambda i, j: (j, 0, 0)),
            pl.BlockSpec((tm, LANES), lambda i, j: (i % blocks_per_batch, 0)),
            pl.BlockSpec((tm, LANES), lambda i, j: (i % blocks_per_batch, 0)),
            pl.BlockSpec((MIXER_WIDTH, MIXER_WIDTH), lambda i, j: (0, 0)),
            pl.BlockSpec(tri.shape, lambda i, j: (0, 0)),
        ] + [pl.BlockSpec((d, MIXER_WIDTH), weight_block(sub)) for sub in range(COL_BLOCKS_PER_STEP)],
        out_specs=[
            pl.BlockSpec((tm, step_cols), lambda i, j: (i, j)),
            pl.BlockSpec((tm, LANES), lambda i, j: (i, 0)),
        ] + dil_specs,
        out_shape=[
            jax.ShapeDtypeStruct((rows, PROJ_COLS), BF16),
            jax.ShapeDtypeStruct((rows, LANES), F32),
        ] + dil_shapes,
        scratch_shapes=[pltpu.VMEM((tm, d), BF16), pltpu.VMEM((1, LANES), F32),
                        pltpu.VMEM((LANE_BLOCKS_PER_MIXER, tm, LANES), F32)],
        compiler_params=pltpu.CompilerParams(
            dimension_semantics=("arbitrary", "arbitrary"), vmem_limit_bytes=VMEM_LIMIT),
        name="inproj",
    )(x2, g, wf_cat, fb_pad, gain_tab, cos_t, sin_t, mavg, tri, *([w_main] * COL_BLOCKS_PER_STEP))


def _band_body(q_ref, k_ref, v_ref, kp_ref, vp_ref, o_ref, lse_ref, *, tq):
    first = pl.program_id(2) == 0
    row = lax.broadcasted_iota(jnp.int32, (BAND, 2 * BAND), 0)
    col = lax.broadcasted_iota(jnp.int32, (BAND, 2 * BAND), 1)
    in_band = jnp.logical_or(jnp.logical_and(col < BAND, col >= row),
                             jnp.logical_and(col >= BAND, col - BAND <= row))
    first_mask = jnp.logical_and(in_band, jnp.logical_or(col >= BAND, jnp.logical_not(first)))
    lane = lax.broadcasted_iota(jnp.int32, (BAND, LANES), 1)
    head0 = lane < HEAD_DIM
    for pair in range(PAIRS_PER_MIXER):
        lanes = slice(pair * LANES, (pair + 1) * LANES)
        for c in range(tq // BAND):
            own = slice(c * BAND, (c + 1) * BAND)
            qc = q_ref[own, lanes]
            if c == 0:
                kw = jnp.concatenate([kp_ref[:, lanes], k_ref[own, lanes]], axis=0)
                vw = jnp.concatenate([vp_ref[:, lanes], v_ref[own, lanes]], axis=0)
                mask = first_mask
            else:
                window = slice((c - 1) * BAND, (c + 1) * BAND)
                kw = k_ref[window, lanes]
                vw = v_ref[window, lanes]
                mask = in_band
            outs, lses = [], []
            for h in range(2):
                s = jnp.where(mask, _dot_nt(_own_head(qc, h), kw), NEG)
                m = jnp.max(s, axis=1, keepdims=True)
                p = jnp.exp2(s - m)
                l = jnp.sum(p, axis=1, keepdims=True)
                outs.append(_dot(p.astype(BF16), vw) / l)
                lses.append(m + jnp.log2(l))
            o_ref[own, lanes] = jnp.where(head0, outs[0], outs[1]).astype(BF16)
            lse_ref[own, lanes] = jnp.where(head0, lses[0], lses[1])


def _band_segment(q_arr, k_arr, v_arr, col_of, *, bsz, seq, dil, tq):
    sub_rows = q_arr.shape[0]
    sub_len = seq // dil
    tq = min(tq, sub_len)
    nq = sub_len // tq
    band_per_tq = tq // BAND

    def cur(section):
        return pl.BlockSpec((tq, MIXER_WIDTH), lambda b, r, i: (b * nq + i, col_of(section, r)))

    def prev(section):
        return pl.BlockSpec(
            (BAND, MIXER_WIDTH),
            lambda b, r, i: (jnp.maximum((b * nq + i) * band_per_tq - 1, 0), col_of(section, r)))

    out_spec = pl.BlockSpec((tq, MIXER_WIDTH), lambda b, r, i: (b * nq + i, r))
    return pl.pallas_call(
        functools.partial(_band_body, tq=tq),
        grid=(bsz, dil, nq),
        in_specs=[cur(0), cur(1), cur(2), prev(1), prev(2)],
        out_specs=[out_spec, out_spec],
        out_shape=[
            jax.ShapeDtypeStruct((sub_rows, dil * MIXER_WIDTH), BF16),
            jax.ShapeDtypeStruct((sub_rows, dil * MIXER_WIDTH), F32),
        ],
        compiler_params=pltpu.CompilerParams(
            dimension_semantics=("arbitrary",) * 3, vmem_limit_bytes=VMEM_LIMIT),
        name=f"band_d{dil}",
    )(q_arr, k_arr, v_arr, k_arr, v_arr)


def _online_init(tq):
    return (jnp.full((tq, 1), NEG, F32), jnp.zeros((tq, 1), F32), jnp.zeros((tq, LANES), F32))


def _online_step(carry, s, v_blk):
    m, l, acc = carry
    m_new = jnp.maximum(m, jnp.max(s, axis=1, keepdims=True))
    alpha = jnp.exp2(m - m_new)
    p = jnp.exp2(s - m_new)
    l = alpha * l + jnp.sum(p, axis=1, keepdims=True)
    acc = alpha * acc + _dot(p.astype(BF16), v_blk)
    return m_new, l, acc


def _online_finish(carry, h):
    _, l, acc = carry
    return acc / l


def _shifted_init(tq):
    return (jnp.zeros((tq, LANES), F32),)


def _shifted_step(carry, s, v_blk):
    acc, = carry
    return (acc + _dot(jnp.exp2(s).astype(BF16), v_blk),)


def _shifted_finish(carry, h):
    acc, = carry
    return acc / _lane_column(acc, _ones_lane(h))


_ONLINE = (_online_init, _online_step, _online_finish, 1)
_SHIFTED = (_shifted_init, _shifted_step, _shifted_finish, 4)
SHIFTED_MAX_SPAN = 100.0


def _ones_lane(h):
    return HEAD_DIM * (1 - h)


def _augment_values(v, h):
    lane = lax.broadcasted_iota(jnp.int32, v.shape, 1)
    ones = jnp.where(lane == _ones_lane(h), 1.0, 0.0)
    return jnp.where(_head_lane_mask(v.shape, h), v.astype(F32), ones).astype(BF16)


def _attend_tile(q_aug, k_tile, v_tile, state, step, heads=(0, 1), first_row=None):
    new = list(state)
    for h in heads:
        start = first_row or 0
        s = _dot_nt(q_aug[h][start:] if start else q_aug[h], k_tile(h))
        if first_row is not None:
            row = lax.broadcasted_iota(jnp.int32, s.shape, 0)
            col = lax.broadcasted_iota(jnp.int32, s.shape, 1)
            s = jnp.where(col <= row, s, NEG)
        if start:
            part = step(tuple(a[start:] for a in state[h]), s, v_tile(h))
            new[h] = tuple(jnp.concatenate([a[:start], b], axis=0) for a, b in zip(state[h], part))
        else:
            new[h] = step(state[h], s, v_tile(h))
    return tuple(new)


def _attend_diagonal(tile_part, tq, init, parts):
    state = (init(tq), init(tq))
    for d in range(parts):
        state = tile_part(state, d * (tq // parts), tq // parts)
    return state


def _score_bound(q_gain, k_gain):
    bound = 1.02 * HEAD_DIM * SCALE * LOG2E * jnp.max(jnp.abs(q_gain)) * jnp.max(jnp.abs(k_gain))
    return bound.astype(BF16).astype(F32).reshape(1)


def _merge_heads(outs):
    lane = lax.broadcasted_iota(jnp.int32, outs[0].shape, 1)
    return jnp.where(lane < HEAD_DIM, outs[0], outs[1])


def _seq_mixer_specs(*, mixer, nq, tq, seq):
    def colblk(section, p):
        return PROJ_BLOCK_POS[(section, mixer)] * PAIRS_PER_MIXER + p

    q_spec = pl.BlockSpec((tq, LANES), lambda b, p, i: (b * nq + i, colblk(0, p)))
    k_spec = pl.BlockSpec((seq, LANES), lambda b, p, i: (b, colblk(1, p)))
    v_spec = pl.BlockSpec((seq, LANES), lambda b, p, i: (b, colblk(2, p)))
    o_spec = pl.BlockSpec((tq, LANES), lambda b, p, i: (b * nq + i, p))
    return q_spec, k_spec, v_spec, o_spec


_SEQ_PARAMS = pltpu.CompilerParams(
    dimension_semantics=("arbitrary",) * 3, vmem_limit_bytes=VMEM_LIMIT)


def _fox_augment(x, cum, head, h, key_side, bound):
    lane = lax.broadcasted_iota(jnp.int32, x.shape, 1)
    g = jnp.broadcast_to(_lane_column(cum, head) * LOG2E, x.shape)
    g1, g2, g3 = (piece.astype(F32) for piece in _split3(g))
    one = jnp.ones_like(g1)
    pieces = (one, one, one, -g1, -g2, -g3, one) if key_side else (g1, g2, g3, one, one, one, -bound * one)
    base = HEAD_DIM * (1 - h)
    aug = jnp.zeros_like(g1)
    for n, piece in enumerate(pieces):
        aug = jnp.where(lane == base + n, piece, aug)
    return jnp.where(_head_lane_mask(x.shape, h), x.astype(F32), aug).astype(BF16)


def _fox_body(fend_ref, bound_ref, q_ref, k_ref, v_ref, cumq_ref, cumk_ref, o_ref, kaug_scr, vaug_scr, *,
              tq, tk, seq):
    b = pl.program_id(0)
    p = pl.program_id(1)
    iq = pl.program_id(2)
    bound = bound_ref[0]

    @pl.when(iq == 0)
    def _augment_keys():
        def chunk(c, _):
            rows = pl.ds(pl.multiple_of(c * tk, tk), tk)
            for h in range(2):
                kaug_scr[h, rows, :] = _fox_augment(k_ref[rows, :], cumk_ref[rows, :], 2 * p + h, h, True, bound)
                vaug_scr[h, rows, :] = _augment_values(v_ref[rows, :], h)
            return 0
        lax.fori_loop(0, seq // tk, chunk, 0)

    q = q_ref[...]
    cumq = cumq_ref[...]
    q_aug = [_fox_augment(q, cumq, 2 * p + h, h, False, bound) for h in range(2)]

    first_past = iq - 1
    last = jnp.maximum(first_past, 0)

    def reaches(j, heads):
        j = jnp.maximum(j, 0)
        alive = [LOG2E * (fend_ref[b, 2 * p + h, last] - fend_ref[b, 2 * p + h, j])
                 + 2.0 * bound > UNDERFLOW_LOG2 for h in heads]
        return functools.reduce(jnp.logical_and, alive)

    def attend(scheme):
        init, step, finish, diagonal_parts = scheme

        def tile(j, state, heads=(0, 1)):
            rows = pl.ds(pl.multiple_of(j * tk, tk), tk)
            return _attend_tile(q_aug, lambda h: kaug_scr[h, rows, :], lambda h: vaug_scr[h, rows, :], state,
                                step, heads)

        def diagonal_part(state, key0, nkeys):
            rows = pl.ds(pl.multiple_of(iq * tk + key0, nkeys), nkeys)
            return _attend_tile(q_aug, lambda h: kaug_scr[h, rows, :], lambda h: vaug_scr[h, rows, :], state,
                                step, first_row=key0)

        loop = (first_past, _attend_diagonal(diagonal_part, tq, init, diagonal_parts))
        for heads in ((0, 1), (0,), (1,)):
            loop = lax.while_loop(
                lambda lp: jnp.logical_and(lp[0] >= 0, reaches(lp[0], heads)),
                lambda lp: (lp[0] - 1, tile(lp[0], lp[1], heads)),
                loop)
        o_ref[...] = _merge_heads([finish(st, h) for h, st in enumerate(loop[1])]).astype(BF16)

    shifted_ok = 2.0 * bound <= SHIFTED_MAX_SPAN
    pl.when(shifted_ok)(functools.partial(attend, _SHIFTED))
    pl.when(jnp.logical_not(shifted_ok))(functools.partial(attend, _ONLINE))


def _fox(proj, cum, tile_end_sums, score_bound, *, bsz, seq, tq, tk):
    rows = proj.shape[0]
    nq = seq // tq
    q_spec, k_spec, v_spec, o_spec = _seq_mixer_specs(mixer=3, nq=nq, tq=tq, seq=seq)
    smem = pl.BlockSpec(memory_space=pltpu.SMEM)
    return pl.pallas_call(
        functools.partial(_fox_body, tq=tq, tk=tk, seq=seq),
        grid=(bsz, PAIRS_PER_MIXER, nq),
        in_specs=[
            smem, smem, q_spec, k_spec, v_spec,
            pl.BlockSpec((tq, LANES), lambda b, p, i: (b * nq + i, 0)),
            pl.BlockSpec((seq, LANES), lambda b, p, i: (b, 0)),
        ],
        out_specs=o_spec,
        out_shape=jax.ShapeDtypeStruct((rows, MIXER_WIDTH), BF16),
        scratch_shapes=[pltpu.VMEM((2, seq, LANES), BF16), pltpu.VMEM((2, seq, LANES), BF16)],
        compiler_params=_SEQ_PARAMS,
        name="fox",
    )(tile_end_sums, score_bound, proj, proj, proj, cum, cum)


def _moba_body(bound_ref, q_ref, k_ref, v_ref, o_ref, kmean_scr, kaug_scr, vaug_scr, *, tq, tk, seq):
    iq = pl.program_id(2)
    nblk = seq // MOBA_BLOCK
    blocks_per_tile = tq // MOBA_BLOCK
    bound = bound_ref[0]

    @pl.when(iq == 0)
    def _prepare_keys():
        r = lax.broadcasted_iota(jnp.int32, (LANES, seq), 0)
        c = lax.broadcasted_iota(jnp.int32, (LANES, seq), 1)
        member = jnp.where(c // MOBA_BLOCK == r % HEAD_DIM, 1.0 / MOBA_BLOCK, 0.0).astype(BF16)
        hi, lo = _split2(_dot(member, k_ref[...]))
        kmean_scr[0] = hi
        kmean_scr[1] = lo

        def chunk(c, _):
            rows = pl.ds(pl.multiple_of(c * tk, tk), tk)
            k = k_ref[rows, :].astype(F32)
            lane = lax.broadcasted_iota(jnp.int32, (tk, LANES), 1)
            blk = (c * tk + lax.broadcasted_iota(jnp.int32, (tk, LANES), 0)) // MOBA_BLOCK
            for h in range(2):
                spare = lane - HEAD_DIM * (1 - h)
                onehot = jnp.where(jnp.logical_or(spare == blk, spare == HEAD_DIM - 1), 1.0, 0.0)
                kaug_scr[h, rows, :] = jnp.where(_head_lane_mask(k.shape, h), k, onehot).astype(BF16)
                vaug_scr[h, rows, :] = _augment_values(v_ref[rows, :], h)
            return 0
        lax.fori_loop(0, seq // tk, chunk, 0)

    q = q_ref[...]
    nslot = -(-nblk // 8) * 8
    blk = lax.broadcasted_iota(jnp.int32, (nslot, tq), 0)
    qblk = blocks_per_tile * iq + lax.broadcasted_iota(jnp.int32, (nslot, tq), 1) // MOBA_BLOCK
    q_aug = []
    tail_row = lax.broadcasted_iota(jnp.int32, (HEAD_DIM - nslot, tq), 0)
    tail = jnp.where(tail_row == HEAD_DIM - nslot - 1, -bound, 0.0)
    for h in range(2):
        qh = _own_head(q, h)
        spare = slice(HEAD_DIM * (1 - h), HEAD_DIM * (1 - h) + nslot)
        gate = _dot_nt(kmean_scr[0, spare, :], qh) + _dot_nt(kmean_scr[1, spare, :], qh)
        past = blk < qblk
        work = jnp.where(past, gate, NEG)
        bias = jnp.where(blk == qblk, 0.0, NEG)
        for _ in range(min(MOBA_TOPK, nblk)):
            best = jnp.max(work, axis=0, keepdims=True)
            idx = jnp.min(jnp.where(work == best, blk, nslot), axis=0, keepdims=True)
            hit = blk == idx
            bias = jnp.where(hit, jnp.where(past, 0.0, bias), bias)
            work = jnp.where(hit, -jnp.inf, work)
        own_half = jnp.zeros((HEAD_DIM, tq), F32)
        halves = [bias, tail, own_half] if h == 1 else [own_half, bias, tail]
        q_aug.append(jnp.where(_head_lane_mask(q.shape, h), q.astype(F32),
                               jnp.concatenate(halves, axis=0).T).astype(BF16))

    def attend(scheme):
        init, step, finish, diagonal_parts = scheme

        def tile(j, state):
            rows = pl.ds(pl.multiple_of(j * tk, tk), tk)
            return _attend_tile(q_aug, lambda h: kaug_scr[h, rows, :], lambda h: vaug_scr[h, rows, :], state, step)

        def diagonal_part(state, key0, nkeys):
            rows = pl.ds(pl.multiple_of(iq * tk + key0, nkeys), nkeys)
            return _attend_tile(q_aug, lambda h: kaug_scr[h, rows, :], lambda h: vaug_scr[h, rows, :], state,
                                step, first_row=key0)

        state = _attend_diagonal(diagonal_part, tq, init, diagonal_parts)
        state = lax.fori_loop(0, iq, tile, state)
        o_ref[...] = _merge_heads([finish(st, h) for h, st in enumerate(state)]).astype(BF16)

    shifted_ok = 2.0 * bound <= SHIFTED_MAX_SPAN
    pl.when(shifted_ok)(functools.partial(attend, _SHIFTED))
    pl.when(jnp.logical_not(shifted_ok))(functools.partial(attend, _ONLINE))


def _moba(proj, score_bound, *, bsz, seq, tq, tk):
    assert tq == tk and tq % MOBA_BLOCK == 0
    assert -(-(seq // MOBA_BLOCK) // 8) * 8 < HEAD_DIM
    rows = proj.shape[0]
    nq = seq // tq
    q_spec, k_spec, v_spec, o_spec = _seq_mixer_specs(mixer=2, nq=nq, tq=tq, seq=seq)
    return pl.pallas_call(
        functools.partial(_moba_body, tq=tq, tk=tk, seq=seq),
        grid=(bsz, PAIRS_PER_MIXER, nq),
        in_specs=[pl.BlockSpec(memory_space=pltpu.SMEM), q_spec, k_spec, v_spec],
        out_specs=o_spec,
        out_shape=jax.ShapeDtypeStruct((rows, MIXER_WIDTH), BF16),
        scratch_shapes=[pltpu.VMEM((2, LANES, LANES), BF16), pltpu.VMEM((2, seq, LANES), BF16),
                        pltpu.VMEM((2, seq, LANES), BF16)],
        compiler_params=_SEQ_PARAMS,
        name="moba",
    )(score_bound, proj, proj, proj)


def _stick_body(q_ref, k_ref, v_ref, o_ref, *, tq, tk):
    iq = pl.program_id(2)
    groups = [slice(g * tk, (g + 1) * tk) for g in range(tq // tk)]
    q = q_ref[...]
    q_own = [_own_head(q, h) for h in range(2)]
    later_row = lax.broadcasted_iota(jnp.int32, (tk, tk), 0)
    later_col = lax.broadcasted_iota(jnp.int32, (tk, tk), 1)
    later = jnp.where(later_row > later_col, 1.0, 0.0).astype(BF16)

    def attend(z, weigh_values, carry_sum, strictly_past=None):
        softplus = jnp.maximum(z, 0.0) + jnp.log2(1.0 + jnp.exp2(-jnp.abs(z)))
        log_keep = -softplus
        if strictly_past is not None:
            log_keep = jnp.where(strictly_past, log_keep, 0.0)
        afters = []
        newer = carry_sum
        for c in reversed(range(z.shape[1] // tk)):
            blk = log_keep[:, c * tk:(c + 1) * tk]
            hi, lo = _split2(blk)
            afters.append(_dot(hi, later) + _dot(lo, later) + newer)
            newer = newer + jnp.sum(blk, axis=1, keepdims=True)
        log_a = z - softplus + jnp.concatenate(afters[::-1], axis=1)
        if strictly_past is not None:
            log_a = jnp.where(strictly_past, log_a, NEG)
        return newer, weigh_values(jnp.exp2(log_a).astype(BF16))

    first_blk, k_win, v_win, masks = [], [], [], []
    row = lax.broadcasted_iota(jnp.int32, (tk, 2 * tk), 0)
    col = lax.broadcasted_iota(jnp.int32, (tk, 2 * tk), 1)
    for g in range(len(groups)):
        own_blk = iq * len(groups) + g
        first_blk.append(jnp.maximum(own_blk - 1, 0))
        keys = pl.ds(pl.multiple_of(first_blk[g] * tk, tk), 2 * tk)
        k_win.append(k_ref[keys, :])
        v_win.append(v_ref[keys, :])
        masks.append(col + (first_blk[g] - own_blk) * tk < row)
    strictly_past = jnp.concatenate(masks, axis=0)
    state = []
    for h in range(2):
        z = jnp.concatenate([_dot_nt(q_own[h][rows], k_win[g]) for g, rows in enumerate(groups)], axis=0)
        state.append(attend(
            z, lambda a: jnp.concatenate([_dot(a[rows], v_win[g]) for g, rows in enumerate(groups)], axis=0),
            jnp.zeros((tq, 1), F32), strictly_past))

    outs = [[None] * len(groups) for _ in range(2)]
    for g, rows in enumerate(groups):
        def cond(loop):
            j, tail = loop
            alive = jnp.max(jnp.maximum(tail[0][0], tail[1][0])) > UNDERFLOW_LOG2
            return jnp.logical_and(j >= 0, alive)

        def body(loop):
            j, tail = loop
            keys = pl.ds(pl.multiple_of(j * tk, tk), tk)
            k_blk = k_ref[keys, :]
            v_blk = v_ref[keys, :]
            new = []
            for h in range(2):
                carry_sum, acc = tail[h]
                carry_sum, add = attend(_dot_nt(q_own[h][rows], k_blk), lambda a: _dot(a, v_blk), carry_sum)
                new.append((carry_sum, acc + add))
            return j - 1, tuple(new)

        _, tail = lax.while_loop(cond, body, (first_blk[g] - 1, tuple((s[rows], a[rows]) for s, a in state)))
        for h in range(2):
            outs[h][g] = tail[h][1]
    o_ref[...] = _merge_heads([jnp.concatenate(outs[h], axis=0) for h in range(2)]).astype(BF16)


def _stick(proj, *, bsz, seq, tq, tk):
    rows = proj.shape[0]
    nq = seq // tq
    q_spec, k_spec, v_spec, o_spec = _seq_mixer_specs(mixer=1, nq=nq, tq=tq, seq=seq)
    return pl.pallas_call(
        functools.partial(_stick_body, tq=tq, tk=tk),
        grid=(bsz, PAIRS_PER_MIXER, nq),
        in_specs=[q_spec, k_spec, v_spec],
        out_specs=o_spec,
        out_shape=jax.ShapeDtypeStruct((rows, MIXER_WIDTH), BF16),
        compiler_params=_SEQ_PARAMS,
        name="stick",
    )(proj, proj, proj)


def _outproj_body(o1_ref, o4_ref, o16_ref, l1_ref, l4_ref, l16_ref, yb_ref, yc_ref, yd_ref,
                  gate_ref, x_ref, w_ref, out_ref, y_scr, o_scr, l_scr, *, tm):
    j = pl.program_id(1)

    @pl.when(j == 0)
    def _gate():
        for slot, (dil, o_ref, l_ref) in enumerate(zip(STRIDED_DILATIONS, (o4_ref, o16_ref), (l4_ref, l16_ref))):
            n = tm // dil
            for r in range(dil):
                for c in range(LANE_BLOCKS_PER_MIXER):
                    cols = slice(r * MIXER_WIDTH + c * LANES, r * MIXER_WIDTH + (c + 1) * LANES)
                    o_scr[slot, c, pl.ds(r, n, stride=dil), :] = o_ref[:, cols].astype(F32)
                    l_scr[slot, c, pl.ds(r, n, stride=dil), :] = l_ref[:, cols]

        def natural(scr, slot):
            return jnp.concatenate([scr[slot, c] for c in range(LANE_BLOCKS_PER_MIXER)], axis=1)

        l1, l4, l16 = l1_ref[...], natural(l_scr, 0), natural(l_scr, 1)
        m = jnp.maximum(jnp.maximum(l1, l4), l16)
        e1, e4, e16 = jnp.exp2(l1 - m), jnp.exp2(l4 - m), jnp.exp2(l16 - m)
        ya = (e1 * o1_ref[...].astype(F32) + e4 * natural(o_scr, 0)
              + e16 * natural(o_scr, 1)) / (e1 + e4 + e16)
        parts = (ya, yb_ref[...].astype(F32), yc_ref[...].astype(F32), yd_ref[...].astype(F32))
        for mxr, y in enumerate(parts):
            g = gate_ref[:, mxr * MIXER_WIDTH:(mxr + 1) * MIXER_WIDTH].astype(F32)
            silu = g / (1.0 + jnp.exp(-g))
            y_scr[:, mxr * MIXER_WIDTH:(mxr + 1) * MIXER_WIDTH] = (y * silu).astype(BF16)

    out_ref[...] = x_ref[...] + _dot(y_scr[...], w_ref[...])


def _outproj(seg_o, seg_lse, yb, yc, yd, proj, x2, w_out, *, tm, tn):
    rows, d = x2.shape
    row_blk = lambda i, j: (i, 0)
    mix_spec = pl.BlockSpec((tm, MIXER_WIDTH), row_blk)
    seg_specs = [pl.BlockSpec((tm // dil, dil * MIXER_WIDTH), row_blk) for dil in DILATIONS]
    return pl.pallas_call(
        functools.partial(_outproj_body, tm=tm),
        grid=(rows // tm, d // tn),
        in_specs=seg_specs + seg_specs + [mix_spec] * 3 + [
            pl.BlockSpec((tm, MIX_WIDTH), lambda i, j: (i, GATE_GROUP)),
            pl.BlockSpec((tm, tn), lambda i, j: (i, j)),
            pl.BlockSpec((MIX_WIDTH, tn), lambda i, j: (0, j)),
        ],
        out_specs=pl.BlockSpec((tm, tn), lambda i, j: (i, j)),
        out_shape=jax.ShapeDtypeStruct((rows, d), F32),
        scratch_shapes=[pltpu.VMEM((tm, MIX_WIDTH), BF16),
                        pltpu.VMEM((len(STRIDED_DILATIONS), LANE_BLOCKS_PER_MIXER, tm, LANES), F32),
                        pltpu.VMEM((len(STRIDED_DILATIONS), LANE_BLOCKS_PER_MIXER, tm, LANES), F32)],
        compiler_params=pltpu.CompilerParams(
            dimension_semantics=("arbitrary", "arbitrary"), vmem_limit_bytes=VMEM_LIMIT),
        name="outproj",
    )(*seg_o, *seg_lse, yb, yc, yd, proj, x2, w_out)


def _cast_body(w_ref, o_ref):
    o_ref[...] = w_ref[...].astype(BF16)


def _cast_columns(w):
    depth, d, n = w.shape
    step = min(CAST_COLS, n)
    spec = pl.BlockSpec((1, d, step), lambda layer, j: (layer, 0, j))
    return pl.pallas_call(
        _cast_body,
        grid=(depth, n // step),
        in_specs=[spec],
        out_specs=spec,
        out_shape=jax.ShapeDtypeStruct((depth, d, n), BF16),
        compiler_params=pltpu.CompilerParams(
            dimension_semantics=("arbitrary", "arbitrary"), vmem_limit_bytes=VMEM_LIMIT),
        name="cast_w_out",
    )(w)


def _cast_w_in_body(w_ref, main_ref, gate_ref, *, main_steps):
    j = pl.program_id(1)

    @pl.when(j < main_steps)
    def _():
        main_ref[...] = w_ref[...].astype(BF16)

    @pl.when(j == main_steps)
    def _():
        w = w_ref[0, :, :LANES]
        lane = lax.broadcasted_iota(jnp.int32, w.shape, 1)
        hi, lo = _split2(jnp.where(lane < HEADS_PER_MIXER, w, 0.0))
        gate_ref[0] = jnp.concatenate([hi, lo], axis=1)


def _cast_w_in(w_in):
    depth, d, n = w_in.shape
    assert n == PROJ_COLS + HEADS_PER_MIXER
    main_steps = PROJ_COLS // CAST_COLS
    return pl.pallas_call(
        functools.partial(_cast_w_in_body, main_steps=main_steps),
        grid=(depth, main_steps + 1),
        in_specs=[pl.BlockSpec((1, d, CAST_COLS), lambda layer, j: (layer, 0, j))],
        out_specs=[
            pl.BlockSpec((1, d, CAST_COLS), lambda layer, j: (layer, 0, jnp.minimum(j, main_steps - 1))),
            pl.BlockSpec((1, d, 2 * LANES), lambda layer, j: (layer, 0, 0)),
        ],
        out_shape=[
            jax.ShapeDtypeStruct((depth, d, PROJ_COLS), BF16),
            jax.ShapeDtypeStruct((depth, d, 2 * LANES), BF16),
        ],
        compiler_params=pltpu.CompilerParams(
            dimension_semantics=("arbitrary", "arbitrary"), vmem_limit_bytes=VMEM_LIMIT),
        name="cast_w_in",
    )(w_in)


def _rope_tables(seq):
    inv = 1.0 / (ROPE_THETA ** (jnp.arange(0, HEAD_DIM, 2, dtype=F32) / HEAD_DIM))
    ang = jnp.arange(seq, dtype=F32)[:, None] * inv[None, :]
    cos, sin = jnp.cos(ang), jnp.sin(ang)
    reps = LANES // HEAD_DIM
    cos_t = jnp.tile(jnp.concatenate([cos, cos], axis=1), (1, reps))
    sin_t = jnp.tile(jnp.concatenate([-sin, sin], axis=1), (1, reps))
    return cos_t, sin_t


def _gain_table(qn, kn):
    ones = jnp.ones((HEAD_DIM,), F32)
    per_mixer = {0: (qn[0], ones, qn[1], qn[2]), 1: (kn[0], ones, kn[1], kn[2])}
    blocks = [per_mixer[section][mixer] if section < 2 else ones for section, mixer in PROJ_BLOCK_ORDER]
    tab = jnp.stack([jnp.tile(g.astype(F32), HEADS_PER_MIXER) for g in blocks])
    return tab[:, None, :]


def kernel(x, norm_gain, w_in, q_norm_gain, k_norm_gain, forget_bias, w_out):
    bsz, seq, d = x.shape
    depth = w_in.shape[0]
    rows = bsz * seq
    tm = min(512, seq)
    tq_seq = min(QUERY_TILE, seq)
    cos_t, sin_t = _rope_tables(seq)
    head_of_lane = jnp.arange(MIXER_WIDTH) // HEAD_DIM
    mavg = jnp.where(head_of_lane[:, None] == head_of_lane[None, :], 1.0 / HEAD_DIM, 0.0).astype(BF16)
    tri = jnp.tril(jnp.ones((256, 256), F32)).astype(BF16)

    def natural_col(section, r):
        return PROJ_BLOCK_POS[(section, 0)]

    def strided_col(section, r):
        return r

    w_main, wf_cat = _cast_w_in(w_in)
    w_out_bf16 = _cast_columns(w_out)

    x2 = x.reshape(rows, d)
    for layer in range(depth):
        fb_pad = jnp.pad(forget_bias[layer].astype(F32), (0, LANES - HEADS_PER_MIXER))[None, :]
        gain_tab = _gain_table(q_norm_gain[layer], k_norm_gain[layer])

        proj, cum, *strided = _inproj(x2, norm_gain[layer][None, :].astype(F32), w_main[layer], wf_cat[layer],
                                      fb_pad, gain_tab, cos_t, sin_t, mavg, tri, seq=seq, tm=tm)
        seg = [_band_segment(proj, proj, proj, natural_col, bsz=bsz, seq=seq, dil=1, tq=512)]
        for di, dil in enumerate(STRIDED_DILATIONS):
            qd, kd, vd = strided[3 * di:3 * di + 3]
            seg.append(_band_segment(qd, kd, vd, strided_col, bsz=bsz, seq=seq, dil=dil, tq=512))
        yb = _stick(proj, bsz=bsz, seq=seq, tq=min(STICK_QUERY_TILE, seq), tk=STICK_KEY_TILE)
        yc = _moba(proj, _score_bound(q_norm_gain[layer, 1], k_norm_gain[layer, 1]),
                   bsz=bsz, seq=seq, tq=tq_seq, tk=KEY_TILE)
        nt = seq // KEY_TILE
        tile_end_sums = cum.reshape(bsz, nt, KEY_TILE, LANES)[:, :, KEY_TILE - 1, :HEADS_PER_MIXER]
        tile_end_sums = tile_end_sums.transpose(0, 2, 1)
        yd = _fox(proj, cum, tile_end_sums, _score_bound(q_norm_gain[layer, 2], k_norm_gain[layer, 2]),
                  bsz=bsz, seq=seq, tq=tq_seq, tk=KEY_TILE)
        x2 = _outproj([s[0] for s in seg], [s[1] for s in seg], yb, yc, yd, proj, x2,
                      w_out_bf16[layer], tm=min(OUT_ROW_TILE, seq), tn=d)
    return x2.reshape(bsz, seq, d)
```

```python
import functools

import jax
import jax.numpy as jnp
from jax import lax
from jax.experimental import pallas as pl
from jax.experimental.pallas import tpu as pltpu

F32 = jnp.float32
BF16 = jnp.bfloat16

HEAD_DIM = 64
HALF_DIM = HEAD_DIM // 2
LANES = 128
MXU_TILE = 256
N_MIXERS = 4
HEADS_PER_MIXER = 8
PAIRS_PER_MIXER = HEADS_PER_MIXER // 2
MIXER_WIDTH = HEADS_PER_MIXER * HEAD_DIM
LANE_BLOCKS_PER_MIXER = MIXER_WIDTH // LANES
MIX_WIDTH = N_MIXERS * MIXER_WIDTH
N_SECTIONS = 4
PROJ_COLS = N_SECTIONS * MIX_WIDTH
COL_BLOCKS_PER_SECTION = MIX_WIDTH // LANES
ROPE_THETA = 10000.0
RMS_EPS = 1e-6
SCALE = HEAD_DIM ** -0.5
LOG2E = 1.4426950408889634
NEG = -1e30
DILATIONS = (1, 4, 16)
STRIDED_DILATIONS = DILATIONS[1:]
BAND = 128
MOBA_BLOCK = 256
MOBA_TOPK = 3
QUERY_TILE = 1024
KEY_TILE = 1024
STICK_QUERY_TILE = 1024
STICK_KEY_TILE = 256
COL_BLOCKS_PER_STEP = 4
PROJ_BLOCK_ORDER = ((0, 0), (1, 0), (0, 2), (1, 2),
                    (0, 3), (1, 3), (0, 1), (1, 1),
                    (2, 0), (2, 1), (2, 2), (2, 3),
                    (3, 0), (3, 1), (3, 2), (3, 3))
PROJ_BLOCK_POS = {block: pos for pos, block in enumerate(PROJ_BLOCK_ORDER)}
GATE_GROUP = PROJ_BLOCK_POS[(3, 0)] // N_MIXERS
CAST_COLS = 1024
OUT_ROW_TILE = 256
UNDERFLOW_LOG2 = -160.0
VMEM_LIMIT = 56 * 1024 * 1024

_NT = (((1,), (1,)), ((), ()))


def _dot(a, b):
    return jnp.dot(a, b, preferred_element_type=F32)


def _dot_nt(a, b):
    return lax.dot_general(a, b, _NT, preferred_element_type=F32)


def _split2(x):
    hi = x.astype(BF16)
    lo = (x - hi.astype(F32)).astype(BF16)
    return hi, lo


def _split3(x):
    b1 = x.astype(BF16)
    r1 = x - b1.astype(F32)
    b2 = r1.astype(BF16)
    r2 = r1 - b2.astype(F32)
    return b1, b2, r2.astype(BF16)


def _head_lane_mask(shape, h):
    lane = lax.broadcasted_iota(jnp.int32, shape, len(shape) - 1)
    return (lane >= h * HEAD_DIM) & (lane < (h + 1) * HEAD_DIM)


def _own_head(x, h):
    return jnp.where(_head_lane_mask(x.shape, h), x, jnp.zeros_like(x))


def _lane_column(x, n):
    lane = lax.broadcasted_iota(jnp.int32, x.shape, 1)
    return jnp.sum(jnp.where(lane == n, x, 0.0), axis=1, keepdims=True)


def _inproj_body(x_ref, g_ref, wf_ref, fb_ref, gain_ref, cos_ref, sin_ref, mavg_ref, tri_ref, *rest,
                 blocks_per_batch, tm):
    w_refs, (proj_ref, cum_ref), rest = rest[:COL_BLOCKS_PER_STEP], rest[COL_BLOCKS_PER_STEP:][:2], \
        rest[COL_BLOCKS_PER_STEP + 2:]
    dil_refs = rest[:3 * len(STRIDED_DILATIONS)]
    h_scr, carry_scr, dil_scr = rest[3 * len(STRIDED_DILATIONS):]
    i = pl.program_id(0)
    j = pl.program_id(1)

    @pl.when(j == 0)
    def _prologue():
        x = x_ref[...]
        ms = jnp.mean(x * x, axis=-1, keepdims=True)
        h = x * lax.rsqrt(ms + RMS_EPS) * g_ref[...]
        h_hi, h_lo = _split2(h)
        h_scr[...] = h_hi
        wf = wf_ref[...]
        t = _dot(h_hi, wf)
        u = _dot(h_lo, wf[:, :LANES])
        logit = t[:, :LANES] + t[:, LANES:] + u + fb_ref[...]
        lf = jnp.minimum(logit, 0.0) - jnp.log1p(jnp.exp(-jnp.abs(logit)))

        @pl.when(i % blocks_per_batch == 0)
        def _():
            carry_scr[...] = jnp.zeros_like(carry_scr)

        carry = carry_scr[...]
        tri = tri_ref[...]
        sub = tri.shape[0]
        for r in range(tm // sub):
            b1, b2, b3 = _split3(lf[r * sub:(r + 1) * sub])
            c = _dot(tri, jnp.concatenate([b1, b2, b3], axis=1))
            c = c[:, :LANES] + c[:, LANES:2 * LANES] + c[:, 2 * LANES:] + carry
            cum_ref[r * sub:(r + 1) * sub, :] = c
            carry = c[sub - 1:sub, :]
        carry_scr[...] = carry

    for group in range(len(PROJ_BLOCK_ORDER) // COL_BLOCKS_PER_STEP):
        @pl.when(j == group)
        def _():
            for sub in range(COL_BLOCKS_PER_STEP):
                section, mixer = PROJ_BLOCK_ORDER[group * COL_BLOCKS_PER_STEP + sub]
                _inproj_column_block(section, mixer, sub * MIXER_WIDTH, h_scr, w_refs[sub], gain_ref.at[sub],
                                     cos_ref, sin_ref, mavg_ref, proj_ref, dil_refs, dil_scr, tm=tm)


def _inproj_column_block(section, mixer, col0, h_scr, w_ref, gain_ref, cos_ref, sin_ref, mavg_ref, proj_ref,
                         dil_refs, dil_scr, *, tm):
    def out_cols(c):
        return slice(col0 + c * LANES, col0 + (c + 1) * LANES)

    acc = _dot(h_scr[...], w_ref[...])
    normed = section < 2 and mixer != 1
    roped = normed and mixer in (0, 2)
    scale = SCALE * LOG2E if section == 0 else 1.0

    if not normed:
        proj_ref[:, col0:col0 + MIXER_WIDTH] = (acc * scale if section == 0 else acc).astype(BF16)
    else:
        sq = (acc * acc).astype(BF16)
        ms = jnp.concatenate([_dot(sq[:, c:c + MXU_TILE], mavg_ref[...])
                              for c in range(0, MIXER_WIDTH, MXU_TILE)], axis=1)
        t = acc * lax.rsqrt(ms + RMS_EPS) * (gain_ref[0] * scale)
        if not roped:
            proj_ref[:, col0:col0 + MIXER_WIDTH] = t.astype(BF16)
        else:
            cosv = cos_ref[...]
            sinv = sin_ref[...]
            lane = lax.broadcasted_iota(jnp.int32, (tm, LANES), 1)
            first_half = (lane % HEAD_DIM) < HALF_DIM
            for c in range(LANE_BLOCKS_PER_MIXER):
                tc = t[:, c * LANES:(c + 1) * LANES]
                partner = jnp.where(first_half,
                                    pltpu.roll(tc, LANES - HALF_DIM, 1),
                                    pltpu.roll(tc, HALF_DIM, 1))
                proj_ref[:, out_cols(c)] = (tc * cosv + partner * sinv).astype(BF16)

    if mixer == 0 and section < 3:
        for c in range(LANE_BLOCKS_PER_MIXER):
            dil_scr[c] = proj_ref[:, out_cols(c)].astype(F32)
        for di, dil in enumerate(STRIDED_DILATIONS):
            ref = dil_refs[3 * di + section]
            n = tm // dil
            for r in range(dil):
                for c in range(LANE_BLOCKS_PER_MIXER):
                    lo = r * MIXER_WIDTH + c * LANES
                    ref[:, lo:lo + LANES] = dil_scr[c, pl.ds(r, n, stride=dil), :].astype(BF16)


def _inproj(x2, g, w_main, wf_cat, fb_pad, gain_tab, cos_t, sin_t, mavg, tri, *, seq, tm):
    rows, d = x2.shape
    blocks_per_batch = seq // tm
    step_cols = COL_BLOCKS_PER_STEP * MIXER_WIDTH
    body = functools.partial(_inproj_body, blocks_per_batch=blocks_per_batch, tm=tm)

    def weight_block(sub):
        origin = [section * N_MIXERS + mixer for section, mixer in PROJ_BLOCK_ORDER[sub::COL_BLOCKS_PER_STEP]]

        def index(i, j):
            blk = origin[0]
            for group in range(1, len(origin)):
                blk = jnp.where(j == group, origin[group], blk)
            return 0, blk
        return index

    dil_specs, dil_shapes = [], []
    for dil in STRIDED_DILATIONS:
        for _ in range(3):
            dil_specs.append(pl.BlockSpec((tm // dil, dil * MIXER_WIDTH), lambda i, j: (i, 0)))
            dil_shapes.append(jax.ShapeDtypeStruct((rows // dil, dil * MIXER_WIDTH), BF16))
    return pl.pallas_call(
        body,
        grid=(rows // tm, PROJ_COLS // step_cols),
        in_specs=[
            pl.BlockSpec((tm, d), lambda i, j: (i, 0)),
            pl.BlockSpec((1, d), lambda i, j: (0, 0)),
            pl.BlockSpec((d, 2 * LANES), lambda i, j: (0, 0)),
            pl.BlockSpec((1, LANES), lambda i, j: (0, 0)),
            pl.BlockSpec((COL_BLOCKS_PER_STEP, 1, MIXER_WIDTH), lambda i, j: (j, 0, 0)),
            pl.BlockSpec((tm, LANES), lambda i, j: (i % blocks_per_batch, 0)),
            pl.BlockSpec((tm, LANES), lambda i, j: (i % blocks_per_batch, 0)),
            pl.BlockSpec((MXU_TILE, MXU_TILE), lambda i, j: (0, 0)),
            pl.BlockSpec(tri.shape, lambda i, j: (0, 0)),
        ] + [pl.BlockSpec((d, MIXER_WIDTH), weight_block(sub)) for sub in range(COL_BLOCKS_PER_STEP)],
        out_specs=[
            pl.BlockSpec((tm, step_cols), lambda i, j: (i, j)),
            pl.BlockSpec((tm, LANES), lambda i, j: (i, 0)),
        ] + dil_specs,
        out_shape=[
            jax.ShapeDtypeStruct((rows, PROJ_COLS), BF16),
            jax.ShapeDtypeStruct((rows, LANES), F32),
        ] + dil_shapes,
        scratch_shapes=[pltpu.VMEM((tm, d), BF16), pltpu.VMEM((1, LANES), F32),
                        pltpu.VMEM((LANE_BLOCKS_PER_MIXER, tm, LANES), F32)],
        compiler_params=pltpu.CompilerParams(
            dimension_semantics=("arbitrary", "arbitrary"), vmem_limit_bytes=VMEM_LIMIT),
        name="inproj",
    )(x2, g, wf_cat, fb_pad, gain_tab, cos_t, sin_t, mavg, tri, *([w_main] * COL_BLOCKS_PER_STEP))


def _band_body(q_ref, k_ref, v_ref, kp_ref, vp_ref, o_ref, lse_ref, *, tq):
    first = pl.program_id(2) == 0
    row = lax.broadcasted_iota(jnp.int32, (BAND, 2 * BAND), 0)
    col = lax.broadcasted_iota(jnp.int32, (BAND, 2 * BAND), 1)
    in_band = jnp.logical_or(jnp.logical_and(col < BAND, col >= row),
                             jnp.logical_and(col >= BAND, col - BAND <= row))
    first_mask = jnp.logical_and(in_band, jnp.logical_or(col >= BAND, jnp.logical_not(first)))
    lane = lax.broadcasted_iota(jnp.int32, (BAND, LANES), 1)
    head0 = lane < HEAD_DIM
    for pair in range(PAIRS_PER_MIXER):
        lanes = slice(pair * LANES, (pair + 1) * LANES)
        for c in range(tq // BAND):
            own = slice(c * BAND, (c + 1) * BAND)
            qc = q_ref[own, lanes]
            if c == 0:
                kw = jnp.concatenate([kp_ref[:, lanes], k_ref[own, lanes]], axis=0)
                vw = jnp.concatenate([vp_ref[:, lanes], v_ref[own, lanes]], axis=0)
                mask = first_mask
            else:
                window = slice((c - 1) * BAND, (c + 1) * BAND)
                kw = k_ref[window, lanes]
                vw = v_ref[window, lanes]
                mask = in_band
            outs, lses = [], []
            for h in range(2):
                s = jnp.where(mask, _dot_nt(_own_head(qc, h), kw), NEG)
                m = jnp.max(s, axis=1, keepdims=True)
                p = jnp.exp2(s - m)
                l = jnp.sum(p, axis=1, keepdims=True)
                outs.append(_dot(p.astype(BF16), vw) / l)
                lses.append(m + jnp.log2(l))
            o_ref[own, lanes] = jnp.where(head0, outs[0], outs[1]).astype(BF16)
            lse_ref[own, lanes] = jnp.where(head0, lses[0], lses[1])


def _band_segment(q_arr, k_arr, v_arr, col_of, *, bsz, seq, dil, tq):
    sub_rows = q_arr.shape[0]
    sub_len = seq // dil
    tq = min(tq, sub_len)
    nq = sub_len // tq
    band_per_tq = tq // BAND

    def cur(section):
        return pl.BlockSpec((tq, MIXER_WIDTH), lambda b, r, i: (b * nq + i, col_of(section, r)))

    def prev(section):
        return pl.BlockSpec(
            (BAND, MIXER_WIDTH),
            lambda b, r, i: (jnp.maximum((b * nq + i) * band_per_tq - 1, 0), col_of(section, r)))

    out_spec = pl.BlockSpec((tq, MIXER_WIDTH), lambda b, r, i: (b * nq + i, r))
    return pl.pallas_call(
        functools.partial(_band_body, tq=tq),
        grid=(bsz, dil, nq),
        in_specs=[cur(0), cur(1), cur(2), prev(1), prev(2)],
        out_specs=[out_spec, out_spec],
        out_shape=[
            jax.ShapeDtypeStruct((sub_rows, dil * MIXER_WIDTH), BF16),
            jax.ShapeDtypeStruct((sub_rows, dil * MIXER_WIDTH), F32),
        ],
        compiler_params=pltpu.CompilerParams(
            dimension_semantics=("arbitrary",) * 3, vmem_limit_bytes=VMEM_LIMIT),
        name=f"band_d{dil}",
    )(q_arr, k_arr, v_arr, k_arr, v_arr)


def _online_init(tq):
    return (jnp.full((tq, 1), NEG, F32), jnp.zeros((tq, 1), F32), jnp.zeros((tq, LANES), F32))


def _online_step(carry, s, v_blk):
    m, l, acc = carry
    m_new = jnp.maximum(m, jnp.max(s, axis=1, keepdims=True))
    alpha = jnp.exp2(m - m_new)
    p = jnp.exp2(s - m_new)
    l = alpha * l + jnp.sum(p, axis=1, keepdims=True)
    acc = alpha * acc + _dot(p.astype(BF16), v_blk)
    return m_new, l, acc


def _online_finish(carry, h):
    _, l, acc = carry
    return acc / l


def _shifted_init(tq):
    return (jnp.zeros((tq, LANES), F32),)


def _shifted_step(carry, s, v_blk):
    acc, = carry
    return (acc + _dot(jnp.exp2(s).astype(BF16), v_blk),)


def _shifted_finish(carry, h):
    acc, = carry
    return acc / _lane_column(acc, _ones_lane(h))


_ONLINE = (_online_init, _online_step, _online_finish, 1)
_SHIFTED = (_shifted_init, _shifted_step, _shifted_finish, 4)
SHIFTED_MAX_SPAN = 100.0


def _ones_lane(h):
    return HEAD_DIM * (1 - h)


def _augment_values(v, h):
    lane = lax.broadcasted_iota(jnp.int32, v.shape, 1)
    ones = jnp.where(lane == _ones_lane(h), 1.0, 0.0)
    return jnp.where(_head_lane_mask(v.shape, h), v.astype(F32), ones).astype(BF16)


def _attend_tile(q_aug, k_tile, v_tile, state, step, heads=(0, 1), first_row=None):
    new = list(state)
    for h in heads:
        start = first_row or 0
        s = _dot_nt(q_aug[h][start:] if start else q_aug[h], k_tile(h))
        if first_row is not None:
            row = lax.broadcasted_iota(jnp.int32, s.shape, 0)
            col = lax.broadcasted_iota(jnp.int32, s.shape, 1)
            s = jnp.where(col <= row, s, NEG)
        if start:
            part = step(tuple(a[start:] for a in state[h]), s, v_tile(h))
            new[h] = tuple(jnp.concatenate([a[:start], b], axis=0) for a, b in zip(state[h], part))
        else:
            new[h] = step(state[h], s, v_tile(h))
    return tuple(new)


def _attend_diagonal(tile_part, tq, init, parts):
    state = (init(tq), init(tq))
    for d in range(parts):
        state = tile_part(state, d * (tq // parts), tq // parts)
    return state


def _score_bound(q_gain, k_gain):
    bound = 1.02 * HEAD_DIM * SCALE * LOG2E * jnp.max(jnp.abs(q_gain)) * jnp.max(jnp.abs(k_gain))
    return bound.astype(BF16).astype(F32).reshape(1)


def _merge_heads(outs):
    lane = lax.broadcasted_iota(jnp.int32, outs[0].shape, 1)
    return jnp.where(lane < HEAD_DIM, outs[0], outs[1])


def _seq_mixer_specs(*, mixer, nq, tq, seq):
    def colblk(section, p):
        return PROJ_BLOCK_POS[(section, mixer)] * PAIRS_PER_MIXER + p

    q_spec = pl.BlockSpec((tq, LANES), lambda b, p, i: (b * nq + i, colblk(0, p)))
    k_spec = pl.BlockSpec((seq, LANES), lambda b, p, i: (b, colblk(1, p)))
    v_spec = pl.BlockSpec((seq, LANES), lambda b, p, i: (b, colblk(2, p)))
    o_spec = pl.BlockSpec((tq, LANES), lambda b, p, i: (b * nq + i, p))
    return q_spec, k_spec, v_spec, o_spec


_SEQ_PARAMS = pltpu.CompilerParams(
    dimension_semantics=("arbitrary",) * 3, vmem_limit_bytes=VMEM_LIMIT)


def _fox_augment(x, cum, head, h, key_side, bound):
    lane = lax.broadcasted_iota(jnp.int32, x.shape, 1)
    g = jnp.broadcast_to(_lane_column(cum, head) * LOG2E, x.shape)
    g1, g2, g3 = (piece.astype(F32) for piece in _split3(g))
    one = jnp.ones_like(g1)
    pieces = (one, one, one, -g1, -g2, -g3, one) if key_side else (g1, g2, g3, one, one, one, -bound * one)
    base = HEAD_DIM * (1 - h)
    aug = jnp.zeros_like(g1)
    for n, piece in enumerate(pieces):
        aug = jnp.where(lane == base + n, piece, aug)
    return jnp.where(_head_lane_mask(x.shape, h), x.astype(F32), aug).astype(BF16)


def _fox_body(fend_ref, bound_ref, q_ref, k_ref, v_ref, cumq_ref, cumk_ref, o_ref, kaug_scr, vaug_scr, *,
              tq, tk, seq):
    b = pl.program_id(0)
    p = pl.program_id(1)
    iq = pl.program_id(2)
    bound = bound_ref[0]

    @pl.when(iq == 0)
    def _augment_keys():
        def chunk(c, _):
            rows = pl.ds(pl.multiple_of(c * tk, tk), tk)
            for h in range(2):
                kaug_scr[h, rows, :] = _fox_augment(k_ref[rows, :], cumk_ref[rows, :], 2 * p + h, h, True, bound)
                vaug_scr[h, rows, :] = _augment_values(v_ref[rows, :], h)
            return 0
        lax.fori_loop(0, seq // tk, chunk, 0)

    q = q_ref[...]
    cumq = cumq_ref[...]
    q_aug = [_fox_augment(q, cumq, 2 * p + h, h, False, bound) for h in range(2)]

    first_past = iq - 1
    last = jnp.maximum(first_past, 0)

    def reaches(j, heads):
        j = jnp.maximum(j, 0)
        alive = [LOG2E * (fend_ref[b, 2 * p + h, last] - fend_ref[b, 2 * p + h, j])
                 + 2.0 * bound > UNDERFLOW_LOG2 for h in heads]
        return functools.reduce(jnp.logical_and, alive)

    def attend(scheme):
        init, step, finish, diagonal_parts = scheme

        def tile(j, state, heads=(0, 1)):
            rows = pl.ds(pl.multiple_of(j * tk, tk), tk)
            return _attend_tile(q_aug, lambda h: kaug_scr[h, rows, :], lambda h: vaug_scr[h, rows, :], state,
                                step, heads)

        def diagonal_part(state, key0, nkeys):
            rows = pl.ds(pl.multiple_of(iq * tk + key0, nkeys), nkeys)
            return _attend_tile(q_aug, lambda h: kaug_scr[h, rows, :], lambda h: vaug_scr[h, rows, :], state,
                                step, first_row=key0)

        loop = (first_past, _attend_diagonal(diagonal_part, tq, init, diagonal_parts))
        for heads in ((0, 1), (0,), (1,)):
            loop = lax.while_loop(
                lambda lp: jnp.logical_and(lp[0] >= 0, reaches(lp[0], heads)),
                lambda lp: (lp[0] - 1, tile(lp[0], lp[1], heads)),
                loop)
        o_ref[...] = _merge_heads([finish(st, h) for h, st in enumerate(loop[1])]).astype(BF16)

    shifted_ok = 2.0 * bound <= SHIFTED_MAX_SPAN
    pl.when(shifted_ok)(functools.partial(attend, _SHIFTED))
    pl.when(jnp.logical_not(shifted_ok))(functools.partial(attend, _ONLINE))


def _fox(proj, cum, tile_end_sums, score_bound, *, bsz, seq, tq, tk):
    rows = proj.shape[0]
    nq = seq // tq
    q_spec, k_spec, v_spec, o_spec = _seq_mixer_specs(mixer=3, nq=nq, tq=tq, seq=seq)
    smem = pl.BlockSpec(memory_space=pltpu.SMEM)
    return pl.pallas_call(
        functools.partial(_fox_body, tq=tq, tk=tk, seq=seq),
        grid=(bsz, PAIRS_PER_MIXER, nq),
        in_specs=[
            smem, smem, q_spec, k_spec, v_spec,
            pl.BlockSpec((tq, LANES), lambda b, p, i: (b * nq + i, 0)),
            pl.BlockSpec((seq, LANES), lambda b, p, i: (b, 0)),
        ],
        out_specs=o_spec,
        out_shape=jax.ShapeDtypeStruct((rows, MIXER_WIDTH), BF16),
        scratch_shapes=[pltpu.VMEM((2, seq, LANES), BF16), pltpu.VMEM((2, seq, LANES), BF16)],
        compiler_params=_SEQ_PARAMS,
        name="fox",
    )(tile_end_sums, score_bound, proj, proj, proj, cum, cum)


def _moba_body(bound_ref, q_ref, k_ref, v_ref, o_ref, kmean_scr, kaug_scr, vaug_scr, *, tq, tk, seq):
    iq = pl.program_id(2)
    nblk = seq // MOBA_BLOCK
    blocks_per_tile = tq // MOBA_BLOCK
    bound = bound_ref[0]

    @pl.when(iq == 0)
    def _prepare_keys():
        r = lax.broadcasted_iota(jnp.int32, (LANES, seq), 0)
        c = lax.broadcasted_iota(jnp.int32, (LANES, seq), 1)
        member = jnp.where(c // MOBA_BLOCK == r % HEAD_DIM, 1.0 / MOBA_BLOCK, 0.0).astype(BF16)
        hi, lo = _split2(_dot(member, k_ref[...]))
        kmean_scr[...] = jnp.concatenate([hi, lo], axis=1)

        def chunk(c, _):
            rows = pl.ds(pl.multiple_of(c * tk, tk), tk)
            k = k_ref[rows, :].astype(F32)
            lane = lax.broadcasted_iota(jnp.int32, (tk, LANES), 1)
            blk = (c * tk + lax.broadcasted_iota(jnp.int32, (tk, LANES), 0)) // MOBA_BLOCK
            for h in range(2):
                spare = lane - HEAD_DIM * (1 - h)
                onehot = jnp.where(jnp.logical_or(spare == blk, spare == HEAD_DIM - 1), 1.0, 0.0)
                kaug_scr[h, rows, :] = jnp.where(_head_lane_mask(k.shape, h), k, onehot).astype(BF16)
                vaug_scr[h, rows, :] = _augment_values(v_ref[rows, :], h)
            return 0
        lax.fori_loop(0, seq // tk, chunk, 0)

    q = q_ref[...]
    nslot = -(-nblk // 8) * 8
    blk = lax.broadcasted_iota(jnp.int32, (nslot, tq), 0)
    qblk = blocks_per_tile * iq + lax.broadcasted_iota(jnp.int32, (nslot, tq), 1) // MOBA_BLOCK
    q_aug = []
    tail_row = lax.broadcasted_iota(jnp.int32, (HEAD_DIM - nslot, tq), 0)
    tail = jnp.where(tail_row == HEAD_DIM - nslot - 1, -bound, 0.0)
    for h in range(2):
        qh = _own_head(q, h)
        spare = slice(HEAD_DIM * (1 - h), HEAD_DIM * (1 - h) + nslot)
        gate = _dot_nt(kmean_scr[spare, :], jnp.concatenate([qh, qh], axis=1))
        past = blk < qblk
        work = jnp.where(past, gate, NEG)
        bias = jnp.where(blk == qblk, 0.0, NEG)
        for _ in range(min(MOBA_TOPK, nblk)):
            best = jnp.max(work, axis=0, keepdims=True)
            idx = jnp.min(jnp.where(work == best, blk, nslot), axis=0, keepdims=True)
            hit = blk == idx
            bias = jnp.where(hit, jnp.where(past, 0.0, bias), bias)
            work = jnp.where(hit, -jnp.inf, work)
        own_half = jnp.zeros((HEAD_DIM, tq), F32)
        halves = [bias, tail, own_half] if h == 1 else [own_half, bias, tail]
        q_aug.append(jnp.where(_head_lane_mask(q.shape, h), q.astype(F32),
                               jnp.concatenate(halves, axis=0).T).astype(BF16))

    def attend(scheme):
        init, step, finish, diagonal_parts = scheme

        def tile(j, state):
            rows = pl.ds(pl.multiple_of(j * tk, tk), tk)
            return _attend_tile(q_aug, lambda h: kaug_scr[h, rows, :], lambda h: vaug_scr[h, rows, :], state, step)

        def diagonal_part(state, key0, nkeys):
            rows = pl.ds(pl.multiple_of(iq * tk + key0, nkeys), nkeys)
            return _attend_tile(q_aug, lambda h: kaug_scr[h, rows, :], lambda h: vaug_scr[h, rows, :], state,
                                step, first_row=key0)

        state = _attend_diagonal(diagonal_part, tq, init, diagonal_parts)
        state = lax.fori_loop(0, iq, tile, state)
        o_ref[...] = _merge_heads([finish(st, h) for h, st in enumerate(state)]).astype(BF16)

    shifted_ok = 2.0 * bound <= SHIFTED_MAX_SPAN
    pl.when(shifted_ok)(functools.partial(attend, _SHIFTED))
    pl.when(jnp.logical_not(shifted_ok))(functools.partial(attend, _ONLINE))


def _moba(proj, score_bound, *, bsz, seq, tq, tk):
    assert tq == tk and tq % MOBA_BLOCK == 0
    assert -(-(seq // MOBA_BLOCK) // 8) * 8 < HEAD_DIM
    rows = proj.shape[0]
    nq = seq // tq
    q_spec, k_spec, v_spec, o_spec = _seq_mixer_specs(mixer=2, nq=nq, tq=tq, seq=seq)
    return pl.pallas_call(
        functools.partial(_moba_body, tq=tq, tk=tk, seq=seq),
        grid=(bsz, PAIRS_PER_MIXER, nq),
        in_specs=[pl.BlockSpec(memory_space=pltpu.SMEM), q_spec, k_spec, v_spec],
        out_specs=o_spec,
        out_shape=jax.ShapeDtypeStruct((rows, MIXER_WIDTH), BF16),
        scratch_shapes=[pltpu.VMEM((LANES, 2 * LANES), BF16), pltpu.VMEM((2, seq, LANES), BF16),
                        pltpu.VMEM((2, seq, LANES), BF16)],
        compiler_params=_SEQ_PARAMS,
        name="moba",
    )(score_bound, proj, proj, proj)


def _stick_body(q_ref, k_ref, v_ref, o_ref, *, tq, tk):
    iq = pl.program_id(2)
    groups = [slice(g * tk, (g + 1) * tk) for g in range(tq // tk)]
    q = q_ref[...]
    q_own = [_own_head(q, h) for h in range(2)]
    later_row = lax.broadcasted_iota(jnp.int32, (tk, tk), 0)
    later_col = lax.broadcasted_iota(jnp.int32, (tk, tk), 1)
    later = jnp.where(later_row > later_col, 1.0, 0.0).astype(BF16)

    def attend(z, weigh_values, carry_sum, strictly_past=None):
        softplus = jnp.maximum(z, 0.0) + jnp.log2(1.0 + jnp.exp2(-jnp.abs(z)))
        log_keep = -softplus
        if strictly_past is not None:
            log_keep = jnp.where(strictly_past, log_keep, 0.0)
        afters = []
        newer = carry_sum
        for c in reversed(range(z.shape[1] // tk)):
            blk = log_keep[:, c * tk:(c + 1) * tk]
            hi, lo = _split2(blk)
            afters.append(_dot(hi, later) + _dot(lo, later) + newer)
            newer = newer + jnp.sum(blk, axis=1, keepdims=True)
        log_a = z - softplus + jnp.concatenate(afters[::-1], axis=1)
        if strictly_past is not None:
            log_a = jnp.where(strictly_past, log_a, NEG)
        return newer, weigh_values(jnp.exp2(log_a).astype(BF16))

    first_blk, k_win, v_win, masks = [], [], [], []
    row = lax.broadcasted_iota(jnp.int32, (tk, 2 * tk), 0)
    col = lax.broadcasted_iota(jnp.int32, (tk, 2 * tk), 1)
    for g in range(len(groups)):
        own_blk = iq * len(groups) + g
        first_blk.append(jnp.maximum(own_blk - 1, 0))
        keys = pl.ds(pl.multiple_of(first_blk[g] * tk, tk), 2 * tk)
        k_win.append(k_ref[keys, :])
        v_win.append(v_ref[keys, :])
        masks.append(col + (first_blk[g] - own_blk) * tk < row)
    strictly_past = jnp.concatenate(masks, axis=0)
    state = []
    for h in range(2):
        z = jnp.concatenate([_dot_nt(q_own[h][rows], k_win[g]) for g, rows in enumerate(groups)], axis=0)
        state.append(attend(
            z, lambda a: jnp.concatenate([_dot(a[rows], v_win[g]) for g, rows in enumerate(groups)], axis=0),
            jnp.zeros((tq, 1), F32), strictly_past))

    outs = [[None] * len(groups) for _ in range(2)]
    for g, rows in enumerate(groups):
        def cond(loop):
            j, tail = loop
            alive = jnp.max(jnp.maximum(tail[0][0], tail[1][0])) > UNDERFLOW_LOG2
            return jnp.logical_and(j >= 0, alive)

        def body(loop):
            j, tail = loop
            keys = pl.ds(pl.multiple_of(j * tk, tk), tk)
            k_blk = k_ref[keys, :]
            v_blk = v_ref[keys, :]
            new = []
            for h in range(2):
                carry_sum, acc = tail[h]
                carry_sum, add = attend(_dot_nt(q_own[h][rows], k_blk), lambda a: _dot(a, v_blk), carry_sum)
                new.append((carry_sum, acc + add))
            return j - 1, tuple(new)

        _, tail = lax.while_loop(cond, body, (first_blk[g] - 1, tuple((s[rows], a[rows]) for s, a in state)))
        for h in range(2):
            outs[h][g] = tail[h][1]
    o_ref[...] = _merge_heads([jnp.concatenate(outs[h], axis=0) for h in range(2)]).astype(BF16)


def _stick(proj, *, bsz, seq, tq, tk):
    rows = proj.shape[0]
    nq = seq // tq
    q_spec, k_spec, v_spec, o_spec = _seq_mixer_specs(mixer=1, nq=nq, tq=tq, seq=seq)
    return pl.pallas_call(
        functools.partial(_stick_body, tq=tq, tk=tk),
        grid=(bsz, PAIRS_PER_MIXER, nq),
        in_specs=[q_spec, k_spec, v_spec],
        out_specs=o_spec,
        out_shape=jax.ShapeDtypeStruct((rows, MIXER_WIDTH), BF16),
        compiler_params=_SEQ_PARAMS,
        name="stick",
    )(proj, proj, proj)


def _outproj_body(o1_ref, o4_ref, o16_ref, l1_ref, l4_ref, l16_ref, yb_ref, yc_ref, yd_ref,
                  gate_ref, x_ref, w_ref, out_ref, o_scr, l_scr, *, tm):
    def projected(y, mixer):
        cols = slice(mixer * MIXER_WIDTH, (mixer + 1) * MIXER_WIDTH)
        g = gate_ref[:, cols].astype(F32)
        return _dot((y * (g / (1.0 + jnp.exp(-g)))).astype(BF16), w_ref[cols, :])

    acc = x_ref[...]
    for mixer, y_ref in ((1, yb_ref), (2, yc_ref), (3, yd_ref)):
        acc = acc + projected(y_ref[...].astype(F32), mixer)

    for slot, (dil, o_ref, l_ref) in enumerate(zip(STRIDED_DILATIONS, (o4_ref, o16_ref), (l4_ref, l16_ref))):
        n = tm // dil
        for r in range(dil):
            for c in range(LANE_BLOCKS_PER_MIXER):
                cols = slice(r * MIXER_WIDTH + c * LANES, r * MIXER_WIDTH + (c + 1) * LANES)
                o_scr[slot, c, pl.ds(r, n, stride=dil), :] = o_ref[:, cols].astype(F32)
                l_scr[slot, c, pl.ds(r, n, stride=dil), :] = l_ref[:, cols]

    def natural(scr, slot):
        return jnp.concatenate([scr[slot, c] for c in range(LANE_BLOCKS_PER_MIXER)], axis=1)

    l1, l4, l16 = l1_ref[...], natural(l_scr, 0), natural(l_scr, 1)
    m = jnp.maximum(jnp.maximum(l1, l4), l16)
    e1, e4, e16 = jnp.exp2(l1 - m), jnp.exp2(l4 - m), jnp.exp2(l16 - m)
    ya = (e1 * o1_ref[...].astype(F32) + e4 * natural(o_scr, 0)
          + e16 * natural(o_scr, 1)) / (e1 + e4 + e16)
    out_ref[...] = acc + projected(ya, 0)


def _outproj(seg_o, seg_lse, yb, yc, yd, proj, x2, w_out, *, tm):
    rows, d = x2.shape
    tn = d
    row_blk = lambda i, j: (i, 0)
    mix_spec = pl.BlockSpec((tm, MIXER_WIDTH), row_blk)
    seg_specs = [pl.BlockSpec((tm // dil, dil * MIXER_WIDTH), row_blk) for dil in DILATIONS]
    return pl.pallas_call(
        functools.partial(_outproj_body, tm=tm),
        grid=(rows // tm, d // tn),
        in_specs=seg_specs + seg_specs + [mix_spec] * 3 + [
            pl.BlockSpec((tm, MIX_WIDTH), lambda i, j: (i, GATE_GROUP)),
            pl.BlockSpec((tm, tn), lambda i, j: (i, j)),
            pl.BlockSpec((MIX_WIDTH, tn), lambda i, j: (0, j)),
        ],
        out_specs=pl.BlockSpec((tm, tn), lambda i, j: (i, j)),
        out_shape=jax.ShapeDtypeStruct((rows, d), F32),
        scratch_shapes=[pltpu.VMEM((len(STRIDED_DILATIONS), LANE_BLOCKS_PER_MIXER, tm, LANES), F32),
                        pltpu.VMEM((len(STRIDED_DILATIONS), LANE_BLOCKS_PER_MIXER, tm, LANES), F32)],
        compiler_params=pltpu.CompilerParams(
            dimension_semantics=("arbitrary", "arbitrary"), vmem_limit_bytes=VMEM_LIMIT),
        name="outproj",
    )(*seg_o, *seg_lse, yb, yc, yd, proj, x2, w_out)


def _cast_body(w_ref, o_ref):
    o_ref[...] = w_ref[...].astype(BF16)


def _cast_columns(w):
    depth, d, n = w.shape
    step = min(CAST_COLS, n)
    spec = pl.BlockSpec((1, d, step), lambda layer, j: (layer, 0, j))
    return pl.pallas_call(
        _cast_body,
        grid=(depth, n // step),
        in_specs=[spec],
        out_specs=spec,
        out_shape=jax.ShapeDtypeStruct((depth, d, n), BF16),
        compiler_params=pltpu.CompilerParams(
            dimension_semantics=("arbitrary", "arbitrary"), vmem_limit_bytes=VMEM_LIMIT),
        name="cast_w_out",
    )(w)


def _cast_w_in_body(w_ref, main_ref, gate_ref, *, main_steps):
    j = pl.program_id(1)

    @pl.when(j < main_steps)
    def _():
        main_ref[...] = w_ref[...].astype(BF16)

    @pl.when(j == main_steps)
    def _():
        w = w_ref[0, :, :LANES]
        lane = lax.broadcasted_iota(jnp.int32, w.shape, 1)
        hi, lo = _split2(jnp.where(lane < HEADS_PER_MIXER, w, 0.0))
        gate_ref[0] = jnp.concatenate([hi, lo], axis=1)


def _cast_w_in(w_in):
    depth, d, n = w_in.shape
    assert n == PROJ_COLS + HEADS_PER_MIXER
    main_steps = PROJ_COLS // CAST_COLS
    return pl.pallas_call(
        functools.partial(_cast_w_in_body, main_steps=main_steps),
        grid=(depth, main_steps + 1),
        in_specs=[pl.BlockSpec((1, d, CAST_COLS), lambda layer, j: (layer, 0, j))],
        out_specs=[
            pl.BlockSpec((1, d, CAST_COLS), lambda layer, j: (layer, 0, jnp.minimum(j, main_steps - 1))),
            pl.BlockSpec((1, d, 2 * LANES), lambda layer, j: (layer, 0, 0)),
        ],
        out_shape=[
            jax.ShapeDtypeStruct((depth, d, PROJ_COLS), BF16),
            jax.ShapeDtypeStruct((depth, d, 2 * LANES), BF16),
        ],
        compiler_params=pltpu.CompilerParams(
            dimension_semantics=("arbitrary", "arbitrary"), vmem_limit_bytes=VMEM_LIMIT),
        name="cast_w_in",
    )(w_in)


def _rope_tables(seq):
    inv = 1.0 / (ROPE_THETA ** (jnp.arange(0, HEAD_DIM, 2, dtype=F32) / HEAD_DIM))
    ang = jnp.arange(seq, dtype=F32)[:, None] * inv[None, :]
    cos, sin = jnp.cos(ang), jnp.sin(ang)
    reps = LANES // HEAD_DIM
    cos_t = jnp.tile(jnp.concatenate([cos, cos], axis=1), (1, reps))
    sin_t = jnp.tile(jnp.concatenate([-sin, sin], axis=1), (1, reps))
    return cos_t, sin_t


def _gain_table(qn, kn):
    ones = jnp.ones((HEAD_DIM,), F32)
    per_mixer = {0: (qn[0], ones, qn[1], qn[2]), 1: (kn[0], ones, kn[1], kn[2])}
    blocks = [per_mixer[section][mixer] if section < 2 else ones for section, mixer in PROJ_BLOCK_ORDER]
    tab = jnp.stack([jnp.tile(g.astype(F32), HEADS_PER_MIXER) for g in blocks])
    return tab[:, None, :]


def kernel(x, norm_gain, w_in, q_norm_gain, k_norm_gain, forget_bias, w_out):
    bsz, seq, d = x.shape
    depth = w_in.shape[0]
    rows = bsz * seq
    tm = min(512, seq)
    tq_seq = min(QUERY_TILE, seq)
    cos_t, sin_t = _rope_tables(seq)
    head_of_lane = jnp.arange(MXU_TILE) // HEAD_DIM
    mavg = jnp.where(head_of_lane[:, None] == head_of_lane[None, :], 1.0 / HEAD_DIM, 0.0).astype(BF16)
    tri = jnp.tril(jnp.ones((MXU_TILE, MXU_TILE), F32)).astype(BF16)

    def natural_col(section, r):
        return PROJ_BLOCK_POS[(section, 0)]

    def strided_col(section, r):
        return r

    w_main, wf_cat = _cast_w_in(w_in)
    w_out_bf16 = _cast_columns(w_out)

    x2 = x.reshape(rows, d)
    for layer in range(depth):
        fb_pad = jnp.pad(forget_bias[layer].astype(F32), (0, LANES - HEADS_PER_MIXER))[None, :]
        gain_tab = _gain_table(q_norm_gain[layer], k_norm_gain[layer])

        proj, cum, *strided = _inproj(x2, norm_gain[layer][None, :].astype(F32), w_main[layer], wf_cat[layer],
                                      fb_pad, gain_tab, cos_t, sin_t, mavg, tri, seq=seq, tm=tm)
        seg = [_band_segment(proj, proj, proj, natural_col, bsz=bsz, seq=seq, dil=1, tq=512)]
        for di, dil in enumerate(STRIDED_DILATIONS):
            qd, kd, vd = strided[3 * di:3 * di + 3]
            seg.append(_band_segment(qd, kd, vd, strided_col, bsz=bsz, seq=seq, dil=dil, tq=512))
        yb = _stick(proj, bsz=bsz, seq=seq, tq=min(STICK_QUERY_TILE, seq), tk=STICK_KEY_TILE)
        yc = _moba(proj, _score_bound(q_norm_gain[layer, 1], k_norm_gain[layer, 1]),
                   bsz=bsz, seq=seq, tq=tq_seq, tk=KEY_TILE)
        nt = seq // KEY_TILE
        tile_end_sums = cum.reshape(bsz, nt, KEY_TILE, LANES)[:, :, KEY_TILE - 1, :HEADS_PER_MIXER]
        tile_end_sums = tile_end_sums.transpose(0, 2, 1)
        yd = _fox(proj, cum, tile_end_sums, _score_bound(q_norm_gain[layer, 2], k_norm_gain[layer, 2]),
                  bsz=bsz, seq=seq, tq=tq_seq, tk=KEY_TILE)
        x2 = _outproj([s[0] for s in seg], [s[1] for s in seg], yb, yc, yd, proj, x2,
                      w_out_bf16[layer], tm=min(OUT_ROW_TILE, seq))
    return x2.reshape(bsz, seq, d)
```

```python
import functools

import jax
import jax.numpy as jnp
from jax import lax
from jax.experimental import pallas as pl
from jax.experimental.pallas import tpu as pltpu

F32 = jnp.float32
BF16 = jnp.bfloat16

HEAD_DIM = 64
HALF_DIM = HEAD_DIM // 2
LANES = 128
MXU_TILE = 256
N_MIXERS = 4
HEADS_PER_MIXER = 8
PAIRS_PER_MIXER = HEADS_PER_MIXER // 2
MIXER_WIDTH = HEADS_PER_MIXER * HEAD_DIM
LANE_BLOCKS_PER_MIXER = MIXER_WIDTH // LANES
MIX_WIDTH = N_MIXERS * MIXER_WIDTH
N_SECTIONS = 4
PROJ_COLS = N_SECTIONS * MIX_WIDTH
ROPE_THETA = 10000.0
RMS_EPS = 1e-6
SCALE = HEAD_DIM ** -0.5
LOG2E = 1.4426950408889634
NEG = -1e30
DILATIONS = (1, 4, 16)
STRIDED_DILATIONS = DILATIONS[1:]
BAND = 128
MOBA_BLOCK = 256
MOBA_TOPK = 3
QUERY_TILE = 1024
KEY_TILE = 1024
STICK_QUERY_TILE = 1024
STICK_KEY_TILE = 256
COL_BLOCKS_PER_STEP = 4
PROJ_BLOCK_ORDER = ((0, 0), (1, 0), (0, 2), (1, 2),
                    (0, 3), (1, 3), (0, 1), (1, 1),
                    (2, 0), (2, 1), (2, 2), (2, 3),
                    (3, 0), (3, 1), (3, 2), (3, 3))
PROJ_BLOCK_POS = {block: pos for pos, block in enumerate(PROJ_BLOCK_ORDER)}
GATE_GROUP = PROJ_BLOCK_POS[(3, 0)] // N_MIXERS
CAST_COLS = 1024
OUT_ROW_TILE = 256
UNDERFLOW_LOG2 = -160.0
VMEM_LIMIT = 56 * 1024 * 1024

_NT = (((1,), (1,)), ((), ()))


def _dot(a, b):
    return jnp.dot(a, b, preferred_element_type=F32)


def _dot_nt(a, b):
    return lax.dot_general(a, b, _NT, preferred_element_type=F32)


def _split2(x):
    hi = x.astype(BF16)
    lo = (x - hi.astype(F32)).astype(BF16)
    return hi, lo


def _split3(x):
    b1 = x.astype(BF16)
    r1 = x - b1.astype(F32)
    b2 = r1.astype(BF16)
    r2 = r1 - b2.astype(F32)
    return b1, b2, r2.astype(BF16)


def _head_lane_mask(shape, h):
    lane = lax.broadcasted_iota(jnp.int32, shape, len(shape) - 1)
    return (lane >= h * HEAD_DIM) & (lane < (h + 1) * HEAD_DIM)


def _own_head(x, h):
    return jnp.where(_head_lane_mask(x.shape, h), x, jnp.zeros_like(x))


def _lane_column(x, n):
    lane = lax.broadcasted_iota(jnp.int32, x.shape, 1)
    return jnp.sum(jnp.where(lane == n, x, 0.0), axis=1, keepdims=True)


def _inproj_body(x_ref, g_ref, wf_ref, fb_ref, gain_ref, cos_ref, sin_ref, mavg_ref, tri_ref, *rest,
                 blocks_per_batch, tm):
    w_refs, (proj_ref, cum_ref), rest = rest[:COL_BLOCKS_PER_STEP], rest[COL_BLOCKS_PER_STEP:][:2], \
        rest[COL_BLOCKS_PER_STEP + 2:]
    dil_refs = rest[:3 * len(STRIDED_DILATIONS)]
    h_scr, carry_scr, dil_scr = rest[3 * len(STRIDED_DILATIONS):]
    i = pl.program_id(0)
    j = pl.program_id(1)

    @pl.when(j == 0)
    def _prologue():
        x = x_ref[...]
        ms = jnp.mean(x * x, axis=-1, keepdims=True)
        h = x * lax.rsqrt(ms + RMS_EPS) * g_ref[...]
        h_hi, h_lo = _split2(h)
        h_scr[...] = h_hi
        wf = wf_ref[...]
        t = _dot(h_hi, wf)
        u = _dot(h_lo, wf[:, :LANES])
        logit = t[:, :LANES] + t[:, LANES:] + u + fb_ref[...]
        lf = jnp.minimum(logit, 0.0) - jnp.log1p(jnp.exp(-jnp.abs(logit)))

        @pl.when(i % blocks_per_batch == 0)
        def _():
            carry_scr[...] = jnp.zeros_like(carry_scr)

        carry = carry_scr[...]
        tri = tri_ref[...]
        sub = tri.shape[0]
        for r in range(tm // sub):
            b1, b2, b3 = _split3(lf[r * sub:(r + 1) * sub])
            c = _dot(tri, jnp.concatenate([b1, b2, b3], axis=1))
            c = c[:, :LANES] + c[:, LANES:2 * LANES] + c[:, 2 * LANES:] + carry
            cum_ref[r * sub:(r + 1) * sub, :] = c
            carry = c[sub - 1:sub, :]
        carry_scr[...] = carry

    for group in range(len(PROJ_BLOCK_ORDER) // COL_BLOCKS_PER_STEP):
        @pl.when(j == group)
        def _():
            for sub in range(COL_BLOCKS_PER_STEP):
                section, mixer = PROJ_BLOCK_ORDER[group * COL_BLOCKS_PER_STEP + sub]
                _inproj_column_block(section, mixer, sub * MIXER_WIDTH, h_scr, w_refs[sub], gain_ref.at[sub],
                                     cos_ref, sin_ref, mavg_ref, proj_ref, dil_refs, dil_scr, tm=tm)


def _inproj_column_block(section, mixer, col0, h_scr, w_ref, gain_ref, cos_ref, sin_ref, mavg_ref, proj_ref,
                         dil_refs, dil_scr, *, tm):
    def out_cols(c):
        return slice(col0 + c * LANES, col0 + (c + 1) * LANES)

    acc = _dot(h_scr[...], w_ref[...])
    normed = section < 2 and mixer != 1
    roped = normed and mixer in (0, 2)
    scale = SCALE * LOG2E if section == 0 else 1.0

    if not normed:
        proj_ref[:, col0:col0 + MIXER_WIDTH] = (acc * scale if section == 0 else acc).astype(BF16)
    else:
        sq = (acc * acc).astype(BF16)
        ms = jnp.concatenate([_dot(sq[:, c:c + MXU_TILE], mavg_ref[...])
                              for c in range(0, MIXER_WIDTH, MXU_TILE)], axis=1)
        t = acc * lax.rsqrt(ms + RMS_EPS) * (gain_ref[0] * scale)
        if not roped:
            proj_ref[:, col0:col0 + MIXER_WIDTH] = t.astype(BF16)
        else:
            cosv = cos_ref[...]
            sinv = sin_ref[...]
            lane = lax.broadcasted_iota(jnp.int32, (tm, LANES), 1)
            first_half = (lane % HEAD_DIM) < HALF_DIM
            for c in range(LANE_BLOCKS_PER_MIXER):
                tc = t[:, c * LANES:(c + 1) * LANES]
                partner = jnp.where(first_half,
                                    pltpu.roll(tc, LANES - HALF_DIM, 1),
                                    pltpu.roll(tc, HALF_DIM, 1))
                proj_ref[:, out_cols(c)] = (tc * cosv + partner * sinv).astype(BF16)

    if mixer == 0 and section < 3:
        for c in range(LANE_BLOCKS_PER_MIXER):
            dil_scr[c] = proj_ref[:, out_cols(c)].astype(F32)
        for di, dil in enumerate(STRIDED_DILATIONS):
            ref = dil_refs[3 * di + section]
            n = tm // dil
            for r in range(dil):
                for c in range(LANE_BLOCKS_PER_MIXER):
                    lo = r * MIXER_WIDTH + c * LANES
                    ref[:, lo:lo + LANES] = dil_scr[c, pl.ds(r, n, stride=dil), :].astype(BF16)


def _inproj(x2, g, w_main, wf_cat, fb_pad, gain_tab, cos_t, sin_t, mavg, tri, *, seq, tm):
    rows, d = x2.shape
    blocks_per_batch = seq // tm
    step_cols = COL_BLOCKS_PER_STEP * MIXER_WIDTH
    body = functools.partial(_inproj_body, blocks_per_batch=blocks_per_batch, tm=tm)

    def weight_block(sub):
        origin = [section * N_MIXERS + mixer for section, mixer in PROJ_BLOCK_ORDER[sub::COL_BLOCKS_PER_STEP]]

        def index(i, j):
            blk = origin[0]
            for group in range(1, len(origin)):
                blk = jnp.where(j == group, origin[group], blk)
            return 0, blk
        return index

    dil_specs, dil_shapes = [], []
    for dil in STRIDED_DILATIONS:
        for _ in range(3):
            dil_specs.append(pl.BlockSpec((tm // dil, dil * MIXER_WIDTH), lambda i, j: (i, 0)))
            dil_shapes.append(jax.ShapeDtypeStruct((rows // dil, dil * MIXER_WIDTH), BF16))
    return pl.pallas_call(
        body,
        grid=(rows // tm, PROJ_COLS // step_cols),
        in_specs=[
            pl.BlockSpec((tm, d), lambda i, j: (i, 0)),
            pl.BlockSpec((1, d), lambda i, j: (0, 0)),
            pl.BlockSpec((d, 2 * LANES), lambda i, j: (0, 0)),
            pl.BlockSpec((1, LANES), lambda i, j: (0, 0)),
            pl.BlockSpec((COL_BLOCKS_PER_STEP, 1, MIXER_WIDTH), lambda i, j: (j, 0, 0)),
            pl.BlockSpec((tm, LANES), lambda i, j: (i % blocks_per_batch, 0)),
            pl.BlockSpec((tm, LANES), lambda i, j: (i % blocks_per_batch, 0)),
            pl.BlockSpec((MXU_TILE, MXU_TILE), lambda i, j: (0, 0)),
            pl.BlockSpec(tri.shape, lambda i, j: (0, 0)),
        ] + [pl.BlockSpec((d, MIXER_WIDTH), weight_block(sub)) for sub in range(COL_BLOCKS_PER_STEP)],
        out_specs=[
            pl.BlockSpec((tm, step_cols), lambda i, j: (i, j)),
            pl.BlockSpec((tm, LANES), lambda i, j: (i, 0)),
        ] + dil_specs,
        out_shape=[
            jax.ShapeDtypeStruct((rows, PROJ_COLS), BF16),
            jax.ShapeDtypeStruct((rows, LANES), F32),
        ] + dil_shapes,
        scratch_shapes=[pltpu.VMEM((tm, d), BF16), pltpu.VMEM((1, LANES), F32),
                        pltpu.VMEM((LANE_BLOCKS_PER_MIXER, tm, LANES), F32)],
        compiler_params=pltpu.CompilerParams(
            dimension_semantics=("arbitrary", "arbitrary"), vmem_limit_bytes=VMEM_LIMIT),
        name="inproj",
    )(x2, g, wf_cat, fb_pad, gain_tab, cos_t, sin_t, mavg, tri, *([w_main] * COL_BLOCKS_PER_STEP))


def _band_body(q_ref, k_ref, v_ref, kp_ref, vp_ref, o_ref, lse_ref, *, tq):
    first = pl.program_id(2) == 0
    row = lax.broadcasted_iota(jnp.int32, (BAND, 2 * BAND), 0)
    col = lax.broadcasted_iota(jnp.int32, (BAND, 2 * BAND), 1)
    in_band = jnp.logical_or(jnp.logical_and(col < BAND, col >= row),
                             jnp.logical_and(col >= BAND, col - BAND <= row))
    first_mask = jnp.logical_and(in_band, jnp.logical_or(col >= BAND, jnp.logical_not(first)))
    lane = lax.broadcasted_iota(jnp.int32, (BAND, LANES), 1)
    head0 = lane < HEAD_DIM
    for c in range(tq // BAND):
        own = slice(c * BAND, (c + 1) * BAND)
        lse_of_head = jnp.zeros((BAND, LANES), F32)
        for pair in range(PAIRS_PER_MIXER):
            lanes = slice(pair * LANES, (pair + 1) * LANES)
            qc = q_ref[own, lanes]
            if c == 0:
                kw = jnp.concatenate([kp_ref[:, lanes], k_ref[own, lanes]], axis=0)
                vw = jnp.concatenate([vp_ref[:, lanes], v_ref[own, lanes]], axis=0)
                mask = first_mask
            else:
                window = slice((c - 1) * BAND, (c + 1) * BAND)
                kw = k_ref[window, lanes]
                vw = v_ref[window, lanes]
                mask = in_band
            outs = []
            for h in range(2):
                s = jnp.where(mask, _dot_nt(_own_head(qc, h), kw), NEG)
                m = jnp.max(s, axis=1, keepdims=True)
                p = jnp.exp2(s - m)
                l = jnp.sum(p, axis=1, keepdims=True)
                outs.append(_dot(p.astype(BF16), vw) / l)
                lse_of_head = jnp.where(lane == 2 * pair + h, m + jnp.log2(l), lse_of_head)
            o_ref[own, lanes] = jnp.where(head0, outs[0], outs[1]).astype(BF16)
        lse_ref[own, :] = lse_of_head


def _band_segment(q_arr, k_arr, v_arr, col_of, *, bsz, seq, dil, tq):
    sub_rows = q_arr.shape[0]
    sub_len = seq // dil
    tq = min(tq, sub_len)
    nq = sub_len // tq
    band_per_tq = tq // BAND

    def cur(section):
        return pl.BlockSpec((tq, MIXER_WIDTH), lambda b, r, i: (b * nq + i, col_of(section, r)))

    def prev(section):
        return pl.BlockSpec(
            (BAND, MIXER_WIDTH),
            lambda b, r, i: (jnp.maximum((b * nq + i) * band_per_tq - 1, 0), col_of(section, r)))

    out_spec = pl.BlockSpec((tq, MIXER_WIDTH), lambda b, r, i: (b * nq + i, r))
    return pl.pallas_call(
        functools.partial(_band_body, tq=tq),
        grid=(bsz, dil, nq),
        in_specs=[cur(0), cur(1), cur(2), prev(1), prev(2)],
        out_specs=[out_spec, pl.BlockSpec((tq, LANES), lambda b, r, i: (b * nq + i, r))],
        out_shape=[
            jax.ShapeDtypeStruct((sub_rows, dil * MIXER_WIDTH), BF16),
            jax.ShapeDtypeStruct((sub_rows, dil * LANES), F32),
        ],
        compiler_params=pltpu.CompilerParams(
            dimension_semantics=("arbitrary",) * 3, vmem_limit_bytes=VMEM_LIMIT),
        name=f"band_d{dil}",
    )(q_arr, k_arr, v_arr, k_arr, v_arr)


def _online_init(tq):
    return (jnp.full((tq, 1), NEG, F32), jnp.zeros((tq, 1), F32), jnp.zeros((tq, LANES), F32))


def _online_step(carry, s, v_blk):
    m, l, acc = carry
    m_new = jnp.maximum(m, jnp.max(s, axis=1, keepdims=True))
    alpha = jnp.exp2(m - m_new)
    p = jnp.exp2(s - m_new)
    l = alpha * l + jnp.sum(p, axis=1, keepdims=True)
    acc = alpha * acc + _dot(p.astype(BF16), v_blk)
    return m_new, l, acc


def _online_finish(carry, h):
    _, l, acc = carry
    return acc / l


def _shifted_init(tq):
    return (jnp.zeros((tq, LANES), F32),)


def _shifted_step(carry, s, v_blk):
    acc, = carry
    return (acc + _dot(jnp.exp2(s).astype(BF16), v_blk),)


def _shifted_finish(carry, h):
    acc, = carry
    return acc / _lane_column(acc, _ones_lane(h))


_ONLINE = (_online_init, _online_step, _online_finish, 1)
_SHIFTED = (_shifted_init, _shifted_step, _shifted_finish, 4)
SHIFTED_MAX_SPAN = 100.0


def _ones_lane(h):
    return HEAD_DIM * (1 - h)


def _augment_values(v, h):
    lane = lax.broadcasted_iota(jnp.int32, v.shape, 1)
    ones = jnp.where(lane == _ones_lane(h), 1.0, 0.0)
    return jnp.where(_head_lane_mask(v.shape, h), v.astype(F32), ones).astype(BF16)


def _attend_tile(q_aug, k_tile, v_tile, state, step, heads=(0, 1), first_row=None):
    new = list(state)
    for h in heads:
        start = first_row or 0
        s = _dot_nt(q_aug[h][start:] if start else q_aug[h], k_tile(h))
        if first_row is not None:
            row = lax.broadcasted_iota(jnp.int32, s.shape, 0)
            col = lax.broadcasted_iota(jnp.int32, s.shape, 1)
            s = jnp.where(col <= row, s, NEG)
        if start:
            part = step(tuple(a[start:] for a in state[h]), s, v_tile(h))
            new[h] = tuple(jnp.concatenate([a[:start], b], axis=0) for a, b in zip(state[h], part))
        else:
            new[h] = step(state[h], s, v_tile(h))
    return tuple(new)


def _attend_diagonal(tile_part, tq, init, parts):
    state = (init(tq), init(tq))
    for d in range(parts):
        state = tile_part(state, d * (tq // parts), tq // parts)
    return state


def _score_bound(q_gain, k_gain):
    bound = 1.02 * HEAD_DIM * SCALE * LOG2E * jnp.max(jnp.abs(q_gain)) * jnp.max(jnp.abs(k_gain))
    return bound.astype(BF16).astype(F32).reshape(1)


def _merge_heads(outs):
    lane = lax.broadcasted_iota(jnp.int32, outs[0].shape, 1)
    return jnp.where(lane < HEAD_DIM, outs[0], outs[1])


def _seq_mixer_specs(*, mixer, nq, tq, seq):
    def colblk(section, p):
        return PROJ_BLOCK_POS[(section, mixer)] * PAIRS_PER_MIXER + p

    q_spec = pl.BlockSpec((tq, LANES), lambda b, p, i: (b * nq + i, colblk(0, p)))
    k_spec = pl.BlockSpec((seq, LANES), lambda b, p, i: (b, colblk(1, p)))
    v_spec = pl.BlockSpec((seq, LANES), lambda b, p, i: (b, colblk(2, p)))
    o_spec = pl.BlockSpec((tq, LANES), lambda b, p, i: (b * nq + i, p))
    return q_spec, k_spec, v_spec, o_spec


_SEQ_PARAMS = pltpu.CompilerParams(
    dimension_semantics=("arbitrary",) * 3, vmem_limit_bytes=VMEM_LIMIT)


def _fox_augment(x, cum, head, h, key_side, bound):
    lane = lax.broadcasted_iota(jnp.int32, x.shape, 1)
    g = jnp.broadcast_to(_lane_column(cum, head) * LOG2E, x.shape)
    g1, g2, g3 = (piece.astype(F32) for piece in _split3(g))
    one = jnp.ones_like(g1)
    pieces = (one, one, one, -g1, -g2, -g3, one) if key_side else (g1, g2, g3, one, one, one, -bound * one)
    base = HEAD_DIM * (1 - h)
    aug = jnp.zeros_like(g1)
    for n, piece in enumerate(pieces):
        aug = jnp.where(lane == base + n, piece, aug)
    return jnp.where(_head_lane_mask(x.shape, h), x.astype(F32), aug).astype(BF16)


def _fox_body(fend_ref, bound_ref, q_ref, k_ref, v_ref, cumq_ref, cumk_ref, o_ref, kaug_scr, vaug_scr, *,
              tq, tk, seq):
    b = pl.program_id(0)
    p = pl.program_id(1)
    iq = pl.program_id(2)
    bound = bound_ref[0]

    @pl.when(iq == 0)
    def _augment_keys():
        def chunk(c, _):
            rows = pl.ds(pl.multiple_of(c * tk, tk), tk)
            for h in range(2):
                kaug_scr[h, rows, :] = _fox_augment(k_ref[rows, :], cumk_ref[rows, :], 2 * p + h, h, True, bound)
                vaug_scr[h, rows, :] = _augment_values(v_ref[rows, :], h)
            return 0
        lax.fori_loop(0, seq // tk, chunk, 0)

    q = q_ref[...]
    cumq = cumq_ref[...]
    q_aug = [_fox_augment(q, cumq, 2 * p + h, h, False, bound) for h in range(2)]

    first_past = iq - 1
    last = jnp.maximum(first_past, 0)

    def reaches(j, heads):
        j = jnp.maximum(j, 0)
        alive = [LOG2E * (fend_ref[b, 2 * p + h, last] - fend_ref[b, 2 * p + h, j])
                 + 2.0 * bound > UNDERFLOW_LOG2 for h in heads]
        return functools.reduce(jnp.logical_and, alive)

    def attend(scheme):
        init, step, finish, diagonal_parts = scheme

        def tile(j, state, heads=(0, 1)):
            rows = pl.ds(pl.multiple_of(j * tk, tk), tk)
            return _attend_tile(q_aug, lambda h: kaug_scr[h, rows, :], lambda h: vaug_scr[h, rows, :], state,
                                step, heads)

        def diagonal_part(state, key0, nkeys):
            rows = pl.ds(pl.multiple_of(iq * tk + key0, nkeys), nkeys)
            return _attend_tile(q_aug, lambda h: kaug_scr[h, rows, :], lambda h: vaug_scr[h, rows, :], state,
                                step, first_row=key0)

        loop = (first_past, _attend_diagonal(diagonal_part, tq, init, diagonal_parts))
        for heads in ((0, 1), (0,), (1,)):
            loop = lax.while_loop(
                lambda lp: jnp.logical_and(lp[0] >= 0, reaches(lp[0], heads)),
                lambda lp: (lp[0] - 1, tile(lp[0], lp[1], heads)),
                loop)
        o_ref[...] = _merge_heads([finish(st, h) for h, st in enumerate(loop[1])]).astype(BF16)

    shifted_ok = 2.0 * bound <= SHIFTED_MAX_SPAN
    pl.when(shifted_ok)(functools.partial(attend, _SHIFTED))
    pl.when(jnp.logical_not(shifted_ok))(functools.partial(attend, _ONLINE))


def _fox(proj, cum, tile_end_sums, score_bound, *, bsz, seq, tq, tk):
    rows = proj.shape[0]
    nq = seq // tq
    q_spec, k_spec, v_spec, o_spec = _seq_mixer_specs(mixer=3, nq=nq, tq=tq, seq=seq)
    smem = pl.BlockSpec(memory_space=pltpu.SMEM)
    return pl.pallas_call(
        functools.partial(_fox_body, tq=tq, tk=tk, seq=seq),
        grid=(bsz, PAIRS_PER_MIXER, nq),
        in_specs=[
            smem, smem, q_spec, k_spec, v_spec,
            pl.BlockSpec((tq, LANES), lambda b, p, i: (b * nq + i, 0)),
            pl.BlockSpec((seq, LANES), lambda b, p, i: (b, 0)),
        ],
        out_specs=o_spec,
        out_shape=jax.ShapeDtypeStruct((rows, MIXER_WIDTH), BF16),
        scratch_shapes=[pltpu.VMEM((2, seq, LANES), BF16), pltpu.VMEM((2, seq, LANES), BF16)],
        compiler_params=_SEQ_PARAMS,
        name="fox",
    )(tile_end_sums, score_bound, proj, proj, proj, cum, cum)


def _moba_body(bound_ref, q_ref, k_ref, v_ref, o_ref, kmean_scr, kaug_scr, vaug_scr, *, tq, tk, seq):
    iq = pl.program_id(2)
    nblk = seq // MOBA_BLOCK
    blocks_per_tile = tq // MOBA_BLOCK
    bound = bound_ref[0]

    @pl.when(iq == 0)
    def _prepare_keys():
        r = lax.broadcasted_iota(jnp.int32, (LANES, seq), 0)
        c = lax.broadcasted_iota(jnp.int32, (LANES, seq), 1)
        member = jnp.where(c // MOBA_BLOCK == r % HEAD_DIM, 1.0 / MOBA_BLOCK, 0.0).astype(BF16)
        hi, lo = _split2(_dot(member, k_ref[...]))
        kmean_scr[...] = jnp.concatenate([hi, lo], axis=1)

        def chunk(c, _):
            rows = pl.ds(pl.multiple_of(c * tk, tk), tk)
            k = k_ref[rows, :].astype(F32)
            lane = lax.broadcasted_iota(jnp.int32, (tk, LANES), 1)
            blk = (c * tk + lax.broadcasted_iota(jnp.int32, (tk, LANES), 0)) // MOBA_BLOCK
            for h in range(2):
                spare = lane - HEAD_DIM * (1 - h)
                onehot = jnp.where(jnp.logical_or(spare == blk, spare == HEAD_DIM - 1), 1.0, 0.0)
                kaug_scr[h, rows, :] = jnp.where(_head_lane_mask(k.shape, h), k, onehot).astype(BF16)
                vaug_scr[h, rows, :] = _augment_values(v_ref[rows, :], h)
            return 0
        lax.fori_loop(0, seq // tk, chunk, 0)

    q = q_ref[...]
    nslot = -(-nblk // 8) * 8
    blk = lax.broadcasted_iota(jnp.int32, (nslot, tq), 0)
    qblk = blocks_per_tile * iq + lax.broadcasted_iota(jnp.int32, (nslot, tq), 1) // MOBA_BLOCK
    q_aug = []
    tail_row = lax.broadcasted_iota(jnp.int32, (HEAD_DIM - nslot, tq), 0)
    tail = jnp.where(tail_row == HEAD_DIM - nslot - 1, -bound, 0.0)
    for h in range(2):
        qh = _own_head(q, h)
        spare = slice(HEAD_DIM * (1 - h), HEAD_DIM * (1 - h) + nslot)
        gate = _dot_nt(kmean_scr[spare, :], jnp.concatenate([qh, qh], axis=1))
        past = blk < qblk
        work = jnp.where(past, gate, NEG)
        bias = jnp.where(blk == qblk, 0.0, NEG)
        for _ in range(min(MOBA_TOPK, nblk)):
            best = jnp.max(work, axis=0, keepdims=True)
            idx = jnp.min(jnp.where(work == best, blk, nslot), axis=0, keepdims=True)
            hit = blk == idx
            bias = jnp.where(hit, jnp.where(past, 0.0, bias), bias)
            work = jnp.where(hit, -jnp.inf, work)
        own_half = jnp.zeros((HEAD_DIM, tq), F32)
        halves = [bias, tail, own_half] if h == 1 else [own_half, bias, tail]
        q_aug.append(jnp.where(_head_lane_mask(q.shape, h), q.astype(F32),
                               jnp.concatenate(halves, axis=0).T).astype(BF16))

    def attend(scheme):
        init, step, finish, diagonal_parts = scheme

        def tile(j, state):
            rows = pl.ds(pl.multiple_of(j * tk, tk), tk)
            return _attend_tile(q_aug, lambda h: kaug_scr[h, rows, :], lambda h: vaug_scr[h, rows, :], state, step)

        def diagonal_part(state, key0, nkeys):
            rows = pl.ds(pl.multiple_of(iq * tk + key0, nkeys), nkeys)
            return _attend_tile(q_aug, lambda h: kaug_scr[h, rows, :], lambda h: vaug_scr[h, rows, :], state,
                                step, first_row=key0)

        state = _attend_diagonal(diagonal_part, tq, init, diagonal_parts)
        state = lax.fori_loop(0, iq, tile, state)
        o_ref[...] = _merge_heads([finish(st, h) for h, st in enumerate(state)]).astype(BF16)

    shifted_ok = 2.0 * bound <= SHIFTED_MAX_SPAN
    pl.when(shifted_ok)(functools.partial(attend, _SHIFTED))
    pl.when(jnp.logical_not(shifted_ok))(functools.partial(attend, _ONLINE))


def _moba(proj, score_bound, *, bsz, seq, tq, tk):
    assert tq == tk and tq % MOBA_BLOCK == 0
    assert -(-(seq // MOBA_BLOCK) // 8) * 8 < HEAD_DIM
    rows = proj.shape[0]
    nq = seq // tq
    q_spec, k_spec, v_spec, o_spec = _seq_mixer_specs(mixer=2, nq=nq, tq=tq, seq=seq)
    return pl.pallas_call(
        functools.partial(_moba_body, tq=tq, tk=tk, seq=seq),
        grid=(bsz, PAIRS_PER_MIXER, nq),
        in_specs=[pl.BlockSpec(memory_space=pltpu.SMEM), q_spec, k_spec, v_spec],
        out_specs=o_spec,
        out_shape=jax.ShapeDtypeStruct((rows, MIXER_WIDTH), BF16),
        scratch_shapes=[pltpu.VMEM((LANES, 2 * LANES), BF16), pltpu.VMEM((2, seq, LANES), BF16),
                        pltpu.VMEM((2, seq, LANES), BF16)],
        compiler_params=_SEQ_PARAMS,
        name="moba",
    )(score_bound, proj, proj, proj)


def _stick_body(q_ref, k_ref, v_ref, o_ref, *, tq, tk):
    iq = pl.program_id(2)
    groups = [slice(g * tk, (g + 1) * tk) for g in range(tq // tk)]
    q = q_ref[...]
    q_own = [_own_head(q, h) for h in range(2)]
    later_row = lax.broadcasted_iota(jnp.int32, (tk, tk), 0)
    later_col = lax.broadcasted_iota(jnp.int32, (tk, tk), 1)
    later = jnp.where(later_row > later_col, 1.0, 0.0).astype(BF16)

    def attend(z, weigh_values, carry_sum, strictly_past=None):
        softplus = jnp.maximum(z, 0.0) + jnp.log2(1.0 + jnp.exp2(-jnp.abs(z)))
        log_keep = -softplus
        if strictly_past is not None:
            log_keep = jnp.where(strictly_past, log_keep, 0.0)
        afters = []
        newer = carry_sum
        for c in reversed(range(z.shape[1] // tk)):
            blk = log_keep[:, c * tk:(c + 1) * tk]
            hi, lo = _split2(blk)
            afters.append(_dot(hi, later) + _dot(lo, later) + newer)
            newer = newer + jnp.sum(blk, axis=1, keepdims=True)
        log_a = z - softplus + jnp.concatenate(afters[::-1], axis=1)
        if strictly_past is not None:
            log_a = jnp.where(strictly_past, log_a, NEG)
        return newer, weigh_values(jnp.exp2(log_a).astype(BF16))

    first_blk, k_win, v_win, masks = [], [], [], []
    row = lax.broadcasted_iota(jnp.int32, (tk, 2 * tk), 0)
    col = lax.broadcasted_iota(jnp.int32, (tk, 2 * tk), 1)
    for g in range(len(groups)):
        own_blk = iq * len(groups) + g
        first_blk.append(jnp.maximum(own_blk - 1, 0))
        keys = pl.ds(pl.multiple_of(first_blk[g] * tk, tk), 2 * tk)
        k_win.append(k_ref[keys, :])
        v_win.append(v_ref[keys, :])
        masks.append(col + (first_blk[g] - own_blk) * tk < row)
    strictly_past = jnp.concatenate(masks, axis=0)
    state = []
    for h in range(2):
        z = jnp.concatenate([_dot_nt(q_own[h][rows], k_win[g]) for g, rows in enumerate(groups)], axis=0)
        state.append(attend(
            z, lambda a: jnp.concatenate([_dot(a[rows], v_win[g]) for g, rows in enumerate(groups)], axis=0),
            jnp.zeros((tq, 1), F32), strictly_past))

    outs = [[None] * len(groups) for _ in range(2)]
    for g, rows in enumerate(groups):
        def cond(loop):
            j, tail = loop
            alive = jnp.max(jnp.maximum(tail[0][0], tail[1][0])) > UNDERFLOW_LOG2
            return jnp.logical_and(j >= 0, alive)

        def body(loop):
            j, tail = loop
            keys = pl.ds(pl.multiple_of(j * tk, tk), tk)
            k_blk = k_ref[keys, :]
            v_blk = v_ref[keys, :]
            new = []
            for h in range(2):
                carry_sum, acc = tail[h]
                carry_sum, add = attend(_dot_nt(q_own[h][rows], k_blk), lambda a: _dot(a, v_blk), carry_sum)
                new.append((carry_sum, acc + add))
            return j - 1, tuple(new)

        _, tail = lax.while_loop(cond, body, (first_blk[g] - 1, tuple((s[rows], a[rows]) for s, a in state)))
        for h in range(2):
            outs[h][g] = tail[h][1]
    o_ref[...] = _merge_heads([jnp.concatenate(outs[h], axis=0) for h in range(2)]).astype(BF16)


def _stick(proj, *, bsz, seq, tq, tk):
    rows = proj.shape[0]
    nq = seq // tq
    q_spec, k_spec, v_spec, o_spec = _seq_mixer_specs(mixer=1, nq=nq, tq=tq, seq=seq)
    return pl.pallas_call(
        functools.partial(_stick_body, tq=tq, tk=tk),
        grid=(bsz, PAIRS_PER_MIXER, nq),
        in_specs=[q_spec, k_spec, v_spec],
        out_specs=o_spec,
        out_shape=jax.ShapeDtypeStruct((rows, MIXER_WIDTH), BF16),
        compiler_params=_SEQ_PARAMS,
        name="stick",
    )(proj, proj, proj)


def _outproj_body(o1_ref, o4_ref, o16_ref, l1_ref, l4_ref, l16_ref, yb_ref, yc_ref, yd_ref,
                  gate_ref, x_ref, w_ref, out_ref, o_scr, l_scr, *, tm):
    def projected(y, mixer):
        cols = slice(mixer * MIXER_WIDTH, (mixer + 1) * MIXER_WIDTH)
        g = gate_ref[:, cols].astype(F32)
        return _dot((y * (g / (1.0 + jnp.exp(-g)))).astype(BF16), w_ref[cols, :])

    acc = x_ref[...]
    for mixer, y_ref in ((1, yb_ref), (2, yc_ref), (3, yd_ref)):
        acc = acc + projected(y_ref[...].astype(F32), mixer)

    for slot, (dil, o_ref, l_ref) in enumerate(zip(STRIDED_DILATIONS, (o4_ref, o16_ref), (l4_ref, l16_ref))):
        n = tm // dil
        for r in range(dil):
            l_scr[slot, pl.ds(r, n, stride=dil), :] = l_ref[:, r * LANES:(r + 1) * LANES]
            for c in range(LANE_BLOCKS_PER_MIXER):
                cols = slice(r * MIXER_WIDTH + c * LANES, r * MIXER_WIDTH + (c + 1) * LANES)
                o_scr[slot, c, pl.ds(r, n, stride=dil), :] = o_ref[:, cols].astype(F32)

    def natural(slot):
        return jnp.concatenate([o_scr[slot, c] for c in range(LANE_BLOCKS_PER_MIXER)], axis=1)

    l1, l4, l16 = l1_ref[...], l_scr[0], l_scr[1]
    m = jnp.maximum(jnp.maximum(l1, l4), l16)
    e1, e4, e16 = jnp.exp2(l1 - m), jnp.exp2(l4 - m), jnp.exp2(l16 - m)
    total = e1 + e4 + e16
    head_of_col = lax.broadcasted_iota(jnp.int32, (LANES, MIXER_WIDTH), 1) // HEAD_DIM
    spread = jnp.where(lax.broadcasted_iota(jnp.int32, (LANES, MIXER_WIDTH), 0) == head_of_col,
                       1.0, 0.0).astype(BF16)

    def per_lane(weight):
        hi, lo = _split2(weight)
        return _dot(hi, spread) + _dot(lo, spread)

    ya = (per_lane(e1 / total) * o1_ref[...].astype(F32) + per_lane(e4 / total) * natural(0)
          + per_lane(e16 / total) * natural(1))
    out_ref[...] = acc + projected(ya, 0)


def _outproj(seg_o, seg_lse, yb, yc, yd, proj, x2, w_out, *, tm):
    rows, d = x2.shape
    tn = d
    row_blk = lambda i, j: (i, 0)
    mix_spec = pl.BlockSpec((tm, MIXER_WIDTH), row_blk)
    seg_specs = [pl.BlockSpec((tm // dil, dil * MIXER_WIDTH), row_blk) for dil in DILATIONS]
    lse_specs = [pl.BlockSpec((tm // dil, dil * LANES), row_blk) for dil in DILATIONS]
    return pl.pallas_call(
        functools.partial(_outproj_body, tm=tm),
        grid=(rows // tm, d // tn),
        in_specs=seg_specs + lse_specs + [mix_spec] * 3 + [
            pl.BlockSpec((tm, MIX_WIDTH), lambda i, j: (i, GATE_GROUP)),
            pl.BlockSpec((tm, tn), lambda i, j: (i, j)),
            pl.BlockSpec((MIX_WIDTH, tn), lambda i, j: (0, j)),
        ],
        out_specs=pl.BlockSpec((tm, tn), lambda i, j: (i, j)),
        out_shape=jax.ShapeDtypeStruct((rows, d), F32),
        scratch_shapes=[pltpu.VMEM((len(STRIDED_DILATIONS), LANE_BLOCKS_PER_MIXER, tm, LANES), F32),
                        pltpu.VMEM((len(STRIDED_DILATIONS), tm, LANES), F32)],
        compiler_params=pltpu.CompilerParams(
            dimension_semantics=("arbitrary", "arbitrary"), vmem_limit_bytes=VMEM_LIMIT),
        name="outproj",
    )(*seg_o, *seg_lse, yb, yc, yd, proj, x2, w_out)


def _cast_body(w_ref, o_ref):
    o_ref[...] = w_ref[...].astype(BF16)


def _cast_columns(w):
    depth, d, n = w.shape
    step = min(CAST_COLS, n)
    spec = pl.BlockSpec((1, d, step), lambda layer, j: (layer, 0, j))
    return pl.pallas_call(
        _cast_body,
        grid=(depth, n // step),
        in_specs=[spec],
        out_specs=spec,
        out_shape=jax.ShapeDtypeStruct((depth, d, n), BF16),
        compiler_params=pltpu.CompilerParams(
            dimension_semantics=("arbitrary", "arbitrary"), vmem_limit_bytes=VMEM_LIMIT),
        name="cast_w_out",
    )(w)


def _cast_w_in_body(w_ref, main_ref, gate_ref, *, main_steps):
    j = pl.program_id(1)

    @pl.when(j < main_steps)
    def _():
        main_ref[...] = w_ref[...].astype(BF16)

    @pl.when(j == main_steps)
    def _():
        w = w_ref[0, :, :LANES]
        lane = lax.broadcasted_iota(jnp.int32, w.shape, 1)
        hi, lo = _split2(jnp.where(lane < HEADS_PER_MIXER, w, 0.0))
        gate_ref[0] = jnp.concatenate([hi, lo], axis=1)


def _cast_w_in(w_in):
    depth, d, n = w_in.shape
    assert n == PROJ_COLS + HEADS_PER_MIXER
    main_steps = PROJ_COLS // CAST_COLS
    return pl.pallas_call(
        functools.partial(_cast_w_in_body, main_steps=main_steps),
        grid=(depth, main_steps + 1),
        in_specs=[pl.BlockSpec((1, d, CAST_COLS), lambda layer, j: (layer, 0, j))],
        out_specs=[
            pl.BlockSpec((1, d, CAST_COLS), lambda layer, j: (layer, 0, jnp.minimum(j, main_steps - 1))),
            pl.BlockSpec((1, d, 2 * LANES), lambda layer, j: (layer, 0, 0)),
        ],
        out_shape=[
            jax.ShapeDtypeStruct((depth, d, PROJ_COLS), BF16),
            jax.ShapeDtypeStruct((depth, d, 2 * LANES), BF16),
        ],
        compiler_params=pltpu.CompilerParams(
            dimension_semantics=("arbitrary", "arbitrary"), vmem_limit_bytes=VMEM_LIMIT),
        name="cast_w_in",
    )(w_in)


def _rope_tables(seq):
    inv = 1.0 / (ROPE_THETA ** (jnp.arange(0, HEAD_DIM, 2, dtype=F32) / HEAD_DIM))
    ang = jnp.arange(seq, dtype=F32)[:, None] * inv[None, :]
    cos, sin = jnp.cos(ang), jnp.sin(ang)
    reps = LANES // HEAD_DIM
    cos_t = jnp.tile(jnp.concatenate([cos, cos], axis=1), (1, reps))
    sin_t = jnp.tile(jnp.concatenate([-sin, sin], axis=1), (1, reps))
    return cos_t, sin_t


def _gain_table(qn, kn):
    ones = jnp.ones((HEAD_DIM,), F32)
    per_mixer = {0: (qn[0], ones, qn[1], qn[2]), 1: (kn[0], ones, kn[1], kn[2])}
    blocks = [per_mixer[section][mixer] if section < 2 else ones for section, mixer in PROJ_BLOCK_ORDER]
    tab = jnp.stack([jnp.tile(g.astype(F32), HEADS_PER_MIXER) for g in blocks])
    return tab[:, None, :]


def kernel(x, norm_gain, w_in, q_norm_gain, k_norm_gain, forget_bias, w_out):
    bsz, seq, d = x.shape
    depth = w_in.shape[0]
    rows = bsz * seq
    tm = min(512, seq)
    tq_seq = min(QUERY_TILE, seq)
    cos_t, sin_t = _rope_tables(seq)
    head_of_lane = jnp.arange(MXU_TILE) // HEAD_DIM
    mavg = jnp.where(head_of_lane[:, None] == head_of_lane[None, :], 1.0 / HEAD_DIM, 0.0).astype(BF16)
    tri = jnp.tril(jnp.ones((MXU_TILE, MXU_TILE), F32)).astype(BF16)

    def natural_col(section, r):
        return PROJ_BLOCK_POS[(section, 0)]

    def strided_col(section, r):
        return r

    w_main, wf_cat = _cast_w_in(w_in)
    w_out_bf16 = _cast_columns(w_out)

    x2 = x.reshape(rows, d)
    for layer in range(depth):
        fb_pad = jnp.pad(forget_bias[layer].astype(F32), (0, LANES - HEADS_PER_MIXER))[None, :]
        gain_tab = _gain_table(q_norm_gain[layer], k_norm_gain[layer])

        proj, cum, *strided = _inproj(x2, norm_gain[layer][None, :].astype(F32), w_main[layer], wf_cat[layer],
                                      fb_pad, gain_tab, cos_t, sin_t, mavg, tri, seq=seq, tm=tm)
        seg = [_band_segment(proj, proj, proj, natural_col, bsz=bsz, seq=seq, dil=1, tq=512)]
        for di, dil in enumerate(STRIDED_DILATIONS):
            qd, kd, vd = strided[3 * di:3 * di + 3]
            seg.append(_band_segment(qd, kd, vd, strided_col, bsz=bsz, seq=seq, dil=dil, tq=512))
        yb = _stick(proj, bsz=bsz, seq=seq, tq=min(STICK_QUERY_TILE, seq), tk=STICK_KEY_TILE)
        yc = _moba(proj, _score_bound(q_norm_gain[layer, 1], k_norm_gain[layer, 1]),
                   bsz=bsz, seq=seq, tq=tq_seq, tk=KEY_TILE)
        nt = seq // KEY_TILE
        tile_end_sums = cum.reshape(bsz, nt, KEY_TILE, LANES)[:, :, KEY_TILE - 1, :HEADS_PER_MIXER]
        tile_end_sums = tile_end_sums.transpose(0, 2, 1)
        yd = _fox(proj, cum, tile_end_sums, _score_bound(q_norm_gain[layer, 2], k_norm_gain[layer, 2]),
                  bsz=bsz, seq=seq, tq=tq_seq, tk=KEY_TILE)
        x2 = _outproj([s[0] for s in seg], [s[1] for s in seg], yb, yc, yd, proj, x2,
                      w_out_bf16[layer], tm=min(OUT_ROW_TILE, seq))
    return x2.reshape(bsz, seq, d)
```

```python
import functools

import jax
import jax.numpy as jnp
from jax import lax
from jax.experimental import pallas as pl
from jax.experimental.pallas import tpu as pltpu

F32 = jnp.float32
BF16 = jnp.bfloat16

HEAD_DIM = 64
HALF_DIM = HEAD_DIM // 2
LANES = 128
MXU_TILE = 256
N_MIXERS = 4
HEADS_PER_MIXER = 8
PAIRS_PER_MIXER = HEADS_PER_MIXER // 2
MIXER_WIDTH = HEADS_PER_MIXER * HEAD_DIM
LANE_BLOCKS_PER_MIXER = MIXER_WIDTH // LANES
MIX_WIDTH = N_MIXERS * MIXER_WIDTH
N_SECTIONS = 4
PROJ_COLS = N_SECTIONS * MIX_WIDTH
COL_BLOCKS_PER_SECTION = MIX_WIDTH // LANES
ROPE_THETA = 10000.0
RMS_EPS = 1e-6
SCALE = HEAD_DIM ** -0.5
LOG2E = 1.4426950408889634
NEG = -1e30
DILATIONS = (1, 4, 16)
STRIDED_DILATIONS = DILATIONS[1:]
BAND = 128
MOBA_BLOCK = 256
MOBA_TOPK = 3
QUERY_TILE = 1024
KEY_TILE = 1024
STICK_QUERY_TILE = 1024
STICK_KEY_TILE = 256
COL_BLOCKS_PER_STEP = 4
PROJ_BLOCK_ORDER = ((0, 0), (1, 0), (0, 2), (1, 2),
                    (0, 3), (1, 3), (0, 1), (1, 1),
                    (2, 0), (2, 1), (2, 2), (2, 3),
                    (3, 0), (3, 1), (3, 2), (3, 3))
PROJ_BLOCK_POS = {block: pos for pos, block in enumerate(PROJ_BLOCK_ORDER)}
GATE_GROUP = PROJ_BLOCK_POS[(3, 0)] // N_MIXERS
CAST_COLS = 1024
OUT_ROW_TILE = 256
UNDERFLOW_LOG2 = -160.0
VMEM_LIMIT = 56 * 1024 * 1024

_NT = (((1,), (1,)), ((), ()))


def _dot(a, b):
    return jnp.dot(a, b, preferred_element_type=F32)


def _dot_nt(a, b):
    return lax.dot_general(a, b, _NT, preferred_element_type=F32)


def _split2(x):
    hi = x.astype(BF16)
    lo = (x - hi.astype(F32)).astype(BF16)
    return hi, lo


def _split3(x):
    b1 = x.astype(BF16)
    r1 = x - b1.astype(F32)
    b2 = r1.astype(BF16)
    r2 = r1 - b2.astype(F32)
    return b1, b2, r2.astype(BF16)


def _head_lane_mask(shape, h):
    lane = lax.broadcasted_iota(jnp.int32, shape, len(shape) - 1)
    return (lane >= h * HEAD_DIM) & (lane < (h + 1) * HEAD_DIM)


def _own_head(x, h):
    return jnp.where(_head_lane_mask(x.shape, h), x, jnp.zeros_like(x))


def _lane_column(x, n):
    lane = lax.broadcasted_iota(jnp.int32, x.shape, 1)
    return jnp.sum(jnp.where(lane == n, x, 0.0), axis=1, keepdims=True)


def _inproj_body(x_ref, g_ref, wf_ref, fb_ref, gain_ref, cos_ref, sin_ref, mavg_ref, tri_ref, *rest,
                 blocks_per_batch, tm):
    w_refs, (proj_ref, cum_ref), rest = rest[:COL_BLOCKS_PER_STEP], rest[COL_BLOCKS_PER_STEP:][:2], \
        rest[COL_BLOCKS_PER_STEP + 2:]
    dil_refs = rest[:3 * len(STRIDED_DILATIONS)]
    h_scr, carry_scr, dil_scr = rest[3 * len(STRIDED_DILATIONS):]
    i = pl.program_id(0)
    j = pl.program_id(1)

    @pl.when(j == 0)
    def _prologue():
        x = x_ref[...]
        ms = jnp.mean(x * x, axis=-1, keepdims=True)
        h = x * lax.rsqrt(ms + RMS_EPS) * g_ref[...]
        h_hi, h_lo = _split2(h)
        h_scr[...] = h_hi
        wf = wf_ref[...]
        t = _dot(h_hi, wf)
        u = _dot(h_lo, wf[:, :LANES])
        logit = t[:, :LANES] + t[:, LANES:] + u + fb_ref[...]
        lf = jnp.minimum(logit, 0.0) - jnp.log1p(jnp.exp(-jnp.abs(logit)))

        @pl.when(i % blocks_per_batch == 0)
        def _():
            carry_scr[...] = jnp.zeros_like(carry_scr)

        carry = carry_scr[...]
        tri = tri_ref[...]
        sub = tri.shape[0]
        for r in range(tm // sub):
            b1, b2, b3 = _split3(lf[r * sub:(r + 1) * sub])
            c = _dot(tri, jnp.concatenate([b1, b2, b3], axis=1))
            c = c[:, :LANES] + c[:, LANES:2 * LANES] + c[:, 2 * LANES:] + carry
            cum_ref[r * sub:(r + 1) * sub, :] = c
            carry = c[sub - 1:sub, :]
        carry_scr[...] = carry

    for group in range(len(PROJ_BLOCK_ORDER) // COL_BLOCKS_PER_STEP):
        @pl.when(j == group)
        def _():
            for sub in range(COL_BLOCKS_PER_STEP):
                section, mixer = PROJ_BLOCK_ORDER[group * COL_BLOCKS_PER_STEP + sub]
                _inproj_column_block(section, mixer, sub * MIXER_WIDTH, h_scr, w_refs[sub], gain_ref.at[sub],
                                     cos_ref, sin_ref, mavg_ref, proj_ref, dil_refs, dil_scr, tm=tm)


def _inproj_column_block(section, mixer, col0, h_scr, w_ref, gain_ref, cos_ref, sin_ref, mavg_ref, proj_ref,
                         dil_refs, dil_scr, *, tm):
    def out_cols(c):
        return slice(col0 + c * LANES, col0 + (c + 1) * LANES)

    acc = _dot(h_scr[...], w_ref[...])
    normed = section < 2 and mixer != 1
    roped = normed and mixer in (0, 2)
    scale = SCALE * LOG2E if section == 0 else 1.0

    if not normed:
        proj_ref[:, col0:col0 + MIXER_WIDTH] = (acc * scale if section == 0 else acc).astype(BF16)
    else:
        sq = (acc * acc).astype(BF16)
        ms = jnp.concatenate([_dot(sq[:, c:c + MXU_TILE], mavg_ref[...])
                              for c in range(0, MIXER_WIDTH, MXU_TILE)], axis=1)
        t = acc * lax.rsqrt(ms + RMS_EPS) * (gain_ref[0] * scale)
        if not roped:
            proj_ref[:, col0:col0 + MIXER_WIDTH] = t.astype(BF16)
        else:
            cosv = cos_ref[...]
            sinv = sin_ref[...]
            lane = lax.broadcasted_iota(jnp.int32, (tm, LANES), 1)
            first_half = (lane % HEAD_DIM) < HALF_DIM
            for c in range(LANE_BLOCKS_PER_MIXER):
                tc = t[:, c * LANES:(c + 1) * LANES]
                partner = jnp.where(first_half,
                                    pltpu.roll(tc, LANES - HALF_DIM, 1),
                                    pltpu.roll(tc, HALF_DIM, 1))
                proj_ref[:, out_cols(c)] = (tc * cosv + partner * sinv).astype(BF16)

    if mixer == 0 and section < 3:
        for c in range(LANE_BLOCKS_PER_MIXER):
            dil_scr[c] = proj_ref[:, out_cols(c)].astype(F32)
        for di, dil in enumerate(STRIDED_DILATIONS):
            ref = dil_refs[3 * di + section]
            n = tm // dil
            for r in range(dil):
                for c in range(LANE_BLOCKS_PER_MIXER):
                    lo = r * MIXER_WIDTH + c * LANES
                    ref[:, lo:lo + LANES] = dil_scr[c, pl.ds(r, n, stride=dil), :].astype(BF16)


def _inproj(x2, g, w_main, wf_cat, fb_pad, gain_tab, cos_t, sin_t, mavg, tri, *, seq, tm):
    rows, d = x2.shape
    blocks_per_batch = seq // tm
    step_cols = COL_BLOCKS_PER_STEP * MIXER_WIDTH
    body = functools.partial(_inproj_body, blocks_per_batch=blocks_per_batch, tm=tm)

    def weight_block(sub):
        origin = [section * N_MIXERS + mixer for section, mixer in PROJ_BLOCK_ORDER[sub::COL_BLOCKS_PER_STEP]]

        def index(i, j):
            blk = origin[0]
            for group in range(1, len(origin)):
                blk = jnp.where(j == group, origin[group], blk)
            return 0, blk
        return index

    dil_specs, dil_shapes = [], []
    for dil in STRIDED_DILATIONS:
        for _ in range(3):
            dil_specs.append(pl.BlockSpec((tm // dil, dil * MIXER_WIDTH), lambda i, j: (i, 0)))
            dil_shapes.append(jax.ShapeDtypeStruct((rows // dil, dil * MIXER_WIDTH), BF16))
    return pl.pallas_call(
        body,
        grid=(rows // tm, PROJ_COLS // step_cols),
        in_specs=[
            pl.BlockSpec((tm, d), lambda i, j: (i, 0)),
            pl.BlockSpec((1, d), lambda i, j: (0, 0)),
            pl.BlockSpec((d, 2 * LANES), lambda i, j: (0, 0)),
            pl.BlockSpec((1, LANES), lambda i, j: (0, 0)),
            pl.BlockSpec((COL_BLOCKS_PER_STEP, 1, MIXER_WIDTH), lambda i, j: (j, 0, 0)),
            pl.BlockSpec((tm, LANES), lambda i, j: (i % blocks_per_batch, 0)),
            pl.BlockSpec((tm, LANES), lambda i, j: (i % blocks_per_batch, 0)),
            pl.BlockSpec((MXU_TILE, MXU_TILE), lambda i, j: (0, 0)),
            pl.BlockSpec(tri.shape, lambda i, j: (0, 0)),
        ] + [pl.BlockSpec((d, MIXER_WIDTH), weight_block(sub)) for sub in range(COL_BLOCKS_PER_STEP)],
        out_specs=[
            pl.BlockSpec((tm, step_cols), lambda i, j: (i, j)),
            pl.BlockSpec((tm, LANES), lambda i, j: (i, 0)),
        ] + dil_specs,
        out_shape=[
            jax.ShapeDtypeStruct((rows, PROJ_COLS), BF16),
            jax.ShapeDtypeStruct((rows, LANES), F32),
        ] + dil_shapes,
        scratch_shapes=[pltpu.VMEM((tm, d), BF16), pltpu.VMEM((1, LANES), F32),
                        pltpu.VMEM((LANE_BLOCKS_PER_MIXER, tm, LANES), F32)],
        compiler_params=pltpu.CompilerParams(
            dimension_semantics=("arbitrary", "arbitrary"), vmem_limit_bytes=VMEM_LIMIT),
        name="inproj",
    )(x2, g, wf_cat, fb_pad, gain_tab, cos_t, sin_t, mavg, tri, *([w_main] * COL_BLOCKS_PER_STEP))


def _band_body(q_ref, k_ref, v_ref, kp_ref, vp_ref, o_ref, lse_ref, *, tq):
    first = pl.program_id(2) == 0
    row = lax.broadcasted_iota(jnp.int32, (BAND, 2 * BAND), 0)
    col = lax.broadcasted_iota(jnp.int32, (BAND, 2 * BAND), 1)
    in_band = jnp.logical_or(jnp.logical_and(col < BAND, col >= row),
                             jnp.logical_and(col >= BAND, col - BAND <= row))
    first_mask = jnp.logical_and(in_band, jnp.logical_or(col >= BAND, jnp.logical_not(first)))
    lane = lax.broadcasted_iota(jnp.int32, (BAND, LANES), 1)
    head0 = lane < HEAD_DIM
    for pair in range(PAIRS_PER_MIXER):
        lanes = slice(pair * LANES, (pair + 1) * LANES)
        for c in range(tq // BAND):
            own = slice(c * BAND, (c + 1) * BAND)
            qc = q_ref[own, lanes]
            if c == 0:
                kw = jnp.concatenate([kp_ref[:, lanes], k_ref[own, lanes]], axis=0)
                vw = jnp.concatenate([vp_ref[:, lanes], v_ref[own, lanes]], axis=0)
                mask = first_mask
            else:
                window = slice((c - 1) * BAND, (c + 1) * BAND)
                kw = k_ref[window, lanes]
                vw = v_ref[window, lanes]
                mask = in_band
            outs, lses = [], []
            for h in range(2):
                s = jnp.where(mask, _dot_nt(_own_head(qc, h), kw), NEG)
                m = jnp.max(s, axis=1, keepdims=True)
                p = jnp.exp2(s - m)
                l = jnp.sum(p, axis=1, keepdims=True)
                outs.append(_dot(p.astype(BF16), vw) / l)
                lses.append(m + jnp.log2(l))
            o_ref[own, lanes] = jnp.where(head0, outs[0], outs[1]).astype(BF16)
            lse_ref[own, lanes] = jnp.where(head0, lses[0], lses[1])


def _band_segment(q_arr, k_arr, v_arr, col_of, *, bsz, seq, dil, tq):
    sub_rows = q_arr.shape[0]
    sub_len = seq // dil
    tq = min(tq, sub_len)
    nq = sub_len // tq
    band_per_tq = tq // BAND

    def cur(section):
        return pl.BlockSpec((tq, MIXER_WIDTH), lambda b, r, i: (b * nq + i, col_of(section, r)))

    def prev(section):
        return pl.BlockSpec(
            (BAND, MIXER_WIDTH),
            lambda b, r, i: (jnp.maximum((b * nq + i) * band_per_tq - 1, 0), col_of(section, r)))

    out_spec = pl.BlockSpec((tq, MIXER_WIDTH), lambda b, r, i: (b * nq + i, r))
    return pl.pallas_call(
        functools.partial(_band_body, tq=tq),
        grid=(bsz, dil, nq),
        in_specs=[cur(0), cur(1), cur(2), prev(1), prev(2)],
        out_specs=[out_spec, out_spec],
        out_shape=[
            jax.ShapeDtypeStruct((sub_rows, dil * MIXER_WIDTH), BF16),
            jax.ShapeDtypeStruct((sub_rows, dil * MIXER_WIDTH), F32),
        ],
        compiler_params=pltpu.CompilerParams(
            dimension_semantics=("arbitrary",) * 3, vmem_limit_bytes=VMEM_LIMIT),
        name=f"band_d{dil}",
    )(q_arr, k_arr, v_arr, k_arr, v_arr)


def _online_init(tq):
    return (jnp.full((tq, 1), NEG, F32), jnp.zeros((tq, 1), F32), jnp.zeros((tq, LANES), F32))


def _online_step(carry, s, v_blk):
    m, l, acc = carry
    m_new = jnp.maximum(m, jnp.max(s, axis=1, keepdims=True))
    alpha = jnp.exp2(m - m_new)
    p = jnp.exp2(s - m_new)
    l = alpha * l + jnp.sum(p, axis=1, keepdims=True)
    acc = alpha * acc + _dot(p.astype(BF16), v_blk)
    return m_new, l, acc


def _online_finish(carry, h):
    _, l, acc = carry
    return acc / l


def _shifted_init(tq):
    return (jnp.zeros((tq, LANES), F32),)


def _shifted_step(carry, s, v_blk):
    acc, = carry
    return (acc + _dot(jnp.exp2(s).astype(BF16), v_blk),)


def _shifted_finish(carry, h):
    acc, = carry
    return acc / _lane_column(acc, _ones_lane(h))


_ONLINE = (_online_init, _online_step, _online_finish, 1)
_SHIFTED = (_shifted_init, _shifted_step, _shifted_finish, 4)
SHIFTED_MAX_SPAN = 100.0


def _ones_lane(h):
    return HEAD_DIM * (1 - h)


def _augment_values(v, h):
    lane = lax.broadcasted_iota(jnp.int32, v.shape, 1)
    ones = jnp.where(lane == _ones_lane(h), 1.0, 0.0)
    return jnp.where(_head_lane_mask(v.shape, h), v.astype(F32), ones).astype(BF16)


def _attend_tile(q_aug, k_tile, v_tile, state, step, heads=(0, 1), first_row=None):
    new = list(state)
    for h in heads:
        start = first_row or 0
        s = _dot_nt(q_aug[h][start:] if start else q_aug[h], k_tile(h))
        if first_row is not None:
            row = lax.broadcasted_iota(jnp.int32, s.shape, 0)
            col = lax.broadcasted_iota(jnp.int32, s.shape, 1)
            s = jnp.where(col <= row, s, NEG)
        if start:
            part = step(tuple(a[start:] for a in state[h]), s, v_tile(h))
            new[h] = tuple(jnp.concatenate([a[:start], b], axis=0) for a, b in zip(state[h], part))
        else:
            new[h] = step(state[h], s, v_tile(h))
    return tuple(new)


def _attend_diagonal(tile_part, tq, init, parts):
    state = (init(tq), init(tq))
    for d in range(parts):
        state = tile_part(state, d * (tq // parts), tq // parts)
    return state


def _score_bound(q_gain, k_gain):
    bound = 1.02 * HEAD_DIM * SCALE * LOG2E * jnp.max(jnp.abs(q_gain)) * jnp.max(jnp.abs(k_gain))
    return bound.astype(BF16).astype(F32).reshape(1)


def _merge_heads(outs):
    lane = lax.broadcasted_iota(jnp.int32, outs[0].shape, 1)
    return jnp.where(lane < HEAD_DIM, outs[0], outs[1])


def _seq_mixer_specs(*, mixer, nq, tq, seq):
    def colblk(section, p):
        return PROJ_BLOCK_POS[(section, mixer)] * PAIRS_PER_MIXER + p

    q_spec = pl.BlockSpec((tq, LANES), lambda b, p, i: (b * nq + i, colblk(0, p)))
    k_spec = pl.BlockSpec((seq, LANES), lambda b, p, i: (b, colblk(1, p)))
    v_spec = pl.BlockSpec((seq, LANES), lambda b, p, i: (b, colblk(2, p)))
    o_spec = pl.BlockSpec((tq, LANES), lambda b, p, i: (b * nq + i, p))
    return q_spec, k_spec, v_spec, o_spec


_SEQ_PARAMS = pltpu.CompilerParams(
    dimension_semantics=("arbitrary",) * 3, vmem_limit_bytes=VMEM_LIMIT)


def _fox_augment(x, cum, head, h, key_side, bound):
    lane = lax.broadcasted_iota(jnp.int32, x.shape, 1)
    g = jnp.broadcast_to(_lane_column(cum, head) * LOG2E, x.shape)
    g1, g2, g3 = (piece.astype(F32) for piece in _split3(g))
    one = jnp.ones_like(g1)
    pieces = (one, one, one, -g1, -g2, -g3, one) if key_side else (g1, g2, g3, one, one, one, -bound * one)
    base = HEAD_DIM * (1 - h)
    aug = jnp.zeros_like(g1)
    for n, piece in enumerate(pieces):
        aug = jnp.where(lane == base + n, piece, aug)
    return jnp.where(_head_lane_mask(x.shape, h), x.astype(F32), aug).astype(BF16)


def _fox_body(fend_ref, bound_ref, q_ref, k_ref, v_ref, cumq_ref, cumk_ref, o_ref, kaug_scr, vaug_scr, *,
              tq, tk, seq):
    b = pl.program_id(0)
    p = pl.program_id(1)
    iq = pl.program_id(2)
    bound = bound_ref[0]

    @pl.when(iq == 0)
    def _augment_keys():
        def chunk(c, _):
            rows = pl.ds(pl.multiple_of(c * tk, tk), tk)
            for h in range(2):
                kaug_scr[h, rows, :] = _fox_augment(k_ref[rows, :], cumk_ref[rows, :], 2 * p + h, h, True, bound)
                vaug_scr[h, rows, :] = _augment_values(v_ref[rows, :], h)
            return 0
        lax.fori_loop(0, seq // tk, chunk, 0)

    q = q_ref[...]
    cumq = cumq_ref[...]
    q_aug = [_fox_augment(q, cumq, 2 * p + h, h, False, bound) for h in range(2)]

    first_past = iq - 1
    last = jnp.maximum(first_past, 0)

    def reaches(j, heads):
        j = jnp.maximum(j, 0)
        alive = [LOG2E * (fend_ref[b, 2 * p + h, last] - fend_ref[b, 2 * p + h, j])
                 + 2.0 * bound > UNDERFLOW_LOG2 for h in heads]
        return functools.reduce(jnp.logical_and, alive)

    def attend(scheme):
        init, step, finish, diagonal_parts = scheme

        def tile(j, state, heads=(0, 1)):
            rows = pl.ds(pl.multiple_of(j * tk, tk), tk)
            return _attend_tile(q_aug, lambda h: kaug_scr[h, rows, :], lambda h: vaug_scr[h, rows, :], state,
                                step, heads)

        def diagonal_part(state, key0, nkeys):
            rows = pl.ds(pl.multiple_of(iq * tk + key0, nkeys), nkeys)
            return _attend_tile(q_aug, lambda h: kaug_scr[h, rows, :], lambda h: vaug_scr[h, rows, :], state,
                                step, first_row=key0)

        loop = (first_past, _attend_diagonal(diagonal_part, tq, init, diagonal_parts))
        for heads in ((0, 1), (0,), (1,)):
            loop = lax.while_loop(
                lambda lp: jnp.logical_and(lp[0] >= 0, reaches(lp[0], heads)),
                lambda lp: (lp[0] - 1, tile(lp[0], lp[1], heads)),
                loop)
        o_ref[...] = _merge_heads([finish(st, h) for h, st in enumerate(loop[1])]).astype(BF16)

    shifted_ok = 2.0 * bound <= SHIFTED_MAX_SPAN
    pl.when(shifted_ok)(functools.partial(attend, _SHIFTED))
    pl.when(jnp.logical_not(shifted_ok))(functools.partial(attend, _ONLINE))


def _fox(proj, cum, tile_end_sums, score_bound, *, bsz, seq, tq, tk):
    rows = proj.shape[0]
    nq = seq // tq
    q_spec, k_spec, v_spec, o_spec = _seq_mixer_specs(mixer=3, nq=nq, tq=tq, seq=seq)
    smem = pl.BlockSpec(memory_space=pltpu.SMEM)
    return pl.pallas_call(
        functools.partial(_fox_body, tq=tq, tk=tk, seq=seq),
        grid=(bsz, PAIRS_PER_MIXER, nq),
        in_specs=[
            smem, smem, q_spec, k_spec, v_spec,
            pl.BlockSpec((tq, LANES), lambda b, p, i: (b * nq + i, 0)),
            pl.BlockSpec((seq, LANES), lambda b, p, i: (b, 0)),
        ],
        out_specs=o_spec,
        out_shape=jax.ShapeDtypeStruct((rows, MIXER_WIDTH), BF16),
        scratch_shapes=[pltpu.VMEM((2, seq, LANES), BF16), pltpu.VMEM((2, seq, LANES), BF16)],
        compiler_params=_SEQ_PARAMS,
        name="fox",
    )(tile_end_sums, score_bound, proj, proj, proj, cum, cum)


def _moba_body(bound_ref, q_ref, k_ref, v_ref, o_ref, kmean_scr, kaug_scr, vaug_scr, *, tq, tk, seq):
    iq = pl.program_id(2)
    nblk = seq // MOBA_BLOCK
    blocks_per_tile = tq // MOBA_BLOCK
    bound = bound_ref[0]

    @pl.when(iq == 0)
    def _prepare_keys():
        r = lax.broadcasted_iota(jnp.int32, (LANES, seq), 0)
        c = lax.broadcasted_iota(jnp.int32, (LANES, seq), 1)
        member = jnp.where(c // MOBA_BLOCK == r % HEAD_DIM, 1.0 / MOBA_BLOCK, 0.0).astype(BF16)
        hi, lo = _split2(_dot(member, k_ref[...]))
        kmean_scr[...] = jnp.concatenate([hi, lo], axis=1)

        def chunk(c, _):
            rows = pl.ds(pl.multiple_of(c * tk, tk), tk)
            k = k_ref[rows, :].astype(F32)
            lane = lax.broadcasted_iota(jnp.int32, (tk, LANES), 1)
            blk = (c * tk + lax.broadcasted_iota(jnp.int32, (tk, LANES), 0)) // MOBA_BLOCK
            for h in range(2):
                spare = lane - HEAD_DIM * (1 - h)
                onehot = jnp.where(jnp.logical_or(spare == blk, spare == HEAD_DIM - 1), 1.0, 0.0)
                kaug_scr[h, rows, :] = jnp.where(_head_lane_mask(k.shape, h), k, onehot).astype(BF16)
                vaug_scr[h, rows, :] = _augment_values(v_ref[rows, :], h)
            return 0
        lax.fori_loop(0, seq // tk, chunk, 0)

    q = q_ref[...]
    nslot = -(-nblk // 8) * 8
    blk = lax.broadcasted_iota(jnp.int32, (nslot, tq), 0)
    qblk = blocks_per_tile * iq + lax.broadcasted_iota(jnp.int32, (nslot, tq), 1) // MOBA_BLOCK
    q_aug = []
    tail_row = lax.broadcasted_iota(jnp.int32, (HEAD_DIM - nslot, tq), 0)
    tail = jnp.where(tail_row == HEAD_DIM - nslot - 1, -bound, 0.0)
    for h in range(2):
        qh = _own_head(q, h)
        spare = slice(HEAD_DIM * (1 - h), HEAD_DIM * (1 - h) + nslot)
        gate = _dot_nt(kmean_scr[spare, :], jnp.concatenate([qh, qh], axis=1))
        past = blk < qblk
        work = jnp.where(past, gate, NEG)
        bias = jnp.where(blk == qblk, 0.0, NEG)
        for _ in range(min(MOBA_TOPK, nblk)):
            best = jnp.max(work, axis=0, keepdims=True)
            idx = jnp.min(jnp.where(work == best, blk, nslot), axis=0, keepdims=True)
            hit = blk == idx
            bias = jnp.where(hit, jnp.where(past, 0.0, bias), bias)
            work = jnp.where(hit, -jnp.inf, work)
        own_half = jnp.zeros((HEAD_DIM, tq), F32)
        halves = [bias, tail, own_half] if h == 1 else [own_half, bias, tail]
        q_aug.append(jnp.where(_head_lane_mask(q.shape, h), q.astype(F32),
                               jnp.concatenate(halves, axis=0).T).astype(BF16))

    def attend(scheme):
        init, step, finish, diagonal_parts = scheme

        def tile(j, state):
            rows = pl.ds(pl.multiple_of(j * tk, tk), tk)
            return _attend_tile(q_aug, lambda h: kaug_scr[h, rows, :], lambda h: vaug_scr[h, rows, :], state, step)

        def diagonal_part(state, key0, nkeys):
            rows = pl.ds(pl.multiple_of(iq * tk + key0, nkeys), nkeys)
            return _attend_tile(q_aug, lambda h: kaug_scr[h, rows, :], lambda h: vaug_scr[h, rows, :], state,
                                step, first_row=key0)

        state = _attend_diagonal(diagonal_part, tq, init, diagonal_parts)
        state = lax.fori_loop(0, iq, tile, state)
        o_ref[...] = _merge_heads([finish(st, h) for h, st in enumerate(state)]).astype(BF16)

    shifted_ok = 2.0 * bound <= SHIFTED_MAX_SPAN
    pl.when(shifted_ok)(functools.partial(attend, _SHIFTED))
    pl.when(jnp.logical_not(shifted_ok))(functools.partial(attend, _ONLINE))


def _moba(proj, score_bound, *, bsz, seq, tq, tk):
    assert tq == tk and tq % MOBA_BLOCK == 0
    assert -(-(seq // MOBA_BLOCK) // 8) * 8 < HEAD_DIM
    rows = proj.shape[0]
    nq = seq // tq
    q_spec, k_spec, v_spec, o_spec = _seq_mixer_specs(mixer=2, nq=nq, tq=tq, seq=seq)
    return pl.pallas_call(
        functools.partial(_moba_body, tq=tq, tk=tk, seq=seq),
        grid=(bsz, PAIRS_PER_MIXER, nq),
        in_specs=[pl.BlockSpec(memory_space=pltpu.SMEM), q_spec, k_spec, v_spec],
        out_specs=o_spec,
        out_shape=jax.ShapeDtypeStruct((rows, MIXER_WIDTH), BF16),
        scratch_shapes=[pltpu.VMEM((LANES, 2 * LANES), BF16), pltpu.VMEM((2, seq, LANES), BF16),
                        pltpu.VMEM((2, seq, LANES), BF16)],
        compiler_params=_SEQ_PARAMS,
        name="moba",
    )(score_bound, proj, proj, proj)


def _stick_body(q_ref, k_ref, v_ref, o_ref, *, tq, tk):
    iq = pl.program_id(2)
    groups = [slice(g * tk, (g + 1) * tk) for g in range(tq // tk)]
    q = q_ref[...]
    q_own = [_own_head(q, h) for h in range(2)]
    later_row = lax.broadcasted_iota(jnp.int32, (tk, tk), 0)
    later_col = lax.broadcasted_iota(jnp.int32, (tk, tk), 1)
    later = jnp.where(later_row > later_col, 1.0, 0.0).astype(BF16)

    def attend(z, weigh_values, carry_sum, strictly_past=None):
        softplus = jnp.maximum(z, 0.0) + jnp.log2(1.0 + jnp.exp2(-jnp.abs(z)))
        log_keep = -softplus
        if strictly_past is not None:
            log_keep = jnp.where(strictly_past, log_keep, 0.0)
        afters = []
        newer = carry_sum
        for c in reversed(range(z.shape[1] // tk)):
            blk = log_keep[:, c * tk:(c + 1) * tk]
            hi, lo = _split2(blk)
            afters.append(_dot(hi, later) + _dot(lo, later) + newer)
            newer = newer + jnp.sum(blk, axis=1, keepdims=True)
        log_a = z - softplus + jnp.concatenate(afters[::-1], axis=1)
        if strictly_past is not None:
            log_a = jnp.where(strictly_past, log_a, NEG)
        return newer, weigh_values(jnp.exp2(log_a).astype(BF16))

    first_blk, k_win, v_win, masks = [], [], [], []
    row = lax.broadcasted_iota(jnp.int32, (tk, 2 * tk), 0)
    col = lax.broadcasted_iota(jnp.int32, (tk, 2 * tk), 1)
    for g in range(len(groups)):
        own_blk = iq * len(groups) + g
        first_blk.append(jnp.maximum(own_blk - 1, 0))
        keys = pl.ds(pl.multiple_of(first_blk[g] * tk, tk), 2 * tk)
        k_win.append(k_ref[keys, :])
        v_win.append(v_ref[keys, :])
        masks.append(col + (first_blk[g] - own_blk) * tk < row)
    strictly_past = jnp.concatenate(masks, axis=0)
    state = []
    for h in range(2):
        z = jnp.concatenate([_dot_nt(q_own[h][rows], k_win[g]) for g, rows in enumerate(groups)], axis=0)
        state.append(attend(
            z, lambda a: jnp.concatenate([_dot(a[rows], v_win[g]) for g, rows in enumerate(groups)], axis=0),
            jnp.zeros((tq, 1), F32), strictly_past))

    outs = [[None] * len(groups) for _ in range(2)]
    for g, rows in enumerate(groups):
        def cond(loop):
            j, tail = loop
            alive = jnp.max(jnp.maximum(tail[0][0], tail[1][0])) > UNDERFLOW_LOG2
            return jnp.logical_and(j >= 0, alive)

        def body(loop):
            j, tail = loop
            keys = pl.ds(pl.multiple_of(j * tk, tk), tk)
            k_blk = k_ref[keys, :]
            v_blk = v_ref[keys, :]
            new = []
            for h in range(2):
                carry_sum, acc = tail[h]
                carry_sum, add = attend(_dot_nt(q_own[h][rows], k_blk), lambda a: _dot(a, v_blk), carry_sum)
                new.append((carry_sum, acc + add))
            return j - 1, tuple(new)

        _, tail = lax.while_loop(cond, body, (first_blk[g] - 1, tuple((s[rows], a[rows]) for s, a in state)))
        for h in range(2):
            outs[h][g] = tail[h][1]
    o_ref[...] = _merge_heads([jnp.concatenate(outs[h], axis=0) for h in range(2)]).astype(BF16)


def _stick(proj, *, bsz, seq, tq, tk):
    rows = proj.shape[0]
    nq = seq // tq
    q_spec, k_spec, v_spec, o_spec = _seq_mixer_specs(mixer=1, nq=nq, tq=tq, seq=seq)
    return pl.pallas_call(
        functools.partial(_stick_body, tq=tq, tk=tk),
        grid=(bsz, PAIRS_PER_MIXER, nq),
        in_specs=[q_spec, k_spec, v_spec],
        out_specs=o_spec,
        out_shape=jax.ShapeDtypeStruct((rows, MIXER_WIDTH), BF16),
        compiler_params=_SEQ_PARAMS,
        name="stick",
    )(proj, proj, proj)


def _outproj_body(o1_ref, o4_ref, o16_ref, l1_ref, l4_ref, l16_ref, yb_ref, yc_ref, yd_ref,
                  gate_ref, x_ref, w_ref, out_ref, o_scr, l_scr, *, tm):
    def projected(y, mixer):
        cols = slice(mixer * MIXER_WIDTH, (mixer + 1) * MIXER_WIDTH)
        g = gate_ref[:, cols].astype(F32)
        return _dot((y * (g / (1.0 + jnp.exp(-g)))).astype(BF16), w_ref[cols, :])

    acc = x_ref[...]
    for mixer, y_ref in ((1, yb_ref), (2, yc_ref), (3, yd_ref)):
        acc = acc + projected(y_ref[...].astype(F32), mixer)

    for slot, (dil, o_ref, l_ref) in enumerate(zip(STRIDED_DILATIONS, (o4_ref, o16_ref), (l4_ref, l16_ref))):
        n = tm // dil
        for r in range(dil):
            for c in range(LANE_BLOCKS_PER_MIXER):
                cols = slice(r * MIXER_WIDTH + c * LANES, r * MIXER_WIDTH + (c + 1) * LANES)
                o_scr[slot, c, pl.ds(r, n, stride=dil), :] = o_ref[:, cols].astype(F32)
                l_scr[slot, c, pl.ds(r, n, stride=dil), :] = l_ref[:, cols]

    def natural(scr, slot):
        return jnp.concatenate([scr[slot, c] for c in range(LANE_BLOCKS_PER_MIXER)], axis=1)

    l1, l4, l16 = l1_ref[...], natural(l_scr, 0), natural(l_scr, 1)
    m = jnp.maximum(jnp.maximum(l1, l4), l16)
    e1, e4, e16 = jnp.exp2(l1 - m), jnp.exp2(l4 - m), jnp.exp2(l16 - m)
    ya = (e1 * o1_ref[...].astype(F32) + e4 * natural(o_scr, 0)
          + e16 * natural(o_scr, 1)) / (e1 + e4 + e16)
    out_ref[...] = acc + projected(ya, 0)


def _outproj(seg_o, seg_lse, yb, yc, yd, proj, x2, w_out, *, tm):
    rows, d = x2.shape
    tn = d
    row_blk = lambda i, j: (i, 0)
    mix_spec = pl.BlockSpec((tm, MIXER_WIDTH), row_blk)
    seg_specs = [pl.BlockSpec((tm // dil, dil * MIXER_WIDTH), row_blk) for dil in DILATIONS]
    return pl.pallas_call(
        functools.partial(_outproj_body, tm=tm),
        grid=(rows // tm, d // tn),
        in_specs=seg_specs + seg_specs + [mix_spec] * 3 + [
            pl.BlockSpec((tm, MIX_WIDTH), lambda i, j: (i, GATE_GROUP)),
            pl.BlockSpec((tm, tn), lambda i, j: (i, j)),
            pl.BlockSpec((MIX_WIDTH, tn), lambda i, j: (0, j)),
        ],
        out_specs=pl.BlockSpec((tm, tn), lambda i, j: (i, j)),
        out_shape=jax.ShapeDtypeStruct((rows, d), F32),
        scratch_shapes=[pltpu.VMEM((len(STRIDED_DILATIONS), LANE_BLOCKS_PER_MIXER, tm, LANES), F32),
                        pltpu.VMEM((len(STRIDED_DILATIONS), LANE_BLOCKS_PER_MIXER, tm, LANES), F32)],
        compiler_params=pltpu.CompilerParams(
            dimension_semantics=("arbitrary", "arbitrary"), vmem_limit_bytes=VMEM_LIMIT),
        name="outproj",
    )(*seg_o, *seg_lse, yb, yc, yd, proj, x2, w_out)


def _cast_body(w_ref, o_ref):
    o_ref[...] = w_ref[...].astype(BF16)


def _cast_columns(w):
    depth, d, n = w.shape
    step = min(CAST_COLS, n)
    spec = pl.BlockSpec((1, d, step), lambda layer, j: (layer, 0, j))
    return pl.pallas_call(
        _cast_body,
        grid=(depth, n // step),
        in_specs=[spec],
        out_specs=spec,
        out_shape=jax.ShapeDtypeStruct((depth, d, n), BF16),
        compiler_params=pltpu.CompilerParams(
            dimension_semantics=("arbitrary", "arbitrary"), vmem_limit_bytes=VMEM_LIMIT),
        name="cast_w_out",
    )(w)


def _cast_w_in_body(w_ref, main_ref, gate_ref, *, main_steps):
    j = pl.program_id(1)

    @pl.when(j < main_steps)
    def _():
        main_ref[0] = w_ref[...].astype(BF16)

    @pl.when(j == main_steps)
    def _():
        w = w_ref[:, :LANES]
        lane = lax.broadcasted_iota(jnp.int32, w.shape, 1)
        hi, lo = _split2(jnp.where(lane < HEADS_PER_MIXER, w, 0.0))
        gate_ref[0] = jnp.concatenate([hi, lo], axis=1)


def _cast_w_in(w_in):
    depth, d, n = w_in.shape
    assert n == PROJ_COLS + HEADS_PER_MIXER
    main_steps = PROJ_COLS // CAST_COLS
    return pl.pallas_call(
        functools.partial(_cast_w_in_body, main_steps=main_steps),
        grid=(depth, main_steps + 1),
        in_specs=[pl.BlockSpec((d, CAST_COLS), lambda layer, j: (layer, j))],
        out_specs=[
            pl.BlockSpec((1, d, CAST_COLS), lambda layer, j: (layer, 0, jnp.minimum(j, main_steps - 1))),
            pl.BlockSpec((1, d, 2 * LANES), lambda layer, j: (layer, 0, 0)),
        ],
        out_shape=[
            jax.ShapeDtypeStruct((depth, d, PROJ_COLS), BF16),
            jax.ShapeDtypeStruct((depth, d, 2 * LANES), BF16),
        ],
        compiler_params=pltpu.CompilerParams(
            dimension_semantics=("arbitrary", "arbitrary"), vmem_limit_bytes=VMEM_LIMIT),
        name="cast_w_in",
    )(w_in.reshape(depth * d, n))


def _rope_tables(seq):
    inv = 1.0 / (ROPE_THETA ** (jnp.arange(0, HEAD_DIM, 2, dtype=F32) / HEAD_DIM))
    ang = jnp.arange(seq, dtype=F32)[:, None] * inv[None, :]
    cos, sin = jnp.cos(ang), jnp.sin(ang)
    reps = LANES // HEAD_DIM
    cos_t = jnp.tile(jnp.concatenate([cos, cos], axis=1), (1, reps))
    sin_t = jnp.tile(jnp.concatenate([-sin, sin], axis=1), (1, reps))
    return cos_t, sin_t


def _gain_table(qn, kn):
    ones = jnp.ones((HEAD_DIM,), F32)
    per_mixer = {0: (qn[0], ones, qn[1], qn[2]), 1: (kn[0], ones, kn[1], kn[2])}
    blocks = [per_mixer[section][mixer] if section < 2 else ones for section, mixer in PROJ_BLOCK_ORDER]
    tab = jnp.stack([jnp.tile(g.astype(F32), HEADS_PER_MIXER) for g in blocks])
    return tab[:, None, :]


def kernel(x, norm_gain, w_in, q_norm_gain, k_norm_gain, forget_bias, w_out):
    bsz, seq, d = x.shape
    depth = w_in.shape[0]
    rows = bsz * seq
    tm = min(512, seq)
    tq_seq = min(QUERY_TILE, seq)
    cos_t, sin_t = _rope_tables(seq)
    head_of_lane = jnp.arange(MXU_TILE) // HEAD_DIM
    mavg = jnp.where(head_of_lane[:, None] == head_of_lane[None, :], 1.0 / HEAD_DIM, 0.0).astype(BF16)
    tri = jnp.tril(jnp.ones((MXU_TILE, MXU_TILE), F32)).astype(BF16)

    def natural_col(section, r):
        return PROJ_BLOCK_POS[(section, 0)]

    def strided_col(section, r):
        return r

    w_main, wf_cat = _cast_w_in(w_in)
    w_out_bf16 = _cast_columns(w_out)

    x2 = x.reshape(rows, d)
    for layer in range(depth):
        fb_pad = jnp.pad(forget_bias[layer].astype(F32), (0, LANES - HEADS_PER_MIXER))[None, :]
        gain_tab = _gain_table(q_norm_gain[layer], k_norm_gain[layer])

        proj, cum, *strided = _inproj(x2, norm_gain[layer][None, :].astype(F32), w_main[layer], wf_cat[layer],
                                      fb_pad, gain_tab, cos_t, sin_t, mavg, tri, seq=seq, tm=tm)
        seg = [_band_segment(proj, proj, proj, natural_col, bsz=bsz, seq=seq, dil=1, tq=512)]
        for di, dil in enumerate(STRIDED_DILATIONS):
            qd, kd, vd = strided[3 * di:3 * di + 3]
            seg.append(_band_segment(qd, kd, vd, strided_col, bsz=bsz, seq=seq, dil=dil, tq=512))
        yb = _stick(proj, bsz=bsz, seq=seq, tq=min(STICK_QUERY_TILE, seq), tk=STICK_KEY_TILE)
        yc = _moba(proj, _score_bound(q_norm_gain[layer, 1], k_norm_gain[layer, 1]),
                   bsz=bsz, seq=seq, tq=tq_seq, tk=KEY_TILE)
        nt = seq // KEY_TILE
        tile_end_sums = cum.reshape(bsz, nt, KEY_TILE, LANES)[:, :, KEY_TILE - 1, :HEADS_PER_MIXER]
        tile_end_sums = tile_end_sums.transpose(0, 2, 1)
        yd = _fox(proj, cum, tile_end_sums, _score_bound(q_norm_gain[layer, 2], k_norm_gain[layer, 2]),
                  bsz=bsz, seq=seq, tq=tq_seq, tk=KEY_TILE)
        x2 = _outproj([s[0] for s in seg], [s[1] for s in seg], yb, yc, yd, proj, x2,
                      w_out_bf16[layer], tm=min(OUT_ROW_TILE, seq))
    return x2.reshape(bsz, seq, d)
```

```python
import functools

import jax
import jax.numpy as jnp
from jax import lax
from jax.experimental import pallas as pl
from jax.experimental.pallas import tpu as pltpu

F32 = jnp.float32
BF16 = jnp.bfloat16

HEAD_DIM = 64
HALF_DIM = HEAD_DIM // 2
LANES = 128
MXU_TILE = 256
N_MIXERS = 4
HEADS_PER_MIXER = 8
PAIRS_PER_MIXER = HEADS_PER_MIXER // 2
MIXER_WIDTH = HEADS_PER_MIXER * HEAD_DIM
LANE_BLOCKS_PER_MIXER = MIXER_WIDTH // LANES
MIX_WIDTH = N_MIXERS * MIXER_WIDTH
N_SECTIONS = 4
PROJ_COLS = N_SECTIONS * MIX_WIDTH
COL_BLOCKS_PER_SECTION = MIX_WIDTH // LANES
ROPE_THETA = 10000.0
RMS_EPS = 1e-6
SCALE = HEAD_DIM ** -0.5
LOG2E = 1.4426950408889634
NEG = -1e30
DILATIONS = (1, 4, 16)
STRIDED_DILATIONS = DILATIONS[1:]
BAND = 128
MOBA_BLOCK = 256
MOBA_TOPK = 3
QUERY_TILE = 1024
KEY_TILE = 1024
STICK_QUERY_TILE = 1024
STICK_KEY_TILE = 256
COL_BLOCKS_PER_STEP = 4
PROJ_BLOCK_ORDER = ((0, 0), (1, 0), (0, 2), (1, 2),
                    (0, 3), (1, 3), (0, 1), (1, 1),
                    (2, 0), (2, 1), (2, 2), (2, 3),
                    (3, 0), (3, 1), (3, 2), (3, 3))
PROJ_BLOCK_POS = {block: pos for pos, block in enumerate(PROJ_BLOCK_ORDER)}
GATE_GROUP = PROJ_BLOCK_POS[(3, 0)] // N_MIXERS
CAST_COLS = 1024
OUT_ROW_TILE = 256
UNDERFLOW_LOG2 = -160.0
VMEM_LIMIT = 56 * 1024 * 1024

_NT = (((1,), (1,)), ((), ()))


def _dot(a, b):
    return jnp.dot(a, b, preferred_element_type=F32)


def _dot_nt(a, b):
    return lax.dot_general(a, b, _NT, preferred_element_type=F32)


def _split2(x):
    hi = x.astype(BF16)
    lo = (x - hi.astype(F32)).astype(BF16)
    return hi, lo


def _split3(x):
    b1 = x.astype(BF16)
    r1 = x - b1.astype(F32)
    b2 = r1.astype(BF16)
    r2 = r1 - b2.astype(F32)
    return b1, b2, r2.astype(BF16)


def _head_lane_mask(shape, h):
    lane = lax.broadcasted_iota(jnp.int32, shape, len(shape) - 1)
    return (lane >= h * HEAD_DIM) & (lane < (h + 1) * HEAD_DIM)


def _own_head(x, h):
    return jnp.where(_head_lane_mask(x.shape, h), x, jnp.zeros_like(x))


def _lane_column(x, n):
    lane = lax.broadcasted_iota(jnp.int32, x.shape, 1)
    return jnp.sum(jnp.where(lane == n, x, 0.0), axis=1, keepdims=True)


def _inproj_body(x_ref, g_ref, wf_ref, fb_ref, gain_ref, cos_ref, sin_ref, mavg_ref, tri_ref, *rest,
                 blocks_per_batch, tm):
    w_refs, (proj_ref, cum_ref, end_ref), rest = rest[:COL_BLOCKS_PER_STEP], rest[COL_BLOCKS_PER_STEP:][:3], \
        rest[COL_BLOCKS_PER_STEP + 3:]
    dil_refs = rest[:3 * len(STRIDED_DILATIONS)]
    h_scr, carry_scr, dil_scr = rest[3 * len(STRIDED_DILATIONS):]
    i = pl.program_id(0)
    j = pl.program_id(1)

    @pl.when(j == 0)
    def _prologue():
        x = x_ref[...]
        ms = jnp.mean(x * x, axis=-1, keepdims=True)
        h = x * lax.rsqrt(ms + RMS_EPS) * g_ref[...]
        h_hi, h_lo = _split2(h)
        h_scr[...] = h_hi
        wf = wf_ref[...]
        t = _dot(h_hi, wf)
        u = _dot(h_lo, wf[:, :LANES])
        logit = t[:, :LANES] + t[:, LANES:] + u + fb_ref[...]
        lf = jnp.minimum(logit, 0.0) - jnp.log1p(jnp.exp(-jnp.abs(logit)))

        @pl.when(i % blocks_per_batch == 0)
        def _():
            carry_scr[...] = jnp.zeros_like(carry_scr)

        carry = carry_scr[...]
        tri = tri_ref[...]
        sub = tri.shape[0]
        for r in range(tm // sub):
            b1, b2, b3 = _split3(lf[r * sub:(r + 1) * sub])
            c = _dot(tri, jnp.concatenate([b1, b2, b3], axis=1))
            c = c[:, :LANES] + c[:, LANES:2 * LANES] + c[:, 2 * LANES:] + carry
            cum_ref[r * sub:(r + 1) * sub, :] = c
            carry = c[sub - 1:sub, :]
        carry_scr[...] = carry
        end_ref[0] = carry

    for group in range(len(PROJ_BLOCK_ORDER) // COL_BLOCKS_PER_STEP):
        @pl.when(j == group)
        def _():
            for sub in range(COL_BLOCKS_PER_STEP):
                section, mixer = PROJ_BLOCK_ORDER[group * COL_BLOCKS_PER_STEP + sub]
                _inproj_column_block(section, mixer, sub * MIXER_WIDTH, h_scr, w_refs[sub], gain_ref.at[sub],
                                     cos_ref, sin_ref, mavg_ref, proj_ref, dil_refs, dil_scr, tm=tm)


def _inproj_column_block(section, mixer, col0, h_scr, w_ref, gain_ref, cos_ref, sin_ref, mavg_ref, proj_ref,
                         dil_refs, dil_scr, *, tm):
    def out_cols(c):
        return slice(col0 + c * LANES, col0 + (c + 1) * LANES)

    acc = _dot(h_scr[...], w_ref[...])
    normed = section < 2 and mixer != 1
    roped = normed and mixer in (0, 2)
    scale = SCALE * LOG2E if section == 0 else 1.0

    if not normed:
        proj_ref[:, col0:col0 + MIXER_WIDTH] = (acc * scale if section == 0 else acc).astype(BF16)
    else:
        sq = (acc * acc).astype(BF16)
        ms = jnp.concatenate([_dot(sq[:, c:c + MXU_TILE], mavg_ref[...])
                              for c in range(0, MIXER_WIDTH, MXU_TILE)], axis=1)
        t = acc * lax.rsqrt(ms + RMS_EPS) * (gain_ref[0] * scale)
        if not roped:
            proj_ref[:, col0:col0 + MIXER_WIDTH] = t.astype(BF16)
        else:
            cosv = cos_ref[...]
            sinv = sin_ref[...]
            lane = lax.broadcasted_iota(jnp.int32, (tm, LANES), 1)
            first_half = (lane % HEAD_DIM) < HALF_DIM
            for c in range(LANE_BLOCKS_PER_MIXER):
                tc = t[:, c * LANES:(c + 1) * LANES]
                partner = jnp.where(first_half,
                                    pltpu.roll(tc, LANES - HALF_DIM, 1),
                                    pltpu.roll(tc, HALF_DIM, 1))
                proj_ref[:, out_cols(c)] = (tc * cosv + partner * sinv).astype(BF16)

    if mixer == 0 and section < 3:
        for c in range(LANE_BLOCKS_PER_MIXER):
            dil_scr[c] = proj_ref[:, out_cols(c)].astype(F32)
        for di, dil in enumerate(STRIDED_DILATIONS):
            ref = dil_refs[3 * di + section]
            n = tm // dil
            for r in range(dil):
                for c in range(LANE_BLOCKS_PER_MIXER):
                    lo = r * MIXER_WIDTH + c * LANES
                    ref[:, lo:lo + LANES] = dil_scr[c, pl.ds(r, n, stride=dil), :].astype(BF16)


def _inproj(x2, g, w_main, wf_cat, fb_pad, gain_tab, cos_t, sin_t, mavg, tri, *, seq, tm):
    rows, d = x2.shape
    blocks_per_batch = seq // tm
    step_cols = COL_BLOCKS_PER_STEP * MIXER_WIDTH
    body = functools.partial(_inproj_body, blocks_per_batch=blocks_per_batch, tm=tm)

    def weight_block(sub):
        origin = [section * N_MIXERS + mixer for section, mixer in PROJ_BLOCK_ORDER[sub::COL_BLOCKS_PER_STEP]]

        def index(i, j):
            blk = origin[0]
            for group in range(1, len(origin)):
                blk = jnp.where(j == group, origin[group], blk)
            return 0, blk
        return index

    dil_specs, dil_shapes = [], []
    for dil in STRIDED_DILATIONS:
        for _ in range(3):
            dil_specs.append(pl.BlockSpec((tm // dil, dil * MIXER_WIDTH), lambda i, j: (i, 0)))
            dil_shapes.append(jax.ShapeDtypeStruct((rows // dil, dil * MIXER_WIDTH), BF16))
    return pl.pallas_call(
        body,
        grid=(rows // tm, PROJ_COLS // step_cols),
        in_specs=[
            pl.BlockSpec((tm, d), lambda i, j: (i, 0)),
            pl.BlockSpec((1, d), lambda i, j: (0, 0)),
            pl.BlockSpec((d, 2 * LANES), lambda i, j: (0, 0)),
            pl.BlockSpec((1, LANES), lambda i, j: (0, 0)),
            pl.BlockSpec((COL_BLOCKS_PER_STEP, 1, MIXER_WIDTH), lambda i, j: (j, 0, 0)),
            pl.BlockSpec((tm, LANES), lambda i, j: (i % blocks_per_batch, 0)),
            pl.BlockSpec((tm, LANES), lambda i, j: (i % blocks_per_batch, 0)),
            pl.BlockSpec((MXU_TILE, MXU_TILE), lambda i, j: (0, 0)),
            pl.BlockSpec(tri.shape, lambda i, j: (0, 0)),
        ] + [pl.BlockSpec((d, MIXER_WIDTH), weight_block(sub)) for sub in range(COL_BLOCKS_PER_STEP)],
        out_specs=[
            pl.BlockSpec((tm, step_cols), lambda i, j: (i, j)),
            pl.BlockSpec((tm, LANES), lambda i, j: (i, 0)),
            pl.BlockSpec((1, 1, LANES), lambda i, j: (i, 0, 0)),
        ] + dil_specs,
        out_shape=[
            jax.ShapeDtypeStruct((rows, PROJ_COLS), BF16),
            jax.ShapeDtypeStruct((rows, LANES), F32),
            jax.ShapeDtypeStruct((rows // tm, 1, LANES), F32),
        ] + dil_shapes,
        scratch_shapes=[pltpu.VMEM((tm, d), BF16), pltpu.VMEM((1, LANES), F32),
                        pltpu.VMEM((LANE_BLOCKS_PER_MIXER, tm, LANES), F32)],
        compiler_params=pltpu.CompilerParams(
            dimension_semantics=("arbitrary", "arbitrary"), vmem_limit_bytes=VMEM_LIMIT),
        name="inproj",
    )(x2, g, wf_cat, fb_pad, gain_tab, cos_t, sin_t, mavg, tri, *([w_main] * COL_BLOCKS_PER_STEP))


def _band_body(q_ref, k_ref, v_ref, kp_ref, vp_ref, o_ref, lse_ref, *, tq):
    first = pl.program_id(2) == 0
    row = lax.broadcasted_iota(jnp.int32, (BAND, 2 * BAND), 0)
    col = lax.broadcasted_iota(jnp.int32, (BAND, 2 * BAND), 1)
    in_band = jnp.logical_or(jnp.logical_and(col < BAND, col >= row),
                             jnp.logical_and(col >= BAND, col - BAND <= row))
    first_mask = jnp.logical_and(in_band, jnp.logical_or(col >= BAND, jnp.logical_not(first)))
    lane = lax.broadcasted_iota(jnp.int32, (BAND, LANES), 1)
    head0 = lane < HEAD_DIM
    for pair in range(PAIRS_PER_MIXER):
        lanes = slice(pair * LANES, (pair + 1) * LANES)
        for c in range(tq // BAND):
            own = slice(c * BAND, (c + 1) * BAND)
            qc = q_ref[own, lanes]
            if c == 0:
                kw = jnp.concatenate([kp_ref[:, lanes], k_ref[own, lanes]], axis=0)
                vw = jnp.concatenate([vp_ref[:, lanes], v_ref[own, lanes]], axis=0)
                mask = first_mask
            else:
                window = slice((c - 1) * BAND, (c + 1) * BAND)
                kw = k_ref[window, lanes]
                vw = v_ref[window, lanes]
                mask = in_band
            outs, lses = [], []
            for h in range(2):
                s = jnp.where(mask, _dot_nt(_own_head(qc, h), kw), NEG)
                m = jnp.max(s, axis=1, keepdims=True)
                p = jnp.exp2(s - m)
                l = jnp.sum(p, axis=1, keepdims=True)
                outs.append(_dot(p.astype(BF16), vw) / l)
                lses.append(m + jnp.log2(l))
            o_ref[own, lanes] = jnp.where(head0, outs[0], outs[1]).astype(BF16)
            lse_ref[own, lanes] = jnp.where(head0, lses[0], lses[1])


def _band_segment(q_arr, k_arr, v_arr, col_of, *, bsz, seq, dil, tq):
    sub_rows = q_arr.shape[0]
    sub_len = seq // dil
    tq = min(tq, sub_len)
    nq = sub_len // tq
    band_per_tq = tq // BAND

    def cur(section):
        return pl.BlockSpec((tq, MIXER_WIDTH), lambda b, r, i: (b * nq + i, col_of(section, r)))

    def prev(section):
        return pl.BlockSpec(
            (BAND, MIXER_WIDTH),
            lambda b, r, i: (jnp.maximum((b * nq + i) * band_per_tq - 1, 0), col_of(section, r)))

    out_spec = pl.BlockSpec((tq, MIXER_WIDTH), lambda b, r, i: (b * nq + i, r))
    return pl.pallas_call(
        functools.partial(_band_body, tq=tq),
        grid=(bsz, dil, nq),
        in_specs=[cur(0), cur(1), cur(2), prev(1), prev(2)],
        out_specs=[out_spec, out_spec],
        out_shape=[
            jax.ShapeDtypeStruct((sub_rows, dil * MIXER_WIDTH), BF16),
            jax.ShapeDtypeStruct((sub_rows, dil * MIXER_WIDTH), F32),
        ],
        compiler_params=pltpu.CompilerParams(
            dimension_semantics=("arbitrary",) * 3, vmem_limit_bytes=VMEM_LIMIT),
        name=f"band_d{dil}",
    )(q_arr, k_arr, v_arr, k_arr, v_arr)


def _online_init(tq):
    return (jnp.full((tq, 1), NEG, F32), jnp.zeros((tq, 1), F32), jnp.zeros((tq, LANES), F32))


def _online_step(carry, s, v_blk):
    m, l, acc = carry
    m_new = jnp.maximum(m, jnp.max(s, axis=1, keepdims=True))
    alpha = jnp.exp2(m - m_new)
    p = jnp.exp2(s - m_new)
    l = alpha * l + jnp.sum(p, axis=1, keepdims=True)
    acc = alpha * acc + _dot(p.astype(BF16), v_blk)
    return m_new, l, acc


def _online_finish(carry, h):
    _, l, acc = carry
    return acc / l


def _shifted_init(tq):
    return (jnp.zeros((tq, LANES), F32),)


def _shifted_step(carry, s, v_blk):
    acc, = carry
    return (acc + _dot(jnp.exp2(s).astype(BF16), v_blk),)


def _shifted_finish(carry, h):
    acc, = carry
    return acc / _lane_column(acc, _ones_lane(h))


_ONLINE = (_online_init, _online_step, _online_finish, 1)
_SHIFTED = (_shifted_init, _shifted_step, _shifted_finish, 4)
SHIFTED_MAX_SPAN = 100.0


def _ones_lane(h):
    return HEAD_DIM * (1 - h)


def _augment_values(v, h):
    lane = lax.broadcasted_iota(jnp.int32, v.shape, 1)
    ones = jnp.where(lane == _ones_lane(h), 1.0, 0.0)
    return jnp.where(_head_lane_mask(v.shape, h), v.astype(F32), ones).astype(BF16)


def _attend_tile(q_aug, k_tile, v_tile, state, step, heads=(0, 1), first_row=None):
    new = list(state)
    for h in heads:
        start = first_row or 0
        s = _dot_nt(q_aug[h][start:] if start else q_aug[h], k_tile(h))
        if first_row is not None:
            row = lax.broadcasted_iota(jnp.int32, s.shape, 0)
            col = lax.broadcasted_iota(jnp.int32, s.shape, 1)
            s = jnp.where(col <= row, s, NEG)
        if start:
            part = step(tuple(a[start:] for a in state[h]), s, v_tile(h))
            new[h] = tuple(jnp.concatenate([a[:start], b], axis=0) for a, b in zip(state[h], part))
        else:
            new[h] = step(state[h], s, v_tile(h))
    return tuple(new)


def _attend_diagonal(tile_part, tq, init, parts):
    state = (init(tq), init(tq))
    for d in range(parts):
        state = tile_part(state, d * (tq // parts), tq // parts)
    return state


def _score_bound(q_gain, k_gain):
    bound = 1.02 * HEAD_DIM * SCALE * LOG2E * jnp.max(jnp.abs(q_gain)) * jnp.max(jnp.abs(k_gain))
    return bound.astype(BF16).astype(F32).reshape(1)


def _merge_heads(outs):
    lane = lax.broadcasted_iota(jnp.int32, outs[0].shape, 1)
    return jnp.where(lane < HEAD_DIM, outs[0], outs[1])


def _seq_mixer_specs(*, mixer, nq, tq, seq):
    def colblk(section, p):
        return PROJ_BLOCK_POS[(section, mixer)] * PAIRS_PER_MIXER + p

    q_spec = pl.BlockSpec((tq, LANES), lambda b, p, i: (b * nq + i, colblk(0, p)))
    k_spec = pl.BlockSpec((seq, LANES), lambda b, p, i: (b, colblk(1, p)))
    v_spec = pl.BlockSpec((seq, LANES), lambda b, p, i: (b, colblk(2, p)))
    o_spec = pl.BlockSpec((tq, LANES), lambda b, p, i: (b * nq + i, p))
    return q_spec, k_spec, v_spec, o_spec


_SEQ_PARAMS = pltpu.CompilerParams(
    dimension_semantics=("arbitrary",) * 3, vmem_limit_bytes=VMEM_LIMIT)


def _fox_augment(x, cum, head, h, key_side, bound):
    lane = lax.broadcasted_iota(jnp.int32, x.shape, 1)
    g = jnp.broadcast_to(_lane_column(cum, head) * LOG2E, x.shape)
    g1, g2, g3 = (piece.astype(F32) for piece in _split3(g))
    one = jnp.ones_like(g1)
    pieces = (one, one, one, -g1, -g2, -g3, one) if key_side else (g1, g2, g3, one, one, one, -bound * one)
    base = HEAD_DIM * (1 - h)
    aug = jnp.zeros_like(g1)
    for n, piece in enumerate(pieces):
        aug = jnp.where(lane == base + n, piece, aug)
    return jnp.where(_head_lane_mask(x.shape, h), x.astype(F32), aug).astype(BF16)


def _fox_body(fend_ref, bound_ref, q_ref, k_ref, v_ref, cumq_ref, cumk_ref, o_ref, kaug_scr, vaug_scr, *,
              tq, tk, seq):
    b = pl.program_id(0)
    p = pl.program_id(1)
    iq = pl.program_id(2)
    bound = bound_ref[0]

    @pl.when(iq == 0)
    def _augment_keys():
        def chunk(c, _):
            rows = pl.ds(pl.multiple_of(c * tk, tk), tk)
            for h in range(2):
                kaug_scr[h, rows, :] = _fox_augment(k_ref[rows, :], cumk_ref[rows, :], 2 * p + h, h, True, bound)
                vaug_scr[h, rows, :] = _augment_values(v_ref[rows, :], h)
            return 0
        lax.fori_loop(0, seq // tk, chunk, 0)

    q = q_ref[...]
    cumq = cumq_ref[...]
    q_aug = [_fox_augment(q, cumq, 2 * p + h, h, False, bound) for h in range(2)]

    first_past = iq - 1
    last = jnp.maximum(first_past, 0)

    def reaches(j, heads):
        j = jnp.maximum(j, 0)
        alive = [LOG2E * (fend_ref[b, 2 * p + h, last] - fend_ref[b, 2 * p + h, j])
                 + 2.0 * bound > UNDERFLOW_LOG2 for h in heads]
        return functools.reduce(jnp.logical_and, alive)

    def attend(scheme):
        init, step, finish, diagonal_parts = scheme

        def tile(j, state, heads=(0, 1)):
            rows = pl.ds(pl.multiple_of(j * tk, tk), tk)
            return _attend_tile(q_aug, lambda h: kaug_scr[h, rows, :], lambda h: vaug_scr[h, rows, :], state,
                                step, heads)

        def diagonal_part(state, key0, nkeys):
            rows = pl.ds(pl.multiple_of(iq * tk + key0, nkeys), nkeys)
            return _attend_tile(q_aug, lambda h: kaug_scr[h, rows, :], lambda h: vaug_scr[h, rows, :], state,
                                step, first_row=key0)

        loop = (first_past, _attend_diagonal(diagonal_part, tq, init, diagonal_parts))
        for heads in ((0, 1), (0,), (1,)):
            loop = lax.while_loop(
                lambda lp: jnp.logical_and(lp[0] >= 0, reaches(lp[0], heads)),
                lambda lp: (lp[0] - 1, tile(lp[0], lp[1], heads)),
                loop)
        o_ref[...] = _merge_heads([finish(st, h) for h, st in enumerate(loop[1])]).astype(BF16)

    shifted_ok = 2.0 * bound <= SHIFTED_MAX_SPAN
    pl.when(shifted_ok)(functools.partial(attend, _SHIFTED))
    pl.when(jnp.logical_not(shifted_ok))(functools.partial(attend, _ONLINE))


def _fox(proj, cum, tile_end_sums, score_bound, *, bsz, seq, tq, tk):
    rows = proj.shape[0]
    nq = seq // tq
    q_spec, k_spec, v_spec, o_spec = _seq_mixer_specs(mixer=3, nq=nq, tq=tq, seq=seq)
    smem = pl.BlockSpec(memory_space=pltpu.SMEM)
    return pl.pallas_call(
        functools.partial(_fox_body, tq=tq, tk=tk, seq=seq),
        grid=(bsz, PAIRS_PER_MIXER, nq),
        in_specs=[
            smem, smem, q_spec, k_spec, v_spec,
            pl.BlockSpec((tq, LANES), lambda b, p, i: (b * nq + i, 0)),
            pl.BlockSpec((seq, LANES), lambda b, p, i: (b, 0)),
        ],
        out_specs=o_spec,
        out_shape=jax.ShapeDtypeStruct((rows, MIXER_WIDTH), BF16),
        scratch_shapes=[pltpu.VMEM((2, seq, LANES), BF16), pltpu.VMEM((2, seq, LANES), BF16)],
        compiler_params=_SEQ_PARAMS,
        name="fox",
    )(tile_end_sums, score_bound, proj, proj, proj, cum, cum)


def _moba_body(bound_ref, q_ref, k_ref, v_ref, o_ref, kmean_scr, kaug_scr, vaug_scr, *, tq, tk, seq):
    iq = pl.program_id(2)
    nblk = seq // MOBA_BLOCK
    blocks_per_tile = tq // MOBA_BLOCK
    bound = bound_ref[0]

    @pl.when(iq == 0)
    def _prepare_keys():
        r = lax.broadcasted_iota(jnp.int32, (LANES, seq), 0)
        c = lax.broadcasted_iota(jnp.int32, (LANES, seq), 1)
        member = jnp.where(c // MOBA_BLOCK == r % HEAD_DIM, 1.0 / MOBA_BLOCK, 0.0).astype(BF16)
        hi, lo = _split2(_dot(member, k_ref[...]))
        kmean_scr[...] = jnp.concatenate([hi, lo], axis=1)

        def chunk(c, _):
            rows = pl.ds(pl.multiple_of(c * tk, tk), tk)
            k = k_ref[rows, :].astype(F32)
            lane = lax.broadcasted_iota(jnp.int32, (tk, LANES), 1)
            blk = (c * tk + lax.broadcasted_iota(jnp.int32, (tk, LANES), 0)) // MOBA_BLOCK
            for h in range(2):
                spare = lane - HEAD_DIM * (1 - h)
                onehot = jnp.where(jnp.logical_or(spare == blk, spare == HEAD_DIM - 1), 1.0, 0.0)
                kaug_scr[h, rows, :] = jnp.where(_head_lane_mask(k.shape, h), k, onehot).astype(BF16)
                vaug_scr[h, rows, :] = _augment_values(v_ref[rows, :], h)
            return 0
        lax.fori_loop(0, seq // tk, chunk, 0)

    q = q_ref[...]
    nslot = -(-nblk // 8) * 8
    blk = lax.broadcasted_iota(jnp.int32, (nslot, tq), 0)
    qblk = blocks_per_tile * iq + lax.broadcasted_iota(jnp.int32, (nslot, tq), 1) // MOBA_BLOCK
    q_aug = []
    tail_row = lax.broadcasted_iota(jnp.int32, (HEAD_DIM - nslot, tq), 0)
    tail = jnp.where(tail_row == HEAD_DIM - nslot - 1, -bound, 0.0)
    for h in range(2):
        qh = _own_head(q, h)
        spare = slice(HEAD_DIM * (1 - h), HEAD_DIM * (1 - h) + nslot)
        gate = _dot_nt(kmean_scr[spare, :], jnp.concatenate([qh, qh], axis=1))
        past = blk < qblk
        work = jnp.where(past, gate, NEG)
        bias = jnp.where(blk == qblk, 0.0, NEG)
        for _ in range(min(MOBA_TOPK, nblk)):
            best = jnp.max(work, axis=0, keepdims=True)
            idx = jnp.min(jnp.where(work == best, blk, nslot), axis=0, keepdims=True)
            hit = blk == idx
            bias = jnp.where(hit, jnp.where(past, 0.0, bias), bias)
            work = jnp.where(hit, -jnp.inf, work)
        own_half = jnp.zeros((HEAD_DIM, tq), F32)
        halves = [bias, tail, own_half] if h == 1 else [own_half, bias, tail]
        q_aug.append(jnp.where(_head_lane_mask(q.shape, h), q.astype(F32),
                               jnp.concatenate(halves, axis=0).T).astype(BF16))

    def attend(scheme):
        init, step, finish, diagonal_parts = scheme

        def tile(j, state):
            rows = pl.ds(pl.multiple_of(j * tk, tk), tk)
            return _attend_tile(q_aug, lambda h: kaug_scr[h, rows, :], lambda h: vaug_scr[h, rows, :], state, step)

        def diagonal_part(state, key0, nkeys):
            rows = pl.ds(pl.multiple_of(iq * tk + key0, nkeys), nkeys)
            return _attend_tile(q_aug, lambda h: kaug_scr[h, rows, :], lambda h: vaug_scr[h, rows, :], state,
                                step, first_row=key0)

        state = _attend_diagonal(diagonal_part, tq, init, diagonal_parts)
        state = lax.fori_loop(0, iq, tile, state)
        o_ref[...] = _merge_heads([finish(st, h) for h, st in enumerate(state)]).astype(BF16)

    shifted_ok = 2.0 * bound <= SHIFTED_MAX_SPAN
    pl.when(shifted_ok)(functools.partial(attend, _SHIFTED))
    pl.when(jnp.logical_not(shifted_ok))(functools.partial(attend, _ONLINE))


def _moba(proj, score_bound, *, bsz, seq, tq, tk):
    assert tq == tk and tq % MOBA_BLOCK == 0
    assert -(-(seq // MOBA_BLOCK) // 8) * 8 < HEAD_DIM
    rows = proj.shape[0]
    nq = seq // tq
    q_spec, k_spec, v_spec, o_spec = _seq_mixer_specs(mixer=2, nq=nq, tq=tq, seq=seq)
    return pl.pallas_call(
        functools.partial(_moba_body, tq=tq, tk=tk, seq=seq),
        grid=(bsz, PAIRS_PER_MIXER, nq),
        in_specs=[pl.BlockSpec(memory_space=pltpu.SMEM), q_spec, k_spec, v_spec],
        out_specs=o_spec,
        out_shape=jax.ShapeDtypeStruct((rows, MIXER_WIDTH), BF16),
        scratch_shapes=[pltpu.VMEM((LANES, 2 * LANES), BF16), pltpu.VMEM((2, seq, LANES), BF16),
                        pltpu.VMEM((2, seq, LANES), BF16)],
        compiler_params=_SEQ_PARAMS,
        name="moba",
    )(score_bound, proj, proj, proj)


def _stick_body(q_ref, k_ref, v_ref, o_ref, *, tq, tk):
    iq = pl.program_id(2)
    groups = [slice(g * tk, (g + 1) * tk) for g in range(tq // tk)]
    q = q_ref[...]
    q_own = [_own_head(q, h) for h in range(2)]
    later_row = lax.broadcasted_iota(jnp.int32, (tk, tk), 0)
    later_col = lax.broadcasted_iota(jnp.int32, (tk, tk), 1)
    later = jnp.where(later_row > later_col, 1.0, 0.0).astype(BF16)

    def attend(z, weigh_values, carry_sum, strictly_past=None):
        softplus = jnp.maximum(z, 0.0) + jnp.log2(1.0 + jnp.exp2(-jnp.abs(z)))
        log_keep = -softplus
        if strictly_past is not None:
            log_keep = jnp.where(strictly_past, log_keep, 0.0)
        afters = []
        newer = carry_sum
        for c in reversed(range(z.shape[1] // tk)):
            blk = log_keep[:, c * tk:(c + 1) * tk]
            hi, lo = _split2(blk)
            afters.append(_dot(hi, later) + _dot(lo, later) + newer)
            newer = newer + jnp.sum(blk, axis=1, keepdims=True)
        log_a = z - softplus + jnp.concatenate(afters[::-1], axis=1)
        if strictly_past is not None:
            log_a = jnp.where(strictly_past, log_a, NEG)
        return newer, weigh_values(jnp.exp2(log_a).astype(BF16))

    first_blk, k_win, v_win, masks = [], [], [], []
    row = lax.broadcasted_iota(jnp.int32, (tk, 2 * tk), 0)
    col = lax.broadcasted_iota(jnp.int32, (tk, 2 * tk), 1)
    for g in range(len(groups)):
        own_blk = iq * len(groups) + g
        first_blk.append(jnp.maximum(own_blk - 1, 0))
        keys = pl.ds(pl.multiple_of(first_blk[g] * tk, tk), 2 * tk)
        k_win.append(k_ref[keys, :])
        v_win.append(v_ref[keys, :])
        masks.append(col + (first_blk[g] - own_blk) * tk < row)
    strictly_past = jnp.concatenate(masks, axis=0)
    state = []
    for h in range(2):
        z = jnp.concatenate([_dot_nt(q_own[h][rows], k_win[g]) for g, rows in enumerate(groups)], axis=0)
        state.append(attend(
            z, lambda a: jnp.concatenate([_dot(a[rows], v_win[g]) for g, rows in enumerate(groups)], axis=0),
            jnp.zeros((tq, 1), F32), strictly_past))

    outs = [[None] * len(groups) for _ in range(2)]
    for g, rows in enumerate(groups):
        def cond(loop):
            j, tail = loop
            alive = jnp.max(jnp.maximum(tail[0][0], tail[1][0])) > UNDERFLOW_LOG2
            return jnp.logical_and(j >= 0, alive)

        def body(loop):
            j, tail = loop
            keys = pl.ds(pl.multiple_of(j * tk, tk), tk)
            k_blk = k_ref[keys, :]
            v_blk = v_ref[keys, :]
            new = []
            for h in range(2):
                carry_sum, acc = tail[h]
                carry_sum, add = attend(_dot_nt(q_own[h][rows], k_blk), lambda a: _dot(a, v_blk), carry_sum)
                new.append((carry_sum, acc + add))
            return j - 1, tuple(new)

        _, tail = lax.while_loop(cond, body, (first_blk[g] - 1, tuple((s[rows], a[rows]) for s, a in state)))
        for h in range(2):
            outs[h][g] = tail[h][1]
    o_ref[...] = _merge_heads([jnp.concatenate(outs[h], axis=0) for h in range(2)]).astype(BF16)


def _stick(proj, *, bsz, seq, tq, tk):
    rows = proj.shape[0]
    nq = seq // tq
    q_spec, k_spec, v_spec, o_spec = _seq_mixer_specs(mixer=1, nq=nq, tq=tq, seq=seq)
    return pl.pallas_call(
        functools.partial(_stick_body, tq=tq, tk=tk),
        grid=(bsz, PAIRS_PER_MIXER, nq),
        in_specs=[q_spec, k_spec, v_spec],
        out_specs=o_spec,
        out_shape=jax.ShapeDtypeStruct((rows, MIXER_WIDTH), BF16),
        compiler_params=_SEQ_PARAMS,
        name="stick",
    )(proj, proj, proj)


def _outproj_body(o1_ref, o4_ref, o16_ref, l1_ref, l4_ref, l16_ref, yb_ref, yc_ref, yd_ref,
                  gate_ref, x_ref, w_ref, out_ref, o_scr, l_scr, *, tm):
    def projected(y, mixer):
        cols = slice(mixer * MIXER_WIDTH, (mixer + 1) * MIXER_WIDTH)
        g = gate_ref[:, cols].astype(F32)
        return _dot((y * (g / (1.0 + jnp.exp(-g)))).astype(BF16), w_ref[cols, :])

    acc = x_ref[...]
    for mixer, y_ref in ((1, yb_ref), (2, yc_ref), (3, yd_ref)):
        acc = acc + projected(y_ref[...].astype(F32), mixer)

    for slot, (dil, o_ref, l_ref) in enumerate(zip(STRIDED_DILATIONS, (o4_ref, o16_ref), (l4_ref, l16_ref))):
        n = tm // dil
        for r in range(dil):
            for c in range(LANE_BLOCKS_PER_MIXER):
                cols = slice(r * MIXER_WIDTH + c * LANES, r * MIXER_WIDTH + (c + 1) * LANES)
                o_scr[slot, c, pl.ds(r, n, stride=dil), :] = o_ref[:, cols].astype(F32)
                l_scr[slot, c, pl.ds(r, n, stride=dil), :] = l_ref[:, cols]

    def natural(scr, slot):
        return jnp.concatenate([scr[slot, c] for c in range(LANE_BLOCKS_PER_MIXER)], axis=1)

    l1, l4, l16 = l1_ref[...], natural(l_scr, 0), natural(l_scr, 1)
    m = jnp.maximum(jnp.maximum(l1, l4), l16)
    e1, e4, e16 = jnp.exp2(l1 - m), jnp.exp2(l4 - m), jnp.exp2(l16 - m)
    ya = (e1 * o1_ref[...].astype(F32) + e4 * natural(o_scr, 0)
          + e16 * natural(o_scr, 1)) / (e1 + e4 + e16)
    out_ref[...] = acc + projected(ya, 0)


def _outproj(seg_o, seg_lse, yb, yc, yd, proj, x2, w_out, *, tm):
    rows, d = x2.shape
    tn = d
    row_blk = lambda i, j: (i, 0)
    mix_spec = pl.BlockSpec((tm, MIXER_WIDTH), row_blk)
    seg_specs = [pl.BlockSpec((tm // dil, dil * MIXER_WIDTH), row_blk) for dil in DILATIONS]
    return pl.pallas_call(
        functools.partial(_outproj_body, tm=tm),
        grid=(rows // tm, d // tn),
        in_specs=seg_specs + seg_specs + [mix_spec] * 3 + [
            pl.BlockSpec((tm, MIX_WIDTH), lambda i, j: (i, GATE_GROUP)),
            pl.BlockSpec((tm, tn), lambda i, j: (i, j)),
            pl.BlockSpec((MIX_WIDTH, tn), lambda i, j: (0, j)),
        ],
        out_specs=pl.BlockSpec((tm, tn), lambda i, j: (i, j)),
        out_shape=jax.ShapeDtypeStruct((rows, d), F32),
        scratch_shapes=[pltpu.VMEM((len(STRIDED_DILATIONS), LANE_BLOCKS_PER_MIXER, tm, LANES), F32),
                        pltpu.VMEM((len(STRIDED_DILATIONS), LANE_BLOCKS_PER_MIXER, tm, LANES), F32)],
        compiler_params=pltpu.CompilerParams(
            dimension_semantics=("arbitrary", "arbitrary"), vmem_limit_bytes=VMEM_LIMIT),
        name="outproj",
    )(*seg_o, *seg_lse, yb, yc, yd, proj, x2, w_out)


def _cast_body(w_ref, o_ref):
    o_ref[...] = w_ref[...].astype(BF16)


def _cast_columns(w):
    depth, d, n = w.shape
    step = min(CAST_COLS, n)
    spec = pl.BlockSpec((1, d, step), lambda layer, j: (layer, 0, j))
    return pl.pallas_call(
        _cast_body,
        grid=(depth, n // step),
        in_specs=[spec],
        out_specs=spec,
        out_shape=jax.ShapeDtypeStruct((depth, d, n), BF16),
        compiler_params=pltpu.CompilerParams(
            dimension_semantics=("arbitrary", "arbitrary"), vmem_limit_bytes=VMEM_LIMIT),
        name="cast_w_out",
    )(w)


def _cast_w_in_body(w_ref, main_ref, gate_ref, *, main_steps):
    j = pl.program_id(1)

    @pl.when(j < main_steps)
    def _():
        main_ref[0] = w_ref[...].astype(BF16)

    @pl.when(j == main_steps)
    def _():
        w = w_ref[:, :LANES]
        lane = lax.broadcasted_iota(jnp.int32, w.shape, 1)
        hi, lo = _split2(jnp.where(lane < HEADS_PER_MIXER, w, 0.0))
        gate_ref[0] = jnp.concatenate([hi, lo], axis=1)


def _cast_w_in(w_in):
    depth, d, n = w_in.shape
    assert n == PROJ_COLS + HEADS_PER_MIXER
    main_steps = PROJ_COLS // CAST_COLS
    return pl.pallas_call(
        functools.partial(_cast_w_in_body, main_steps=main_steps),
        grid=(depth, main_steps + 1),
        in_specs=[pl.BlockSpec((d, CAST_COLS), lambda layer, j: (layer, j))],
        out_specs=[
            pl.BlockSpec((1, d, CAST_COLS), lambda layer, j: (layer, 0, jnp.minimum(j, main_steps - 1))),
            pl.BlockSpec((1, d, 2 * LANES), lambda layer, j: (layer, 0, 0)),
        ],
        out_shape=[
            jax.ShapeDtypeStruct((depth, d, PROJ_COLS), BF16),
            jax.ShapeDtypeStruct((depth, d, 2 * LANES), BF16),
        ],
        compiler_params=pltpu.CompilerParams(
            dimension_semantics=("arbitrary", "arbitrary"), vmem_limit_bytes=VMEM_LIMIT),
        name="cast_w_in",
    )(w_in.reshape(depth * d, n))


def _rope_tables(seq):
    inv = 1.0 / (ROPE_THETA ** (jnp.arange(0, HEAD_DIM, 2, dtype=F32) / HEAD_DIM))
    ang = jnp.arange(seq, dtype=F32)[:, None] * inv[None, :]
    cos, sin = jnp.cos(ang), jnp.sin(ang)
    reps = LANES // HEAD_DIM
    cos_t = jnp.tile(jnp.concatenate([cos, cos], axis=1), (1, reps))
    sin_t = jnp.tile(jnp.concatenate([-sin, sin], axis=1), (1, reps))
    return cos_t, sin_t


def _gain_table(qn, kn):
    ones = jnp.ones((HEAD_DIM,), F32)
    per_mixer = {0: (qn[0], ones, qn[1], qn[2]), 1: (kn[0], ones, kn[1], kn[2])}
    blocks = [per_mixer[section][mixer] if section < 2 else ones for section, mixer in PROJ_BLOCK_ORDER]
    tab = jnp.stack([jnp.tile(g.astype(F32), HEADS_PER_MIXER) for g in blocks])
    return tab[:, None, :]


def kernel(x, norm_gain, w_in, q_norm_gain, k_norm_gain, forget_bias, w_out):
    bsz, seq, d = x.shape
    depth = w_in.shape[0]
    rows = bsz * seq
    tm = min(512, seq)
    tq_seq = min(QUERY_TILE, seq)
    cos_t, sin_t = _rope_tables(seq)
    head_of_lane = jnp.arange(MXU_TILE) // HEAD_DIM
    mavg = jnp.where(head_of_lane[:, None] == head_of_lane[None, :], 1.0 / HEAD_DIM, 0.0).astype(BF16)
    tri = jnp.tril(jnp.ones((MXU_TILE, MXU_TILE), F32)).astype(BF16)

    def natural_col(section, r):
        return PROJ_BLOCK_POS[(section, 0)]

    def strided_col(section, r):
        return r

    w_main, wf_cat = _cast_w_in(w_in)
    w_out_bf16 = _cast_columns(w_out)

    x2 = x.reshape(rows, d)
    for layer in range(depth):
        fb_pad = jnp.pad(forget_bias[layer].astype(F32), (0, LANES - HEADS_PER_MIXER))[None, :]
        gain_tab = _gain_table(q_norm_gain[layer], k_norm_gain[layer])

        proj, cum, block_end_sums, *strided = _inproj(x2, norm_gain[layer][None, :].astype(F32), w_main[layer], wf_cat[layer],
                                      fb_pad, gain_tab, cos_t, sin_t, mavg, tri, seq=seq, tm=tm)
        seg = [_band_segment(proj, proj, proj, natural_col, bsz=bsz, seq=seq, dil=1, tq=512)]
        for di, dil in enumerate(STRIDED_DILATIONS):
            qd, kd, vd = strided[3 * di:3 * di + 3]
            seg.append(_band_segment(qd, kd, vd, strided_col, bsz=bsz, seq=seq, dil=dil, tq=512))
        yb = _stick(proj, bsz=bsz, seq=seq, tq=min(STICK_QUERY_TILE, seq), tk=STICK_KEY_TILE)
        yc = _moba(proj, _score_bound(q_norm_gain[layer, 1], k_norm_gain[layer, 1]),
                   bsz=bsz, seq=seq, tq=tq_seq, tk=KEY_TILE)
        nt = seq // KEY_TILE
        per_tile = KEY_TILE // tm
        tile_end_sums = block_end_sums.reshape(bsz, nt, per_tile, LANES)[:, :, per_tile - 1, :HEADS_PER_MIXER]
        tile_end_sums = tile_end_sums.transpose(0, 2, 1)
        yd = _fox(proj, cum, tile_end_sums, _score_bound(q_norm_gain[layer, 2], k_norm_gain[layer, 2]),
                  bsz=bsz, seq=seq, tq=tq_seq, tk=KEY_TILE)
        x2 = _outproj([s[0] for s in seg], [s[1] for s in seg], yb, yc, yd, proj, x2,
                      w_out_bf16[layer], tm=min(OUT_ROW_TILE, seq))
    return x2.reshape(bsz, seq, d)
```

```python
import functools

import jax
import jax.numpy as jnp
from jax import lax
from jax.experimental import pallas as pl
from jax.experimental.pallas import tpu as pltpu

F32 = jnp.float32
BF16 = jnp.bfloat16

HEAD_DIM = 64
HALF_DIM = HEAD_DIM // 2
LANES = 128
MXU_TILE = 256
N_MIXERS = 4
HEADS_PER_MIXER = 8
PAIRS_PER_MIXER = HEADS_PER_MIXER // 2
MIXER_WIDTH = HEADS_PER_MIXER * HEAD_DIM
LANE_BLOCKS_PER_MIXER = MIXER_WIDTH // LANES
MIX_WIDTH = N_MIXERS * MIXER_WIDTH
N_SECTIONS = 4
PROJ_COLS = N_SECTIONS * MIX_WIDTH
COL_BLOCKS_PER_SECTION = MIX_WIDTH // LANES
ROPE_THETA = 10000.0
RMS_EPS = 1e-6
SCALE = HEAD_DIM ** -0.5
LOG2E = 1.4426950408889634
NEG = -1e30
DILATIONS = (1, 4, 16)
STRIDED_DILATIONS = DILATIONS[1:]
BAND = 128
MOBA_BLOCK = 256
MOBA_TOPK = 3
QUERY_TILE = 1024
KEY_TILE = 1024
STICK_QUERY_TILE = 1024
STICK_KEY_TILE = 256
COL_BLOCKS_PER_STEP = 4
PROJ_BLOCK_ORDER = ((0, 0), (1, 0), (0, 2), (1, 2),
                    (0, 3), (1, 3), (0, 1), (1, 1),
                    (2, 0), (2, 1), (2, 2), (2, 3),
                    (3, 0), (3, 1), (3, 2), (3, 3))
PROJ_BLOCK_POS = {block: pos for pos, block in enumerate(PROJ_BLOCK_ORDER)}
GATE_GROUP = PROJ_BLOCK_POS[(3, 0)] // N_MIXERS
CAST_COLS = 1024
OUT_ROW_TILE = 256
UNDERFLOW_LOG2 = -160.0
VMEM_LIMIT = 56 * 1024 * 1024

_NT = (((1,), (1,)), ((), ()))


def _dot(a, b):
    return jnp.dot(a, b, preferred_element_type=F32)


def _dot_nt(a, b):
    return lax.dot_general(a, b, _NT, preferred_element_type=F32)


def _split2(x):
    hi = x.astype(BF16)
    lo = (x - hi.astype(F32)).astype(BF16)
    return hi, lo


def _split3(x):
    b1 = x.astype(BF16)
    r1 = x - b1.astype(F32)
    b2 = r1.astype(BF16)
    r2 = r1 - b2.astype(F32)
    return b1, b2, r2.astype(BF16)


def _head_lane_mask(shape, h):
    lane = lax.broadcasted_iota(jnp.int32, shape, len(shape) - 1)
    return (lane >= h * HEAD_DIM) & (lane < (h + 1) * HEAD_DIM)


def _own_head(x, h):
    return jnp.where(_head_lane_mask(x.shape, h), x, jnp.zeros_like(x))


def _lane_column(x, n):
    lane = lax.broadcasted_iota(jnp.int32, x.shape, 1)
    return jnp.sum(jnp.where(lane == n, x, 0.0), axis=1, keepdims=True)


def _inproj_body(x_ref, g_ref, wf_ref, fb_ref, gain_ref, cos_ref, sin_ref, mavg_ref, tri_ref, *rest,
                 blocks_per_batch, tm):
    w_refs, (proj_ref, cum_ref), rest = rest[:COL_BLOCKS_PER_STEP], rest[COL_BLOCKS_PER_STEP:][:2], \
        rest[COL_BLOCKS_PER_STEP + 2:]
    dil_refs = rest[:3 * len(STRIDED_DILATIONS)]
    h_scr, carry_scr, dil_scr = rest[3 * len(STRIDED_DILATIONS):]
    i = pl.program_id(0)
    j = pl.program_id(1)

    @pl.when(j == 0)
    def _prologue():
        x = x_ref[...]
        ms = jnp.mean(x * x, axis=-1, keepdims=True)
        h = x * lax.rsqrt(ms + RMS_EPS) * g_ref[...]
        h_hi, h_lo = _split2(h)
        h_scr[...] = h_hi
        wf = wf_ref[...]
        t = _dot(h_hi, wf)
        u = _dot(h_lo, wf[:, :LANES])
        logit = t[:, :LANES] + t[:, LANES:] + u + fb_ref[...]
        lf = jnp.minimum(logit, 0.0) - jnp.log1p(jnp.exp(-jnp.abs(logit)))

        @pl.when(i % blocks_per_batch == 0)
        def _():
            carry_scr[...] = jnp.zeros_like(carry_scr)

        carry = carry_scr[...]
        tri = tri_ref[...]
        sub = tri.shape[0]
        for r in range(tm // sub):
            b1, b2, b3 = _split3(lf[r * sub:(r + 1) * sub])
            c = _dot(tri, jnp.concatenate([b1, b2, b3], axis=1))
            c = c[:, :LANES] + c[:, LANES:2 * LANES] + c[:, 2 * LANES:] + carry
            cum_ref[r * sub:(r + 1) * sub, :] = c
            carry = c[sub - 1:sub, :]
        carry_scr[...] = carry

    for group in range(len(PROJ_BLOCK_ORDER) // COL_BLOCKS_PER_STEP):
        @pl.when(j == group)
        def _():
            for sub in range(COL_BLOCKS_PER_STEP):
                section, mixer = PROJ_BLOCK_ORDER[group * COL_BLOCKS_PER_STEP + sub]
                _inproj_column_block(section, mixer, sub * MIXER_WIDTH, h_scr, w_refs[sub], gain_ref.at[sub],
                                     cos_ref, sin_ref, mavg_ref, proj_ref, dil_refs, dil_scr, tm=tm)


def _inproj_column_block(section, mixer, col0, h_scr, w_ref, gain_ref, cos_ref, sin_ref, mavg_ref, proj_ref,
                         dil_refs, dil_scr, *, tm):
    def out_cols(c):
        return slice(col0 + c * LANES, col0 + (c + 1) * LANES)

    acc = _dot(h_scr[...], w_ref[...])
    normed = section < 2 and mixer != 1
    roped = normed and mixer in (0, 2)
    scale = SCALE * LOG2E if section == 0 else 1.0

    if not normed:
        proj_ref[:, col0:col0 + MIXER_WIDTH] = (acc * scale if section == 0 else acc).astype(BF16)
    else:
        sq = (acc * acc).astype(BF16)
        ms = jnp.concatenate([_dot(sq[:, c:c + MXU_TILE], mavg_ref[...])
                              for c in range(0, MIXER_WIDTH, MXU_TILE)], axis=1)
        t = acc * lax.rsqrt(ms + RMS_EPS) * (gain_ref[0] * scale)
        if not roped:
            proj_ref[:, col0:col0 + MIXER_WIDTH] = t.astype(BF16)
        else:
            cosv = cos_ref[...]
            sinv = sin_ref[...]
            lane = lax.broadcasted_iota(jnp.int32, (tm, LANES), 1)
            first_half = (lane % HEAD_DIM) < HALF_DIM
            for c in range(LANE_BLOCKS_PER_MIXER):
                tc = t[:, c * LANES:(c + 1) * LANES]
                partner = jnp.where(first_half,
                                    pltpu.roll(tc, LANES - HALF_DIM, 1),
                                    pltpu.roll(tc, HALF_DIM, 1))
                proj_ref[:, out_cols(c)] = (tc * cosv + partner * sinv).astype(BF16)

    if mixer == 0 and section < 3:
        for c in range(LANE_BLOCKS_PER_MIXER):
            dil_scr[c] = proj_ref[:, out_cols(c)].astype(F32)
        for di, dil in enumerate(STRIDED_DILATIONS):
            ref = dil_refs[3 * di + section]
            n = tm // dil
            for r in range(dil):
                for c in range(LANE_BLOCKS_PER_MIXER):
                    lo = r * MIXER_WIDTH + c * LANES
                    ref[:, lo:lo + LANES] = dil_scr[c, pl.ds(r, n, stride=dil), :].astype(BF16)


def _inproj(x2, g, w_main, wf_cat, fb_pad, gain_tab, cos_t, sin_t, mavg, tri, *, seq, tm):
    rows, d = x2.shape
    blocks_per_batch = seq // tm
    step_cols = COL_BLOCKS_PER_STEP * MIXER_WIDTH
    body = functools.partial(_inproj_body, blocks_per_batch=blocks_per_batch, tm=tm)

    def weight_block(sub):
        origin = [section * N_MIXERS + mixer for section, mixer in PROJ_BLOCK_ORDER[sub::COL_BLOCKS_PER_STEP]]

        def index(i, j):
            blk = origin[0]
            for group in range(1, len(origin)):
                blk = jnp.where(j == group, origin[group], blk)
            return 0, blk
        return index

    dil_specs, dil_shapes = [], []
    for dil in STRIDED_DILATIONS:
        for _ in range(3):
            dil_specs.append(pl.BlockSpec((tm // dil, dil * MIXER_WIDTH), lambda i, j: (i, 0)))
            dil_shapes.append(jax.ShapeDtypeStruct((rows // dil, dil * MIXER_WIDTH), BF16))
    return pl.pallas_call(
        body,
        grid=(rows // tm, PROJ_COLS // step_cols),
        in_specs=[
            pl.BlockSpec((tm, d), lambda i, j: (i, 0)),
            pl.BlockSpec((1, d), lambda i, j: (0, 0)),
            pl.BlockSpec((d, 2 * LANES), lambda i, j: (0, 0)),
            pl.BlockSpec((1, LANES), lambda i, j: (0, 0)),
            pl.BlockSpec((COL_BLOCKS_PER_STEP, 1, MIXER_WIDTH), lambda i, j: (j, 0, 0)),
            pl.BlockSpec((tm, LANES), lambda i, j: (i % blocks_per_batch, 0)),
            pl.BlockSpec((tm, LANES), lambda i, j: (i % blocks_per_batch, 0)),
            pl.BlockSpec((MXU_TILE, MXU_TILE), lambda i, j: (0, 0)),
            pl.BlockSpec(tri.shape, lambda i, j: (0, 0)),
        ] + [pl.BlockSpec((d, MIXER_WIDTH), weight_block(sub)) for sub in range(COL_BLOCKS_PER_STEP)],
        out_specs=[
            pl.BlockSpec((tm, step_cols), lambda i, j: (i, j)),
            pl.BlockSpec((tm, LANES), lambda i, j: (i, 0)),
        ] + dil_specs,
        out_shape=[
            jax.ShapeDtypeStruct((rows, PROJ_COLS), BF16),
            jax.ShapeDtypeStruct((rows, LANES), F32),
        ] + dil_shapes,
        scratch_shapes=[pltpu.VMEM((tm, d), BF16), pltpu.VMEM((1, LANES), F32),
                        pltpu.VMEM((LANE_BLOCKS_PER_MIXER, tm, LANES), F32)],
        compiler_params=pltpu.CompilerParams(
            dimension_semantics=("arbitrary", "arbitrary"), vmem_limit_bytes=VMEM_LIMIT),
        name="inproj",
    )(x2, g, wf_cat, fb_pad, gain_tab, cos_t, sin_t, mavg, tri, *([w_main] * COL_BLOCKS_PER_STEP))


def _band_body(q_ref, k_ref, v_ref, kp_ref, vp_ref, o_ref, lse_ref, *, tq):
    first = pl.program_id(2) == 0
    row = lax.broadcasted_iota(jnp.int32, (BAND, 2 * BAND), 0)
    col = lax.broadcasted_iota(jnp.int32, (BAND, 2 * BAND), 1)
    in_band = jnp.logical_or(jnp.logical_and(col < BAND, col >= row),
                             jnp.logical_and(col >= BAND, col - BAND <= row))
    first_mask = jnp.logical_and(in_band, jnp.logical_or(col >= BAND, jnp.logical_not(first)))
    lane = lax.broadcasted_iota(jnp.int32, (BAND, LANES), 1)
    head0 = lane < HEAD_DIM
    for pair in range(PAIRS_PER_MIXER):
        lanes = slice(pair * LANES, (pair + 1) * LANES)
        for c in range(tq // BAND):
            own = slice(c * BAND, (c + 1) * BAND)
            qc = q_ref[own, lanes]
            if c == 0:
                kw = jnp.concatenate([kp_ref[:, lanes], k_ref[own, lanes]], axis=0)
                vw = jnp.concatenate([vp_ref[:, lanes], v_ref[own, lanes]], axis=0)
                mask = first_mask
            else:
                window = slice((c - 1) * BAND, (c + 1) * BAND)
                kw = k_ref[window, lanes]
                vw = v_ref[window, lanes]
                mask = in_band
            outs, lses = [], []
            for h in range(2):
                s = jnp.where(mask, _dot_nt(_own_head(qc, h), kw), NEG)
                m = jnp.max(s, axis=1, keepdims=True)
                p = jnp.exp2(s - m)
                l = jnp.sum(p, axis=1, keepdims=True)
                outs.append(_dot(p.astype(BF16), vw) / l)
                lses.append(m + jnp.log2(l))
            o_ref[own, lanes] = jnp.where(head0, outs[0], outs[1]).astype(BF16)
            lse_ref[own, lanes] = jnp.where(head0, lses[0], lses[1])


def _band_segment(q_arr, k_arr, v_arr, col_of, *, bsz, seq, dil, tq):
    sub_rows = q_arr.shape[0]
    sub_len = seq // dil
    tq = min(tq, sub_len)
    nq = sub_len // tq
    band_per_tq = tq // BAND

    def cur(section):
        return pl.BlockSpec((tq, MIXER_WIDTH), lambda b, r, i: (b * nq + i, col_of(section, r)))

    def prev(section):
        return pl.BlockSpec(
            (BAND, MIXER_WIDTH),
            lambda b, r, i: (jnp.maximum((b * nq + i) * band_per_tq - 1, 0), col_of(section, r)))

    out_spec = pl.BlockSpec((tq, MIXER_WIDTH), lambda b, r, i: (b * nq + i, r))
    return pl.pallas_call(
        functools.partial(_band_body, tq=tq),
        grid=(bsz, dil, nq),
        in_specs=[cur(0), cur(1), cur(2), prev(1), prev(2)],
        out_specs=[out_spec, out_spec],
        out_shape=[
            jax.ShapeDtypeStruct((sub_rows, dil * MIXER_WIDTH), BF16),
            jax.ShapeDtypeStruct((sub_rows, dil * MIXER_WIDTH), F32),
        ],
        compiler_params=pltpu.CompilerParams(
            dimension_semantics=("arbitrary",) * 3, vmem_limit_bytes=VMEM_LIMIT),
        name=f"band_d{dil}",
    )(q_arr, k_arr, v_arr, k_arr, v_arr)


def _online_init(tq):
    return (jnp.full((tq, 1), NEG, F32), jnp.zeros((tq, 1), F32), jnp.zeros((tq, LANES), F32))


def _online_step(carry, s, v_blk):
    m, l, acc = carry
    m_new = jnp.maximum(m, jnp.max(s, axis=1, keepdims=True))
    alpha = jnp.exp2(m - m_new)
    p = jnp.exp2(s - m_new)
    l = alpha * l + jnp.sum(p, axis=1, keepdims=True)
    acc = alpha * acc + _dot(p.astype(BF16), v_blk)
    return m_new, l, acc


def _online_finish(carry, h):
    _, l, acc = carry
    return acc / l


def _shifted_init(tq):
    return (jnp.zeros((tq, LANES), F32),)


def _shifted_step(carry, s, v_blk):
    acc, = carry
    return (acc + _dot(jnp.exp2(s).astype(BF16), v_blk),)


def _shifted_finish(carry, h):
    acc, = carry
    return acc / _lane_column(acc, _ones_lane(h))


_ONLINE = (_online_init, _online_step, _online_finish, 1)
_SHIFTED = (_shifted_init, _shifted_step, _shifted_finish, 4)
SHIFTED_MAX_SPAN = 100.0


def _ones_lane(h):
    return HEAD_DIM * (1 - h)


def _augment_values(v, h):
    lane = lax.broadcasted_iota(jnp.int32, v.shape, 1)
    ones = jnp.where(lane == _ones_lane(h), 1.0, 0.0)
    return jnp.where(_head_lane_mask(v.shape, h), v.astype(F32), ones).astype(BF16)


def _attend_tile(q_aug, k_tile, v_tile, state, step, heads=(0, 1), first_row=None):
    new = list(state)
    for h in heads:
        start = first_row or 0
        s = _dot_nt(q_aug[h][start:] if start else q_aug[h], k_tile(h))
        if first_row is not None:
            row = lax.broadcasted_iota(jnp.int32, s.shape, 0)
            col = lax.broadcasted_iota(jnp.int32, s.shape, 1)
            s = jnp.where(col <= row, s, NEG)
        if start:
            part = step(tuple(a[start:] for a in state[h]), s, v_tile(h))
            new[h] = tuple(jnp.concatenate([a[:start], b], axis=0) for a, b in zip(state[h], part))
        else:
            new[h] = step(state[h], s, v_tile(h))
    return tuple(new)


def _attend_diagonal(tile_part, tq, init, parts):
    state = (init(tq), init(tq))
    for d in range(parts):
        state = tile_part(state, d * (tq // parts), tq // parts)
    return state


def _score_bound(q_gain, k_gain):
    bound = 1.02 * HEAD_DIM * SCALE * LOG2E * jnp.max(jnp.abs(q_gain)) * jnp.max(jnp.abs(k_gain))
    return bound.astype(BF16).astype(F32).reshape(1)


def _merge_heads(outs):
    lane = lax.broadcasted_iota(jnp.int32, outs[0].shape, 1)
    return jnp.where(lane < HEAD_DIM, outs[0], outs[1])


def _seq_mixer_specs(*, mixer, nq, tq, seq):
    def colblk(section, p):
        return PROJ_BLOCK_POS[(section, mixer)] * PAIRS_PER_MIXER + p

    q_spec = pl.BlockSpec((tq, LANES), lambda b, p, i: (b * nq + i, colblk(0, p)))
    k_spec = pl.BlockSpec((seq, LANES), lambda b, p, i: (b, colblk(1, p)))
    v_spec = pl.BlockSpec((seq, LANES), lambda b, p, i: (b, colblk(2, p)))
    o_spec = pl.BlockSpec((tq, LANES), lambda b, p, i: (b * nq + i, p))
    return q_spec, k_spec, v_spec, o_spec


_SEQ_PARAMS = pltpu.CompilerParams(
    dimension_semantics=("arbitrary",) * 3, vmem_limit_bytes=VMEM_LIMIT)


def _fox_augment(x, cum, head, h, key_side, bound):
    lane = lax.broadcasted_iota(jnp.int32, x.shape, 1)
    g = jnp.broadcast_to(_lane_column(cum, head) * LOG2E, x.shape)
    g1, g2, g3 = (piece.astype(F32) for piece in _split3(g))
    one = jnp.ones_like(g1)
    pieces = (one, one, one, -g1, -g2, -g3, one) if key_side else (g1, g2, g3, one, one, one, -bound * one)
    base = HEAD_DIM * (1 - h)
    aug = jnp.zeros_like(g1)
    for n, piece in enumerate(pieces):
        aug = jnp.where(lane == base + n, piece, aug)
    return jnp.where(_head_lane_mask(x.shape, h), x.astype(F32), aug).astype(BF16)


def _fox_body(fend_ref, bound_ref, q_ref, k_ref, v_ref, cumq_ref, cumk_ref, o_ref, kaug_scr, vaug_scr, *,
              tq, tk, seq):
    b = pl.program_id(0)
    p = pl.program_id(1)
    iq = pl.program_id(2)
    bound = bound_ref[0]

    @pl.when(iq == 0)
    def _augment_keys():
        def chunk(c, _):
            rows = pl.ds(pl.multiple_of(c * tk, tk), tk)
            for h in range(2):
                kaug_scr[h, rows, :] = _fox_augment(k_ref[rows, :], cumk_ref[rows, :], 2 * p + h, h, True, bound)
                vaug_scr[h, rows, :] = _augment_values(v_ref[rows, :], h)
            return 0
        lax.fori_loop(0, seq // tk, chunk, 0)

    q = q_ref[...]
    cumq = cumq_ref[...]
    q_aug = [_fox_augment(q, cumq, 2 * p + h, h, False, bound) for h in range(2)]

    first_past = iq - 1
    last = jnp.maximum(first_past, 0)

    def reaches(j, heads):
        j = jnp.maximum(j, 0)
        alive = [LOG2E * (fend_ref[b, 2 * p + h, last] - fend_ref[b, 2 * p + h, j])
                 + 2.0 * bound > UNDERFLOW_LOG2 for h in heads]
        return functools.reduce(jnp.logical_and, alive)

    def attend(scheme):
        init, step, finish, diagonal_parts = scheme

        def tile(j, state, heads=(0, 1)):
            rows = pl.ds(pl.multiple_of(j * tk, tk), tk)
            return _attend_tile(q_aug, lambda h: kaug_scr[h, rows, :], lambda h: vaug_scr[h, rows, :], state,
                                step, heads)

        def diagonal_part(state, key0, nkeys):
            rows = pl.ds(pl.multiple_of(iq * tk + key0, nkeys), nkeys)
            return _attend_tile(q_aug, lambda h: kaug_scr[h, rows, :], lambda h: vaug_scr[h, rows, :], state,
                                step, first_row=key0)

        loop = (first_past, _attend_diagonal(diagonal_part, tq, init, diagonal_parts))
        for heads in ((0, 1), (0,), (1,)):
            loop = lax.while_loop(
                lambda lp: jnp.logical_and(lp[0] >= 0, reaches(lp[0], heads)),
                lambda lp: (lp[0] - 1, tile(lp[0], lp[1], heads)),
                loop)
        o_ref[...] = _merge_heads([finish(st, h) for h, st in enumerate(loop[1])]).astype(BF16)

    shifted_ok = 2.0 * bound <= SHIFTED_MAX_SPAN
    pl.when(shifted_ok)(functools.partial(attend, _SHIFTED))
    pl.when(jnp.logical_not(shifted_ok))(functools.partial(attend, _ONLINE))


def _fox(proj, cum, tile_end_sums, score_bound, *, bsz, seq, tq, tk):
    rows = proj.shape[0]
    nq = seq // tq
    q_spec, k_spec, v_spec, o_spec = _seq_mixer_specs(mixer=3, nq=nq, tq=tq, seq=seq)
    smem = pl.BlockSpec(memory_space=pltpu.SMEM)
    return pl.pallas_call(
        functools.partial(_fox_body, tq=tq, tk=tk, seq=seq),
        grid=(bsz, PAIRS_PER_MIXER, nq),
        in_specs=[
            smem, smem, q_spec, k_spec, v_spec,
            pl.BlockSpec((tq, LANES), lambda b, p, i: (b * nq + i, 0)),
            pl.BlockSpec((seq, LANES), lambda b, p, i: (b, 0)),
        ],
        out_specs=o_spec,
        out_shape=jax.ShapeDtypeStruct((rows, MIXER_WIDTH), BF16),
        scratch_shapes=[pltpu.VMEM((2, seq, LANES), BF16), pltpu.VMEM((2, seq, LANES), BF16)],
        compiler_params=_SEQ_PARAMS,
        name="fox",
    )(tile_end_sums, score_bound, proj, proj, proj, cum, cum)


def _moba_body(bound_ref, q_ref, k_ref, v_ref, o_ref, kmean_scr, kaug_scr, vaug_scr, *, tq, tk, seq):
    iq = pl.program_id(2)
    nblk = seq // MOBA_BLOCK
    blocks_per_tile = tq // MOBA_BLOCK
    bound = bound_ref[0]

    @pl.when(iq == 0)
    def _prepare_keys():
        r = lax.broadcasted_iota(jnp.int32, (LANES, seq), 0)
        c = lax.broadcasted_iota(jnp.int32, (LANES, seq), 1)
        member = jnp.where(c // MOBA_BLOCK == r % HEAD_DIM, 1.0 / MOBA_BLOCK, 0.0).astype(BF16)
        hi, lo = _split2(_dot(member, k_ref[...]))
        kmean_scr[...] = jnp.concatenate([hi, lo], axis=1)

        def chunk(c, _):
            rows = pl.ds(pl.multiple_of(c * tk, tk), tk)
            k = k_ref[rows, :].astype(F32)
            lane = lax.broadcasted_iota(jnp.int32, (tk, LANES), 1)
            blk = (c * tk + lax.broadcasted_iota(jnp.int32, (tk, LANES), 0)) // MOBA_BLOCK
            for h in range(2):
                spare = lane - HEAD_DIM * (1 - h)
                onehot = jnp.where(jnp.logical_or(spare == blk, spare == HEAD_DIM - 1), 1.0, 0.0)
                kaug_scr[h, rows, :] = jnp.where(_head_lane_mask(k.shape, h), k, onehot).astype(BF16)
                vaug_scr[h, rows, :] = _augment_values(v_ref[rows, :], h)
            return 0
        lax.fori_loop(0, seq // tk, chunk, 0)

    q = q_ref[...]
    nslot = -(-nblk // 8) * 8
    blk = lax.broadcasted_iota(jnp.int32, (nslot, tq), 0)
    qblk = blocks_per_tile * iq + lax.broadcasted_iota(jnp.int32, (nslot, tq), 1) // MOBA_BLOCK
    q_aug = []
    tail_row = lax.broadcasted_iota(jnp.int32, (HEAD_DIM - nslot, tq), 0)
    tail = jnp.where(tail_row == HEAD_DIM - nslot - 1, -bound, 0.0)
    for h in range(2):
        qh = _own_head(q, h)
        spare = slice(HEAD_DIM * (1 - h), HEAD_DIM * (1 - h) + nslot)
        gate = _dot_nt(kmean_scr[spare, :], jnp.concatenate([qh, qh], axis=1))
        past = blk < qblk
        work = jnp.where(past, gate, NEG)
        bias = jnp.where(blk == qblk, 0.0, NEG)
        for _ in range(min(MOBA_TOPK, nblk)):
            best = jnp.max(work, axis=0, keepdims=True)
            idx = jnp.min(jnp.where(work == best, blk, nslot), axis=0, keepdims=True)
            hit = blk == idx
            bias = jnp.where(hit, jnp.where(past, 0.0, bias), bias)
            work = jnp.where(hit, -jnp.inf, work)
        own_half = jnp.zeros((HEAD_DIM, tq), F32)
        halves = [bias, tail, own_half] if h == 1 else [own_half, bias, tail]
        q_aug.append(jnp.where(_head_lane_mask(q.shape, h), q.astype(F32),
                               jnp.concatenate(halves, axis=0).T).astype(BF16))

    def attend(scheme):
        init, step, finish, diagonal_parts = scheme

        def tile(j, state):
            rows = pl.ds(pl.multiple_of(j * tk, tk), tk)
            return _attend_tile(q_aug, lambda h: kaug_scr[h, rows, :], lambda h: vaug_scr[h, rows, :], state, step)

        def diagonal_part(state, key0, nkeys):
            rows = pl.ds(pl.multiple_of(iq * tk + key0, nkeys), nkeys)
            return _attend_tile(q_aug, lambda h: kaug_scr[h, rows, :], lambda h: vaug_scr[h, rows, :], state,
                                step, first_row=key0)

        state = _attend_diagonal(diagonal_part, tq, init, diagonal_parts)
        state = lax.fori_loop(0, iq, tile, state)
        o_ref[...] = _merge_heads([finish(st, h) for h, st in enumerate(state)]).astype(BF16)

    shifted_ok = 2.0 * bound <= SHIFTED_MAX_SPAN
    pl.when(shifted_ok)(functools.partial(attend, _SHIFTED))
    pl.when(jnp.logical_not(shifted_ok))(functools.partial(attend, _ONLINE))


def _moba(proj, score_bound, *, bsz, seq, tq, tk):
    assert tq == tk and tq % MOBA_BLOCK == 0
    assert -(-(seq // MOBA_BLOCK) // 8) * 8 < HEAD_DIM
    rows = proj.shape[0]
    nq = seq // tq
    q_spec, k_spec, v_spec, o_spec = _seq_mixer_specs(mixer=2, nq=nq, tq=tq, seq=seq)
    return pl.pallas_call(
        functools.partial(_moba_body, tq=tq, tk=tk, seq=seq),
        grid=(bsz, PAIRS_PER_MIXER, nq),
        in_specs=[pl.BlockSpec(memory_space=pltpu.SMEM), q_spec, k_spec, v_spec],
        out_specs=o_spec,
        out_shape=jax.ShapeDtypeStruct((rows, MIXER_WIDTH), BF16),
        scratch_shapes=[pltpu.VMEM((LANES, 2 * LANES), BF16), pltpu.VMEM((2, seq, LANES), BF16),
                        pltpu.VMEM((2, seq, LANES), BF16)],
        compiler_params=_SEQ_PARAMS,
        name="moba",
    )(score_bound, proj, proj, proj)


def _stick_body(q_ref, k_ref, v_ref, o_ref, *, tq, tk):
    iq = pl.program_id(2)
    groups = [slice(g * tk, (g + 1) * tk) for g in range(tq // tk)]
    q = q_ref[...]
    q_own = [_own_head(q, h) for h in range(2)]
    later_row = lax.broadcasted_iota(jnp.int32, (tk, tk), 0)
    later_col = lax.broadcasted_iota(jnp.int32, (tk, tk), 1)
    later = jnp.where(later_row > later_col, 1.0, 0.0).astype(BF16)
    later2 = jnp.concatenate([later, later], axis=0)

    def attend(z, weigh_values, carry_sum, strictly_past=None):
        softplus = jnp.maximum(z, 0.0) + jnp.log2(1.0 + jnp.exp2(-jnp.abs(z)))
        log_keep = -softplus
        if strictly_past is not None:
            log_keep = jnp.where(strictly_past, log_keep, 0.0)
        afters = []
        newer = carry_sum
        for c in reversed(range(z.shape[1] // tk)):
            blk = log_keep[:, c * tk:(c + 1) * tk]
            hi_f32 = pltpu.bitcast(pltpu.bitcast(blk, jnp.uint32) & jnp.uint32(0xFFFF0000), F32)
            pieces = jnp.concatenate([hi_f32.astype(BF16), (blk - hi_f32).astype(BF16)], axis=1)
            afters.append(_dot(pieces, later2) + newer)
            newer = newer + jnp.sum(blk, axis=1, keepdims=True)
        log_a = z - softplus + jnp.concatenate(afters[::-1], axis=1)
        if strictly_past is not None:
            log_a = jnp.where(strictly_past, log_a, NEG)
        return newer, weigh_values(jnp.exp2(log_a).astype(BF16))

    first_blk, k_win, v_win, masks = [], [], [], []
    row = lax.broadcasted_iota(jnp.int32, (tk, 2 * tk), 0)
    col = lax.broadcasted_iota(jnp.int32, (tk, 2 * tk), 1)
    for g in range(len(groups)):
        own_blk = iq * len(groups) + g
        first_blk.append(jnp.maximum(own_blk - 1, 0))
        keys = pl.ds(pl.multiple_of(first_blk[g] * tk, tk), 2 * tk)
        k_win.append(k_ref[keys, :])
        v_win.append(v_ref[keys, :])
        masks.append(col + (first_blk[g] - own_blk) * tk < row)
    strictly_past = jnp.concatenate(masks, axis=0)
    state = []
    for h in range(2):
        z = jnp.concatenate([_dot_nt(q_own[h][rows], k_win[g]) for g, rows in enumerate(groups)], axis=0)
        state.append(attend(
            z, lambda a: jnp.concatenate([_dot(a[rows], v_win[g]) for g, rows in enumerate(groups)], axis=0),
            jnp.zeros((tq, 1), F32), strictly_past))

    outs = [[None] * len(groups) for _ in range(2)]
    for g, rows in enumerate(groups):
        def cond(loop):
            j, tail = loop
            alive = jnp.max(jnp.maximum(tail[0][0], tail[1][0])) > UNDERFLOW_LOG2
            return jnp.logical_and(j >= 0, alive)

        def body(loop):
            j, tail = loop
            keys = pl.ds(pl.multiple_of(j * tk, tk), tk)
            k_blk = k_ref[keys, :]
            v_blk = v_ref[keys, :]
            new = []
            for h in range(2):
                carry_sum, acc = tail[h]
                carry_sum, add = attend(_dot_nt(q_own[h][rows], k_blk), lambda a: _dot(a, v_blk), carry_sum)
                new.append((carry_sum, acc + add))
            return j - 1, tuple(new)

        _, tail = lax.while_loop(cond, body, (first_blk[g] - 1, tuple((s[rows], a[rows]) for s, a in state)))
        for h in range(2):
            outs[h][g] = tail[h][1]
    o_ref[...] = _merge_heads([jnp.concatenate(outs[h], axis=0) for h in range(2)]).astype(BF16)


def _stick(proj, *, bsz, seq, tq, tk):
    rows = proj.shape[0]
    nq = seq // tq
    q_spec, k_spec, v_spec, o_spec = _seq_mixer_specs(mixer=1, nq=nq, tq=tq, seq=seq)
    return pl.pallas_call(
        functools.partial(_stick_body, tq=tq, tk=tk),
        grid=(bsz, PAIRS_PER_MIXER, nq),
        in_specs=[q_spec, k_spec, v_spec],
        out_specs=o_spec,
        out_shape=jax.ShapeDtypeStruct((rows, MIXER_WIDTH), BF16),
        compiler_params=_SEQ_PARAMS,
        name="stick",
    )(proj, proj, proj)


def _outproj_body(o1_ref, o4_ref, o16_ref, l1_ref, l4_ref, l16_ref, yb_ref, yc_ref, yd_ref,
                  gate_ref, x_ref, w_ref, out_ref, o_scr, l_scr, *, tm):
    def projected(y, mixer):
        cols = slice(mixer * MIXER_WIDTH, (mixer + 1) * MIXER_WIDTH)
        g = gate_ref[:, cols].astype(F32)
        return _dot((y * (g / (1.0 + jnp.exp(-g)))).astype(BF16), w_ref[cols, :])

    acc = x_ref[...]
    for mixer, y_ref in ((1, yb_ref), (2, yc_ref), (3, yd_ref)):
        acc = acc + projected(y_ref[...].astype(F32), mixer)

    for slot, (dil, o_ref, l_ref) in enumerate(zip(STRIDED_DILATIONS, (o4_ref, o16_ref), (l4_ref, l16_ref))):
        n = tm // dil
        for r in range(dil):
            for c in range(LANE_BLOCKS_PER_MIXER):
                cols = slice(r * MIXER_WIDTH + c * LANES, r * MIXER_WIDTH + (c + 1) * LANES)
                o_scr[slot, c, pl.ds(r, n, stride=dil), :] = o_ref[:, cols].astype(F32)
                l_scr[slot, c, pl.ds(r, n, stride=dil), :] = l_ref[:, cols]

    def natural(scr, slot):
        return jnp.concatenate([scr[slot, c] for c in range(LANE_BLOCKS_PER_MIXER)], axis=1)

    l1, l4, l16 = l1_ref[...], natural(l_scr, 0), natural(l_scr, 1)
    m = jnp.maximum(jnp.maximum(l1, l4), l16)
    e1, e4, e16 = jnp.exp2(l1 - m), jnp.exp2(l4 - m), jnp.exp2(l16 - m)
    ya = (e1 * o1_ref[...].astype(F32) + e4 * natural(o_scr, 0)
          + e16 * natural(o_scr, 1)) / (e1 + e4 + e16)
    out_ref[...] = acc + projected(ya, 0)


def _outproj(seg_o, seg_lse, yb, yc, yd, proj, x2, w_out, *, tm):
    rows, d = x2.shape
    tn = d
    row_blk = lambda i, j: (i, 0)
    mix_spec = pl.BlockSpec((tm, MIXER_WIDTH), row_blk)
    seg_specs = [pl.BlockSpec((tm // dil, dil * MIXER_WIDTH), row_blk) for dil in DILATIONS]
    return pl.pallas_call(
        functools.partial(_outproj_body, tm=tm),
        grid=(rows // tm, d // tn),
        in_specs=seg_specs + seg_specs + [mix_spec] * 3 + [
            pl.BlockSpec((tm, MIX_WIDTH), lambda i, j: (i, GATE_GROUP)),
            pl.BlockSpec((tm, tn), lambda i, j: (i, j)),
            pl.BlockSpec((MIX_WIDTH, tn), lambda i, j: (0, j)),
        ],
        out_specs=pl.BlockSpec((tm, tn), lambda i, j: (i, j)),
        out_shape=jax.ShapeDtypeStruct((rows, d), F32),
        scratch_shapes=[pltpu.VMEM((len(STRIDED_DILATIONS), LANE_BLOCKS_PER_MIXER, tm, LANES), F32),
                        pltpu.VMEM((len(STRIDED_DILATIONS), LANE_BLOCKS_PER_MIXER, tm, LANES), F32)],
        compiler_params=pltpu.CompilerParams(
            dimension_semantics=("arbitrary", "arbitrary"), vmem_limit_bytes=VMEM_LIMIT),
        name="outproj",
    )(*seg_o, *seg_lse, yb, yc, yd, proj, x2, w_out)


def _cast_body(w_ref, o_ref):
    o_ref[...] = w_ref[...].astype(BF16)


def _cast_columns(w):
    depth, d, n = w.shape
    step = min(CAST_COLS, n)
    spec = pl.BlockSpec((1, d, step), lambda layer, j: (layer, 0, j))
    return pl.pallas_call(
        _cast_body,
        grid=(depth, n // step),
        in_specs=[spec],
        out_specs=spec,
        out_shape=jax.ShapeDtypeStruct((depth, d, n), BF16),
        compiler_params=pltpu.CompilerParams(
            dimension_semantics=("arbitrary", "arbitrary"), vmem_limit_bytes=VMEM_LIMIT),
        name="cast_w_out",
    )(w)


def _cast_w_in_body(w_ref, main_ref, gate_ref, *, main_steps):
    j = pl.program_id(1)

    @pl.when(j < main_steps)
    def _():
        main_ref[0] = w_ref[...].astype(BF16)

    @pl.when(j == main_steps)
    def _():
        w = w_ref[:, :LANES]
        lane = lax.broadcasted_iota(jnp.int32, w.shape, 1)
        hi, lo = _split2(jnp.where(lane < HEADS_PER_MIXER, w, 0.0))
        gate_ref[0] = jnp.concatenate([hi, lo], axis=1)


def _cast_w_in(w_in):
    depth, d, n = w_in.shape
    assert n == PROJ_COLS + HEADS_PER_MIXER
    main_steps = PROJ_COLS // CAST_COLS
    return pl.pallas_call(
        functools.partial(_cast_w_in_body, main_steps=main_steps),
        grid=(depth, main_steps + 1),
        in_specs=[pl.BlockSpec((d, CAST_COLS), lambda layer, j: (layer, j))],
        out_specs=[
            pl.BlockSpec((1, d, CAST_COLS), lambda layer, j: (layer, 0, jnp.minimum(j, main_steps - 1))),
            pl.BlockSpec((1, d, 2 * LANES), lambda layer, j: (layer, 0, 0)),
        ],
        out_shape=[
            jax.ShapeDtypeStruct((depth, d, PROJ_COLS), BF16),
            jax.ShapeDtypeStruct((depth, d, 2 * LANES), BF16),
        ],
        compiler_params=pltpu.CompilerParams(
            dimension_semantics=("arbitrary", "arbitrary"), vmem_limit_bytes=VMEM_LIMIT),
        name="cast_w_in",
    )(w_in.reshape(depth * d, n))


def _rope_tables(seq):
    inv = 1.0 / (ROPE_THETA ** (jnp.arange(0, HEAD_DIM, 2, dtype=F32) / HEAD_DIM))
    ang = jnp.arange(seq, dtype=F32)[:, None] * inv[None, :]
    cos, sin = jnp.cos(ang), jnp.sin(ang)
    reps = LANES // HEAD_DIM
    cos_t = jnp.tile(jnp.concatenate([cos, cos], axis=1), (1, reps))
    sin_t = jnp.tile(jnp.concatenate([-sin, sin], axis=1), (1, reps))
    return cos_t, sin_t


def _gain_table(qn, kn):
    ones = jnp.ones((HEAD_DIM,), F32)
    per_mixer = {0: (qn[0], ones, qn[1], qn[2]), 1: (kn[0], ones, kn[1], kn[2])}
    blocks = [per_mixer[section][mixer] if section < 2 else ones for section, mixer in PROJ_BLOCK_ORDER]
    tab = jnp.stack([jnp.tile(g.astype(F32), HEADS_PER_MIXER) for g in blocks])
    return tab[:, None, :]


def kernel(x, norm_gain, w_in, q_norm_gain, k_norm_gain, forget_bias, w_out):
    bsz, seq, d = x.shape
    depth = w_in.shape[0]
    rows = bsz * seq
    tm = min(512, seq)
    tq_seq = min(QUERY_TILE, seq)
    cos_t, sin_t = _rope_tables(seq)
    head_of_lane = jnp.arange(MXU_TILE) // HEAD_DIM
    mavg = jnp.where(head_of_lane[:, None] == head_of_lane[None, :], 1.0 / HEAD_DIM, 0.0).astype(BF16)
    tri = jnp.tril(jnp.ones((MXU_TILE, MXU_TILE), F32)).astype(BF16)

    def natural_col(section, r):
        return PROJ_BLOCK_POS[(section, 0)]

    def strided_col(section, r):
        return r

    w_main, wf_cat = _cast_w_in(w_in)
    w_out_bf16 = _cast_columns(w_out)

    x2 = x.reshape(rows, d)
    for layer in range(depth):
        fb_pad = jnp.pad(forget_bias[layer].astype(F32), (0, LANES - HEADS_PER_MIXER))[None, :]
        gain_tab = _gain_table(q_norm_gain[layer], k_norm_gain[layer])

        proj, cum, *strided = _inproj(x2, norm_gain[layer][None, :].astype(F32), w_main[layer], wf_cat[layer],
                                      fb_pad, gain_tab, cos_t, sin_t, mavg, tri, seq=seq, tm=tm)
        seg = [_band_segment(proj, proj, proj, natural_col, bsz=bsz, seq=seq, dil=1, tq=512)]
        for di, dil in enumerate(STRIDED_DILATIONS):
            qd, kd, vd = strided[3 * di:3 * di + 3]
            seg.append(_band_segment(qd, kd, vd, strided_col, bsz=bsz, seq=seq, dil=dil, tq=512))
        yb = _stick(proj, bsz=bsz, seq=seq, tq=min(STICK_QUERY_TILE, seq), tk=STICK_KEY_TILE)
        yc = _moba(proj, _score_bound(q_norm_gain[layer, 1], k_norm_gain[layer, 1]),
                   bsz=bsz, seq=seq, tq=tq_seq, tk=KEY_TILE)
        nt = seq // KEY_TILE
        tile_end_sums = cum.reshape(bsz, nt, KEY_TILE, LANES)[:, :, KEY_TILE - 1, :HEADS_PER_MIXER]
        tile_end_sums = tile_end_sums.transpose(0, 2, 1)
        yd = _fox(proj, cum, tile_end_sums, _score_bound(q_norm_gain[layer, 2], k_norm_gain[layer, 2]),
                  bsz=bsz, seq=seq, tq=tq_seq, tk=KEY_TILE)
        x2 = _outproj([s[0] for s in seg], [s[1] for s in seg], yb, yc, yd, proj, x2,
                      w_out_bf16[layer], tm=min(OUT_ROW_TILE, seq))
    return x2.reshape(bsz, seq, d)
```

```python
import functools

import jax
import jax.numpy as jnp
from jax import lax
from jax.experimental import pallas as pl
from jax.experimental.pallas import tpu as pltpu

F32 = jnp.float32
BF16 = jnp.bfloat16

HEAD_DIM = 64
HALF_DIM = HEAD_DIM // 2
LANES = 128
MXU_TILE = 256
N_MIXERS = 4
HEADS_PER_MIXER = 8
PAIRS_PER_MIXER = HEADS_PER_MIXER // 2
MIXER_WIDTH = HEADS_PER_MIXER * HEAD_DIM
LANE_BLOCKS_PER_MIXER = MIXER_WIDTH // LANES
MIX_WIDTH = N_MIXERS * MIXER_WIDTH
N_SECTIONS = 4
PROJ_COLS = N_SECTIONS * MIX_WIDTH
COL_BLOCKS_PER_SECTION = MIX_WIDTH // LANES
ROPE_THETA = 10000.0
RMS_EPS = 1e-6
SCALE = HEAD_DIM ** -0.5
LOG2E = 1.4426950408889634
NEG = -1e30
DILATIONS = (1, 4, 16)
STRIDED_DILATIONS = DILATIONS[1:]
BAND = 128
MOBA_BLOCK = 256
MOBA_TOPK = 3
QUERY_TILE = 1024
KEY_TILE = 1024
STICK_QUERY_TILE = 1024
STICK_KEY_TILE = 256
COL_BLOCKS_PER_STEP = 4
PROJ_BLOCK_ORDER = ((0, 0), (1, 0), (0, 2), (1, 2),
                    (0, 3), (1, 3), (0, 1), (1, 1),
                    (2, 0), (2, 1), (2, 2), (2, 3),
                    (3, 0), (3, 1), (3, 2), (3, 3))
PROJ_BLOCK_POS = {block: pos for pos, block in enumerate(PROJ_BLOCK_ORDER)}
GATE_GROUP = PROJ_BLOCK_POS[(3, 0)] // N_MIXERS
CAST_COLS = 1024
OUT_ROW_TILE = 256
UNDERFLOW_LOG2 = -160.0
VMEM_LIMIT = 56 * 1024 * 1024

_NT = (((1,), (1,)), ((), ()))


def _dot(a, b):
    return jnp.dot(a, b, preferred_element_type=F32)


def _dot_nt(a, b):
    return lax.dot_general(a, b, _NT, preferred_element_type=F32)


def _split2(x):
    hi = x.astype(BF16)
    lo = (x - hi.astype(F32)).astype(BF16)
    return hi, lo


def _split3(x):
    b1 = x.astype(BF16)
    r1 = x - b1.astype(F32)
    b2 = r1.astype(BF16)
    r2 = r1 - b2.astype(F32)
    return b1, b2, r2.astype(BF16)


def _head_lane_mask(shape, h):
    lane = lax.broadcasted_iota(jnp.int32, shape, len(shape) - 1)
    return (lane >= h * HEAD_DIM) & (lane < (h + 1) * HEAD_DIM)


def _own_head(x, h):
    return jnp.where(_head_lane_mask(x.shape, h), x, jnp.zeros_like(x))


def _lane_column(x, n):
    lane = lax.broadcasted_iota(jnp.int32, x.shape, 1)
    return jnp.sum(jnp.where(lane == n, x, 0.0), axis=1, keepdims=True)


def _inproj_body(x_ref, g_ref, wf_ref, fb_ref, gain_ref, cos_ref, sin_ref, mavg_ref, tri_ref, *rest,
                 blocks_per_batch, tm):
    w_refs, (proj_ref, cum_ref), rest = rest[:COL_BLOCKS_PER_STEP], rest[COL_BLOCKS_PER_STEP:][:2], \
        rest[COL_BLOCKS_PER_STEP + 2:]
    dil_refs = rest[:3 * len(STRIDED_DILATIONS)]
    h_scr, carry_scr, dil_scr = rest[3 * len(STRIDED_DILATIONS):]
    i = pl.program_id(0)
    j = pl.program_id(1)

    @pl.when(j == 0)
    def _prologue():
        x = x_ref[...]
        ms = jnp.mean(x * x, axis=-1, keepdims=True)
        h = x * lax.rsqrt(ms + RMS_EPS) * g_ref[...]
        h_hi, h_lo = _split2(h)
        h_scr[...] = h_hi
        wf = wf_ref[...]
        t = _dot(h_hi, wf)
        u = _dot(h_lo, wf[:, :LANES])
        logit = t[:, :LANES] + t[:, LANES:] + u + fb_ref[...]
        lf = jnp.minimum(logit, 0.0) - jnp.log1p(jnp.exp(-jnp.abs(logit)))

        @pl.when(i % blocks_per_batch == 0)
        def _():
            carry_scr[...] = jnp.zeros_like(carry_scr)

        carry = carry_scr[...]
        tri = tri_ref[...]
        sub = tri.shape[0]
        for r in range(tm // sub):
            b1, b2, b3 = _split3(lf[r * sub:(r + 1) * sub])
            c = _dot(tri, jnp.concatenate([b1, b2, b3], axis=1))
            c = c[:, :LANES] + c[:, LANES:2 * LANES] + c[:, 2 * LANES:] + carry
            cum_ref[r * sub:(r + 1) * sub, :] = c
            carry = c[sub - 1:sub, :]
        carry_scr[...] = carry

    for group in range(len(PROJ_BLOCK_ORDER) // COL_BLOCKS_PER_STEP):
        @pl.when(j == group)
        def _():
            for sub in range(COL_BLOCKS_PER_STEP):
                section, mixer = PROJ_BLOCK_ORDER[group * COL_BLOCKS_PER_STEP + sub]
                _inproj_column_block(section, mixer, sub * MIXER_WIDTH, h_scr, w_refs[sub], gain_ref.at[sub],
                                     cos_ref, sin_ref, mavg_ref, proj_ref, dil_refs, dil_scr, tm=tm)


def _inproj_column_block(section, mixer, col0, h_scr, w_ref, gain_ref, cos_ref, sin_ref, mavg_ref, proj_ref,
                         dil_refs, dil_scr, *, tm):
    def out_cols(c):
        return slice(col0 + c * LANES, col0 + (c + 1) * LANES)

    acc = _dot(h_scr[...], w_ref[...])
    normed = section < 2 and mixer != 1
    roped = normed and mixer in (0, 2)
    scale = SCALE * LOG2E if section == 0 else 1.0

    if not normed:
        proj_ref[:, col0:col0 + MIXER_WIDTH] = (acc * scale if section == 0 else acc).astype(BF16)
    else:
        sq = (acc * acc).astype(BF16)
        ms = jnp.concatenate([_dot(sq[:, c:c + MXU_TILE], mavg_ref[...])
                              for c in range(0, MIXER_WIDTH, MXU_TILE)], axis=1)
        t = acc * lax.rsqrt(ms + RMS_EPS) * (gain_ref[0] * scale)
        if not roped:
            proj_ref[:, col0:col0 + MIXER_WIDTH] = t.astype(BF16)
        else:
            cosv = cos_ref[...]
            sinv = sin_ref[...]
            lane = lax.broadcasted_iota(jnp.int32, (tm, LANES), 1)
            first_half = (lane % HEAD_DIM) < HALF_DIM
            for c in range(LANE_BLOCKS_PER_MIXER):
                tc = t[:, c * LANES:(c + 1) * LANES]
                partner = jnp.where(first_half,
                                    pltpu.roll(tc, LANES - HALF_DIM, 1),
                                    pltpu.roll(tc, HALF_DIM, 1))
                proj_ref[:, out_cols(c)] = (tc * cosv + partner * sinv).astype(BF16)

    if mixer == 0 and section < 3:
        for c in range(LANE_BLOCKS_PER_MIXER):
            dil_scr[c] = proj_ref[:, out_cols(c)].astype(F32)
        for di, dil in enumerate(STRIDED_DILATIONS):
            ref = dil_refs[3 * di + section]
            n = tm // dil
            for r in range(dil):
                for c in range(LANE_BLOCKS_PER_MIXER):
                    lo = r * MIXER_WIDTH + c * LANES
                    ref[:, lo:lo + LANES] = dil_scr[c, pl.ds(r, n, stride=dil), :].astype(BF16)


def _inproj(x2, g, w_main, wf_cat, fb_pad, gain_tab, cos_t, sin_t, mavg, tri, *, seq, tm):
    rows, d = x2.shape
    blocks_per_batch = seq // tm
    step_cols = COL_BLOCKS_PER_STEP * MIXER_WIDTH
    body = functools.partial(_inproj_body, blocks_per_batch=blocks_per_batch, tm=tm)

    def weight_block(sub):
        origin = [section * N_MIXERS + mixer for section, mixer in PROJ_BLOCK_ORDER[sub::COL_BLOCKS_PER_STEP]]

        def index(i, j):
            blk = origin[0]
            for group in range(1, len(origin)):
                blk = jnp.where(j == group, origin[group], blk)
            return 0, blk
        return index

    dil_specs, dil_shapes = [], []
    for dil in STRIDED_DILATIONS:
        for _ in range(3):
            dil_specs.append(pl.BlockSpec((tm // dil, dil * MIXER_WIDTH), lambda i, j: (i, 0)))
            dil_shapes.append(jax.ShapeDtypeStruct((rows // dil, dil * MIXER_WIDTH), BF16))
    return pl.pallas_call(
        body,
        grid=(rows // tm, PROJ_COLS // step_cols),
        in_specs=[
            pl.BlockSpec((tm, d), lambda i, j: (i, 0)),
            pl.BlockSpec((1, d), lambda i, j: (0, 0)),
            pl.BlockSpec((d, 2 * LANES), lambda i, j: (0, 0)),
            pl.BlockSpec((1, LANES), lambda i, j: (0, 0)),
            pl.BlockSpec((COL_BLOCKS_PER_STEP, 1, MIXER_WIDTH), lambda i, j: (j, 0, 0)),
            pl.BlockSpec((tm, LANES), lambda i, j: (i % blocks_per_batch, 0)),
            pl.BlockSpec((tm, LANES), lambda i, j: (i % blocks_per_batch, 0)),
            pl.BlockSpec((MXU_TILE, MXU_TILE), lambda i, j: (0, 0)),
            pl.BlockSpec(tri.shape, lambda i, j: (0, 0)),
        ] + [pl.BlockSpec((d, MIXER_WIDTH), weight_block(sub)) for sub in range(COL_BLOCKS_PER_STEP)],
        out_specs=[
            pl.BlockSpec((tm, step_cols), lambda i, j: (i, j)),
            pl.BlockSpec((tm, LANES), lambda i, j: (i, 0)),
        ] + dil_specs,
        out_shape=[
            jax.ShapeDtypeStruct((rows, PROJ_COLS), BF16),
            jax.ShapeDtypeStruct((rows, LANES), F32),
        ] + dil_shapes,
        scratch_shapes=[pltpu.VMEM((tm, d), BF16), pltpu.VMEM((1, LANES), F32),
                        pltpu.VMEM((LANE_BLOCKS_PER_MIXER, tm, LANES), F32)],
        compiler_params=pltpu.CompilerParams(
            dimension_semantics=("arbitrary", "arbitrary"), vmem_limit_bytes=VMEM_LIMIT),
        name="inproj",
    )(x2, g, wf_cat, fb_pad, gain_tab, cos_t, sin_t, mavg, tri, *([w_main] * COL_BLOCKS_PER_STEP))


def _band_body(q_ref, k_ref, v_ref, kp_ref, vp_ref, o_ref, lse_ref, *, tq):
    first = pl.program_id(2) == 0
    row = lax.broadcasted_iota(jnp.int32, (BAND, 2 * BAND), 0)
    col = lax.broadcasted_iota(jnp.int32, (BAND, 2 * BAND), 1)
    in_band = jnp.logical_or(jnp.logical_and(col < BAND, col >= row),
                             jnp.logical_and(col >= BAND, col - BAND <= row))
    first_mask = jnp.logical_and(in_band, jnp.logical_or(col >= BAND, jnp.logical_not(first)))
    lane = lax.broadcasted_iota(jnp.int32, (BAND, LANES), 1)
    head0 = lane < HEAD_DIM
    for pair in range(PAIRS_PER_MIXER):
        lanes = slice(pair * LANES, (pair + 1) * LANES)
        for c in range(tq // BAND):
            own = slice(c * BAND, (c + 1) * BAND)
            qc = q_ref[own, lanes]
            if c == 0:
                kw = jnp.concatenate([kp_ref[:, lanes], k_ref[own, lanes]], axis=0)
                vw = jnp.concatenate([vp_ref[:, lanes], v_ref[own, lanes]], axis=0)
                mask = first_mask
            else:
                window = slice((c - 1) * BAND, (c + 1) * BAND)
                kw = k_ref[window, lanes]
                vw = v_ref[window, lanes]
                mask = in_band
            outs, lses = [], []
            for h in range(2):
                s = jnp.where(mask, _dot_nt(_own_head(qc, h), kw), NEG)
                m = jnp.max(s, axis=1, keepdims=True)
                p = jnp.exp2(s - m)
                l = jnp.sum(p, axis=1, keepdims=True)
                outs.append(_dot(p.astype(BF16), vw) / l)
                lses.append(m + jnp.log2(l))
            o_ref[own, lanes] = jnp.where(head0, outs[0], outs[1]).astype(BF16)
            lse_ref[own, lanes] = jnp.where(head0, lses[0], lses[1])


def _band_segment(q_arr, k_arr, v_arr, col_of, *, bsz, seq, dil, tq):
    sub_rows = q_arr.shape[0]
    sub_len = seq // dil
    tq = min(tq, sub_len)
    nq = sub_len // tq
    band_per_tq = tq // BAND

    def cur(section):
        return pl.BlockSpec((tq, MIXER_WIDTH), lambda b, r, i: (b * nq + i, col_of(section, r)))

    def prev(section):
        return pl.BlockSpec(
            (BAND, MIXER_WIDTH),
            lambda b, r, i: (jnp.maximum((b * nq + i) * band_per_tq - 1, 0), col_of(section, r)))

    out_spec = pl.BlockSpec((tq, MIXER_WIDTH), lambda b, r, i: (b * nq + i, r))
    return pl.pallas_call(
        functools.partial(_band_body, tq=tq),
        grid=(bsz, dil, nq),
        in_specs=[cur(0), cur(1), cur(2), prev(1), prev(2)],
        out_specs=[out_spec, out_spec],
        out_shape=[
            jax.ShapeDtypeStruct((sub_rows, dil * MIXER_WIDTH), BF16),
            jax.ShapeDtypeStruct((sub_rows, dil * MIXER_WIDTH), F32),
        ],
        compiler_params=pltpu.CompilerParams(
            dimension_semantics=("arbitrary",) * 3, vmem_limit_bytes=VMEM_LIMIT),
        name=f"band_d{dil}",
    )(q_arr, k_arr, v_arr, k_arr, v_arr)


def _online_init(tq):
    return (jnp.full((tq, 1), NEG, F32), jnp.zeros((tq, 1), F32), jnp.zeros((tq, LANES), F32))


def _online_step(carry, s, v_blk):
    m, l, acc = carry
    m_new = jnp.maximum(m, jnp.max(s, axis=1, keepdims=True))
    alpha = jnp.exp2(m - m_new)
    p = jnp.exp2(s - m_new)
    l = alpha * l + jnp.sum(p, axis=1, keepdims=True)
    acc = alpha * acc + _dot(p.astype(BF16), v_blk)
    return m_new, l, acc


def _online_finish(carry, h):
    _, l, acc = carry
    return acc / l


def _shifted_init(tq):
    return (jnp.zeros((tq, LANES), F32),)


def _shifted_step(carry, s, v_blk):
    acc, = carry
    for c in range(0, s.shape[1], 2 * MXU_TILE):
        acc = acc + _dot(jnp.exp2(s[:, c:c + 2 * MXU_TILE]).astype(BF16), v_blk[c:c + 2 * MXU_TILE])
    return (acc,)


def _shifted_finish(carry, h):
    acc, = carry
    return acc / _lane_column(acc, _ones_lane(h))


_ONLINE = (_online_init, _online_step, _online_finish, 1)
_SHIFTED = (_shifted_init, _shifted_step, _shifted_finish, 4)
SHIFTED_MAX_SPAN = 100.0


def _ones_lane(h):
    return HEAD_DIM * (1 - h)


def _augment_values(v, h):
    lane = lax.broadcasted_iota(jnp.int32, v.shape, 1)
    ones = jnp.where(lane == _ones_lane(h), 1.0, 0.0)
    return jnp.where(_head_lane_mask(v.shape, h), v.astype(F32), ones).astype(BF16)


def _attend_tile(q_aug, k_tile, v_tile, state, step, heads=(0, 1), first_row=None):
    new = list(state)
    for h in heads:
        start = first_row or 0
        s = _dot_nt(q_aug[h][start:] if start else q_aug[h], k_tile(h))
        if first_row is not None:
            row = lax.broadcasted_iota(jnp.int32, s.shape, 0)
            col = lax.broadcasted_iota(jnp.int32, s.shape, 1)
            s = jnp.where(col <= row, s, NEG)
        if start:
            part = step(tuple(a[start:] for a in state[h]), s, v_tile(h))
            new[h] = tuple(jnp.concatenate([a[:start], b], axis=0) for a, b in zip(state[h], part))
        else:
            new[h] = step(state[h], s, v_tile(h))
    return tuple(new)


def _attend_diagonal(tile_part, tq, init, parts):
    state = (init(tq), init(tq))
    for d in range(parts):
        state = tile_part(state, d * (tq // parts), tq // parts)
    return state


def _score_bound(q_gain, k_gain):
    bound = 1.02 * HEAD_DIM * SCALE * LOG2E * jnp.max(jnp.abs(q_gain)) * jnp.max(jnp.abs(k_gain))
    return bound.astype(BF16).astype(F32).reshape(1)


def _merge_heads(outs):
    lane = lax.broadcasted_iota(jnp.int32, outs[0].shape, 1)
    return jnp.where(lane < HEAD_DIM, outs[0], outs[1])


def _seq_mixer_specs(*, mixer, nq, tq, seq):
    def colblk(section, p):
        return PROJ_BLOCK_POS[(section, mixer)] * PAIRS_PER_MIXER + p

    q_spec = pl.BlockSpec((tq, LANES), lambda b, p, i: (b * nq + i, colblk(0, p)))
    k_spec = pl.BlockSpec((seq, LANES), lambda b, p, i: (b, colblk(1, p)))
    v_spec = pl.BlockSpec((seq, LANES), lambda b, p, i: (b, colblk(2, p)))
    o_spec = pl.BlockSpec((tq, LANES), lambda b, p, i: (b * nq + i, p))
    return q_spec, k_spec, v_spec, o_spec


_SEQ_PARAMS = pltpu.CompilerParams(
    dimension_semantics=("arbitrary",) * 3, vmem_limit_bytes=VMEM_LIMIT)


def _fox_augment(x, cum, head, h, key_side, bound):
    lane = lax.broadcasted_iota(jnp.int32, x.shape, 1)
    g = jnp.broadcast_to(_lane_column(cum, head) * LOG2E, x.shape)
    g1, g2, g3 = (piece.astype(F32) for piece in _split3(g))
    one = jnp.ones_like(g1)
    pieces = (one, one, one, -g1, -g2, -g3, one) if key_side else (g1, g2, g3, one, one, one, -bound * one)
    base = HEAD_DIM * (1 - h)
    aug = jnp.zeros_like(g1)
    for n, piece in enumerate(pieces):
        aug = jnp.where(lane == base + n, piece, aug)
    return jnp.where(_head_lane_mask(x.shape, h), x.astype(F32), aug).astype(BF16)


def _fox_body(fend_ref, bound_ref, q_ref, k_ref, v_ref, cumq_ref, cumk_ref, o_ref, kaug_scr, vaug_scr, *,
              tq, tk, seq):
    b = pl.program_id(0)
    p = pl.program_id(1)
    iq = pl.program_id(2)
    bound = bound_ref[0]

    @pl.when(iq == 0)
    def _augment_keys():
        def chunk(c, _):
            rows = pl.ds(pl.multiple_of(c * tk, tk), tk)
            for h in range(2):
                kaug_scr[h, rows, :] = _fox_augment(k_ref[rows, :], cumk_ref[rows, :], 2 * p + h, h, True, bound)
                vaug_scr[h, rows, :] = _augment_values(v_ref[rows, :], h)
            return 0
        lax.fori_loop(0, seq // tk, chunk, 0)

    q = q_ref[...]
    cumq = cumq_ref[...]
    q_aug = [_fox_augment(q, cumq, 2 * p + h, h, False, bound) for h in range(2)]

    first_past = iq - 1
    last = jnp.maximum(first_past, 0)

    def reaches(j, heads):
        j = jnp.maximum(j, 0)
        alive = [LOG2E * (fend_ref[b, 2 * p + h, last] - fend_ref[b, 2 * p + h, j])
                 + 2.0 * bound > UNDERFLOW_LOG2 for h in heads]
        return functools.reduce(jnp.logical_and, alive)

    def attend(scheme):
        init, step, finish, diagonal_parts = scheme

        def tile(j, state, heads=(0, 1)):
            rows = pl.ds(pl.multiple_of(j * tk, tk), tk)
            return _attend_tile(q_aug, lambda h: kaug_scr[h, rows, :], lambda h: vaug_scr[h, rows, :], state,
                                step, heads)

        def diagonal_part(state, key0, nkeys):
            rows = pl.ds(pl.multiple_of(iq * tk + key0, nkeys), nkeys)
            return _attend_tile(q_aug, lambda h: kaug_scr[h, rows, :], lambda h: vaug_scr[h, rows, :], state,
                                step, first_row=key0)

        loop = (first_past, _attend_diagonal(diagonal_part, tq, init, diagonal_parts))
        for heads in ((0, 1), (0,), (1,)):
            loop = lax.while_loop(
                lambda lp: jnp.logical_and(lp[0] >= 0, reaches(lp[0], heads)),
                lambda lp: (lp[0] - 1, tile(lp[0], lp[1], heads)),
                loop)
        o_ref[...] = _merge_heads([finish(st, h) for h, st in enumerate(loop[1])]).astype(BF16)

    shifted_ok = 2.0 * bound <= SHIFTED_MAX_SPAN
    pl.when(shifted_ok)(functools.partial(attend, _SHIFTED))
    pl.when(jnp.logical_not(shifted_ok))(functools.partial(attend, _ONLINE))


def _fox(proj, cum, tile_end_sums, score_bound, *, bsz, seq, tq, tk):
    rows = proj.shape[0]
    nq = seq // tq
    q_spec, k_spec, v_spec, o_spec = _seq_mixer_specs(mixer=3, nq=nq, tq=tq, seq=seq)
    smem = pl.BlockSpec(memory_space=pltpu.SMEM)
    return pl.pallas_call(
        functools.partial(_fox_body, tq=tq, tk=tk, seq=seq),
        grid=(bsz, PAIRS_PER_MIXER, nq),
        in_specs=[
            smem, smem, q_spec, k_spec, v_spec,
            pl.BlockSpec((tq, LANES), lambda b, p, i: (b * nq + i, 0)),
            pl.BlockSpec((seq, LANES), lambda b, p, i: (b, 0)),
        ],
        out_specs=o_spec,
        out_shape=jax.ShapeDtypeStruct((rows, MIXER_WIDTH), BF16),
        scratch_shapes=[pltpu.VMEM((2, seq, LANES), BF16), pltpu.VMEM((2, seq, LANES), BF16)],
        compiler_params=_SEQ_PARAMS,
        name="fox",
    )(tile_end_sums, score_bound, proj, proj, proj, cum, cum)


def _moba_body(bound_ref, q_ref, k_ref, v_ref, o_ref, kmean_scr, kaug_scr, vaug_scr, *, tq, tk, seq):
    iq = pl.program_id(2)
    nblk = seq // MOBA_BLOCK
    blocks_per_tile = tq // MOBA_BLOCK
    bound = bound_ref[0]

    @pl.when(iq == 0)
    def _prepare_keys():
        r = lax.broadcasted_iota(jnp.int32, (LANES, seq), 0)
        c = lax.broadcasted_iota(jnp.int32, (LANES, seq), 1)
        member = jnp.where(c // MOBA_BLOCK == r % HEAD_DIM, 1.0 / MOBA_BLOCK, 0.0).astype(BF16)
        hi, lo = _split2(_dot(member, k_ref[...]))
        kmean_scr[...] = jnp.concatenate([hi, lo], axis=1)

        def chunk(c, _):
            rows = pl.ds(pl.multiple_of(c * tk, tk), tk)
            k = k_ref[rows, :].astype(F32)
            lane = lax.broadcasted_iota(jnp.int32, (tk, LANES), 1)
            blk = (c * tk + lax.broadcasted_iota(jnp.int32, (tk, LANES), 0)) // MOBA_BLOCK
            for h in range(2):
                spare = lane - HEAD_DIM * (1 - h)
                onehot = jnp.where(jnp.logical_or(spare == blk, spare == HEAD_DIM - 1), 1.0, 0.0)
                kaug_scr[h, rows, :] = jnp.where(_head_lane_mask(k.shape, h), k, onehot).astype(BF16)
                vaug_scr[h, rows, :] = _augment_values(v_ref[rows, :], h)
            return 0
        lax.fori_loop(0, seq // tk, chunk, 0)

    q = q_ref[...]
    nslot = -(-nblk // 8) * 8
    blk = lax.broadcasted_iota(jnp.int32, (nslot, tq), 0)
    qblk = blocks_per_tile * iq + lax.broadcasted_iota(jnp.int32, (nslot, tq), 1) // MOBA_BLOCK
    q_aug = []
    tail_row = lax.broadcasted_iota(jnp.int32, (HEAD_DIM - nslot, tq), 0)
    tail = jnp.where(tail_row == HEAD_DIM - nslot - 1, -bound, 0.0)
    for h in range(2):
        qh = _own_head(q, h)
        spare = slice(HEAD_DIM * (1 - h), HEAD_DIM * (1 - h) + nslot)
        gate = _dot_nt(kmean_scr[spare, :], jnp.concatenate([qh, qh], axis=1))
        past = blk < qblk
        work = jnp.where(past, gate, NEG)
        bias = jnp.where(blk == qblk, 0.0, NEG)
        for _ in range(min(MOBA_TOPK, nblk)):
            best = jnp.max(work, axis=0, keepdims=True)
            idx = jnp.min(jnp.where(work == best, blk, nslot), axis=0, keepdims=True)
            hit = blk == idx
            bias = jnp.where(hit, jnp.where(past, 0.0, bias), bias)
            work = jnp.where(hit, -jnp.inf, work)
        own_half = jnp.zeros((HEAD_DIM, tq), F32)
        halves = [bias, tail, own_half] if h == 1 else [own_half, bias, tail]
        q_aug.append(jnp.where(_head_lane_mask(q.shape, h), q.astype(F32),
                               jnp.concatenate(halves, axis=0).T).astype(BF16))

    def attend(scheme):
        init, step, finish, diagonal_parts = scheme

        def tile(j, state):
            rows = pl.ds(pl.multiple_of(j * tk, tk), tk)
            return _attend_tile(q_aug, lambda h: kaug_scr[h, rows, :], lambda h: vaug_scr[h, rows, :], state, step)

        def diagonal_part(state, key0, nkeys):
            rows = pl.ds(pl.multiple_of(iq * tk + key0, nkeys), nkeys)
            return _attend_tile(q_aug, lambda h: kaug_scr[h, rows, :], lambda h: vaug_scr[h, rows, :], state,
                                step, first_row=key0)

        state = _attend_diagonal(diagonal_part, tq, init, diagonal_parts)
        state = lax.fori_loop(0, iq, tile, state)
        o_ref[...] = _merge_heads([finish(st, h) for h, st in enumerate(state)]).astype(BF16)

    shifted_ok = 2.0 * bound <= SHIFTED_MAX_SPAN
    pl.when(shifted_ok)(functools.partial(attend, _SHIFTED))
    pl.when(jnp.logical_not(shifted_ok))(functools.partial(attend, _ONLINE))


def _moba(proj, score_bound, *, bsz, seq, tq, tk):
    assert tq == tk and tq % MOBA_BLOCK == 0
    assert -(-(seq // MOBA_BLOCK) // 8) * 8 < HEAD_DIM
    rows = proj.shape[0]
    nq = seq // tq
    q_spec, k_spec, v_spec, o_spec = _seq_mixer_specs(mixer=2, nq=nq, tq=tq, seq=seq)
    return pl.pallas_call(
        functools.partial(_moba_body, tq=tq, tk=tk, seq=seq),
        grid=(bsz, PAIRS_PER_MIXER, nq),
        in_specs=[pl.BlockSpec(memory_space=pltpu.SMEM), q_spec, k_spec, v_spec],
        out_specs=o_spec,
        out_shape=jax.ShapeDtypeStruct((rows, MIXER_WIDTH), BF16),
        scratch_shapes=[pltpu.VMEM((LANES, 2 * LANES), BF16), pltpu.VMEM((2, seq, LANES), BF16),
                        pltpu.VMEM((2, seq, LANES), BF16)],
        compiler_params=_SEQ_PARAMS,
        name="moba",
    )(score_bound, proj, proj, proj)


def _stick_body(q_ref, k_ref, v_ref, o_ref, *, tq, tk):
    iq = pl.program_id(2)
    groups = [slice(g * tk, (g + 1) * tk) for g in range(tq // tk)]
    q = q_ref[...]
    q_own = [_own_head(q, h) for h in range(2)]
    later_row = lax.broadcasted_iota(jnp.int32, (tk, tk), 0)
    later_col = lax.broadcasted_iota(jnp.int32, (tk, tk), 1)
    later = jnp.where(later_row > later_col, 1.0, 0.0).astype(BF16)
    later2 = jnp.concatenate([later, later], axis=0)

    def attend(z, weigh_values, carry_sum, strictly_past=None):
        softplus = jnp.maximum(z, 0.0) + jnp.log2(1.0 + jnp.exp2(-jnp.abs(z)))
        log_keep = -softplus
        if strictly_past is not None:
            log_keep = jnp.where(strictly_past, log_keep, 0.0)
        afters = []
        newer = carry_sum
        for c in reversed(range(z.shape[1] // tk)):
            blk = log_keep[:, c * tk:(c + 1) * tk]
            hi_f32 = pltpu.bitcast(pltpu.bitcast(blk, jnp.uint32) & jnp.uint32(0xFFFF0000), F32)
            pieces = jnp.concatenate([hi_f32.astype(BF16), (blk - hi_f32).astype(BF16)], axis=1)
            afters.append(_dot(pieces, later2) + newer)
            newer = newer + jnp.sum(blk, axis=1, keepdims=True)
        log_a = z - softplus + jnp.concatenate(afters[::-1], axis=1)
        if strictly_past is not None:
            log_a = jnp.where(strictly_past, log_a, NEG)
        return newer, weigh_values(jnp.exp2(log_a).astype(BF16))

    first_blk, k_win, v_win, masks = [], [], [], []
    row = lax.broadcasted_iota(jnp.int32, (tk, 2 * tk), 0)
    col = lax.broadcasted_iota(jnp.int32, (tk, 2 * tk), 1)
    for g in range(len(groups)):
        own_blk = iq * len(groups) + g
        first_blk.append(jnp.maximum(own_blk - 1, 0))
        keys = pl.ds(pl.multiple_of(first_blk[g] * tk, tk), 2 * tk)
        k_win.append(k_ref[keys, :])
        v_win.append(v_ref[keys, :])
        masks.append(col + (first_blk[g] - own_blk) * tk < row)
    strictly_past = jnp.concatenate(masks, axis=0)
    state = []
    for h in range(2):
        z = jnp.concatenate([_dot_nt(q_own[h][rows], k_win[g]) for g, rows in enumerate(groups)], axis=0)
        state.append(attend(
            z, lambda a: jnp.concatenate([_dot(a[rows], v_win[g]) for g, rows in enumerate(groups)], axis=0),
            jnp.zeros((tq, 1), F32), strictly_past))

    outs = [[None] * len(groups) for _ in range(2)]
    for g, rows in enumerate(groups):
        def cond(loop):
            j, tail = loop
            alive = jnp.max(jnp.maximum(tail[0][0], tail[1][0])) > UNDERFLOW_LOG2
            return jnp.logical_and(j >= 0, alive)

        def body(loop):
            j, tail = loop
            keys = pl.ds(pl.multiple_of(j * tk, tk), tk)
            k_blk = k_ref[keys, :]
            v_blk = v_ref[keys, :]
            new = []
            for h in range(2):
                carry_sum, acc = tail[h]
                carry_sum, add = attend(_dot_nt(q_own[h][rows], k_blk), lambda a: _dot(a, v_blk), carry_sum)
                new.append((carry_sum, acc + add))
            return j - 1, tuple(new)

        _, tail = lax.while_loop(cond, body, (first_blk[g] - 1, tuple((s[rows], a[rows]) for s, a in state)))
        for h in range(2):
            outs[h][g] = tail[h][1]
    o_ref[...] = _merge_heads([jnp.concatenate(outs[h], axis=0) for h in range(2)]).astype(BF16)


def _stick(proj, *, bsz, seq, tq, tk):
    rows = proj.shape[0]
    nq = seq // tq
    q_spec, k_spec, v_spec, o_spec = _seq_mixer_specs(mixer=1, nq=nq, tq=tq, seq=seq)
    return pl.pallas_call(
        functools.partial(_stick_body, tq=tq, tk=tk),
        grid=(bsz, PAIRS_PER_MIXER, nq),
        in_specs=[q_spec, k_spec, v_spec],
        out_specs=o_spec,
        out_shape=jax.ShapeDtypeStruct((rows, MIXER_WIDTH), BF16),
        compiler_params=_SEQ_PARAMS,
        name="stick",
    )(proj, proj, proj)


def _outproj_body(o1_ref, o4_ref, o16_ref, l1_ref, l4_ref, l16_ref, yb_ref, yc_ref, yd_ref,
                  gate_ref, x_ref, w_ref, out_ref, o_scr, l_scr, *, tm):
    def projected(y, mixer):
        cols = slice(mixer * MIXER_WIDTH, (mixer + 1) * MIXER_WIDTH)
        g = gate_ref[:, cols].astype(F32)
        return _dot((y * (g / (1.0 + jnp.exp(-g)))).astype(BF16), w_ref[cols, :])

    acc = x_ref[...]
    for mixer, y_ref in ((1, yb_ref), (2, yc_ref), (3, yd_ref)):
        acc = acc + projected(y_ref[...].astype(F32), mixer)

    for slot, (dil, o_ref, l_ref) in enumerate(zip(STRIDED_DILATIONS, (o4_ref, o16_ref), (l4_ref, l16_ref))):
        n = tm // dil
        for r in range(dil):
            for c in range(LANE_BLOCKS_PER_MIXER):
                cols = slice(r * MIXER_WIDTH + c * LANES, r * MIXER_WIDTH + (c + 1) * LANES)
                o_scr[slot, c, pl.ds(r, n, stride=dil), :] = o_ref[:, cols].astype(F32)
                l_scr[slot, c, pl.ds(r, n, stride=dil), :] = l_ref[:, cols]

    def natural(scr, slot):
        return jnp.concatenate([scr[slot, c] for c in range(LANE_BLOCKS_PER_MIXER)], axis=1)

    l1, l4, l16 = l1_ref[...], natural(l_scr, 0), natural(l_scr, 1)
    m = jnp.maximum(jnp.maximum(l1, l4), l16)
    e1, e4, e16 = jnp.exp2(l1 - m), jnp.exp2(l4 - m), jnp.exp2(l16 - m)
    ya = (e1 * o1_ref[...].astype(F32) + e4 * natural(o_scr, 0)
          + e16 * natural(o_scr, 1)) / (e1 + e4 + e16)
    out_ref[...] = acc + projected(ya, 0)


def _outproj(seg_o, seg_lse, yb, yc, yd, proj, x2, w_out, *, tm):
    rows, d = x2.shape
    tn = d
    row_blk = lambda i, j: (i, 0)
    mix_spec = pl.BlockSpec((tm, MIXER_WIDTH), row_blk)
    seg_specs = [pl.BlockSpec((tm // dil, dil * MIXER_WIDTH), row_blk) for dil in DILATIONS]
    return pl.pallas_call(
        functools.partial(_outproj_body, tm=tm),
        grid=(rows // tm, d // tn),
        in_specs=seg_specs + seg_specs + [mix_spec] * 3 + [
            pl.BlockSpec((tm, MIX_WIDTH), lambda i, j: (i, GATE_GROUP)),
            pl.BlockSpec((tm, tn), lambda i, j: (i, j)),
            pl.BlockSpec((MIX_WIDTH, tn), lambda i, j: (0, j)),
        ],
        out_specs=pl.BlockSpec((tm, tn), lambda i, j: (i, j)),
        out_shape=jax.ShapeDtypeStruct((rows, d), F32),
        scratch_shapes=[pltpu.VMEM((len(STRIDED_DILATIONS), LANE_BLOCKS_PER_MIXER, tm, LANES), F32),
                        pltpu.VMEM((len(STRIDED_DILATIONS), LANE_BLOCKS_PER_MIXER, tm, LANES), F32)],
        compiler_params=pltpu.CompilerParams(
            dimension_semantics=("arbitrary", "arbitrary"), vmem_limit_bytes=VMEM_LIMIT),
        name="outproj",
    )(*seg_o, *seg_lse, yb, yc, yd, proj, x2, w_out)


def _cast_body(w_ref, o_ref):
    o_ref[...] = w_ref[...].astype(BF16)


def _cast_columns(w):
    depth, d, n = w.shape
    step = min(CAST_COLS, n)
    spec = pl.BlockSpec((1, d, step), lambda layer, j: (layer, 0, j))
    return pl.pallas_call(
        _cast_body,
        grid=(depth, n // step),
        in_specs=[spec],
        out_specs=spec,
        out_shape=jax.ShapeDtypeStruct((depth, d, n), BF16),
        compiler_params=pltpu.CompilerParams(
            dimension_semantics=("arbitrary", "arbitrary"), vmem_limit_bytes=VMEM_LIMIT),
        name="cast_w_out",
    )(w)


def _cast_w_in_body(w_ref, main_ref, gate_ref, *, main_steps):
    j = pl.program_id(1)

    @pl.when(j < main_steps)
    def _():
        main_ref[0] = w_ref[...].astype(BF16)

    @pl.when(j == main_steps)
    def _():
        w = w_ref[:, :LANES]
        lane = lax.broadcasted_iota(jnp.int32, w.shape, 1)
        hi, lo = _split2(jnp.where(lane < HEADS_PER_MIXER, w, 0.0))
        gate_ref[0] = jnp.concatenate([hi, lo], axis=1)


def _cast_w_in(w_in):
    depth, d, n = w_in.shape
    assert n == PROJ_COLS + HEADS_PER_MIXER
    main_steps = PROJ_COLS // CAST_COLS
    return pl.pallas_call(
        functools.partial(_cast_w_in_body, main_steps=main_steps),
        grid=(depth, main_steps + 1),
        in_specs=[pl.BlockSpec((d, CAST_COLS), lambda layer, j: (layer, j))],
        out_specs=[
            pl.BlockSpec((1, d, CAST_COLS), lambda layer, j: (layer, 0, jnp.minimum(j, main_steps - 1))),
            pl.BlockSpec((1, d, 2 * LANES), lambda layer, j: (layer, 0, 0)),
        ],
        out_shape=[
            jax.ShapeDtypeStruct((depth, d, PROJ_COLS), BF16),
            jax.ShapeDtypeStruct((depth, d, 2 * LANES), BF16),
        ],
        compiler_params=pltpu.CompilerParams(
            dimension_semantics=("arbitrary", "arbitrary"), vmem_limit_bytes=VMEM_LIMIT),
        name="cast_w_in",
    )(w_in.reshape(depth * d, n))


def _rope_tables(seq):
    inv = 1.0 / (ROPE_THETA ** (jnp.arange(0, HEAD_DIM, 2, dtype=F32) / HEAD_DIM))
    ang = jnp.arange(seq, dtype=F32)[:, None] * inv[None, :]
    cos, sin = jnp.cos(ang), jnp.sin(ang)
    reps = LANES // HEAD_DIM
    cos_t = jnp.tile(jnp.concatenate([cos, cos], axis=1), (1, reps))
    sin_t = jnp.tile(jnp.concatenate([-sin, sin], axis=1), (1, reps))
    return cos_t, sin_t


def _gain_table(qn, kn):
    ones = jnp.ones((HEAD_DIM,), F32)
    per_mixer = {0: (qn[0], ones, qn[1], qn[2]), 1: (kn[0], ones, kn[1], kn[2])}
    blocks = [per_mixer[section][mixer] if section < 2 else ones for section, mixer in PROJ_BLOCK_ORDER]
    tab = jnp.stack([jnp.tile(g.astype(F32), HEADS_PER_MIXER) for g in blocks])
    return tab[:, None, :]


def kernel(x, norm_gain, w_in, q_norm_gain, k_norm_gain, forget_bias, w_out):
    bsz, seq, d = x.shape
    depth = w_in.shape[0]
    rows = bsz * seq
    tm = min(512, seq)
    tq_seq = min(QUERY_TILE, seq)
    cos_t, sin_t = _rope_tables(seq)
    head_of_lane = jnp.arange(MXU_TILE) // HEAD_DIM
    mavg = jnp.where(head_of_lane[:, None] == head_of_lane[None, :], 1.0 / HEAD_DIM, 0.0).astype(BF16)
    tri = jnp.tril(jnp.ones((MXU_TILE, MXU_TILE), F32)).astype(BF16)

    def natural_col(section, r):
        return PROJ_BLOCK_POS[(section, 0)]

    def strided_col(section, r):
        return r

    w_main, wf_cat = _cast_w_in(w_in)
    w_out_bf16 = _cast_columns(w_out)

    x2 = x.reshape(rows, d)
    for layer in range(depth):
        fb_pad = jnp.pad(forget_bias[layer].astype(F32), (0, LANES - HEADS_PER_MIXER))[None, :]
        gain_tab = _gain_table(q_norm_gain[layer], k_norm_gain[layer])

        proj, cum, *strided = _inproj(x2, norm_gain[layer][None, :].astype(F32), w_main[layer], wf_cat[layer],
                                      fb_pad, gain_tab, cos_t, sin_t, mavg, tri, seq=seq, tm=tm)
        seg = [_band_segment(proj, proj, proj, natural_col, bsz=bsz, seq=seq, dil=1, tq=512)]
        for di, dil in enumerate(STRIDED_DILATIONS):
            qd, kd, vd = strided[3 * di:3 * di + 3]
            seg.append(_band_segment(qd, kd, vd, strided_col, bsz=bsz, seq=seq, dil=dil, tq=512))
        yb = _stick(proj, bsz=bsz, seq=seq, tq=min(STICK_QUERY_TILE, seq), tk=STICK_KEY_TILE)
        yc = _moba(proj, _score_bound(q_norm_gain[layer, 1], k_norm_gain[layer, 1]),
                   bsz=bsz, seq=seq, tq=tq_seq, tk=KEY_TILE)
        nt = seq // KEY_TILE
        tile_end_sums = cum.reshape(bsz, nt, KEY_TILE, LANES)[:, :, KEY_TILE - 1, :HEADS_PER_MIXER]
        tile_end_sums = tile_end_sums.transpose(0, 2, 1)
        yd = _fox(proj, cum, tile_end_sums, _score_bound(q_norm_gain[layer, 2], k_norm_gain[layer, 2]),
                  bsz=bsz, seq=seq, tq=tq_seq, tk=KEY_TILE)
        x2 = _outproj([s[0] for s in seg], [s[1] for s in seg], yb, yc, yd, proj, x2,
                      w_out_bf16[layer], tm=min(OUT_ROW_TILE, seq))
    return x2.reshape(bsz, seq, d)
```
